```python
import math
import jax
import jax.numpy as jnp
from jax import lax
import numpy as np

D_MODEL = 1024
BATCH = 8
SEQ = 2048
DEPTH = 4
DEC_BATCH = 32
DEC_SEQ = 4
PAST_LEN = 16384
PAGE_SIZE = 128

N_MIXERS = 3
N_MLA_LAYERS = (DEPTH + 2) // 3
N_GLA_LAYERS = (DEPTH + 1) // 3
N_DSA_LAYERS = DEPTH // 3

MLA_HEADS = 8
MLA_NOPE = 128
MLA_ROPE = 64
MLA_V = 128
MLA_Q_LORA = 384
MLA_KV_LORA = 256
MLA_IN = MLA_Q_LORA + MLA_KV_LORA + MLA_ROPE
ROPE_BASE = 10000.0
Q_BLOCK = 128

GLA_HEADS = 4
GLA_DK = D_MODEL // (2 * GLA_HEADS)
GLA_DV = D_MODEL // GLA_HEADS
GLA_GATE_RANK = 16
GLA_TAU = 16.0
GLA_CHUNK = 64
GLA_SIZES = (GLA_HEADS * GLA_DK, GLA_HEADS * GLA_DK, GLA_HEADS * GLA_DV, GLA_HEADS * GLA_DV, GLA_GATE_RANK)
GLA_IN = sum(GLA_SIZES)

DSA_HEADS = 8
DSA_KV_HEADS = 4
DSA_GROUP = DSA_HEADS // DSA_KV_HEADS
DSA_HEAD_DIM = D_MODEL // DSA_HEADS
DSA_IDX_HEADS = 8
DSA_IDX_DIM = 64
DSA_TOPK = 256
DSA_QBLOCK = 32
DSA_SIZES = (DSA_HEADS * DSA_HEAD_DIM, DSA_KV_HEADS * DSA_HEAD_DIM, DSA_KV_HEADS * DSA_HEAD_DIM,
             DSA_IDX_HEADS * DSA_IDX_DIM, DSA_IDX_DIM, DSA_IDX_HEADS)
DSA_IN = sum(DSA_SIZES)

N_BUCKETS = 32
MAX_DISTANCE = 128

D_FF = 2816
CONV_W = 3
N_MOD = 6
EPS = 1e-6

kernel_name = 'hybrid_mla_gla_dsa_convffn_adaln_step'


def rms_norm(x, g):
    xf = x.astype(jnp.float32)
    y = xf * lax.rsqrt(jnp.mean(xf * xf, axis=-1, keepdims=True) + EPS)
    return (y * g.astype(jnp.float32)).astype(x.dtype)


def split_last(a, sizes):
    offs = np.cumsum([0] + list(sizes))
    return [a[..., int(offs[i]):int(offs[i + 1])] for i in range(len(sizes))]


def rope(x, pos):
    half = x.shape[-1] // 2
    inv = ROPE_BASE ** (-jnp.arange(half, dtype=jnp.float32) / half)
    ang = pos.astype(jnp.float32)[:, None] * inv[None, :]
    cos = jnp.cos(ang)[None, :, None, :]
    sin = jnp.sin(ang)[None, :, None, :]
    x1 = x[..., :half].astype(jnp.float32)
    x2 = x[..., half:].astype(jnp.float32)
    return jnp.concatenate([x1 * cos - x2 * sin, x2 * cos + x1 * sin], axis=-1).astype(x.dtype)


def query_blocked(fn, q_args, q_pos, block):
    T = q_pos.shape[0]
    if T <= block or T % block:
        return fn(*q_args, q_pos)
    nb = T // block

    def to_blocks(a):
        return jnp.moveaxis(a.reshape(a.shape[0], nb, block, *a.shape[2:]), 1, 0)

    out = lax.map(lambda xs: fn(*xs[0], xs[1]),
                  (tuple(to_blocks(a) for a in q_args), q_pos.reshape(nb, block)))
    out = jnp.moveaxis(out, 0, 1)
    return out.reshape(out.shape[0], T, *out.shape[3:])


def t5_bucket(dist):
    n = jnp.maximum(dist, 0)
    exact = N_BUCKETS // 2
    log_ratio = jnp.log(jnp.maximum(n, 1).astype(jnp.float32) / exact) / math.log(MAX_DISTANCE / exact)
    large = jnp.minimum(exact + (log_ratio * (N_BUCKETS - exact)).astype(jnp.int32), N_BUCKETS - 1)
    return jnp.where(n < exact, n, large)


def mla_attend(q_lat, q_pe, q_pos, segments, w_uv):
    scale = (MLA_NOPE + MLA_ROPE) ** -0.5
    logits = []
    for ckv, kpe, k_pos in segments:
        s = jnp.einsum('bthr,bsr->bhts', q_lat, ckv) + jnp.einsum('bthp,bsp->bhts', q_pe, kpe)
        s = s.astype(jnp.float32) * scale
        logits.append(jnp.where(k_pos[None, :] <= q_pos[:, None], s, -jnp.inf))
    p = jax.nn.softmax(jnp.concatenate(logits, axis=-1), axis=-1)
    o_lat = None
    start = 0
    for ckv, _, k_pos in segments:
        n = k_pos.shape[0]
        part = jnp.einsum('bhts,bsr->bthr', p[..., start:start + n], ckv)
        o_lat = part if o_lat is None else o_lat + part
        start += n
    B, t = q_lat.shape[:2]
    return jnp.einsum('bthr,rhv->bthv', o_lat, w_uv).reshape(B, t, MLA_HEADS * MLA_V)


def mla_mixer(h, pos, past_segments, w_in, g_q, g_kv, w_uq, w_uk, w_uv, w_o):
    cq, ckv, kpe = split_last(h @ w_in, (MLA_Q_LORA, MLA_KV_LORA, MLA_ROPE))
    q = jnp.einsum('btr,rhe->bthe', rms_norm(cq, g_q), w_uq)
    q_nope = q[..., :MLA_NOPE]
    q_pe = rope(q[..., MLA_NOPE:], pos)
    ckv = rms_norm(ckv, g_kv)
    kpe = rope(kpe[:, :, None, :], pos)[:, :, 0, :]
    q_lat = jnp.einsum('bthn,rhn->bthr', q_nope, w_uk)
    segments = past_segments + [(ckv, kpe, pos)]

    def block(q_lat_b, q_pe_b, pos_b):
        return mla_attend(q_lat_b, q_pe_b, pos_b, segments, w_uv)

    o = query_blocked(block, (q_lat, q_pe), pos, Q_BLOCK)
    return o.astype(h.dtype) @ w_o, ckv, kpe


def gla_recurrence(q, k, v, log_a, s0):
    B, T, H, DK = q.shape
    DV = v.shape[-1]
    C = GLA_CHUNK if T % GLA_CHUNK == 0 else T
    n = T // C

    def to_chunks(a):
        return jnp.moveaxis(a.astype(jnp.float32).reshape(B, n, C, *a.shape[2:]), 1, 0)

    causal = jnp.tril(jnp.ones((C, C), dtype=bool))

    def step(S, xs):
        qc, kc, vc, lc = xs
        b = jnp.cumsum(lc, axis=1)
        diff = jnp.where(causal[None, :, :, None, None], b[:, :, None] - b[:, None, :], -jnp.inf)
        att = jnp.einsum('bthk,btshk,bshk->bhts', qc, jnp.exp(diff), kc)
        o = jnp.einsum('bhts,bshv->bthv', att, vc) + jnp.einsum('bthk,bhkv->bthv', qc * jnp.exp(b), S)
        b_last = b[:, -1]
        S = jnp.exp(b_last)[..., None] * S + jnp.einsum('bshk,bshv->bhkv', kc * jnp.exp(b_last[:, None] - b), vc)
        return S, o

    S, o = lax.scan(step, s0.astype(jnp.float32), (to_chunks(q), to_chunks(k), to_chunks(v), to_chunks(log_a)))
    return jnp.moveaxis(o, 0, 1).reshape(B, T, H, DV), S


def gla_mixer(h, s0, w_in, w_a2, b_a2, g_o, w_o):
    B, T, _ = h.shape
    q, k, v, r, a_low = split_last(h @ w_in, GLA_SIZES)
    q = q.reshape(B, T, GLA_HEADS, GLA_DK) * GLA_DK ** -0.5
    k = k.reshape(B, T, GLA_HEADS, GLA_DK)
    v = v.reshape(B, T, GLA_HEADS, GLA_DV)
    log_a = jax.nn.log_sigmoid((a_low @ w_a2 + b_a2).astype(jnp.float32)) / GLA_TAU
    log_a = log_a.reshape(B, T, GLA_HEADS, GLA_DK)
    o, s = gla_recurrence(q, k, v, log_a, s0)
    o = rms_norm(o, g_o).reshape(B, T, GLA_HEADS * GLA_DV) * jax.nn.silu(r.astype(jnp.float32))
    return o.astype(h.dtype) @ w_o, s


def dsa_attend(q, qi, wi, q_pos, ki_all, fetch_kv, rel_bias, topk):
    B, t = q.shape[:2]
    L = ki_all.shape[1]
    k_pos = jnp.arange(L, dtype=jnp.int32)
    logits = jnp.einsum('bthd,bsd->bths', qi, ki_all).astype(jnp.float32) * DSA_IDX_DIM ** -0.5
    score = jnp.einsum('bth,bths->bts', wi.astype(jnp.float32), jax.nn.relu(logits))
    score = jnp.where((k_pos[None, :] <= q_pos[:, None])[None], score, -jnp.inf)
    top_val, top_idx = lax.top_k(score, topk)
    valid = top_val > -jnp.inf
    kg, vg = fetch_kv(top_idx)
    qg = q.reshape(B, t, DSA_KV_HEADS, DSA_GROUP, DSA_HEAD_DIM)
    s = jnp.einsum('btgrd,btkgd->btgrk', qg, kg).astype(jnp.float32) * DSA_HEAD_DIM ** -0.5
    bias = rel_bias[t5_bucket(q_pos[None, :, None] - top_idx)]
    bias = jnp.moveaxis(bias.reshape(B, t, topk, DSA_KV_HEADS, DSA_GROUP), 2, -1)
    s = jnp.where(valid[:, :, None, None, :], s + bias.astype(jnp.float32), -jnp.inf)
    p = jax.nn.softmax(s, axis=-1)
    o = jnp.einsum('btgrk,btkgd->btgrd', p, vg)
    return o.reshape(B, t, DSA_HEADS * DSA_HEAD_DIM)


def dsa_mixer(h, pos, layer_j, past, rel_bias, w_in, w_o):
    B, T, _ = h.shape
    q, k, v, qi, ki, wi = split_last(h @ w_in, DSA_SIZES)
    q = q.reshape(B, T, DSA_HEADS, DSA_HEAD_DIM)
    k = k.reshape(B, T, DSA_KV_HEADS, DSA_HEAD_DIM)
    v = v.reshape(B, T, DSA_KV_HEADS, DSA_HEAD_DIM)
    qi = qi.reshape(B, T, DSA_IDX_HEADS, DSA_IDX_DIM)
    wi = wi * DSA_IDX_HEADS ** -0.5
    bidx = jnp.arange(B)[:, None, None]
    if past is None:
        ki_all = ki

        def fetch_kv(idx):
            return k[bidx, idx], v[bidx, idx]
    else:
        page_table, cache_k, cache_v, cache_ki = past
        past_len = page_table.shape[1] * PAGE_SIZE
        ki_past = cache_ki[layer_j, page_table].reshape(B, past_len, DSA_IDX_DIM)
        ki_all = jnp.concatenate([ki_past, ki], axis=1)

        def fetch_kv(idx):
            in_past = (idx < past_len)[..., None, None]
            p_idx = jnp.minimum(idx, past_len - 1)
            phys = page_table[bidx, p_idx // PAGE_SIZE]
            off = p_idx % PAGE_SIZE
            n_idx = jnp.clip(idx - past_len, 0, T - 1)
            kk = jnp.where(in_past, cache_k[layer_j, phys, off], k[bidx, n_idx])
            vv = jnp.where(in_past, cache_v[layer_j, phys, off], v[bidx, n_idx])
            return kk, vv
    topk = min(DSA_TOPK, ki_all.shape[1] // 4)

    def block(q_b, qi_b, wi_b, pos_b):
        return dsa_attend(q_b, qi_b, wi_b, pos_b, ki_all, fetch_kv, rel_bias, topk)

    o = query_blocked(block, (q, qi, wi), pos, DSA_QBLOCK)
    return o.astype(h.dtype) @ w_o, k, v, ki


def conv_ffn(h, prev, w_up, conv_w, conv_b, w_down):
    gate, val = split_last(h @ w_up, (D_FF, D_FF))
    ext = jnp.concatenate([prev.astype(gate.dtype), gate], axis=1)
    conv = lax.conv_general_dilated(ext, conv_w.astype(ext.dtype)[:, None, :], window_strides=(1,),
                                    padding='VALID', dimension_numbers=('NWC', 'WIO', 'NWC'),
                                    feature_group_count=D_FF) + conv_b
    return (jax.nn.silu(conv) * val) @ w_down, ext[:, -(CONV_W - 1):]


def setup_inputs(seed: int = 0) -> dict:
    key = jax.random.key(seed)
    keys = jax.random.split(key, 48)
    counter = iter(range(48))

    def nrm(shape, scale=1.0):
        return scale * jax.random.normal(keys[next(counter)], shape, jnp.float32)

    def gain(shape):
        return 1.0 + nrm(shape, 0.05)

    D = D_MODEL
    n_pages = PAST_LEN // PAGE_SIZE
    used = DEC_BATCH * n_pages
    n_phys = used + (used + 3) // 4
    x_prompt = nrm((BATCH, SEQ, D))
    x_sample = nrm((DEC_BATCH, DEC_SEQ, D))
    cache_mla_ckv = nrm((N_MLA_LAYERS, n_phys, PAGE_SIZE, MLA_KV_LORA))
    cache_mla_kpe = nrm((N_MLA_LAYERS, n_phys, PAGE_SIZE, MLA_ROPE))
    state_gla = nrm((N_GLA_LAYERS, DEC_BATCH, GLA_HEADS, GLA_DK, GLA_DV))
    cache_dsa_k = nrm((N_DSA_LAYERS, n_phys, PAGE_SIZE, DSA_KV_HEADS, DSA_HEAD_DIM))
    cache_dsa_v = nrm((N_DSA_LAYERS, n_phys, PAGE_SIZE, DSA_KV_HEADS, DSA_HEAD_DIM))
    cache_dsa_kidx = nrm((N_DSA_LAYERS, n_phys, PAGE_SIZE, DSA_IDX_DIM))
    state_ffn_conv = nrm((DEPTH, DEC_BATCH, CONV_W - 1, D_FF))
    perm = jax.random.permutation(keys[next(counter)], n_phys)
    page_table = perm[:used].reshape(DEC_BATCH, n_pages).astype(jnp.int32)
    return {
        'x_prompt': x_prompt,
        'x_sample': x_sample,
        'cache_mla_ckv': cache_mla_ckv,
        'cache_mla_kpe': cache_mla_kpe,
        'state_gla': state_gla,
        'cache_dsa_k': cache_dsa_k,
        'cache_dsa_v': cache_dsa_v,
        'cache_dsa_kidx': cache_dsa_kidx,
        'state_ffn_conv': state_ffn_conv,
        'page_table': page_table,
        'c_prompt': nrm((BATCH, D)),
        'c_sample': nrm((DEC_BATCH, D)),
        'ada_w': nrm((DEPTH, D, N_MOD * D), 0.5 * D ** -0.5),
        'ada_b': nrm((DEPTH, N_MOD * D), 0.02),
        'norm1_g': gain((DEPTH, D)),
        'norm2_g': gain((DEPTH, D)),
        'final_g': gain((D,)),
        'mla_w_in': nrm((N_MLA_LAYERS, D, MLA_IN), D ** -0.5),
        'mla_g_q': gain((N_MLA_LAYERS, MLA_Q_LORA)),
        'mla_g_kv': gain((N_MLA_LAYERS, MLA_KV_LORA)),
        'mla_w_uq': nrm((N_MLA_LAYERS, MLA_Q_LORA, MLA_HEADS, MLA_NOPE + MLA_ROPE), MLA_Q_LORA ** -0.5),
        'mla_w_uk': nrm((N_MLA_LAYERS, MLA_KV_LORA, MLA_HEADS, MLA_NOPE), MLA_KV_LORA ** -0.5),
        'mla_w_uv': nrm((N_MLA_LAYERS, MLA_KV_LORA, MLA_HEADS, MLA_V), MLA_KV_LORA ** -0.5),
        'mla_w_o': nrm((N_MLA_LAYERS, MLA_HEADS * MLA_V, D), (MLA_HEADS * MLA_V) ** -0.5),
        'gla_w_in': nrm((N_GLA_LAYERS, D, GLA_IN), D ** -0.5),
        'gla_w_a2': nrm((N_GLA_LAYERS, GLA_GATE_RANK, GLA_HEADS * GLA_DK), GLA_GATE_RANK ** -0.5),
        'gla_b_a2': nrm((N_GLA_LAYERS, GLA_HEADS * GLA_DK), 0.02),
        'gla_g_o': gain((N_GLA_LAYERS, GLA_DV)),
        'gla_w_o': nrm((N_GLA_LAYERS, GLA_HEADS * GLA_DV, D), (GLA_HEADS * GLA_DV) ** -0.5),
        'dsa_w_in': nrm((N_DSA_LAYERS, D, DSA_IN), D ** -0.5),
        'dsa_w_o': nrm((N_DSA_LAYERS, DSA_HEADS * DSA_HEAD_DIM, D), (DSA_HEADS * DSA_HEAD_DIM) ** -0.5),
        'rel_bias': nrm((N_BUCKETS, DSA_HEADS), 0.5),
        'ffn_w_up': nrm((DEPTH, D, 2 * D_FF), D ** -0.5),
        'ffn_conv_w': nrm((DEPTH, CONV_W, D_FF), CONV_W ** -0.5),
        'ffn_conv_b': nrm((DEPTH, D_FF), 0.02),
        'ffn_w_down': nrm((DEPTH, D_FF, D), D_FF ** -0.5),
    }


def reference(x_prompt, x_sample, cache_mla_ckv, cache_mla_kpe, state_gla, cache_dsa_k, cache_dsa_v,
              cache_dsa_kidx, state_ffn_conv, page_table, c_prompt, c_sample, ada_w, ada_b, norm1_g, norm2_g,
              final_g, mla_w_in, mla_g_q, mla_g_kv, mla_w_uq, mla_w_uk, mla_w_uv, mla_w_o, gla_w_in, gla_w_a2,
              gla_b_a2, gla_g_o, gla_w_o, dsa_w_in, dsa_w_o, rel_bias, ffn_w_up, ffn_conv_w, ffn_conv_b,
              ffn_w_down):
    past_len = page_table.shape[1] * PAGE_SIZE

    def trunk(x, c, pos, sample):
        B = x.shape[0]
        c_act = jax.nn.silu(c)
        mla_ckv, mla_kpe, gla_s, dsa_k, dsa_v, dsa_ki, conv_s = [], [], [], [], [], [], []
        for l in range(DEPTH):
            mod = (c_act @ ada_w[l] + ada_b[l])[:, None, :]
            sh1, sc1, gt1, sh2, sc2, gt2 = split_last(mod, (D_MODEL,) * N_MOD)
            h = rms_norm(x, norm1_g[l]) * (1 + sc1) + sh1
            j = l // N_MIXERS
            if l % N_MIXERS == 0:
                segs = []
                if sample:
                    segs = [(cache_mla_ckv[j, page_table].reshape(B, past_len, MLA_KV_LORA),
                             cache_mla_kpe[j, page_table].reshape(B, past_len, MLA_ROPE),
                             jnp.arange(past_len, dtype=jnp.int32))]
                y, ckv, kpe = mla_mixer(h, pos, segs, mla_w_in[j], mla_g_q[j], mla_g_kv[j], mla_w_uq[j],
                                        mla_w_uk[j], mla_w_uv[j], mla_w_o[j])
                mla_ckv.append(ckv)
                mla_kpe.append(kpe)
            elif l % N_MIXERS == 1:
                s0 = state_gla[j] if sample else jnp.zeros((B, GLA_HEADS, GLA_DK, GLA_DV), jnp.float32)
                y, s = gla_mixer(h, s0, gla_w_in[j], gla_w_a2[j], gla_b_a2[j], gla_g_o[j], gla_w_o[j])
                gla_s.append(s.astype(s0.dtype))
            else:
                past = (page_table, cache_dsa_k, cache_dsa_v, cache_dsa_kidx) if sample else None
                y, k, v, ki = dsa_mixer(h, pos, j, past, rel_bias, dsa_w_in[j], dsa_w_o[j])
                dsa_k.append(k)
                dsa_v.append(v)
                dsa_ki.append(ki)
            x = x + gt1 * y
            h = rms_norm(x, norm2_g[l]) * (1 + sc2) + sh2
            prev = state_ffn_conv[l] if sample else jnp.zeros((B, CONV_W - 1, D_FF), x.dtype)
            y, conv_new = conv_ffn(h, prev, ffn_w_up[l], ffn_conv_w[l], ffn_conv_b[l], ffn_w_down[l])
            conv_s.append(conv_new)
            x = x + gt2 * y
        return (rms_norm(x, final_g), jnp.stack(mla_ckv), jnp.stack(mla_kpe), jnp.stack(gla_s),
                jnp.stack(dsa_k), jnp.stack(dsa_v), jnp.stack(dsa_ki), jnp.stack(conv_s))

    pos_p = jnp.arange(x_prompt.shape[1], dtype=jnp.int32)
    pos_s = past_len + jnp.arange(x_sample.shape[1], dtype=jnp.int32)
    y_p, mla_ckv_p, mla_kpe_p, gla_p, dsa_k_p, dsa_v_p, dsa_ki_p, conv_p = trunk(x_prompt, c_prompt, pos_p, False)
    y_s, mla_ckv_s, mla_kpe_s, gla_s, dsa_k_s, dsa_v_s, dsa_ki_s, conv_s = trunk(x_sample, c_sample, pos_s, True)
    return (y_p, y_s, mla_ckv_p, mla_kpe_p, gla_p, dsa_k_p, dsa_v_p, dsa_ki_p, conv_p,
            mla_ckv_s, mla_kpe_s, gla_s, dsa_k_s, dsa_v_s, dsa_ki_s, conv_s)
```

```python
import functools
import math

import numpy as np
import jax
import jax.numpy as jnp
from jax import lax
from jax.experimental import pallas as pl
from jax.experimental.pallas import tpu as pltpu

F32 = jnp.float32
CDT = jnp.bfloat16
EPS = 1e-6
ROPE_BASE = 10000.0
GLA_TAU = 16.0
N_BUCKETS = 32
MAX_DISTANCE = 128
DSA_TOPK = 256
LANE = 128
NEG_INF = float("-inf")
INT_MIN = -2 ** 31
VMEM_LIMIT = 56 * 1024 * 1024


def _bucket_thresholds():
    d = np.arange(0, 4 * MAX_DISTANCE)
    exact = N_BUCKETS // 2
    lr = np.log(np.maximum(d, 1).astype(np.float32) / np.float32(exact)) / np.float32(math.log(MAX_DISTANCE / exact))
    large = np.minimum(exact + (lr * np.float32(N_BUCKETS - exact)).astype(np.int32), N_BUCKETS - 1)
    b = np.where(d < exact, d, large)
    return [int(np.argmax(b >= j)) for j in range(N_BUCKETS)]


BUCKET_LO = _bucket_thresholds()


def _cparams(sem, vmem=VMEM_LIMIT):
    return pltpu.CompilerParams(dimension_semantics=sem, vmem_limit_bytes=vmem)


def _tile(n, target, mult=LANE):
    if n <= target:
        return n
    t = (target // mult) * mult
    while t > mult and n % t:
        t -= mult
    assert n % t == 0, (n, target)
    return t


def _dot(a, b):
    return jnp.dot(a.astype(CDT), b.astype(CDT), preferred_element_type=F32)


def _dot_nt(a, b):
    return lax.dot_general(a.astype(CDT), b.astype(CDT), (((1,), (1,)), ((), ())), preferred_element_type=F32)


def _dot_tn(a, b):
    return lax.dot_general(a.astype(CDT), b.astype(CDT), (((0,), (0,)), ((), ())), preferred_element_type=F32)


def _split(x):
    hi = x.astype(CDT)
    lo = (x - hi.astype(F32)).astype(CDT)
    return hi, lo


def _dot_nt3(ah, al, bh, bl):
    dn = (((1,), (1,)), ((), ()))
    return (lax.dot_general(ah, bh, dn, preferred_element_type=F32)
            + lax.dot_general(ah, bl, dn, preferred_element_type=F32)
            + lax.dot_general(al, bh, dn, preferred_element_type=F32))


def _rms(x, g):
    return x * lax.rsqrt(jnp.mean(x * x, axis=-1, keepdims=True) + EPS) * g


def _silu(x):
    return x * jax.nn.sigmoid(x)


def _ada_kernel(c_ref, w_ref, b_ref, o_ref):
    ca = _silu(c_ref[...])
    o_ref[0] = _dot(ca, w_ref[0]) + b_ref[0]


def _ada_mod(c, ada_w, ada_b):
    L, D, N = ada_w.shape
    R = c.shape[0]
    tn = _tile(N, 1536)
    return pl.pallas_call(
        _ada_kernel,
        grid=(L, N // tn),
        in_specs=[pl.BlockSpec((R, D), lambda l, j: (0, 0)),
                  pl.BlockSpec((1, D, tn), lambda l, j: (l, 0, j)),
                  pl.BlockSpec((1, 1, tn), lambda l, j: (l, 0, j))],
        out_specs=pl.BlockSpec((1, R, tn), lambda l, j: (l, 0, j)),
        out_shape=jax.ShapeDtypeStruct((L, R, N), F32),
        compiler_params=_cparams(("parallel", "parallel")),
        name="ada_mod",
    )(c, ada_w, ada_b.reshape(L, 1, N))


def _nm_linear_kernel(x_ref, g_ref, sc_ref, sh_ref, w_ref, o_ref, h_ref):
    @pl.when(pl.program_id(1) == 0)
    def _():
        h = _rms(x_ref[...], g_ref[...]) * (1.0 + sc_ref[0]) + sh_ref[0]
        h_ref[...] = h.astype(CDT)

    o_ref[...] = jnp.dot(h_ref[...], w_ref[...], preferred_element_type=F32)


def _mod_spec(mod, tm, rows_per_group):
    G, R, D = mod.shape
    tpg = max(rows_per_group // tm, 1)
    return pl.BlockSpec((1, R, D), lambda i, j: (i // tpg, 0, 0))


def _nm_linear(x, g, sc, sh, w, rows_per_group, tm_target=512, tn_target=1024):
    M, D = x.shape
    N = w.shape[1]
    tm = _tile(M, tm_target, 8)
    tn = _tile(N, tn_target)
    return pl.pallas_call(
        _nm_linear_kernel,
        grid=(M // tm, N // tn),
        in_specs=[pl.BlockSpec((tm, D), lambda i, j: (i, 0)),
                  pl.BlockSpec((1, D), lambda i, j: (0, 0)),
                  _mod_spec(sc, tm, rows_per_group),
                  _mod_spec(sh, tm, rows_per_group),
                  pl.BlockSpec((D, tn), lambda i, j: (0, j))],
        out_specs=pl.BlockSpec((tm, tn), lambda i, j: (i, j)),
        out_shape=jax.ShapeDtypeStruct((M, N), F32),
        scratch_shapes=[pltpu.VMEM((tm, D), CDT)],
        compiler_params=_cparams(("parallel", "arbitrary")),
        name="nm_linear",
    )(x, g.reshape(1, D), sc, sh, w)


def _mla_prep_kernel(p_ref, gq_ref, gkv_ref, cq_ref, sq_ref, ck_ref, sk_ref, wqn_ref, wqp_ref, wqs_ref, wuk_ref,
                     ckv_ref, kpe_ref, kcat_ref, qcat_ref, *, n_heads, q_lora, kv_lora, rope, nope):
    p = p_ref[...]
    cq = p[:, :q_lora]
    ckv = p[:, q_lora:q_lora + kv_lora]
    kpe = p[:, q_lora + kv_lora:q_lora + kv_lora + rope]
    kpe_sw = p[:, q_lora + kv_lora + rope:q_lora + kv_lora + 2 * rope]
    cqn = _rms(cq, gq_ref[...]).astype(CDT)
    q_nope = jnp.dot(cqn, wqn_ref[...], preferred_element_type=F32)
    q_pe = (jnp.dot(cqn, wqp_ref[...], preferred_element_type=F32) * cq_ref[...]
            + jnp.dot(cqn, wqs_ref[...], preferred_element_type=F32) * sq_ref[...])
    ckv_n = _rms(ckv, gkv_ref[...])
    kpe_r = kpe * ck_ref[...] + kpe_sw * sk_ref[...]
    ckv_ref[...] = ckv_n
    kpe_ref[...] = kpe_r
    tm = p.shape[0]
    hd = kv_lora + LANE
    pad = jnp.zeros((tm, LANE - rope), CDT)
    kcat_ref[...] = jnp.concatenate([ckv_n.astype(CDT), kpe_r.astype(CDT), pad], axis=1)
    for h in range(n_heads):
        q_lat = _dot(q_nope[:, h * nope:(h + 1) * nope], wuk_ref[h])
        qcat_ref[:, h * hd:(h + 1) * hd] = jnp.concatenate(
            [q_lat.astype(CDT), q_pe[:, h * rope:(h + 1) * rope].astype(CDT), pad], axis=1)


def _mla_prep(p, g_q, g_kv, tabs, wqn, wqp, wqs, wukT, dims):
    n_heads, q_lora, kv_lora, rope, nope = dims
    M, NP = p.shape
    cosq, sinq, cosk, sink = tabs
    tm = _tile(M, 256, 8)
    ntab = cosq.shape[0] // tm
    hd = kv_lora + LANE
    row = lambda i: (i, 0)
    tab = lambda i: (i % ntab, 0)
    full2 = lambda i: (0, 0)
    full3 = lambda i: (0, 0, 0)
    kern = functools.partial(_mla_prep_kernel, n_heads=n_heads, q_lora=q_lora, kv_lora=kv_lora, rope=rope, nope=nope)
    return pl.pallas_call(
        kern,
        grid=(M // tm,),
        in_specs=[pl.BlockSpec((tm, NP), row),
                  pl.BlockSpec((1, q_lora), full2),
                  pl.BlockSpec((1, kv_lora), full2),
                  pl.BlockSpec((tm, n_heads * rope), tab),
                  pl.BlockSpec((tm, n_heads * rope), tab),
                  pl.BlockSpec((tm, rope), tab),
                  pl.BlockSpec((tm, rope), tab),
                  pl.BlockSpec(wqn.shape, full2),
                  pl.BlockSpec(wqp.shape, full2),
                  pl.BlockSpec(wqs.shape, full2),
                  pl.BlockSpec(wukT.shape, full3)],
        out_specs=[pl.BlockSpec((tm, kv_lora), row),
                   pl.BlockSpec((tm, rope), row),
                   pl.BlockSpec((tm, hd), row),
                   pl.BlockSpec((tm, n_heads * hd), row)],
        out_shape=[jax.ShapeDtypeStruct((M, kv_lora), F32),
                   jax.ShapeDtypeStruct((M, rope), F32),
                   jax.ShapeDtypeStruct((M, hd), CDT),
                   jax.ShapeDtypeStruct((M, n_heads * hd), CDT)],
        compiler_params=_cparams(("parallel",)),
        name="mla_prep",
    )(p, g_q.reshape(1, -1), g_kv.reshape(1, -1), cosq, sinq, cosk, sink, wqn, wqp, wqs, wukT)


def _mla_flash_kernel(q_ref, k_ref, o_ref, m_ref, l_ref, acc_ref, *, n_heads, tq, tk, hd, dv, scale):
    qi = pl.program_id(1)
    ki = pl.program_id(2)

    @pl.when(ki == 0)
    def _():
        m_ref[...] = jnp.full(m_ref.shape, NEG_INF, F32)
        l_ref[...] = jnp.zeros(l_ref.shape, F32)
        acc_ref[...] = jnp.zeros(acc_ref.shape, F32)

    @pl.when(ki * tk <= qi * tq + tq - 1)
    def _():
        k = k_ref[...]
        v = k[:, :dv]
        row = qi * tq + lax.broadcasted_iota(jnp.int32, (tq, tk), 0)
        col = ki * tk + lax.broadcasted_iota(jnp.int32, (tq, tk), 1)
        mask = col <= row
        for h in range(n_heads):
            s = _dot_nt(q_ref[:, h * hd:(h + 1) * hd], k) * scale
            s = jnp.where(mask, s, NEG_INF)
            m_prev = m_ref[h]
            m_new = jnp.maximum(m_prev, jnp.max(s, axis=-1, keepdims=True))
            alpha = jnp.exp(m_prev - m_new)
            p = jnp.exp(s - m_new)
            l_ref[h] = alpha * l_ref[h] + jnp.sum(p, axis=-1, keepdims=True)
            acc_ref[h] = alpha * acc_ref[h] + _dot(p, v)
            m_ref[h] = m_new

    @pl.when(ki == pl.num_programs(2) - 1)
    def _():
        for h in range(n_heads):
            o_ref[:, h * dv:(h + 1) * dv] = (acc_ref[h] / l_ref[h]).astype(o_ref.dtype)


def _mla_flash(qcat, kcat, B, T, n_heads, dv, scale):
    M, hd = kcat.shape
    tq = _tile(T, 256, 8)
    tk = _tile(T, 512, 8)
    nq, nk = T // tq, T // tk
    kern = functools.partial(_mla_flash_kernel, n_heads=n_heads, tq=tq, tk=tk, hd=hd, dv=dv, scale=scale)

    def kmap(b, qi, ki):
        return (b * nk + jnp.minimum(ki, (qi * tq + tq - 1) // tk), 0)

    return pl.pallas_call(
        kern,
        grid=(B, nq, nk),
        in_specs=[pl.BlockSpec((tq, n_heads * hd), lambda b, qi, ki: (b * nq + qi, 0)),
                  pl.BlockSpec((tk, hd), kmap)],
        out_specs=pl.BlockSpec((tq, n_heads * dv), lambda b, qi, ki: (b * nq + qi, 0)),
        out_shape=jax.ShapeDtypeStruct((M, n_heads * dv), CDT),
        scratch_shapes=[pltpu.VMEM((n_heads, tq, 1), F32),
                        pltpu.VMEM((n_heads, tq, 1), F32),
                        pltpu.VMEM((n_heads, tq, dv), F32)],
        compiler_params=_cparams(("parallel", "parallel", "arbitrary")),
        name="mla_flash",
    )(qcat, kcat)


def _mla_decode_kernel(pt_ref, q_ref, kn_ref, *refs, n_pages_step, n_heads, t_new, dv, rope, scale):
    G = n_pages_step
    ckv_refs = refs[:G]
    kpe_refs = refs[G:2 * G]
    o_ref, m_ref, l_ref, acc_ref = refs[2 * G:]
    g = pl.program_id(1)

    @pl.when(g == 0)
    def _():
        m_ref[...] = jnp.full(m_ref.shape, NEG_INF, F32)
        l_ref[...] = jnp.zeros(l_ref.shape, F32)
        acc_ref[...] = jnp.zeros(acc_ref.shape, F32)

    q = q_ref[0]

    def update(s, v):
        m_prev = m_ref[...]
        m_new = jnp.maximum(m_prev, jnp.max(s, axis=-1, keepdims=True))
        alpha = jnp.exp(m_prev - m_new)
        p = jnp.exp(s - m_new)
        l_ref[...] = alpha * l_ref[...] + jnp.sum(p, axis=-1, keepdims=True)
        acc_ref[...] = alpha * acc_ref[...] + _dot(p, v)
        m_ref[...] = m_new

    ckv = jnp.concatenate([r[0, 0] for r in ckv_refs], axis=0).astype(CDT)
    kpe = jnp.concatenate([r[0, 0] for r in kpe_refs], axis=0).astype(CDT)
    s = (_dot_nt(q[:, :dv], ckv) + _dot_nt(q[:, dv:dv + rope], kpe)) * scale
    update(s, ckv)

    @pl.when(g == pl.num_programs(1) - 1)
    def _():
        kn = kn_ref[0]
        s2 = _dot_nt(q, kn) * scale
        r, c = s2.shape
        t_row = lax.broadcasted_iota(jnp.int32, (r, c), 0) // n_heads
        col = lax.broadcasted_iota(jnp.int32, (r, c), 1)
        s2 = jnp.where((col <= t_row) & (col < t_new), s2, NEG_INF)
        update(s2, kn[:, :dv])
        o_ref[0] = (acc_ref[...] / l_ref[...]).astype(o_ref.dtype)


def _mla_decode(qcat, kcat, cache_ckv, cache_kpe, layer, page_table, B, Ts, n_heads, dv, rope, scale, G):
    hd = kcat.shape[1]
    n_pages = page_table.shape[1]
    page = cache_ckv.shape[2]
    assert n_pages % G == 0
    NG = n_pages // G
    R = Ts * n_heads
    q3 = qcat.reshape(B, R, hd)
    npad = 16
    kn = jnp.pad(kcat.reshape(B, Ts, hd), ((0, 0), (0, npad - Ts), (0, 0)))
    kern = functools.partial(_mla_decode_kernel, n_pages_step=G, n_heads=n_heads, t_new=Ts, dv=dv, rope=rope,
                             scale=scale)

    def page_map(i):
        return lambda b, g, pt: (layer, pt[b, g * G + i], 0, 0)

    in_specs = [pl.BlockSpec((1, R, hd), lambda b, g, pt: (b, 0, 0)),
                pl.BlockSpec((1, npad, hd), lambda b, g, pt: (b, 0, 0))]
    in_specs += [pl.BlockSpec((1, 1, page, dv), page_map(i)) for i in range(G)]
    in_specs += [pl.BlockSpec((1, 1, page, rope), page_map(i)) for i in range(G)]
    out = pl.pallas_call(
        kern,
        grid_spec=pltpu.PrefetchScalarGridSpec(
            num_scalar_prefetch=1,
            grid=(B, NG),
            in_specs=in_specs,
            out_specs=pl.BlockSpec((1, R, dv), lambda b, g, pt: (b, 0, 0)),
            scratch_shapes=[pltpu.VMEM((R, 1), F32), pltpu.VMEM((R, 1), F32), pltpu.VMEM((R, dv), F32)]),
        out_shape=jax.ShapeDtypeStruct((B, R, dv), CDT),
        compiler_params=_cparams(("parallel", "arbitrary")),
        name="mla_decode",
    )(page_table, q3, kn, *([cache_ckv] * G), *([cache_kpe] * G))
    return out.reshape(B * Ts, n_heads * dv)


def _mla_out_kernel(o_ref, wuv_ref, wo_ref, x_ref, gt_ref, y_ref, *, n_heads, dv):
    parts = [_dot(o_ref[:, h * dv:(h + 1) * dv], wuv_ref[h]).astype(CDT) for h in range(n_heads)]
    o = jnp.concatenate(parts, axis=1)
    y = jnp.dot(o, wo_ref[...], preferred_element_type=F32)
    y_ref[...] = x_ref[...] + gt_ref[0] * y


def _mla_out(o_lat, wuv, wo, x, gt, rows_per_group, n_heads, dv):
    M, D = x.shape
    tm = _tile(M, 512, 8)
    tpg = max(rows_per_group // tm, 1)
    kern = functools.partial(_mla_out_kernel, n_heads=n_heads, dv=dv)
    return pl.pallas_call(
        kern,
        grid=(M // tm,),
        in_specs=[pl.BlockSpec((tm, n_heads * dv), lambda i: (i, 0)),
                  pl.BlockSpec(wuv.shape, lambda i: (0, 0, 0)),
                  pl.BlockSpec(wo.shape, lambda i: (0, 0)),
                  pl.BlockSpec((tm, D), lambda i: (i, 0)),
                  pl.BlockSpec((1, gt.shape[1], D), lambda i: (i // tpg, 0, 0))],
        out_specs=pl.BlockSpec((tm, D), lambda i: (i, 0)),
        out_shape=jax.ShapeDtypeStruct((M, D), F32),
        compiler_params=_cparams(("parallel",)),
        name="mla_out",
    )(o_lat, wuv, wo, x, gt)


def _proj_res_kernel(o_ref, wo_ref, x_ref, gt_ref, y_ref):
    y = jnp.dot(o_ref[...], wo_ref[...], preferred_element_type=F32)
    y_ref[...] = x_ref[...] + gt_ref[0] * y


def _proj_res(o, wo, x, gt, rows_per_group):
    M, D = x.shape
    K = o.shape[1]
    tm = _tile(M, 512, 8)
    tpg = max(rows_per_group // tm, 1)
    return pl.pallas_call(
        _proj_res_kernel,
        grid=(M // tm,),
        in_specs=[pl.BlockSpec((tm, K), lambda i: (i, 0)),
                  pl.BlockSpec(wo.shape, lambda i: (0, 0)),
                  pl.BlockSpec((tm, D), lambda i: (i, 0)),
                  pl.BlockSpec((1, gt.shape[1], D), lambda i: (i // tpg, 0, 0))],
        out_specs=pl.BlockSpec((tm, D), lambda i: (i, 0)),
        out_shape=jax.ShapeDtypeStruct((M, D), F32),
        compiler_params=_cparams(("parallel",)),
        name="proj_res",
    )(o, wo, x, gt)


def _gla_gate_kernel(a_ref, w_ref, b_ref, o_ref):
    z = _dot(a_ref[...], w_ref[...]) + b_ref[...]
    o_ref[...] = (jnp.minimum(z, 0.0) - jnp.log(1.0 + jnp.exp(-jnp.abs(z)))) / GLA_TAU


def _gla_gate(proj, col_block, w_a2p, b_a2):
    M = proj.shape[0]
    N = w_a2p.shape[1]
    tm = _tile(M, 1024, 8)
    return pl.pallas_call(
        _gla_gate_kernel,
        grid=(M // tm,),
        in_specs=[pl.BlockSpec((tm, LANE), lambda i: (i, col_block)),
                  pl.BlockSpec(w_a2p.shape, lambda i: (0, 0)),
                  pl.BlockSpec((1, N), lambda i: (0, 0))],
        out_specs=pl.BlockSpec((tm, N), lambda i: (i, 0)),
        out_shape=jax.ShapeDtypeStruct((M, N), F32),
        compiler_params=_cparams(("parallel",)),
        name="gla_gate",
    )(proj, w_a2p, b_a2.reshape(1, N))


def _cumsum_rows(x):
    C = x.shape[0]
    row = lax.broadcasted_iota(jnp.int32, x.shape, 0)
    if C <= 8:
        out = jnp.zeros_like(x)
        for s in range(C):
            out = out + jnp.where(row >= s, x[s:s + 1], 0.0)
        return out
    sh = 1
    while sh < C:
        x = x + jnp.where(row >= sh, pltpu.roll(x, sh, 0), 0.0)
        sh *= 2
    return x


def _gla_kernel(q_ref, k_ref, v_ref, la_ref, s0_ref, o_ref, sf_ref, st_ref, *, n_heads, dk, dv, sub, qscale):
    c = pl.program_id(1)

    @pl.when(c == 0)
    def _():
        for h in range(n_heads):
            st_ref[h] = s0_ref[0, h].T

    C = q_ref.shape[1]
    nsub = C // sub
    for h in range(n_heads):
        q = q_ref[0, :, h * dk:(h + 1) * dk] * qscale
        k = k_ref[0, :, h * dk:(h + 1) * dk]
        v = v_ref[0, :, h * dv:(h + 1) * dv]
        b = _cumsum_rows(la_ref[0, :, h * dk:(h + 1) * dk])
        st = st_ref[h]
        o_inter = _dot_nt(q * jnp.exp(b), st)
        b_last = b[C - 1:C]
        k_dec = k * jnp.exp(b_last - b)
        st_ref[h] = jnp.exp(b_last) * st + _dot_tn(v, k_dec)
        outs = []
        for i in range(nsub):
            r0 = i * sub
            b_i = b[r0:r0 + sub]
            q_i = q[r0:r0 + sub]
            k_i = k[r0:r0 + sub]
            v_i = v[r0:r0 + sub]
            o_i = o_inter[r0:r0 + sub]
            if i > 0:
                ref_row = b[r0:r0 + 1]
                att = _dot_nt(q_i * jnp.exp(b_i - ref_row), k[:r0] * jnp.exp(ref_row - b[:r0]))
                o_i = o_i + _dot(att, v[:r0])
            t_loc = lax.broadcasted_iota(jnp.int32, (sub, 1), 0)
            for s in range(sub):
                w = jnp.exp(jnp.minimum(b_i - b_i[s:s + 1], 0.0))
                col = jnp.sum(q_i * w * k_i[s:s + 1], axis=-1, keepdims=True)
                o_i = o_i + jnp.where(t_loc >= s, col, 0.0) * v_i[s:s + 1]
            outs.append(o_i)
        o_ref[0, :, h * dv:(h + 1) * dv] = outs[0] if nsub == 1 else jnp.concatenate(outs, axis=0)

    @pl.when(c == pl.num_programs(1) - 1)
    def _():
        for h in range(n_heads):
            sf_ref[0, h] = st_ref[h].T


def _gla_recurrence(proj, log_a, s0, B, T, n_heads, dk, dv):
    C = 64 if T % 64 == 0 else T
    sub = min(16, C)
    nc = T // C
    Np = proj.shape[1]
    p3 = proj.reshape(B * nc, C, Np)
    la3 = log_a.reshape(B * nc, C, n_heads * dk)
    hk, hv = n_heads * dk, n_heads * dv
    assert hv % hk == 0
    kern = functools.partial(_gla_kernel, n_heads=n_heads, dk=dk, dv=dv, sub=sub, qscale=dk ** -0.5)
    o, sf = pl.pallas_call(
        kern,
        grid=(B, nc),
        in_specs=[pl.BlockSpec((1, C, hk), lambda b, c: (b * nc + c, 0, 0)),
                  pl.BlockSpec((1, C, hk), lambda b, c: (b * nc + c, 0, 1)),
                  pl.BlockSpec((1, C, hv), lambda b, c: (b * nc + c, 0, 2 * hk // hv)),
                  pl.BlockSpec((1, C, hk), lambda b, c: (b * nc + c, 0, 0)),
                  pl.BlockSpec((1, n_heads, dk, dv), lambda b, c: (b, 0, 0, 0))],
        out_specs=[pl.BlockSpec((1, C, hv), lambda b, c: (b * nc + c, 0, 0)),
                   pl.BlockSpec((1, n_heads, dk, dv), lambda b, c: (b, 0, 0, 0))],
        out_shape=[jax.ShapeDtypeStruct((B * nc, C, hv), F32),
                   jax.ShapeDtypeStruct((B, n_heads, dk, dv), F32)],
        scratch_shapes=[pltpu.VMEM((n_heads, dv, dk), F32)],
        compiler_params=_cparams(("parallel", "arbitrary")),
        name="gla_recurrence",
    )(p3, p3, p3, la3, s0)
    return o.reshape(B * T, hv), sf


def _gla_out_kernel(o_ref, r_ref, g_ref, wo_ref, x_ref, gt_ref, y_ref, *, n_heads, dv):
    parts = []
    for h in range(n_heads):
        sl = slice(h * dv, (h + 1) * dv)
        parts.append((_rms(o_ref[:, sl], g_ref[...]) * _silu(r_ref[:, sl])).astype(CDT))
    y = jnp.dot(jnp.concatenate(parts, axis=1), wo_ref[...], preferred_element_type=F32)
    y_ref[...] = x_ref[...] + gt_ref[0] * y


def _gla_out(o, proj, r_block, g_o, wo, x, gt, rows_per_group, n_heads, dv):
    M, D = x.shape
    hv = n_heads * dv
    tm = _tile(M, 512, 8)
    tpg = max(rows_per_group // tm, 1)
    kern = functools.partial(_gla_out_kernel, n_heads=n_heads, dv=dv)
    return pl.pallas_call(
        kern,
        grid=(M // tm,),
        in_specs=[pl.BlockSpec((tm, hv), lambda i: (i, 0)),
                  pl.BlockSpec((tm, hv), lambda i: (i, r_block)),
                  pl.BlockSpec((1, dv), lambda i: (0, 0)),
                  pl.BlockSpec(wo.shape, lambda i: (0, 0)),
                  pl.BlockSpec((tm, D), lambda i: (i, 0)),
                  pl.BlockSpec((1, gt.shape[1], D), lambda i: (i // tpg, 0, 0))],
        out_specs=pl.BlockSpec((tm, D), lambda i: (i, 0)),
        out_shape=jax.ShapeDtypeStruct((M, D), F32),
        compiler_params=_cparams(("parallel",)),
        name="gla_out",
    )(o, proj, g_o.reshape(1, dv), wo, x, gt)


def _order_key(score):
    score = jnp.where(score == 0.0, 0.0, score)
    bits = pltpu.bitcast(score, jnp.int32)
    return jnp.where(bits < 0, bits ^ jnp.int32(0x7FFFFFFF), bits)


def _kth_largest_key(key, topk, axes):
    shape = tuple(1 if a in axes else s for a, s in enumerate(key.shape))

    def body(it, t):
        cand = t + lax.shift_left(jnp.int32(1), jnp.int32(31) - it)
        cnt = jnp.sum(jnp.where(key >= cand, 1, 0), axis=axes, keepdims=True)
        return jnp.where(cnt >= topk, cand, t)

    return lax.fori_loop(0, 32, body, jnp.full(shape, INT_MIN, jnp.int32))


def _bias_chain(dist, value_of_bucket):
    val = value_of_bucket(0)
    for j in range(1, N_BUCKETS):
        val = jnp.where(dist >= BUCKET_LO[j], value_of_bucket(j), val)
    return val


def _strict_upper(n):
    a = lax.broadcasted_iota(jnp.int32, (n, n), 0)
    b = lax.broadcasted_iota(jnp.int32, (n, n), 1)
    return jnp.where(a < b, 1.0, 0.0).astype(CDT)


def _dsa_prompt_kernel(rb_ref, q_ref, qi_ref, wq_ref, k_ref, v_ref, kk_ref, o_ref, sel_ref, band_ref, *,
                       n_heads, n_kv, hd, n_idx, di, topk, tq, T):
    b_id = pl.program_id(0)
    qt = pl.program_id(1)
    group = n_heads // n_kv
    nkb = T // LANE

    @pl.when((b_id == 0) & (qt == 0))
    def _():
        i = lax.broadcasted_iota(jnp.int32, (tq, LANE), 0)
        j = lax.broadcasted_iota(jnp.int32, (tq, LANE), 1)
        for h in range(n_heads):
            far = rb_ref[N_BUCKETS - 1, h]
            for w in range(tq // LANE + 1):
                dist = i - j + w * LANE
                band_ref[w, h] = _bias_chain(dist, lambda bk: rb_ref[bk, h]) - far

    row_pos = qt * tq + lax.broadcasted_iota(jnp.int32, (tq, T), 0)
    col_pos = lax.broadcasted_iota(jnp.int32, (tq, T), 1)
    causal = col_pos <= row_pos

    kh, kl = _split(kk_ref[:, :di])
    wi = wq_ref[:, di:di + n_idx] * (n_idx ** -0.5)
    score = jnp.zeros((tq, T), F32)
    for h in range(n_idx):
        qh, ql = _split(qi_ref[:, h * di:(h + 1) * di])
        lg = _dot_nt3(qh, ql, kh, kl) * (di ** -0.5)
        score = score + wi[:, h:h + 1] * jnp.maximum(lg, 0.0)
    score = jnp.where(causal, score, NEG_INF)
    key = _order_key(score)
    thr = _kth_largest_key(key, topk, (1,))
    gt = key > thr
    eq = key == thr
    n_gt = jnp.sum(jnp.where(gt, 1, 0), axis=1, keepdims=True)
    n_eq = jnp.sum(jnp.where(eq, 1, 0), axis=1, keepdims=True)
    need = topk - n_gt
    tie = jnp.max(n_eq - need) > 0

    @pl.when(jnp.logical_not(tie))
    def _():
        sel_ref[...] = jnp.where((gt | eq) & causal, 0.0, NEG_INF)

    @pl.when(tie)
    def _():
        upper = _strict_upper(LANE)
        run = jnp.zeros((tq, 1), F32)
        needf = need.astype(F32)
        for kb in range(nkb):
            sl = slice(kb * LANE, (kb + 1) * LANE)
            eqb = jnp.where(eq[:, sl], 1.0, 0.0)
            pre = jnp.dot(eqb.astype(CDT), upper, preferred_element_type=F32) + run
            keep = gt[:, sl] | (eq[:, sl] & (pre < needf))
            sel_ref[:, sl] = jnp.where(keep & causal[:, sl], 0.0, NEG_INF)
            run = run + jnp.sum(eqb, axis=1, keepdims=True)

    scale = hd ** -0.5
    nband = tq // LANE + 1
    for g in range(n_kv):
        kg = k_ref[:, g * hd:(g + 1) * hd].astype(CDT)
        vg = v_ref[:, g * hd:(g + 1) * hd].astype(CDT)
        for r in range(group):
            h = g * group + r
            s = _dot_nt(q_ref[:, h * hd:(h + 1) * hd], kg) * scale + rb_ref[N_BUCKETS - 1, h]
            blocks = []
            for kb in range(nkb):
                w = qt * (tq // LANE) - kb
                add = jnp.zeros((tq, LANE), F32)
                for wv in range(nband):
                    add = jnp.where(w == wv, band_ref[wv, h], add)
                blocks.append(add)
            s = s + jnp.concatenate(blocks, axis=1) + sel_ref[...]
            m = jnp.max(s, axis=-1, keepdims=True)
            p = jnp.exp(s - m)
            l = jnp.sum(p, axis=-1, keepdims=True)
            o_ref[:, h * hd:(h + 1) * hd] = (_dot(p, vg) / l).astype(o_ref.dtype)


def _dsa_prompt(proj, rel_bias, B, T, dims, topk):
    n_heads, n_kv, hd, n_idx, di = dims
    M, Np = proj.shape
    tq = LANE
    nqt = T // tq
    hq, hkv, hi = n_heads * hd, n_kv * hd, n_idx * di
    assert hq % hkv == 0 and (hq + 2 * hkv) % hi == 0 and (hq + 2 * hkv + hi) % LANE == 0
    kk_blk = (hq + 2 * hkv + hi) // LANE
    kern = functools.partial(_dsa_prompt_kernel, n_heads=n_heads, n_kv=n_kv, hd=hd, n_idx=n_idx, di=di, topk=topk,
                             tq=tq, T=T)
    return pl.pallas_call(
        kern,
        grid=(B, nqt),
        in_specs=[pl.BlockSpec(memory_space=pltpu.SMEM),
                  pl.BlockSpec((tq, hq), lambda b, t: (b * nqt + t, 0)),
                  pl.BlockSpec((tq, hi), lambda b, t: (b * nqt + t, (hq + 2 * hkv) // hi)),
                  pl.BlockSpec((tq, LANE), lambda b, t: (b * nqt + t, kk_blk)),
                  pl.BlockSpec((T, hkv), lambda b, t: (b, hq // hkv)),
                  pl.BlockSpec((T, hkv), lambda b, t: (b, hq // hkv + 1)),
                  pl.BlockSpec((T, LANE), lambda b, t: (b, kk_blk))],
        out_specs=pl.BlockSpec((tq, hq), lambda b, t: (b * nqt + t, 0)),
        out_shape=jax.ShapeDtypeStruct((M, hq), CDT),
        scratch_shapes=[pltpu.VMEM((tq, T), F32),
                        pltpu.VMEM((tq // LANE + 1, n_heads, tq, LANE), F32)],
        compiler_params=_cparams(("arbitrary", "arbitrary")),
        name="dsa_prompt",
    )(rel_bias, proj, proj, proj, proj, proj, proj)


def _dsa_select_kernel(pt_ref, qi_ref, wq_ref, kn_ref, *refs, n_pages_step, n_idx, di, topk, t_new):
    G = n_pages_step
    ki_refs = refs[:G]
    mask_ref, sc_ref = refs[G:]
    g = pl.program_id(1)
    NG = pl.num_programs(1)
    R = qi_ref.shape[1]
    wi = wq_ref[0][:, di:di + n_idx] * (n_idx ** -0.5)
    qs = [_split(qi_ref[0][:, h * di:(h + 1) * di]) for h in range(n_idx)]

    def scores(keys):
        kh, kl = _split(keys)
        sc = jnp.zeros((R, keys.shape[0]), F32)
        for h in range(n_idx):
            lg = _dot_nt3(qs[h][0], qs[h][1], kh, kl) * (di ** -0.5)
            sc = sc + wi[:, h:h + 1] * jnp.maximum(lg, 0.0)
        return sc

    past = scores(jnp.concatenate([r[0, 0] for r in ki_refs], axis=0))
    GW = past.shape[1]
    sc_ref[g] = past

    @pl.when(g == NG - 1)
    def _():
        new = scores(kn_ref[0][:, :di])
        t_row = lax.broadcasted_iota(jnp.int32, new.shape, 0) % t_new
        col = lax.broadcasted_iota(jnp.int32, new.shape, 1)
        new = jnp.where((col <= t_row) & (col < t_new), new, NEG_INF)
        sc_ref[NG] = jnp.concatenate([new, jnp.full((R, GW - LANE), NEG_INF, F32)], axis=1)
        score = sc_ref[...]
        valid = score > NEG_INF
        key = _order_key(score)
        thr = _kth_largest_key(key, topk, (0, 2))
        gt = key > thr
        eq = key == thr
        n_gt = jnp.sum(jnp.where(gt, 1, 0), axis=(0, 2), keepdims=True)
        n_eq = jnp.sum(jnp.where(eq, 1, 0), axis=(0, 2), keepdims=True)
        need = topk - n_gt
        tie = jnp.max(n_eq - need) > 0

        @pl.when(jnp.logical_not(tie))
        def _():
            mask_ref[0] = jnp.where((gt | eq) & valid, 0.0, NEG_INF)

        @pl.when(tie)
        def _():
            upper = _strict_upper(LANE)
            needf = need[0].astype(F32)

            def blk(gi, run):
                k_g = _order_key(sc_ref[gi])
                v_g = sc_ref[gi] > NEG_INF
                for c in range(GW // LANE):
                    sl = slice(c * LANE, (c + 1) * LANE)
                    eqb = jnp.where(k_g[:, sl] == thr[0], 1.0, 0.0)
                    pre = jnp.dot(eqb.astype(CDT), upper, preferred_element_type=F32) + run
                    keep = (k_g[:, sl] > thr[0]) | ((eqb > 0.0) & (pre < needf))
                    mask_ref[0, gi, :, sl] = jnp.where(keep & v_g[:, sl], 0.0, NEG_INF)
                    run = run + jnp.sum(eqb, axis=1, keepdims=True)
                return run

            lax.fori_loop(0, NG + 1, blk, jnp.zeros((R, 1), F32))


def _dsa_select(qi8, wq8, kn, cache_ki, layer, page_table, B, n_idx, di, topk, t_new, G):
    n_pages = page_table.shape[1]
    page = cache_ki.shape[2]
    NG = n_pages // G
    GW = G * page
    R = qi8.shape[1]
    kern = functools.partial(_dsa_select_kernel, n_pages_step=G, n_idx=n_idx, di=di, topk=topk, t_new=t_new)

    def page_map(i):
        return lambda b, g, pt: (layer, pt[b, g * G + i], 0, 0)

    in_specs = [pl.BlockSpec((1, R, qi8.shape[2]), lambda b, g, pt: (b, 0, 0)),
                pl.BlockSpec((1, R, LANE), lambda b, g, pt: (b, 0, 0)),
                pl.BlockSpec((1, LANE, LANE), lambda b, g, pt: (b, 0, 0))]
    in_specs += [pl.BlockSpec((1, 1, page, di), page_map(i)) for i in range(G)]
    return pl.pallas_call(
        kern,
        grid_spec=pltpu.PrefetchScalarGridSpec(
            num_scalar_prefetch=1,
            grid=(B, NG),
            in_specs=in_specs,
            out_specs=pl.BlockSpec((1, NG + 1, R, GW), lambda b, g, pt: (b, 0, 0, 0)),
            scratch_shapes=[pltpu.VMEM((NG + 1, R, GW), F32)]),
        out_shape=jax.ShapeDtypeStruct((B, NG + 1, R, GW), F32),
        compiler_params=_cparams(("parallel", "arbitrary")),
        name="dsa_select",
    )(page_table, qi8, wq8, kn, *([cache_ki] * G))


def _dsa_decode_kernel(pt_ref, rb_ref, q_ref, kn_ref, vn_ref, mask_ref, mnew_ref, *refs,
                       n_pages_step, n_kv, group, hd, t_new, past_len):
    G = n_pages_step
    k_refs = refs[:G]
    v_refs = refs[G:2 * G]
    o_ref, m_ref, l_ref, acc_ref = refs[2 * G:]
    g = pl.program_id(1)
    NG = pl.num_programs(1)
    R = group * t_new
    scale = hd ** -0.5

    @pl.when(g == 0)
    def _():
        m_ref[...] = jnp.full(m_ref.shape, NEG_INF, F32)
        l_ref[...] = jnp.zeros(l_ref.shape, F32)
        acc_ref[...] = jnp.zeros(acc_ref.shape, F32)

    def head_val(kvh, bucket):
        row = lax.broadcasted_iota(jnp.int32, (R, 1), 0)
        val = jnp.full((R, 1), rb_ref[bucket, kvh * group], F32)
        for r in range(1, group):
            val = jnp.where(row >= r * t_new, rb_ref[bucket, kvh * group + r], val)
        return val

    def update(kvh, s, v):
        m_prev = m_ref[kvh]
        m_new = jnp.maximum(m_prev, jnp.max(s, axis=-1, keepdims=True))
        m_safe = jnp.where(m_new > NEG_INF, m_new, 0.0)
        alpha = jnp.exp(m_prev - m_safe)
        p = jnp.exp(s - m_safe)
        l_ref[kvh] = alpha * l_ref[kvh] + jnp.sum(p, axis=-1, keepdims=True)
        acc_ref[kvh] = alpha * acc_ref[kvh] + _dot(p, v)
        m_ref[kvh] = m_new

    kcat = jnp.concatenate([r[0, 0] for r in k_refs], axis=0).astype(CDT)
    vcat = jnp.concatenate([r[0, 0] for r in v_refs], axis=0).astype(CDT)
    GW = kcat.shape[0]
    mask = mask_ref[0, 0]
    t_row = lax.broadcasted_iota(jnp.int32, (R, GW), 0) % t_new
    col = lax.broadcasted_iota(jnp.int32, (R, GW), 1)
    dist = past_len + t_row - (g * GW + col)
    near = g == NG - 1
    for kvh in range(n_kv):
        sl = slice(kvh * hd, (kvh + 1) * hd)
        s = _dot_nt(q_ref[0, kvh], kcat[:, sl]) * scale
        bias = lax.cond(near,
                        lambda: _bias_chain(dist, functools.partial(head_val, kvh)),
                        lambda: jnp.broadcast_to(head_val(kvh, N_BUCKETS - 1), (R, GW)))
        update(kvh, s + bias + mask, vcat[:, sl])

    @pl.when(near)
    def _():
        kn = kn_ref[0].astype(CDT)
        vn = vn_ref[0].astype(CDT)
        mnew = mnew_ref[0, 0][:, :LANE]
        t_r = lax.broadcasted_iota(jnp.int32, (R, LANE), 0) % t_new
        c = lax.broadcasted_iota(jnp.int32, (R, LANE), 1)
        d_new = jnp.maximum(t_r - c, 0)
        for kvh in range(n_kv):
            sl = slice(kvh * hd, (kvh + 1) * hd)
            s = _dot_nt(q_ref[0, kvh], kn[:, sl]) * scale
            bias = _bias_chain(d_new, functools.partial(head_val, kvh))
            update(kvh, s + bias + mnew, vn[:, sl])
            o_ref[0, kvh] = (acc_ref[kvh] / l_ref[kvh]).astype(o_ref.dtype)


def _dsa_decode(q4, kn, vn, mask, rel_bias, cache_k, cache_v, layer, page_table, B, n_kv, group, hd, t_new, G):
    n_pages = page_table.shape[1]
    page = cache_k.shape[2]
    NG = n_pages // G
    GW = G * page
    R = group * t_new
    kern = functools.partial(_dsa_decode_kernel, n_pages_step=G, n_kv=n_kv, group=group, hd=hd, t_new=t_new,
                             past_len=n_pages * page)

    def page_map(i):
        return lambda b, g, pt: (layer, pt[b, g * G + i], 0, 0)

    in_specs = [pl.BlockSpec(memory_space=pltpu.SMEM),
                pl.BlockSpec((1, n_kv, R, hd), lambda b, g, pt: (b, 0, 0, 0)),
                pl.BlockSpec((1, LANE, n_kv * hd), lambda b, g, pt: (b, 0, 0)),
                pl.BlockSpec((1, LANE, n_kv * hd), lambda b, g, pt: (b, 0, 0)),
                pl.BlockSpec((1, 1, R, GW), lambda b, g, pt: (b, g, 0, 0)),
                pl.BlockSpec((1, 1, R, GW), lambda b, g, pt: (b, NG, 0, 0))]
    in_specs += [pl.BlockSpec((1, 1, page, n_kv * hd), page_map(i)) for i in range(G)]
    in_specs += [pl.BlockSpec((1, 1, page, n_kv * hd), page_map(i)) for i in range(G)]
    return pl.pallas_call(
        kern,
        grid_spec=pltpu.PrefetchScalarGridSpec(
            num_scalar_prefetch=1,
            grid=(B, NG),
            in_specs=in_specs,
            out_specs=pl.BlockSpec((1, n_kv, R, hd), lambda b, g, pt: (b, 0, 0, 0)),
            scratch_shapes=[pltpu.VMEM((n_kv, R, 1), F32), pltpu.VMEM((n_kv, R, 1), F32),
                            pltpu.VMEM((n_kv, R, hd), F32)]),
        out_shape=jax.ShapeDtypeStruct((B, n_kv, R, hd), CDT),
        compiler_params=_cparams(("parallel", "arbitrary")),
        name="dsa_decode",
    )(page_table, rel_bias, q4, kn, vn, mask, mask, *([cache_k] * G), *([cache_v] * G))


def _ffn_kernel(*refs, seq_tiles, t_seq, conv_w):
    if seq_tiles:
        (x_ref, g_ref, sc_ref, sh_ref, gt_ref, wg_ref, wv_ref, wd_ref, cw_ref, cb_ref,
         y_ref, tail_ref, h_ref, acc_ref, stash_ref) = refs
    else:
        (x_ref, g_ref, sc_ref, sh_ref, gt_ref, wg_ref, wv_ref, wd_ref, cw_ref, cb_ref, p1_ref, p2_ref,
         y_ref, tail_ref, h_ref, acc_ref) = refs
    i = pl.program_id(0)
    j = pl.program_id(1)

    @pl.when(j == 0)
    def _():
        h = _rms(x_ref[...], g_ref[...]) * (1.0 + sc_ref[0]) + sh_ref[0]
        h_ref[...] = h.astype(CDT)
        acc_ref[...] = jnp.zeros(acc_ref.shape, F32)

    gate = jnp.dot(h_ref[...], wg_ref[...], preferred_element_type=F32)
    val = jnp.dot(h_ref[...], wv_ref[...], preferred_element_type=F32)
    tm = gate.shape[0]
    row = lax.broadcasted_iota(jnp.int32, gate.shape, 0)
    g1 = pltpu.roll(gate, 1, 0)
    g2 = pltpu.roll(gate, 2, 0)
    if seq_tiles:
        @pl.when((i == 0) & (j == 0))
        def _():
            stash_ref[...] = jnp.zeros(stash_ref.shape, F32)

        prev = jnp.where(i % seq_tiles == 0, 0.0, stash_ref[j])
        g1 = jnp.where(row == 0, prev[7:8], g1)
        g2 = jnp.where(row == 0, prev[6:7], jnp.where(row == 1, prev[7:8], g2))
        stash_ref[j] = gate[tm - 8:]
        tail_ref[0] = gate[tm - 8:]
    else:
        t = row % t_seq
        g1 = jnp.where(t == 0, p1_ref[...], g1)
        g2 = jnp.where(t < 2, p2_ref[...], g2)
        tail_ref[...] = gate
    cw = cw_ref[...]
    conv = cw[0:1] * g2 + cw[1:2] * g1 + cw[2:3] * gate + cb_ref[...]
    act = (_silu(conv) * val).astype(CDT)
    acc_ref[...] += jnp.dot(act, wd_ref[...], preferred_element_type=F32)

    @pl.when(j == pl.num_programs(1) - 1)
    def _():
        y_ref[...] = x_ref[...] + gt_ref[0] * acc_ref[...]


def _ffn(x, g, sc, sh, gt, w_up, w_down, conv_w, conv_b, rows_per_group, t_seq, prev=None):
    M, D = x.shape
    Fd = w_down.shape[0]
    assert conv_w.shape[0] == 3
    tf = _tile(Fd, 256)
    nf = Fd // tf
    cw = jnp.pad(conv_w, ((0, 8 - conv_w.shape[0]), (0, 0)))
    cb = conv_b.reshape(1, Fd)
    seq_mode = prev is None
    if seq_mode:
        tm = _tile(t_seq, 512, 8)
        assert t_seq % tm == 0 and tm >= 8
        seq_tiles = t_seq // tm
    else:
        tm = M
        seq_tiles = 0
    nt = M // tm
    tpg = max(rows_per_group // tm, 1)
    mod = lambda m: pl.BlockSpec((1, m.shape[1], D), lambda i, j: (i // tpg, 0, 0))
    in_specs = [pl.BlockSpec((tm, D), lambda i, j: (i, 0)),
                pl.BlockSpec((1, D), lambda i, j: (0, 0)),
                mod(sc), mod(sh), mod(gt),
                pl.BlockSpec((D, tf), lambda i, j: (0, j)),
                pl.BlockSpec((D, tf), lambda i, j: (0, nf + j)),
                pl.BlockSpec((tf, D), lambda i, j: (j, 0)),
                pl.BlockSpec((8, tf), lambda i, j: (0, j)),
                pl.BlockSpec((1, tf), lambda i, j: (0, j))]
    args = [x, g.reshape(1, D), sc, sh, gt, w_up, w_up, w_down, cw, cb]
    scratch = [pltpu.VMEM((tm, D), CDT), pltpu.VMEM((tm, D), F32)]
    if seq_mode:
        tail_shape = jax.ShapeDtypeStruct((nt, 8, Fd), F32)
        tail_spec = pl.BlockSpec((1, 8, tf), lambda i, j: (i, 0, j))
        scratch.append(pltpu.VMEM((nf, 8, tf), F32))
    else:
        p1 = jnp.concatenate([prev[:, 1:2], jnp.zeros_like(prev[:, :1]).repeat(t_seq - 1, axis=1)], axis=1)
        p2 = jnp.concatenate([prev[:, 0:2], jnp.zeros_like(prev[:, :1]).repeat(t_seq - 2, axis=1)], axis=1)
        args += [p1.reshape(M, Fd), p2.reshape(M, Fd)]
        in_specs += [pl.BlockSpec((tm, tf), lambda i, j: (i, j))] * 2
        tail_shape = jax.ShapeDtypeStruct((M, Fd), F32)
        tail_spec = pl.BlockSpec((tm, tf), lambda i, j: (i, j))
    kern = functools.partial(_ffn_kernel, seq_tiles=seq_tiles, t_seq=t_seq, conv_w=conv_w.shape[0])
    return pl.pallas_call(
        kern,
        grid=(nt, nf),
        in_specs=in_specs,
        out_specs=[pl.BlockSpec((tm, D), lambda i, j: (i, 0)), tail_spec],
        out_shape=[jax.ShapeDtypeStruct((M, D), F32), tail_shape],
        scratch_shapes=scratch,
        compiler_params=_cparams(("arbitrary", "arbitrary")),
        name="ffn",
    )(*args)


def _final_norm_kernel(x_ref, g_ref, o_ref):
    o_ref[...] = _rms(x_ref[...], g_ref[...])


def _final_norm(x, g):
    M, D = x.shape
    tm = _tile(M, 1024, 8)
    return pl.pallas_call(
        _final_norm_kernel,
        grid=(M // tm,),
        in_specs=[pl.BlockSpec((tm, D), lambda i: (i, 0)), pl.BlockSpec((1, D), lambda i: (0, 0))],
        out_specs=pl.BlockSpec((tm, D), lambda i: (i, 0)),
        out_shape=jax.ShapeDtypeStruct((M, D), F32),
        compiler_params=_cparams(("parallel",)),
        name="final_norm",
    )(x, g.reshape(1, D))


def _rope_tables(pos, rope, n_heads, reps):
    half = rope // 2
    inv = ROPE_BASE ** (-jnp.arange(half, dtype=F32) / half)
    ang = pos.astype(F32)[:, None] * inv[None, :]
    cos = jnp.cos(ang)
    sin = jnp.sin(ang)
    cosk = jnp.concatenate([cos, cos], axis=1)
    sink = jnp.concatenate([-sin, sin], axis=1)
    tabs = (jnp.tile(cosk, (1, n_heads)), jnp.tile(sink, (1, n_heads)), cosk, sink)
    return tuple(jnp.tile(t, (reps, 1)) for t in tabs)


def _swap_halves(w, rope):
    half = rope // 2
    return jnp.concatenate([w[..., half:], w[..., :half]], axis=-1)


def _pad_cols(w, n):
    return jnp.pad(w, ((0, 0), (0, n - w.shape[1])))


def kernel(x_prompt, x_sample, cache_mla_ckv, cache_mla_kpe, state_gla, cache_dsa_k, cache_dsa_v, cache_dsa_kidx,
           state_ffn_conv, page_table, c_prompt, c_sample, ada_w, ada_b, norm1_g, norm2_g, final_g, mla_w_in,
           mla_g_q, mla_g_kv, mla_w_uq, mla_w_uk, mla_w_uv, mla_w_o, gla_w_in, gla_w_a2, gla_b_a2, gla_g_o,
           gla_w_o, dsa_w_in, dsa_w_o, rel_bias, ffn_w_up, ffn_conv_w, ffn_conv_b, ffn_w_down):
    Bp, Tp, D = x_prompt.shape
    Bs, Ts, _ = x_sample.shape
    depth = ada_w.shape[0]
    n_mod = ada_w.shape[2] // D
    n_pages, page = page_table.shape[1], cache_mla_ckv.shape[2]
    past_len = n_pages * page
    G = math.gcd(n_pages, 8)

    q_lora, mla_h, qk_dim = mla_w_uq.shape[1:]
    kv_lora, _, nope = mla_w_uk.shape[1:]
    mla_v = mla_w_uv.shape[3]
    rope = qk_dim - nope
    gla_h = state_gla.shape[2]
    gla_dk, gla_dv = state_gla.shape[3:]
    gla_rank = gla_w_a2.shape[1]
    dsa_kv, dsa_hd = cache_dsa_k.shape[3:]
    dsa_di = cache_dsa_kidx.shape[3]
    dsa_h = dsa_w_o.shape[1] // dsa_hd
    dsa_hi = (dsa_w_in.shape[2] - (dsa_h + 2 * dsa_kv) * dsa_hd - dsa_di) // (dsa_di + 1)
    assert kv_lora % LANE == 0 and q_lora % LANE == 0 and rope <= LANE

    mod_all = _ada_mod(jnp.concatenate([c_prompt, c_sample], axis=0), ada_w, ada_b)

    def mods(l, sample):
        m = mod_all[l, Bp:] if sample else mod_all[l, :Bp]
        parts = [m[:, i * D:(i + 1) * D] for i in range(n_mod)]
        if sample:
            return [jnp.repeat(p, Ts, axis=0).reshape(1, Bs * Ts, D) for p in parts]
        return [p.reshape(Bp, 1, D) for p in parts]

    mla_w = []
    for j in range(mla_w_in.shape[0]):
        w_in = mla_w_in[j]
        kcol = q_lora + kv_lora
        w_in_ext = jnp.concatenate([w_in, _swap_halves(w_in[:, kcol:kcol + rope], rope)], axis=1)
        w_in_ext = _pad_cols(w_in_ext, -(-w_in_ext.shape[1] // LANE) * LANE).astype(CDT)
        uq = mla_w_uq[j]
        wqn = uq[:, :, :nope].reshape(q_lora, mla_h * nope).astype(CDT)
        wqp = uq[:, :, nope:].reshape(q_lora, mla_h * rope).astype(CDT)
        wqs = _swap_halves(uq[:, :, nope:], rope).reshape(q_lora, mla_h * rope).astype(CDT)
        wukT = jnp.transpose(mla_w_uk[j], (1, 2, 0)).astype(CDT)
        wuv = jnp.transpose(mla_w_uv[j], (1, 0, 2)).astype(CDT)
        mla_w.append((w_in_ext, wqn, wqp, wqs, wukT, wuv, mla_w_o[j].astype(CDT)))
    gla_np = -(-gla_w_in.shape[2] // LANE) * LANE
    gla_w = []
    for j in range(gla_w_in.shape[0]):
        w_a2p = jnp.pad(gla_w_a2[j], ((0, LANE - gla_rank), (0, 0))).astype(CDT)
        gla_w.append((_pad_cols(gla_w_in[j], gla_np).astype(CDT), w_a2p, gla_w_o[j].astype(CDT)))
    dsa_np = -(-dsa_w_in.shape[2] // LANE) * LANE
    dsa_w = [(_pad_cols(dsa_w_in[j], dsa_np).astype(CDT), dsa_w_o[j].astype(CDT)) for j in range(dsa_w_in.shape[0])]
    ffn_up = ffn_w_up.astype(CDT)
    ffn_down = ffn_w_down.astype(CDT)

    mla_dims = (mla_h, q_lora, kv_lora, rope, nope)
    mla_scale = qk_dim ** -0.5

    def trunk(x3, sample):
        B, T, _ = x3.shape
        M = B * T
        x = x3.reshape(M, D)
        rpg = M if sample else T
        pos = (past_len if sample else 0) + jnp.arange(T, dtype=jnp.int32)
        tabs = _rope_tables(pos, rope, mla_h, B if sample else 1)
        outs = dict(mla_ckv=[], mla_kpe=[], gla=[], dsa_k=[], dsa_v=[], dsa_ki=[], conv=[])
        for l in range(depth):
            sh1, sc1, gt1, sh2, sc2, gt2 = mods(l, sample)
            j = l // 3
            if l % 3 == 0:
                w_in_ext, wqn, wqp, wqs, wukT, wuv, wo = mla_w[j]
                p = _nm_linear(x, norm1_g[l], sc1, sh1, w_in_ext, rpg)
                ckv, kpe, kcat, qcat = _mla_prep(p, mla_g_q[j], mla_g_kv[j], tabs, wqn, wqp, wqs, wukT, mla_dims)
                if sample:
                    o_lat = _mla_decode(qcat, kcat, cache_mla_ckv, cache_mla_kpe, j, page_table, B, T, mla_h,
                                        kv_lora, rope, mla_scale, G)
                else:
                    o_lat = _mla_flash(qcat, kcat, B, T, mla_h, kv_lora, mla_scale)
                x = _mla_out(o_lat, wuv, wo, x, gt1, rpg, mla_h, kv_lora)
                outs["mla_ckv"].append(ckv.reshape(B, T, kv_lora))
                outs["mla_kpe"].append(kpe.reshape(B, T, rope))
            elif l % 3 == 1:
                w_in_p, w_a2p, wo = gla_w[j]
                hk, hv = gla_h * gla_dk, gla_h * gla_dv
                p = _nm_linear(x, norm1_g[l], sc1, sh1, w_in_p, rpg, tn_target=640)
                log_a = _gla_gate(p, (2 * hk + 2 * hv) // LANE, w_a2p, gla_b_a2[j])
                s0 = state_gla[j] if sample else jnp.zeros((B, gla_h, gla_dk, gla_dv), F32)
                o, s_fin = _gla_recurrence(p, log_a, s0, B, T, gla_h, gla_dk, gla_dv)
                x = _gla_out(o, p, (2 * hk + hv) // hv, gla_g_o[j], wo, x, gt1, rpg, gla_h, gla_dv)
                outs["gla"].append(s_fin)
            else:
                w_in_p, wo = dsa_w[j]
                hq, hkv, hi = dsa_h * dsa_hd, dsa_kv * dsa_hd, dsa_hi * dsa_di
                p = _nm_linear(x, norm1_g[l], sc1, sh1, w_in_p, rpg, tn_target=896)
                k_new = p[:, hq:hq + hkv]
                v_new = p[:, hq + hkv:hq + 2 * hkv]
                kk = p[:, hq + 2 * hkv + hi:]
                L_keys = (past_len if sample else 0) + T
                topk = min(DSA_TOPK, L_keys // 4)
                if sample:
                    group = dsa_h // dsa_kv
                    dup = lambda a: jnp.concatenate([a.reshape(B, T, -1)] * group, axis=1)
                    qi8 = dup(p[:, hq + 2 * hkv:hq + 2 * hkv + hi])
                    wq8 = dup(kk)
                    padn = lambda a: jnp.pad(a.reshape(B, T, -1), ((0, 0), (0, LANE - T), (0, 0)))
                    mask = _dsa_select(qi8, wq8, padn(kk), cache_dsa_kidx, j, page_table, B, dsa_hi, dsa_di, topk,
                                       T, G)
                    q4 = p[:, :hq].reshape(B, T, dsa_kv, group, dsa_hd).transpose(0, 2, 3, 1, 4)
                    q4 = q4.reshape(B, dsa_kv, group * T, dsa_hd)
                    ck = cache_dsa_k.reshape(*cache_dsa_k.shape[:3], hkv)
                    cv = cache_dsa_v.reshape(*cache_dsa_v.shape[:3], hkv)
                    o4 = _dsa_decode(q4, padn(k_new), padn(v_new), mask, rel_bias, ck, cv, j, page_table, B,
                                     dsa_kv, group, dsa_hd, T, G)
                    o = o4.reshape(B, dsa_kv, group, T, dsa_hd).transpose(0, 3, 1, 2, 4).reshape(M, hq)
                else:
                    o = _dsa_prompt(p, rel_bias, B, T, (dsa_h, dsa_kv, dsa_hd, dsa_hi, dsa_di), topk)
                x = _proj_res(o, wo, x, gt1, rpg)
                outs["dsa_k"].append(k_new.reshape(B, T, dsa_kv, dsa_hd))
                outs["dsa_v"].append(v_new.reshape(B, T, dsa_kv, dsa_hd))
                outs["dsa_ki"].append(kk[:, :dsa_di].reshape(B, T, dsa_di))
            if sample:
                x, tail = _ffn(x, norm2_g[l], sc2, sh2, gt2, ffn_up[l], ffn_down[l], ffn_conv_w[l], ffn_conv_b[l],
                               rpg, T, prev=state_ffn_conv[l])
                outs["conv"].append(tail.reshape(B, T, -1)[:, T - 2:])
            else:
                x, tail = _ffn(x, norm2_g[l], sc2, sh2, gt2, ffn_up[l], ffn_down[l], ffn_conv_w[l], ffn_conv_b[l],
                               rpg, T)
                nt = tail.shape[0] // B
                outs["conv"].append(tail.reshape(B, nt, 8, -1)[:, nt - 1, 6:8])
        y = _final_norm(x, final_g).reshape(B, T, D)
        return (y, jnp.stack(outs["mla_ckv"]), jnp.stack(outs["mla_kpe"]), jnp.stack(outs["gla"]),
                jnp.stack(outs["dsa_k"]), jnp.stack(outs["dsa_v"]), jnp.stack(outs["dsa_ki"]),
                jnp.stack(outs["conv"]))

    rp = trunk(x_prompt, False)
    rs = trunk(x_sample, True)
    return (rp[0], rs[0]) + tuple(rp[1:]) + tuple(rs[1:])
```

```python
import functools
import math

import numpy as np
import jax
import jax.numpy as jnp
from jax import lax
from jax.experimental import pallas as pl
from jax.experimental.pallas import tpu as pltpu

F32 = jnp.float32
CDT = jnp.bfloat16
EPS = 1e-6
ROPE_BASE = 10000.0
GLA_TAU = 16.0
N_BUCKETS = 32
MAX_DISTANCE = 128
DSA_TOPK = 256
LANE = 128
NEG_INF = float("-inf")
INT_MIN = -2 ** 31
VMEM_LIMIT = 56 * 1024 * 1024
PAGES_PER_STEP_MLA = 32
PAGES_PER_STEP_SELECT = 32
PAGES_PER_STEP_DECODE = 16
FFN_TILE_F = 1408


def _bucket_thresholds():
    d = np.arange(0, 4 * MAX_DISTANCE)
    exact = N_BUCKETS // 2
    lr = np.log(np.maximum(d, 1).astype(np.float32) / np.float32(exact)) / np.float32(math.log(MAX_DISTANCE / exact))
    large = np.minimum(exact + (lr * np.float32(N_BUCKETS - exact)).astype(np.int32), N_BUCKETS - 1)
    b = np.where(d < exact, d, large)
    return [int(np.argmax(b >= j)) for j in range(N_BUCKETS)]


BUCKET_LO = _bucket_thresholds()


def _cparams(sem, vmem=VMEM_LIMIT):
    return pltpu.CompilerParams(dimension_semantics=sem, vmem_limit_bytes=vmem)


def _tile(n, target, mult=LANE):
    if n <= target:
        return n
    t = (target // mult) * mult
    while t > mult and n % t:
        t -= mult
    assert n % t == 0, (n, target)
    return t


def _dot(a, b):
    return jnp.dot(a.astype(CDT), b.astype(CDT), preferred_element_type=F32)


def _dot_nt(a, b):
    return lax.dot_general(a.astype(CDT), b.astype(CDT), (((1,), (1,)), ((), ())), preferred_element_type=F32)


def _dot_tn(a, b):
    return lax.dot_general(a.astype(CDT), b.astype(CDT), (((0,), (0,)), ((), ())), preferred_element_type=F32)


def _split(x):
    hi = x.astype(CDT)
    lo = (x - hi.astype(F32)).astype(CDT)
    return hi, lo


def _dot_nt3(ah, al, bh, bl):
    dn = (((1,), (1,)), ((), ()))
    return (lax.dot_general(ah, bh, dn, preferred_element_type=F32)
            + lax.dot_general(ah, bl, dn, preferred_element_type=F32)
            + lax.dot_general(al, bh, dn, preferred_element_type=F32))


def _rms(x, g):
    return x * lax.rsqrt(jnp.mean(x * x, axis=-1, keepdims=True) + EPS) * g


def _silu(x):
    return x * jax.nn.sigmoid(x)


def _ada_kernel(c_ref, w_ref, b_ref, o_ref):
    ca = _silu(c_ref[...])
    o_ref[0] = _dot(ca, w_ref[0]) + b_ref[0]


def _ada_mod(c, ada_w, ada_b):
    L, D, N = ada_w.shape
    R = c.shape[0]
    tn = _tile(N, 1536)
    return pl.pallas_call(
        _ada_kernel,
        grid=(L, N // tn),
        in_specs=[pl.BlockSpec((R, D), lambda l, j: (0, 0)),
                  pl.BlockSpec((1, D, tn), lambda l, j: (l, 0, j)),
                  pl.BlockSpec((1, 1, tn), lambda l, j: (l, 0, j))],
        out_specs=pl.BlockSpec((1, R, tn), lambda l, j: (l, 0, j)),
        out_shape=jax.ShapeDtypeStruct((L, R, N), F32),
        compiler_params=_cparams(("parallel", "parallel")),
        name="ada_mod",
    )(c, ada_w, ada_b.reshape(L, 1, N))


def _nm_linear_kernel(x_ref, g_ref, sc_ref, sh_ref, w_ref, o_ref, h_ref):
    @pl.when(pl.program_id(1) == 0)
    def _():
        h = _rms(x_ref[...], g_ref[...]) * (1.0 + sc_ref[0]) + sh_ref[0]
        h_ref[...] = h.astype(CDT)

    o_ref[...] = jnp.dot(h_ref[...], w_ref[...], preferred_element_type=F32)


def _mod_spec(mod, tm, rows_per_group):
    G, R, D = mod.shape
    tpg = max(rows_per_group // tm, 1)
    return pl.BlockSpec((1, R, D), lambda i, j: (i // tpg, 0, 0))


def _nm_linear(x, g, sc, sh, w, rows_per_group, tm_target=512, tn_target=1024):
    M, D = x.shape
    N = w.shape[1]
    tm = _tile(M, tm_target, 8)
    tn = _tile(N, tn_target)
    return pl.pallas_call(
        _nm_linear_kernel,
        grid=(M // tm, N // tn),
        in_specs=[pl.BlockSpec((tm, D), lambda i, j: (i, 0)),
                  pl.BlockSpec((1, D), lambda i, j: (0, 0)),
                  _mod_spec(sc, tm, rows_per_group),
                  _mod_spec(sh, tm, rows_per_group),
                  pl.BlockSpec((D, tn), lambda i, j: (0, j))],
        out_specs=pl.BlockSpec((tm, tn), lambda i, j: (i, j)),
        out_shape=jax.ShapeDtypeStruct((M, N), F32),
        scratch_shapes=[pltpu.VMEM((tm, D), CDT)],
        compiler_params=_cparams(("parallel", "arbitrary")),
        name="nm_linear",
    )(x, g.reshape(1, D), sc, sh, w)


def _mla_prep_kernel(p_ref, gq_ref, gkv_ref, cq_ref, sq_ref, ck_ref, sk_ref, wqn_ref, wqp_ref, wqs_ref, wuk_ref,
                     ckv_ref, kpe_ref, kcat_ref, qcat_ref, *, n_heads, q_lora, kv_lora, rope, nope):
    p = p_ref[...]
    cq = p[:, :q_lora]
    ckv = p[:, q_lora:q_lora + kv_lora]
    kpe = p[:, q_lora + kv_lora:q_lora + kv_lora + rope]
    kpe_sw = p[:, q_lora + kv_lora + rope:q_lora + kv_lora + 2 * rope]
    cqn = _rms(cq, gq_ref[...]).astype(CDT)
    q_nope = jnp.dot(cqn, wqn_ref[...], preferred_element_type=F32)
    q_pe = (jnp.dot(cqn, wqp_ref[...], preferred_element_type=F32) * cq_ref[...]
            + jnp.dot(cqn, wqs_ref[...], preferred_element_type=F32) * sq_ref[...])
    ckv_n = _rms(ckv, gkv_ref[...])
    kpe_r = kpe * ck_ref[...] + kpe_sw * sk_ref[...]
    ckv_ref[...] = ckv_n
    kpe_ref[...] = kpe_r
    tm = p.shape[0]
    hd = kv_lora + LANE
    pad = jnp.zeros((tm, LANE - rope), CDT)
    kcat_ref[...] = jnp.concatenate([ckv_n.astype(CDT), kpe_r.astype(CDT), pad], axis=1)
    for h in range(n_heads):
        q_lat = _dot(q_nope[:, h * nope:(h + 1) * nope], wuk_ref[h])
        qcat_ref[:, h * hd:(h + 1) * hd] = jnp.concatenate(
            [q_lat.astype(CDT), q_pe[:, h * rope:(h + 1) * rope].astype(CDT), pad], axis=1)


def _mla_prep(p, g_q, g_kv, tabs, wqn, wqp, wqs, wukT, dims):
    n_heads, q_lora, kv_lora, rope, nope = dims
    M, NP = p.shape
    cosq, sinq, cosk, sink = tabs
    tm = _tile(M, 256, 8)
    ntab = cosq.shape[0] // tm
    hd = kv_lora + LANE
    row = lambda i: (i, 0)
    tab = lambda i: (i % ntab, 0)
    full2 = lambda i: (0, 0)
    full3 = lambda i: (0, 0, 0)
    kern = functools.partial(_mla_prep_kernel, n_heads=n_heads, q_lora=q_lora, kv_lora=kv_lora, rope=rope, nope=nope)
    return pl.pallas_call(
        kern,
        grid=(M // tm,),
        in_specs=[pl.BlockSpec((tm, NP), row),
                  pl.BlockSpec((1, q_lora), full2),
                  pl.BlockSpec((1, kv_lora), full2),
                  pl.BlockSpec((tm, n_heads * rope), tab),
                  pl.BlockSpec((tm, n_heads * rope), tab),
                  pl.BlockSpec((tm, rope), tab),
                  pl.BlockSpec((tm, rope), tab),
                  pl.BlockSpec(wqn.shape, full2),
                  pl.BlockSpec(wqp.shape, full2),
                  pl.BlockSpec(wqs.shape, full2),
                  pl.BlockSpec(wukT.shape, full3)],
        out_specs=[pl.BlockSpec((tm, kv_lora), row),
                   pl.BlockSpec((tm, rope), row),
                   pl.BlockSpec((tm, hd), row),
                   pl.BlockSpec((tm, n_heads * hd), row)],
        out_shape=[jax.ShapeDtypeStruct((M, kv_lora), F32),
                   jax.ShapeDtypeStruct((M, rope), F32),
                   jax.ShapeDtypeStruct((M, hd), CDT),
                   jax.ShapeDtypeStruct((M, n_heads * hd), CDT)],
        compiler_params=_cparams(("parallel",)),
        name="mla_prep",
    )(p, g_q.reshape(1, -1), g_kv.reshape(1, -1), cosq, sinq, cosk, sink, wqn, wqp, wqs, wukT)


def _mla_flash_kernel(q_ref, k_ref, o_ref, m_ref, l_ref, acc_ref, *, n_heads, tq, tk, hd, dv, scale):
    qi = pl.program_id(1)
    ki = pl.program_id(2)

    @pl.when(ki == 0)
    def _():
        m_ref[...] = jnp.full(m_ref.shape, NEG_INF, F32)
        l_ref[...] = jnp.zeros(l_ref.shape, F32)
        acc_ref[...] = jnp.zeros(acc_ref.shape, F32)

    @pl.when(ki * tk <= qi * tq + tq - 1)
    def _():
        k = k_ref[...]
        v = k[:, :dv]
        row = qi * tq + lax.broadcasted_iota(jnp.int32, (tq, tk), 0)
        col = ki * tk + lax.broadcasted_iota(jnp.int32, (tq, tk), 1)
        mask = col <= row
        for h in range(n_heads):
            s = _dot_nt(q_ref[:, h * hd:(h + 1) * hd], k) * scale
            s = jnp.where(mask, s, NEG_INF)
            m_prev = m_ref[h]
            m_new = jnp.maximum(m_prev, jnp.max(s, axis=-1, keepdims=True))
            alpha = jnp.exp(m_prev - m_new)
            p = jnp.exp(s - m_new)
            l_ref[h] = alpha * l_ref[h] + jnp.sum(p, axis=-1, keepdims=True)
            acc_ref[h] = alpha * acc_ref[h] + _dot(p, v)
            m_ref[h] = m_new

    @pl.when(ki == pl.num_programs(2) - 1)
    def _():
        for h in range(n_heads):
            o_ref[:, h * dv:(h + 1) * dv] = (acc_ref[h] / l_ref[h]).astype(o_ref.dtype)


def _mla_flash(qcat, kcat, B, T, n_heads, dv, scale):
    M, hd = kcat.shape
    tq = _tile(T, 256, 8)
    tk = _tile(T, 512, 8)
    nq, nk = T // tq, T // tk
    kern = functools.partial(_mla_flash_kernel, n_heads=n_heads, tq=tq, tk=tk, hd=hd, dv=dv, scale=scale)

    def kmap(b, qi, ki):
        return (b * nk + jnp.minimum(ki, (qi * tq + tq - 1) // tk), 0)

    return pl.pallas_call(
        kern,
        grid=(B, nq, nk),
        in_specs=[pl.BlockSpec((tq, n_heads * hd), lambda b, qi, ki: (b * nq + qi, 0)),
                  pl.BlockSpec((tk, hd), kmap)],
        out_specs=pl.BlockSpec((tq, n_heads * dv), lambda b, qi, ki: (b * nq + qi, 0)),
        out_shape=jax.ShapeDtypeStruct((M, n_heads * dv), CDT),
        scratch_shapes=[pltpu.VMEM((n_heads, tq, 1), F32),
                        pltpu.VMEM((n_heads, tq, 1), F32),
                        pltpu.VMEM((n_heads, tq, dv), F32)],
        compiler_params=_cparams(("parallel", "parallel", "arbitrary")),
        name="mla_flash",
    )(qcat, kcat)


def _mla_decode_kernel(pt_ref, q_ref, kn_ref, *refs, n_pages_step, n_heads, t_new, dv, rope, scale):
    G = n_pages_step
    ckv_refs = refs[:G]
    kpe_refs = refs[G:2 * G]
    o_ref, m_ref, l_ref, acc_ref = refs[2 * G:]
    g = pl.program_id(1)

    @pl.when(g == 0)
    def _():
        m_ref[...] = jnp.full(m_ref.shape, NEG_INF, F32)
        l_ref[...] = jnp.zeros(l_ref.shape, F32)
        acc_ref[...] = jnp.zeros(acc_ref.shape, F32)

    q = q_ref[0]

    def update(s, v):
        m_prev = m_ref[...]
        m_new = jnp.maximum(m_prev, jnp.max(s, axis=-1, keepdims=True))
        alpha = jnp.exp(m_prev - m_new)
        p = jnp.exp(s - m_new)
        l_ref[...] = alpha * l_ref[...] + jnp.sum(p, axis=-1, keepdims=True)
        acc_ref[...] = alpha * acc_ref[...] + _dot(p, v)
        m_ref[...] = m_new

    ckv = jnp.concatenate([r[0, 0] for r in ckv_refs], axis=0).astype(CDT)
    kpe_t = jnp.concatenate([r[0, 0] for r in kpe_refs], axis=1).astype(CDT)
    s = (_dot_nt(q[:, :dv], ckv) + _dot(q[:, dv:dv + rope], kpe_t)) * scale
    update(s, ckv)

    @pl.when(g == pl.num_programs(1) - 1)
    def _():
        kn = kn_ref[0]
        s2 = _dot_nt(q, kn) * scale
        r, c = s2.shape
        t_row = lax.broadcasted_iota(jnp.int32, (r, c), 0) // n_heads
        col = lax.broadcasted_iota(jnp.int32, (r, c), 1)
        s2 = jnp.where((col <= t_row) & (col < t_new), s2, NEG_INF)
        update(s2, kn[:, :dv])
        o_ref[0] = (acc_ref[...] / l_ref[...]).astype(o_ref.dtype)


def _mla_decode(qcat, kcat, cache_ckv, cache_kpe_t, layer, page_table, B, Ts, n_heads, dv, rope, scale):
    hd = kcat.shape[1]
    n_pages = page_table.shape[1]
    page = cache_ckv.shape[2]
    G = math.gcd(n_pages, PAGES_PER_STEP_MLA)
    NG = n_pages // G
    R = Ts * n_heads
    q3 = qcat.reshape(B, R, hd)
    npad = 16
    kn = jnp.pad(kcat.reshape(B, Ts, hd), ((0, 0), (0, npad - Ts), (0, 0)))
    kern = functools.partial(_mla_decode_kernel, n_pages_step=G, n_heads=n_heads, t_new=Ts, dv=dv, rope=rope,
                             scale=scale)

    def page_map(i):
        return lambda b, g, pt: (layer, pt[b, g * G + i], 0, 0)

    in_specs = [pl.BlockSpec((1, R, hd), lambda b, g, pt: (b, 0, 0)),
                pl.BlockSpec((1, npad, hd), lambda b, g, pt: (b, 0, 0))]
    in_specs += [pl.BlockSpec((1, 1, page, dv), page_map(i)) for i in range(G)]
    in_specs += [pl.BlockSpec((1, 1, rope, page), page_map(i)) for i in range(G)]
    out = pl.pallas_call(
        kern,
        grid_spec=pltpu.PrefetchScalarGridSpec(
            num_scalar_prefetch=1,
            grid=(B, NG),
            in_specs=in_specs,
            out_specs=pl.BlockSpec((1, R, dv), lambda b, g, pt: (b, 0, 0)),
            scratch_shapes=[pltpu.VMEM((R, 1), F32), pltpu.VMEM((R, 1), F32), pltpu.VMEM((R, dv), F32)]),
        out_shape=jax.ShapeDtypeStruct((B, R, dv), CDT),
        compiler_params=_cparams(("parallel", "arbitrary")),
        name="mla_decode",
    )(page_table, q3, kn, *([cache_ckv] * G), *([cache_kpe_t] * G))
    return out.reshape(B * Ts, n_heads * dv)


def _mla_out_kernel(o_ref, wuv_ref, wo_ref, x_ref, gt_ref, y_ref, *, n_heads, dv):
    parts = [_dot(o_ref[:, h * dv:(h + 1) * dv], wuv_ref[h]).astype(CDT) for h in range(n_heads)]
    o = jnp.concatenate(parts, axis=1)
    y = jnp.dot(o, wo_ref[...], preferred_element_type=F32)
    y_ref[...] = x_ref[...] + gt_ref[0] * y


def _mla_out(o_lat, wuv, wo, x, gt, rows_per_group, n_heads, dv):
    M, D = x.shape
    tm = _tile(M, 512, 8)
    tpg = max(rows_per_group // tm, 1)
    kern = functools.partial(_mla_out_kernel, n_heads=n_heads, dv=dv)
    return pl.pallas_call(
        kern,
        grid=(M // tm,),
        in_specs=[pl.BlockSpec((tm, n_heads * dv), lambda i: (i, 0)),
                  pl.BlockSpec(wuv.shape, lambda i: (0, 0, 0)),
                  pl.BlockSpec(wo.shape, lambda i: (0, 0)),
                  pl.BlockSpec((tm, D), lambda i: (i, 0)),
                  pl.BlockSpec((1, gt.shape[1], D), lambda i: (i // tpg, 0, 0))],
        out_specs=pl.BlockSpec((tm, D), lambda i: (i, 0)),
        out_shape=jax.ShapeDtypeStruct((M, D), F32),
        compiler_params=_cparams(("parallel",)),
        name="mla_out",
    )(o_lat, wuv, wo, x, gt)


def _proj_res_kernel(o_ref, wo_ref, x_ref, gt_ref, y_ref):
    y = jnp.dot(o_ref[...], wo_ref[...], preferred_element_type=F32)
    y_ref[...] = x_ref[...] + gt_ref[0] * y


def _proj_res(o, wo, x, gt, rows_per_group):
    M, D = x.shape
    K = o.shape[1]
    tm = _tile(M, 512, 8)
    tpg = max(rows_per_group // tm, 1)
    return pl.pallas_call(
        _proj_res_kernel,
        grid=(M // tm,),
        in_specs=[pl.BlockSpec((tm, K), lambda i: (i, 0)),
                  pl.BlockSpec(wo.shape, lambda i: (0, 0)),
                  pl.BlockSpec((tm, D), lambda i: (i, 0)),
                  pl.BlockSpec((1, gt.shape[1], D), lambda i: (i // tpg, 0, 0))],
        out_specs=pl.BlockSpec((tm, D), lambda i: (i, 0)),
        out_shape=jax.ShapeDtypeStruct((M, D), F32),
        compiler_params=_cparams(("parallel",)),
        name="proj_res",
    )(o, wo, x, gt)


def _gla_gate_kernel(a_ref, w_ref, b_ref, o_ref):
    z = _dot(a_ref[...], w_ref[...]) + b_ref[...]
    o_ref[...] = (jnp.minimum(z, 0.0) - jnp.log(1.0 + jnp.exp(-jnp.abs(z)))) / GLA_TAU


def _gla_gate(proj, col_block, w_a2p, b_a2):
    M = proj.shape[0]
    N = w_a2p.shape[1]
    tm = _tile(M, 1024, 8)
    return pl.pallas_call(
        _gla_gate_kernel,
        grid=(M // tm,),
        in_specs=[pl.BlockSpec((tm, LANE), lambda i: (i, col_block)),
                  pl.BlockSpec(w_a2p.shape, lambda i: (0, 0)),
                  pl.BlockSpec((1, N), lambda i: (0, 0))],
        out_specs=pl.BlockSpec((tm, N), lambda i: (i, 0)),
        out_shape=jax.ShapeDtypeStruct((M, N), F32),
        compiler_params=_cparams(("parallel",)),
        name="gla_gate",
    )(proj, w_a2p, b_a2.reshape(1, N))


def _cumsum_rows(x):
    C = x.shape[0]
    row = lax.broadcasted_iota(jnp.int32, x.shape, 0)
    if C <= 8:
        out = jnp.zeros_like(x)
        for s in range(C):
            out = out + jnp.where(row >= s, x[s:s + 1], 0.0)
        return out
    sh = 1
    while sh < C:
        x = x + jnp.where(row >= sh, pltpu.roll(x, sh, 0), 0.0)
        sh *= 2
    return x


def _gla_kernel(q_ref, k_ref, v_ref, la_ref, s0_ref, o_ref, sf_ref, st_ref, *, n_heads, dk, dv, sub, qscale):
    c = pl.program_id(1)

    @pl.when(c == 0)
    def _():
        for h in range(n_heads):
            st_ref[h] = s0_ref[0, h].T

    C = q_ref.shape[1]
    nsub = C // sub
    for h in range(n_heads):
        q = q_ref[0, :, h * dk:(h + 1) * dk] * qscale
        k = k_ref[0, :, h * dk:(h + 1) * dk]
        v = v_ref[0, :, h * dv:(h + 1) * dv]
        b = _cumsum_rows(la_ref[0, :, h * dk:(h + 1) * dk])
        st = st_ref[h]
        o_inter = _dot_nt(q * jnp.exp(b), st)
        b_last = b[C - 1:C]
        k_dec = k * jnp.exp(b_last - b)
        st_ref[h] = jnp.exp(b_last) * st + _dot_tn(v, k_dec)
        outs = []
        for i in range(nsub):
            r0 = i * sub
            b_i = b[r0:r0 + sub]
            q_i = q[r0:r0 + sub]
            k_i = k[r0:r0 + sub]
            v_i = v[r0:r0 + sub]
            o_i = o_inter[r0:r0 + sub]
            if i > 0:
                ref_row = b[r0:r0 + 1]
                att = _dot_nt(q_i * jnp.exp(b_i - ref_row), k[:r0] * jnp.exp(ref_row - b[:r0]))
                o_i = o_i + _dot(att, v[:r0])
            t_loc = lax.broadcasted_iota(jnp.int32, (sub, 1), 0)
            for s in range(sub):
                w = jnp.exp(jnp.minimum(b_i - b_i[s:s + 1], 0.0))
                col = jnp.sum(q_i * w * k_i[s:s + 1], axis=-1, keepdims=True)
                o_i = o_i + jnp.where(t_loc >= s, col, 0.0) * v_i[s:s + 1]
            outs.append(o_i)
        o_ref[0, :, h * dv:(h + 1) * dv] = outs[0] if nsub == 1 else jnp.concatenate(outs, axis=0)

    @pl.when(c == pl.num_programs(1) - 1)
    def _():
        for h in range(n_heads):
            sf_ref[0, h] = st_ref[h].T


def _gla_recurrence(proj, log_a, s0, B, T, n_heads, dk, dv):
    C = 64 if T % 64 == 0 else T
    sub = min(16, C)
    nc = T // C
    Np = proj.shape[1]
    p3 = proj.reshape(B * nc, C, Np)
    la3 = log_a.reshape(B * nc, C, n_heads * dk)
    hk, hv = n_heads * dk, n_heads * dv
    assert hv % hk == 0
    kern = functools.partial(_gla_kernel, n_heads=n_heads, dk=dk, dv=dv, sub=sub, qscale=dk ** -0.5)
    o, sf = pl.pallas_call(
        kern,
        grid=(B, nc),
        in_specs=[pl.BlockSpec((1, C, hk), lambda b, c: (b * nc + c, 0, 0)),
                  pl.BlockSpec((1, C, hk), lambda b, c: (b * nc + c, 0, 1)),
                  pl.BlockSpec((1, C, hv), lambda b, c: (b * nc + c, 0, 2 * hk // hv)),
                  pl.BlockSpec((1, C, hk), lambda b, c: (b * nc + c, 0, 0)),
                  pl.BlockSpec((1, n_heads, dk, dv), lambda b, c: (b, 0, 0, 0))],
        out_specs=[pl.BlockSpec((1, C, hv), lambda b, c: (b * nc + c, 0, 0)),
                   pl.BlockSpec((1, n_heads, dk, dv), lambda b, c: (b, 0, 0, 0))],
        out_shape=[jax.ShapeDtypeStruct((B * nc, C, hv), F32),
                   jax.ShapeDtypeStruct((B, n_heads, dk, dv), F32)],
        scratch_shapes=[pltpu.VMEM((n_heads, dv, dk), F32)],
        compiler_params=_cparams(("parallel", "arbitrary")),
        name="gla_recurrence",
    )(p3, p3, p3, la3, s0)
    return o.reshape(B * T, hv), sf


def _gla_out_kernel(o_ref, r_ref, g_ref, wo_ref, x_ref, gt_ref, y_ref, *, n_heads, dv):
    parts = []
    for h in range(n_heads):
        sl = slice(h * dv, (h + 1) * dv)
        parts.append((_rms(o_ref[:, sl], g_ref[...]) * _silu(r_ref[:, sl])).astype(CDT))
    y = jnp.dot(jnp.concatenate(parts, axis=1), wo_ref[...], preferred_element_type=F32)
    y_ref[...] = x_ref[...] + gt_ref[0] * y


def _gla_out(o, proj, r_block, g_o, wo, x, gt, rows_per_group, n_heads, dv):
    M, D = x.shape
    hv = n_heads * dv
    tm = _tile(M, 512, 8)
    tpg = max(rows_per_group // tm, 1)
    kern = functools.partial(_gla_out_kernel, n_heads=n_heads, dv=dv)
    return pl.pallas_call(
        kern,
        grid=(M // tm,),
        in_specs=[pl.BlockSpec((tm, hv), lambda i: (i, 0)),
                  pl.BlockSpec((tm, hv), lambda i: (i, r_block)),
                  pl.BlockSpec((1, dv), lambda i: (0, 0)),
                  pl.BlockSpec(wo.shape, lambda i: (0, 0)),
                  pl.BlockSpec((tm, D), lambda i: (i, 0)),
                  pl.BlockSpec((1, gt.shape[1], D), lambda i: (i // tpg, 0, 0))],
        out_specs=pl.BlockSpec((tm, D), lambda i: (i, 0)),
        out_shape=jax.ShapeDtypeStruct((M, D), F32),
        compiler_params=_cparams(("parallel",)),
        name="gla_out",
    )(o, proj, g_o.reshape(1, dv), wo, x, gt)


def _order_key(score):
    score = jnp.where(score == 0.0, 0.0, score)
    bits = pltpu.bitcast(score, jnp.int32)
    return jnp.where(bits < 0, bits ^ jnp.int32(0x7FFFFFFF), bits)


def _kth_largest_key(key, topk, axes):
    shape = tuple(1 if a in axes else s for a, s in enumerate(key.shape))

    def body(it, t):
        cand = t + lax.shift_left(jnp.int32(1), jnp.int32(31) - it)
        cnt = jnp.sum(jnp.where(key >= cand, 1, 0), axis=axes, keepdims=True)
        return jnp.where(cnt >= topk, cand, t)

    return lax.fori_loop(0, 32, body, jnp.full(shape, INT_MIN, jnp.int32))


def _bias_chain(dist, value_of_bucket):
    val = value_of_bucket(0)
    for j in range(1, N_BUCKETS):
        val = jnp.where(dist >= BUCKET_LO[j], value_of_bucket(j), val)
    return val


def _strict_upper(n):
    a = lax.broadcasted_iota(jnp.int32, (n, n), 0)
    b = lax.broadcasted_iota(jnp.int32, (n, n), 1)
    return jnp.where(a < b, 1.0, 0.0).astype(CDT)


def _dsa_prompt_kernel(rb_ref, q_ref, qi_ref, wq_ref, k_ref, v_ref, kk_ref, o_ref, sel_ref, band_ref, *,
                       n_heads, n_kv, hd, n_idx, di, topk, tq, T):
    b_id = pl.program_id(0)
    qt = pl.program_id(1)
    group = n_heads // n_kv
    nkb = T // LANE

    @pl.when((b_id == 0) & (qt == 0))
    def _():
        i = lax.broadcasted_iota(jnp.int32, (tq, LANE), 0)
        j = lax.broadcasted_iota(jnp.int32, (tq, LANE), 1)
        for h in range(n_heads):
            far = rb_ref[N_BUCKETS - 1, h]
            for w in range(tq // LANE + 1):
                dist = i - j + w * LANE
                band_ref[w, h] = _bias_chain(dist, lambda bk: rb_ref[bk, h]) - far

    row_pos = qt * tq + lax.broadcasted_iota(jnp.int32, (tq, T), 0)
    col_pos = lax.broadcasted_iota(jnp.int32, (tq, T), 1)
    causal = col_pos <= row_pos

    kh, kl = _split(kk_ref[:, :di])
    wi = wq_ref[:, di:di + n_idx] * (n_idx ** -0.5)
    score = jnp.zeros((tq, T), F32)
    for h in range(n_idx):
        qh, ql = _split(qi_ref[:, h * di:(h + 1) * di])
        lg = _dot_nt3(qh, ql, kh, kl) * (di ** -0.5)
        score = score + wi[:, h:h + 1] * jnp.maximum(lg, 0.0)
    score = jnp.where(causal, score, NEG_INF)
    key = _order_key(score)
    thr = _kth_largest_key(key, topk, (1,))
    gt = key > thr
    eq = key == thr
    n_gt = jnp.sum(jnp.where(gt, 1, 0), axis=1, keepdims=True)
    n_eq = jnp.sum(jnp.where(eq, 1, 0), axis=1, keepdims=True)
    need = topk - n_gt
    tie = jnp.max(n_eq - need) > 0

    @pl.when(jnp.logical_not(tie))
    def _():
        sel_ref[...] = jnp.where((gt | eq) & causal, 0.0, NEG_INF)

    @pl.when(tie)
    def _():
        upper = _strict_upper(LANE)
        run = jnp.zeros((tq, 1), F32)
        needf = need.astype(F32)
        for kb in range(nkb):
            sl = slice(kb * LANE, (kb + 1) * LANE)
            eqb = jnp.where(eq[:, sl], 1.0, 0.0)
            pre = jnp.dot(eqb.astype(CDT), upper, preferred_element_type=F32) + run
            keep = gt[:, sl] | (eq[:, sl] & (pre < needf))
            sel_ref[:, sl] = jnp.where(keep & causal[:, sl], 0.0, NEG_INF)
            run = run + jnp.sum(eqb, axis=1, keepdims=True)

    scale = hd ** -0.5
    nband = tq // LANE + 1
    for g in range(n_kv):
        kg = k_ref[:, g * hd:(g + 1) * hd].astype(CDT)
        vg = v_ref[:, g * hd:(g + 1) * hd].astype(CDT)
        for r in range(group):
            h = g * group + r
            s = _dot_nt(q_ref[:, h * hd:(h + 1) * hd], kg) * scale + rb_ref[N_BUCKETS - 1, h]
            blocks = []
            for kb in range(nkb):
                w = qt * (tq // LANE) - kb
                add = jnp.zeros((tq, LANE), F32)
                for wv in range(nband):
                    add = jnp.where(w == wv, band_ref[wv, h], add)
                blocks.append(add)
            s = s + jnp.concatenate(blocks, axis=1) + sel_ref[...]
            m = jnp.max(s, axis=-1, keepdims=True)
            p = jnp.exp(s - m)
            l = jnp.sum(p, axis=-1, keepdims=True)
            o_ref[:, h * hd:(h + 1) * hd] = (_dot(p, vg) / l).astype(o_ref.dtype)


def _dsa_prompt(proj, rel_bias, B, T, dims, topk):
    n_heads, n_kv, hd, n_idx, di = dims
    M, Np = proj.shape
    tq = LANE
    nqt = T // tq
    hq, hkv, hi = n_heads * hd, n_kv * hd, n_idx * di
    assert hq % hkv == 0 and (hq + 2 * hkv) % hi == 0 and (hq + 2 * hkv + hi) % LANE == 0
    kk_blk = (hq + 2 * hkv + hi) // LANE
    kern = functools.partial(_dsa_prompt_kernel, n_heads=n_heads, n_kv=n_kv, hd=hd, n_idx=n_idx, di=di, topk=topk,
                             tq=tq, T=T)
    return pl.pallas_call(
        kern,
        grid=(B, nqt),
        in_specs=[pl.BlockSpec(memory_space=pltpu.SMEM),
                  pl.BlockSpec((tq, hq), lambda b, t: (b * nqt + t, 0)),
                  pl.BlockSpec((tq, hi), lambda b, t: (b * nqt + t, (hq + 2 * hkv) // hi)),
                  pl.BlockSpec((tq, LANE), lambda b, t: (b * nqt + t, kk_blk)),
                  pl.BlockSpec((T, hkv), lambda b, t: (b, hq // hkv)),
                  pl.BlockSpec((T, hkv), lambda b, t: (b, hq // hkv + 1)),
                  pl.BlockSpec((T, LANE), lambda b, t: (b, kk_blk))],
        out_specs=pl.BlockSpec((tq, hq), lambda b, t: (b * nqt + t, 0)),
        out_shape=jax.ShapeDtypeStruct((M, hq), CDT),
        scratch_shapes=[pltpu.VMEM((tq, T), F32),
                        pltpu.VMEM((tq // LANE + 1, n_heads, tq, LANE), F32)],
        compiler_params=_cparams(("arbitrary", "arbitrary")),
        name="dsa_prompt",
    )(rel_bias, proj, proj, proj, proj, proj, proj)


def _dsa_select_kernel(pt_ref, qi_ref, wi_ref, kn_ref, *refs, n_pages_step, n_idx, di, topk, t_new):
    G = n_pages_step
    ki_refs = refs[:G]
    mask_ref, sc_ref = refs[G:]
    g = pl.program_id(1)
    NG = pl.num_programs(1)
    R = qi_ref.shape[1] // n_idx
    qh, ql = _split(qi_ref[0])
    wi = wi_ref[0] * (n_idx ** -0.5)

    def head_sum(lg):
        w = wi * jnp.maximum(lg * (di ** -0.5), 0.0)
        sc = w[0:R]
        for h in range(1, n_idx):
            sc = sc + w[h * R:(h + 1) * R]
        return sc

    kh, kl = _split(jnp.concatenate([r[0, 0] for r in ki_refs], axis=1))
    dot = functools.partial(jnp.dot, preferred_element_type=F32)
    past = head_sum(dot(qh, kh) + dot(qh, kl) + dot(ql, kh))
    GW = past.shape[1]
    sc_ref[g] = past

    @pl.when(g == NG - 1)
    def _():
        nh, nl = _split(kn_ref[0][:, :di])
        new = head_sum(_dot_nt3(qh, ql, nh, nl))
        t_row = lax.broadcasted_iota(jnp.int32, new.shape, 0) % t_new
        col = lax.broadcasted_iota(jnp.int32, new.shape, 1)
        new = jnp.where((col <= t_row) & (col < t_new), new, NEG_INF)
        sc_ref[NG] = jnp.concatenate([new, jnp.full((R, GW - LANE), NEG_INF, F32)], axis=1)
        score = sc_ref[...]
        valid = score > NEG_INF
        key = _order_key(score)
        thr = _kth_largest_key(key, topk, (0, 2))
        gt = key > thr
        eq = key == thr
        n_gt = jnp.sum(jnp.where(gt, 1, 0), axis=(0, 2), keepdims=True)
        n_eq = jnp.sum(jnp.where(eq, 1, 0), axis=(0, 2), keepdims=True)
        need = topk - n_gt
        tie = jnp.max(n_eq - need) > 0

        @pl.when(jnp.logical_not(tie))
        def _():
            mask_ref[0] = jnp.where((gt | eq) & valid, 0.0, NEG_INF)

        @pl.when(tie)
        def _():
            upper = _strict_upper(LANE)
            needf = need[0].astype(F32)

            def blk(gi, run):
                k_g = _order_key(sc_ref[gi])
                v_g = sc_ref[gi] > NEG_INF
                for c in range(GW // LANE):
                    sl = slice(c * LANE, (c + 1) * LANE)
                    eqb = jnp.where(k_g[:, sl] == thr[0], 1.0, 0.0)
                    pre = jnp.dot(eqb.astype(CDT), upper, preferred_element_type=F32) + run
                    keep = (k_g[:, sl] > thr[0]) | ((eqb > 0.0) & (pre < needf))
                    mask_ref[0, gi, :, sl] = jnp.where(keep & v_g[:, sl], 0.0, NEG_INF)
                    run = run + jnp.sum(eqb, axis=1, keepdims=True)
                return run

            lax.fori_loop(0, NG + 1, blk, jnp.zeros((R, 1), F32))


def _dsa_select(qi_hm, wi_hm, kn, cache_ki_t, layer, page_table, B, n_idx, di, topk, t_new):
    n_pages = page_table.shape[1]
    page = cache_ki_t.shape[3]
    G = math.gcd(n_pages, PAGES_PER_STEP_SELECT)
    NG = n_pages // G
    GW = G * page
    R = qi_hm.shape[1] // n_idx
    kern = functools.partial(_dsa_select_kernel, n_pages_step=G, n_idx=n_idx, di=di, topk=topk, t_new=t_new)

    def page_map(i):
        return lambda b, g, pt: (layer, pt[b, g * G + i], 0, 0)

    in_specs = [pl.BlockSpec((1, n_idx * R, di), lambda b, g, pt: (b, 0, 0)),
                pl.BlockSpec((1, n_idx * R, 1), lambda b, g, pt: (b, 0, 0)),
                pl.BlockSpec((1, LANE, LANE), lambda b, g, pt: (b, 0, 0))]
    in_specs += [pl.BlockSpec((1, 1, di, page), page_map(i)) for i in range(G)]
    return pl.pallas_call(
        kern,
        grid_spec=pltpu.PrefetchScalarGridSpec(
            num_scalar_prefetch=1,
            grid=(B, NG),
            in_specs=in_specs,
            out_specs=pl.BlockSpec((1, NG + 1, R, GW), lambda b, g, pt: (b, 0, 0, 0)),
            scratch_shapes=[pltpu.VMEM((NG + 1, R, GW), F32)]),
        out_shape=jax.ShapeDtypeStruct((B, NG + 1, R, GW), F32),
        compiler_params=_cparams(("parallel", "arbitrary")),
        name="dsa_select",
    )(page_table, qi_hm, wi_hm, kn, *([cache_ki_t] * G))


def _dsa_decode_kernel(pt_ref, rb_ref, q_ref, kn_ref, vn_ref, mask_ref, mnew_ref, *refs,
                       n_pages_step, n_kv, group, hd, t_new, past_len):
    G = n_pages_step
    k_refs = refs[:G]
    v_refs = refs[G:2 * G]
    o_ref, m_ref, l_ref, acc_ref = refs[2 * G:]
    g = pl.program_id(1)
    NG = pl.num_programs(1)
    R = group * t_new
    scale = hd ** -0.5

    @pl.when(g == 0)
    def _():
        m_ref[...] = jnp.full(m_ref.shape, NEG_INF, F32)
        l_ref[...] = jnp.zeros(l_ref.shape, F32)
        acc_ref[...] = jnp.zeros(acc_ref.shape, F32)

    def head_val(kvh, bucket):
        row = lax.broadcasted_iota(jnp.int32, (R, 1), 0)
        val = jnp.full((R, 1), rb_ref[bucket, kvh * group], F32)
        for r in range(1, group):
            val = jnp.where(row >= r * t_new, rb_ref[bucket, kvh * group + r], val)
        return val

    def update(kvh, s, v):
        m_prev = m_ref[kvh]
        m_new = jnp.maximum(m_prev, jnp.max(s, axis=-1, keepdims=True))
        m_safe = jnp.where(m_new > NEG_INF, m_new, 0.0)
        alpha = jnp.exp(m_prev - m_safe)
        p = jnp.exp(s - m_safe)
        l_ref[kvh] = alpha * l_ref[kvh] + jnp.sum(p, axis=-1, keepdims=True)
        acc_ref[kvh] = alpha * acc_ref[kvh] + _dot(p, v)
        m_ref[kvh] = m_new

    page = k_refs[0].shape[2] // n_kv
    GW = G * page
    mask = mask_ref[0, 0]
    t_row = lax.broadcasted_iota(jnp.int32, (R, GW), 0) % t_new
    col = lax.broadcasted_iota(jnp.int32, (R, GW), 1)
    dist = past_len + t_row - (g * GW + col)
    near = g == NG - 1

    def head_rows(refs_, kvh):
        return jnp.concatenate([r[0, 0, pl.ds(kvh, page, stride=n_kv), :] for r in refs_], axis=0).astype(CDT)

    for kvh in range(n_kv):
        s = _dot_nt(q_ref[0, kvh], head_rows(k_refs, kvh)) * scale
        bias = lax.cond(near,
                        lambda: _bias_chain(dist, functools.partial(head_val, kvh)),
                        lambda: jnp.broadcast_to(head_val(kvh, N_BUCKETS - 1), (R, GW)))
        update(kvh, s + bias + mask, head_rows(v_refs, kvh))

    @pl.when(near)
    def _():
        kn = kn_ref[0].astype(CDT)
        vn = vn_ref[0].astype(CDT)
        mnew = mnew_ref[0, 0][:, :LANE]
        t_r = lax.broadcasted_iota(jnp.int32, (R, LANE), 0) % t_new
        c = lax.broadcasted_iota(jnp.int32, (R, LANE), 1)
        d_new = jnp.maximum(t_r - c, 0)
        for kvh in range(n_kv):
            sl = slice(kvh * hd, (kvh + 1) * hd)
            s = _dot_nt(q_ref[0, kvh], kn[:, sl]) * scale
            bias = _bias_chain(d_new, functools.partial(head_val, kvh))
            update(kvh, s + bias + mnew, vn[:, sl])
            o_ref[0, kvh] = (acc_ref[kvh] / l_ref[kvh]).astype(o_ref.dtype)


def _dsa_decode(q4, kn, vn, mask, rel_bias, cache_k, cache_v, layer, page_table, B, n_kv, group, hd, t_new):
    n_pages = page_table.shape[1]
    page = cache_k.shape[2] // n_kv
    GWs = mask.shape[3]
    G = math.gcd(math.gcd(n_pages, PAGES_PER_STEP_DECODE), GWs // page)
    NG = n_pages // G
    GW = G * page
    ratio = GWs // GW
    R = group * t_new
    kern = functools.partial(_dsa_decode_kernel, n_pages_step=G, n_kv=n_kv, group=group, hd=hd, t_new=t_new,
                             past_len=n_pages * page)

    def page_map(i):
        return lambda b, g, pt: (layer, pt[b, g * G + i], 0, 0)

    in_specs = [pl.BlockSpec(memory_space=pltpu.SMEM),
                pl.BlockSpec((1, n_kv, R, hd), lambda b, g, pt: (b, 0, 0, 0)),
                pl.BlockSpec((1, LANE, n_kv * hd), lambda b, g, pt: (b, 0, 0)),
                pl.BlockSpec((1, LANE, n_kv * hd), lambda b, g, pt: (b, 0, 0)),
                pl.BlockSpec((1, 1, R, GW), lambda b, g, pt: (b, g // ratio, 0, g % ratio)),
                pl.BlockSpec((1, 1, R, GW), lambda b, g, pt: (b, mask.shape[1] - 1, 0, 0))]
    in_specs += [pl.BlockSpec((1, 1, page * n_kv, hd), page_map(i)) for i in range(G)]
    in_specs += [pl.BlockSpec((1, 1, page * n_kv, hd), page_map(i)) for i in range(G)]
    return pl.pallas_call(
        kern,
        grid_spec=pltpu.PrefetchScalarGridSpec(
            num_scalar_prefetch=1,
            grid=(B, NG),
            in_specs=in_specs,
            out_specs=pl.BlockSpec((1, n_kv, R, hd), lambda b, g, pt: (b, 0, 0, 0)),
            scratch_shapes=[pltpu.VMEM((n_kv, R, 1), F32), pltpu.VMEM((n_kv, R, 1), F32),
                            pltpu.VMEM((n_kv, R, hd), F32)]),
        out_shape=jax.ShapeDtypeStruct((B, n_kv, R, hd), CDT),
        compiler_params=_cparams(("parallel", "arbitrary")),
        name="dsa_decode",
    )(page_table, rel_bias, q4, kn, vn, mask, mask, *([cache_k] * G), *([cache_v] * G))


def _ffn_kernel(*refs, seq_tiles, t_seq, conv_w):
    if seq_tiles:
        (x_ref, g_ref, sc_ref, sh_ref, gt_ref, wg_ref, wv_ref, wd_ref, cw_ref, cb_ref,
         y_ref, tail_ref, h_ref, acc_ref, stash_ref) = refs
    else:
        (x_ref, g_ref, sc_ref, sh_ref, gt_ref, wg_ref, wv_ref, wd_ref, cw_ref, cb_ref, p1_ref, p2_ref,
         y_ref, tail_ref, h_ref, acc_ref) = refs
    i = pl.program_id(0)
    j = pl.program_id(1)

    @pl.when(j == 0)
    def _():
        h = _rms(x_ref[...], g_ref[...]) * (1.0 + sc_ref[0]) + sh_ref[0]
        h_ref[...] = h.astype(CDT)
        acc_ref[...] = jnp.zeros(acc_ref.shape, F32)

    gate = jnp.dot(h_ref[...], wg_ref[...], preferred_element_type=F32)
    val = jnp.dot(h_ref[...], wv_ref[...], preferred_element_type=F32)
    tm = gate.shape[0]
    row = lax.broadcasted_iota(jnp.int32, gate.shape, 0)
    g1 = pltpu.roll(gate, 1, 0)
    g2 = pltpu.roll(gate, 2, 0)
    if seq_tiles:
        @pl.when((i == 0) & (j == 0))
        def _():
            stash_ref[...] = jnp.zeros(stash_ref.shape, F32)

        prev = jnp.where(i % seq_tiles == 0, 0.0, stash_ref[j])
        g1 = jnp.where(row == 0, prev[7:8], g1)
        g2 = jnp.where(row == 0, prev[6:7], jnp.where(row == 1, prev[7:8], g2))
        stash_ref[j] = gate[tm - 8:]
        tail_ref[0] = gate[tm - 8:]
    else:
        t = row % t_seq
        g1 = jnp.where(t == 0, p1_ref[...], g1)
        g2 = jnp.where(t < 2, p2_ref[...], g2)
        tail_ref[...] = gate
    cw = cw_ref[...]
    conv = cw[0:1] * g2 + cw[1:2] * g1 + cw[2:3] * gate + cb_ref[...]
    act = (_silu(conv) * val).astype(CDT)
    acc_ref[...] += jnp.dot(act, wd_ref[...], preferred_element_type=F32)

    @pl.when(j == pl.num_programs(1) - 1)
    def _():
        y_ref[...] = x_ref[...] + gt_ref[0] * acc_ref[...]


def _ffn(x, g, sc, sh, gt, w_up, w_down, conv_w, conv_b, rows_per_group, t_seq, prev=None):
    M, D = x.shape
    Fd = w_down.shape[0]
    assert conv_w.shape[0] == 3
    tf = _tile(Fd, FFN_TILE_F)
    nf = Fd // tf
    cw = jnp.pad(conv_w, ((0, 8 - conv_w.shape[0]), (0, 0)))
    cb = conv_b.reshape(1, Fd)
    seq_mode = prev is None
    if seq_mode:
        tm = _tile(t_seq, 512, 8)
        assert t_seq % tm == 0 and tm >= 8
        seq_tiles = t_seq // tm
    else:
        tm = M
        seq_tiles = 0
    nt = M // tm
    tpg = max(rows_per_group // tm, 1)
    mod = lambda m: pl.BlockSpec((1, m.shape[1], D), lambda i, j: (i // tpg, 0, 0))
    in_specs = [pl.BlockSpec((tm, D), lambda i, j: (i, 0)),
                pl.BlockSpec((1, D), lambda i, j: (0, 0)),
                mod(sc), mod(sh), mod(gt),
                pl.BlockSpec((D, tf), lambda i, j: (0, j)),
                pl.BlockSpec((D, tf), lambda i, j: (0, nf + j)),
                pl.BlockSpec((tf, D), lambda i, j: (j, 0)),
                pl.BlockSpec((8, tf), lambda i, j: (0, j)),
                pl.BlockSpec((1, tf), lambda i, j: (0, j))]
    args = [x, g.reshape(1, D), sc, sh, gt, w_up, w_up, w_down, cw, cb]
    scratch = [pltpu.VMEM((tm, D), CDT), pltpu.VMEM((tm, D), F32)]
    if seq_mode:
        tail_shape = jax.ShapeDtypeStruct((nt, 8, Fd), F32)
        tail_spec = pl.BlockSpec((1, 8, tf), lambda i, j: (i, 0, j))
        scratch.append(pltpu.VMEM((nf, 8, tf), F32))
    else:
        p1 = jnp.concatenate([prev[:, 1:2], jnp.zeros_like(prev[:, :1]).repeat(t_seq - 1, axis=1)], axis=1)
        p2 = jnp.concatenate([prev[:, 0:2], jnp.zeros_like(prev[:, :1]).repeat(t_seq - 2, axis=1)], axis=1)
        args += [p1.reshape(M, Fd), p2.reshape(M, Fd)]
        in_specs += [pl.BlockSpec((tm, tf), lambda i, j: (i, j))] * 2
        tail_shape = jax.ShapeDtypeStruct((M, Fd), F32)
        tail_spec = pl.BlockSpec((tm, tf), lambda i, j: (i, j))
    kern = functools.partial(_ffn_kernel, seq_tiles=seq_tiles, t_seq=t_seq, conv_w=conv_w.shape[0])
    return pl.pallas_call(
        kern,
        grid=(nt, nf),
        in_specs=in_specs,
        out_specs=[pl.BlockSpec((tm, D), lambda i, j: (i, 0)), tail_spec],
        out_shape=[jax.ShapeDtypeStruct((M, D), F32), tail_shape],
        scratch_shapes=scratch,
        compiler_params=_cparams(("arbitrary", "arbitrary")),
        name="ffn",
    )(*args)


def _final_norm_kernel(x_ref, g_ref, o_ref):
    o_ref[...] = _rms(x_ref[...], g_ref[...])


def _final_norm(x, g):
    M, D = x.shape
    tm = _tile(M, 1024, 8)
    return pl.pallas_call(
        _final_norm_kernel,
        grid=(M // tm,),
        in_specs=[pl.BlockSpec((tm, D), lambda i: (i, 0)), pl.BlockSpec((1, D), lambda i: (0, 0))],
        out_specs=pl.BlockSpec((tm, D), lambda i: (i, 0)),
        out_shape=jax.ShapeDtypeStruct((M, D), F32),
        compiler_params=_cparams(("parallel",)),
        name="final_norm",
    )(x, g.reshape(1, D))


def _rope_tables(pos, rope, n_heads, reps):
    half = rope // 2
    inv = ROPE_BASE ** (-jnp.arange(half, dtype=F32) / half)
    ang = pos.astype(F32)[:, None] * inv[None, :]
    cos = jnp.cos(ang)
    sin = jnp.sin(ang)
    cosk = jnp.concatenate([cos, cos], axis=1)
    sink = jnp.concatenate([-sin, sin], axis=1)
    tabs = (jnp.tile(cosk, (1, n_heads)), jnp.tile(sink, (1, n_heads)), cosk, sink)
    return tuple(jnp.tile(t, (reps, 1)) for t in tabs)


def _swap_halves(w, rope):
    half = rope // 2
    return jnp.concatenate([w[..., half:], w[..., :half]], axis=-1)


def _pad_cols(w, n):
    return jnp.pad(w, ((0, 0), (0, n - w.shape[1])))


def kernel(x_prompt, x_sample, cache_mla_ckv, cache_mla_kpe, state_gla, cache_dsa_k, cache_dsa_v, cache_dsa_kidx,
           state_ffn_conv, page_table, c_prompt, c_sample, ada_w, ada_b, norm1_g, norm2_g, final_g, mla_w_in,
           mla_g_q, mla_g_kv, mla_w_uq, mla_w_uk, mla_w_uv, mla_w_o, gla_w_in, gla_w_a2, gla_b_a2, gla_g_o,
           gla_w_o, dsa_w_in, dsa_w_o, rel_bias, ffn_w_up, ffn_conv_w, ffn_conv_b, ffn_w_down):
    Bp, Tp, D = x_prompt.shape
    Bs, Ts, _ = x_sample.shape
    depth = ada_w.shape[0]
    n_mod = ada_w.shape[2] // D
    n_pages, page = page_table.shape[1], cache_mla_ckv.shape[2]
    past_len = n_pages * page
    cache_kpe_t = jnp.swapaxes(cache_mla_kpe, 2, 3)
    cache_ki_t = jnp.swapaxes(cache_dsa_kidx, 2, 3)

    q_lora, mla_h, qk_dim = mla_w_uq.shape[1:]
    kv_lora, _, nope = mla_w_uk.shape[1:]
    mla_v = mla_w_uv.shape[3]
    rope = qk_dim - nope
    gla_h = state_gla.shape[2]
    gla_dk, gla_dv = state_gla.shape[3:]
    gla_rank = gla_w_a2.shape[1]
    dsa_kv, dsa_hd = cache_dsa_k.shape[3:]
    dsa_di = cache_dsa_kidx.shape[3]
    dsa_h = dsa_w_o.shape[1] // dsa_hd
    dsa_hi = (dsa_w_in.shape[2] - (dsa_h + 2 * dsa_kv) * dsa_hd - dsa_di) // (dsa_di + 1)
    assert kv_lora % LANE == 0 and q_lora % LANE == 0 and rope <= LANE

    mod_all = _ada_mod(jnp.concatenate([c_prompt, c_sample], axis=0), ada_w, ada_b)

    def mods(l, sample):
        m = mod_all[l, Bp:] if sample else mod_all[l, :Bp]
        parts = [m[:, i * D:(i + 1) * D] for i in range(n_mod)]
        if sample:
            return [jnp.repeat(p, Ts, axis=0).reshape(1, Bs * Ts, D) for p in parts]
        return [p.reshape(Bp, 1, D) for p in parts]

    mla_w = []
    for j in range(mla_w_in.shape[0]):
        w_in = mla_w_in[j]
        kcol = q_lora + kv_lora
        w_in_ext = jnp.concatenate([w_in, _swap_halves(w_in[:, kcol:kcol + rope], rope)], axis=1)
        w_in_ext = _pad_cols(w_in_ext, -(-w_in_ext.shape[1] // LANE) * LANE).astype(CDT)
        uq = mla_w_uq[j]
        wqn = uq[:, :, :nope].reshape(q_lora, mla_h * nope).astype(CDT)
        wqp = uq[:, :, nope:].reshape(q_lora, mla_h * rope).astype(CDT)
        wqs = _swap_halves(uq[:, :, nope:], rope).reshape(q_lora, mla_h * rope).astype(CDT)
        wukT = jnp.transpose(mla_w_uk[j], (1, 2, 0)).astype(CDT)
        wuv = jnp.transpose(mla_w_uv[j], (1, 0, 2)).astype(CDT)
        mla_w.append((w_in_ext, wqn, wqp, wqs, wukT, wuv, mla_w_o[j].astype(CDT)))
    gla_np = -(-gla_w_in.shape[2] // LANE) * LANE
    gla_w = []
    for j in range(gla_w_in.shape[0]):
        w_a2p = jnp.pad(gla_w_a2[j], ((0, LANE - gla_rank), (0, 0))).astype(CDT)
        gla_w.append((_pad_cols(gla_w_in[j], gla_np).astype(CDT), w_a2p, gla_w_o[j].astype(CDT)))
    dsa_np = -(-dsa_w_in.shape[2] // LANE) * LANE
    dsa_w = [(_pad_cols(dsa_w_in[j], dsa_np).astype(CDT), dsa_w_o[j].astype(CDT)) for j in range(dsa_w_in.shape[0])]
    ffn_up = ffn_w_up.astype(CDT)
    ffn_down = ffn_w_down.astype(CDT)

    mla_dims = (mla_h, q_lora, kv_lora, rope, nope)
    mla_scale = qk_dim ** -0.5

    def trunk(x3, sample):
        B, T, _ = x3.shape
        M = B * T
        x = x3.reshape(M, D)
        rpg = M if sample else T
        pos = (past_len if sample else 0) + jnp.arange(T, dtype=jnp.int32)
        tabs = _rope_tables(pos, rope, mla_h, B if sample else 1)
        outs = dict(mla_ckv=[], mla_kpe=[], gla=[], dsa_k=[], dsa_v=[], dsa_ki=[], conv=[])
        for l in range(depth):
            sh1, sc1, gt1, sh2, sc2, gt2 = mods(l, sample)
            j = l // 3
            if l % 3 == 0:
                w_in_ext, wqn, wqp, wqs, wukT, wuv, wo = mla_w[j]
                p = _nm_linear(x, norm1_g[l], sc1, sh1, w_in_ext, rpg)
                ckv, kpe, kcat, qcat = _mla_prep(p, mla_g_q[j], mla_g_kv[j], tabs, wqn, wqp, wqs, wukT, mla_dims)
                if sample:
                    o_lat = _mla_decode(qcat, kcat, cache_mla_ckv, cache_kpe_t, j, page_table, B, T, mla_h,
                                        kv_lora, rope, mla_scale)
                else:
                    o_lat = _mla_flash(qcat, kcat, B, T, mla_h, kv_lora, mla_scale)
                x = _mla_out(o_lat, wuv, wo, x, gt1, rpg, mla_h, kv_lora)
                outs["mla_ckv"].append(ckv.reshape(B, T, kv_lora))
                outs["mla_kpe"].append(kpe.reshape(B, T, rope))
            elif l % 3 == 1:
                w_in_p, w_a2p, wo = gla_w[j]
                hk, hv = gla_h * gla_dk, gla_h * gla_dv
                p = _nm_linear(x, norm1_g[l], sc1, sh1, w_in_p, rpg, tn_target=640)
                log_a = _gla_gate(p, (2 * hk + 2 * hv) // LANE, w_a2p, gla_b_a2[j])
                s0 = state_gla[j] if sample else jnp.zeros((B, gla_h, gla_dk, gla_dv), F32)
                o, s_fin = _gla_recurrence(p, log_a, s0, B, T, gla_h, gla_dk, gla_dv)
                x = _gla_out(o, p, (2 * hk + hv) // hv, gla_g_o[j], wo, x, gt1, rpg, gla_h, gla_dv)
                outs["gla"].append(s_fin)
            else:
                w_in_p, wo = dsa_w[j]
                hq, hkv, hi = dsa_h * dsa_hd, dsa_kv * dsa_hd, dsa_hi * dsa_di
                p = _nm_linear(x, norm1_g[l], sc1, sh1, w_in_p, rpg, tn_target=896)
                k_new = p[:, hq:hq + hkv]
                v_new = p[:, hq + hkv:hq + 2 * hkv]
                kk = p[:, hq + 2 * hkv + hi:]
                L_keys = (past_len if sample else 0) + T
                topk = min(DSA_TOPK, L_keys // 4)
                if sample:
                    group = dsa_h // dsa_kv
                    dup = lambda a: jnp.concatenate([a.reshape(B, T, dsa_hi, -1)] * group, axis=1)
                    qi_hm = dup(p[:, hq + 2 * hkv:hq + 2 * hkv + hi]).transpose(0, 2, 1, 3)
                    qi_hm = qi_hm.reshape(B, dsa_hi * group * T, dsa_di)
                    wi_hm = dup(kk[:, dsa_di:dsa_di + dsa_hi]).transpose(0, 2, 1, 3).reshape(B, dsa_hi * group * T, 1)
                    padn = lambda a: jnp.pad(a.reshape(B, T, -1), ((0, 0), (0, LANE - T), (0, 0)))
                    mask = _dsa_select(qi_hm, wi_hm, padn(kk), cache_ki_t, j, page_table, B, dsa_hi, dsa_di, topk, T)
                    q4 = p[:, :hq].reshape(B, T, dsa_kv, group, dsa_hd).transpose(0, 2, 3, 1, 4)
                    q4 = q4.reshape(B, dsa_kv, group * T, dsa_hd)
                    ck = cache_dsa_k.reshape(*cache_dsa_k.shape[:2], page * dsa_kv, dsa_hd)
                    cv = cache_dsa_v.reshape(*cache_dsa_v.shape[:2], page * dsa_kv, dsa_hd)
                    o4 = _dsa_decode(q4, padn(k_new), padn(v_new), mask, rel_bias, ck, cv, j, page_table, B,
                                     dsa_kv, group, dsa_hd, T)
                    o = o4.reshape(B, dsa_kv, group, T, dsa_hd).transpose(0, 3, 1, 2, 4).reshape(M, hq)
                else:
                    o = _dsa_prompt(p, rel_bias, B, T, (dsa_h, dsa_kv, dsa_hd, dsa_hi, dsa_di), topk)
                x = _proj_res(o, wo, x, gt1, rpg)
                outs["dsa_k"].append(k_new.reshape(B, T, dsa_kv, dsa_hd))
                outs["dsa_v"].append(v_new.reshape(B, T, dsa_kv, dsa_hd))
                outs["dsa_ki"].append(kk[:, :dsa_di].reshape(B, T, dsa_di))
            if sample:
                x, tail = _ffn(x, norm2_g[l], sc2, sh2, gt2, ffn_up[l], ffn_down[l], ffn_conv_w[l], ffn_conv_b[l],
                               rpg, T, prev=state_ffn_conv[l])
                outs["conv"].append(tail.reshape(B, T, -1)[:, T - 2:])
            else:
                x, tail = _ffn(x, norm2_g[l], sc2, sh2, gt2, ffn_up[l], ffn_down[l], ffn_conv_w[l], ffn_conv_b[l],
                               rpg, T)
                nt = tail.shape[0] // B
                outs["conv"].append(tail.reshape(B, nt, 8, -1)[:, nt - 1, 6:8])
        y = _final_norm(x, final_g).reshape(B, T, D)
        return (y, jnp.stack(outs["mla_ckv"]), jnp.stack(outs["mla_kpe"]), jnp.stack(outs["gla"]),
                jnp.stack(outs["dsa_k"]), jnp.stack(outs["dsa_v"]), jnp.stack(outs["dsa_ki"]),
                jnp.stack(outs["conv"]))

    rp = trunk(x_prompt, False)
    rs = trunk(x_sample, True)
    return (rp[0], rs[0]) + tuple(rp[1:]) + tuple(rs[1:])
```

```python
import functools
import math

import numpy as np
import jax
import jax.numpy as jnp
from jax import lax
from jax.experimental import pallas as pl
from jax.experimental.pallas import tpu as pltpu

F32 = jnp.float32
CDT = jnp.bfloat16
EPS = 1e-6
ROPE_BASE = 10000.0
GLA_TAU = 16.0
N_BUCKETS = 32
MAX_DISTANCE = 128
DSA_TOPK = 256
LANE = 128
NEG_INF = float("-inf")
INT_MIN = -2 ** 31
VMEM_LIMIT = 56 * 1024 * 1024
PAGES_PER_STEP_MLA = 32
PAGES_PER_STEP_SELECT = 32
PAGES_PER_STEP_DECODE = 16
FFN_TILE_F = 1408
DSA_EXTENT_CLASSES = 4
FLASH_TQ = 512
FLASH_TK = 512


def _bucket_thresholds():
    d = np.arange(0, 4 * MAX_DISTANCE)
    exact = N_BUCKETS // 2
    lr = np.log(np.maximum(d, 1).astype(np.float32) / np.float32(exact)) / np.float32(math.log(MAX_DISTANCE / exact))
    large = np.minimum(exact + (lr * np.float32(N_BUCKETS - exact)).astype(np.int32), N_BUCKETS - 1)
    b = np.where(d < exact, d, large)
    return [int(np.argmax(b >= j)) for j in range(N_BUCKETS)]


BUCKET_LO = _bucket_thresholds()


def _cparams(sem, vmem=VMEM_LIMIT):
    return pltpu.CompilerParams(dimension_semantics=sem, vmem_limit_bytes=vmem)


def _tile(n, target, mult=LANE):
    if n <= target:
        return n
    t = (target // mult) * mult
    while t > mult and n % t:
        t -= mult
    assert n % t == 0, (n, target)
    return t


def _dot(a, b):
    return jnp.dot(a.astype(CDT), b.astype(CDT), preferred_element_type=F32)


def _dot_nt(a, b):
    return lax.dot_general(a.astype(CDT), b.astype(CDT), (((1,), (1,)), ((), ())), preferred_element_type=F32)


def _dot_tn(a, b):
    return lax.dot_general(a.astype(CDT), b.astype(CDT), (((0,), (0,)), ((), ())), preferred_element_type=F32)


def _rms(x, g):
    return x * lax.rsqrt(jnp.mean(x * x, axis=-1, keepdims=True) + EPS) * g


def _silu(x):
    return x * jax.nn.sigmoid(x)


def _ada_kernel(c_ref, w_ref, b_ref, o_ref):
    ca = _silu(c_ref[...])
    o_ref[0] = _dot(ca, w_ref[0]) + b_ref[0]


def _ada_mod(c, ada_w, ada_b):
    L, D, N = ada_w.shape
    R = c.shape[0]
    tn = _tile(N, 1536)
    return pl.pallas_call(
        _ada_kernel,
        grid=(L, N // tn),
        in_specs=[pl.BlockSpec((R, D), lambda l, j: (0, 0)),
                  pl.BlockSpec((1, D, tn), lambda l, j: (l, 0, j)),
                  pl.BlockSpec((1, 1, tn), lambda l, j: (l, 0, j))],
        out_specs=pl.BlockSpec((1, R, tn), lambda l, j: (l, 0, j)),
        out_shape=jax.ShapeDtypeStruct((L, R, N), F32),
        compiler_params=_cparams(("parallel", "parallel")),
        name="ada_mod",
    )(c, ada_w, ada_b.reshape(L, 1, N))


def _nm_linear_kernel(x_ref, g_ref, sc_ref, sh_ref, w_ref, o_ref, h_ref):
    @pl.when(pl.program_id(1) == 0)
    def _():
        h = _rms(x_ref[...], g_ref[...]) * (1.0 + sc_ref[0]) + sh_ref[0]
        h_ref[...] = h.astype(CDT)

    o_ref[...] = jnp.dot(h_ref[...], w_ref[...], preferred_element_type=F32)


def _mod_spec(mod, tm, rows_per_group):
    G, R, D = mod.shape
    tpg = max(rows_per_group // tm, 1)
    return pl.BlockSpec((1, R, D), lambda i, j: (i // tpg, 0, 0))


def _nm_linear(x, g, sc, sh, w, rows_per_group, tm_target=512, tn_target=1024):
    M, D = x.shape
    N = w.shape[1]
    tm = _tile(M, tm_target, 8)
    tn = _tile(N, tn_target)
    return pl.pallas_call(
        _nm_linear_kernel,
        grid=(M // tm, N // tn),
        in_specs=[pl.BlockSpec((tm, D), lambda i, j: (i, 0)),
                  pl.BlockSpec((1, D), lambda i, j: (0, 0)),
                  _mod_spec(sc, tm, rows_per_group),
                  _mod_spec(sh, tm, rows_per_group),
                  pl.BlockSpec((D, tn), lambda i, j: (0, j))],
        out_specs=pl.BlockSpec((tm, tn), lambda i, j: (i, j)),
        out_shape=jax.ShapeDtypeStruct((M, N), F32),
        scratch_shapes=[pltpu.VMEM((tm, D), CDT)],
        compiler_params=_cparams(("parallel", "arbitrary")),
        name="nm_linear",
    )(x, g.reshape(1, D), sc, sh, w)


def _mla_prep_kernel(p_ref, gq_ref, gkv_ref, cq_ref, sq_ref, ck_ref, sk_ref, wqn_ref, wqp_ref, wqs_ref, wuk_ref,
                     ckv_ref, kpe_ref, kcat_ref, qcat_ref, *, n_heads, q_lora, kv_lora, rope, nope):
    p = p_ref[...]
    cq = p[:, :q_lora]
    ckv = p[:, q_lora:q_lora + kv_lora]
    kpe = p[:, q_lora + kv_lora:q_lora + kv_lora + rope]
    kpe_sw = p[:, q_lora + kv_lora + rope:q_lora + kv_lora + 2 * rope]
    cqn = _rms(cq, gq_ref[...]).astype(CDT)
    q_nope = jnp.dot(cqn, wqn_ref[...], preferred_element_type=F32)
    q_pe = (jnp.dot(cqn, wqp_ref[...], preferred_element_type=F32) * cq_ref[...]
            + jnp.dot(cqn, wqs_ref[...], preferred_element_type=F32) * sq_ref[...])
    ckv_n = _rms(ckv, gkv_ref[...])
    kpe_r = kpe * ck_ref[...] + kpe_sw * sk_ref[...]
    ckv_ref[...] = ckv_n
    kpe_ref[...] = kpe_r
    tm = p.shape[0]
    hd = kv_lora + LANE
    pad = jnp.zeros((tm, LANE - rope), CDT)
    kcat_ref[...] = jnp.concatenate([ckv_n.astype(CDT), kpe_r.astype(CDT), pad], axis=1)
    for h in range(n_heads):
        q_lat = _dot(q_nope[:, h * nope:(h + 1) * nope], wuk_ref[h])
        qcat_ref[:, h * hd:(h + 1) * hd] = jnp.concatenate(
            [q_lat.astype(CDT), q_pe[:, h * rope:(h + 1) * rope].astype(CDT), pad], axis=1)


def _mla_prep(p, g_q, g_kv, tabs, wqn, wqp, wqs, wukT, dims):
    n_heads, q_lora, kv_lora, rope, nope = dims
    M, NP = p.shape
    cosq, sinq, cosk, sink = tabs
    tm = _tile(M, 256, 8)
    ntab = cosq.shape[0] // tm
    hd = kv_lora + LANE
    row = lambda i: (i, 0)
    tab = lambda i: (i % ntab, 0)
    full2 = lambda i: (0, 0)
    full3 = lambda i: (0, 0, 0)
    kern = functools.partial(_mla_prep_kernel, n_heads=n_heads, q_lora=q_lora, kv_lora=kv_lora, rope=rope, nope=nope)
    return pl.pallas_call(
        kern,
        grid=(M // tm,),
        in_specs=[pl.BlockSpec((tm, NP), row),
                  pl.BlockSpec((1, q_lora), full2),
                  pl.BlockSpec((1, kv_lora), full2),
                  pl.BlockSpec((tm, n_heads * rope), tab),
                  pl.BlockSpec((tm, n_heads * rope), tab),
                  pl.BlockSpec((tm, rope), tab),
                  pl.BlockSpec((tm, rope), tab),
                  pl.BlockSpec(wqn.shape, full2),
                  pl.BlockSpec(wqp.shape, full2),
                  pl.BlockSpec(wqs.shape, full2),
                  pl.BlockSpec(wukT.shape, full3)],
        out_specs=[pl.BlockSpec((tm, kv_lora), row),
                   pl.BlockSpec((tm, rope), row),
                   pl.BlockSpec((tm, hd), row),
                   pl.BlockSpec((tm, n_heads * hd), row)],
        out_shape=[jax.ShapeDtypeStruct((M, kv_lora), F32),
                   jax.ShapeDtypeStruct((M, rope), F32),
                   jax.ShapeDtypeStruct((M, hd), CDT),
                   jax.ShapeDtypeStruct((M, n_heads * hd), CDT)],
        compiler_params=_cparams(("parallel",)),
        name="mla_prep",
    )(p, g_q.reshape(1, -1), g_kv.reshape(1, -1), cosq, sinq, cosk, sink, wqn, wqp, wqs, wukT)


def _mla_flash_kernel(q_ref, k_ref, o_ref, m_ref, l_ref, acc_ref, *, n_heads, tq, tk, hd, dv, scale):
    qi = pl.program_id(1)
    ki = pl.program_id(2)

    @pl.when(ki == 0)
    def _():
        m_ref[...] = jnp.full(m_ref.shape, NEG_INF, F32)
        l_ref[...] = jnp.zeros(l_ref.shape, F32)
        acc_ref[...] = jnp.zeros(acc_ref.shape, F32)

    c = scale * math.log2(math.e)

    def block(masked):
        k = k_ref[...]
        v = k[:, :dv]
        if masked:
            row = qi * tq + lax.broadcasted_iota(jnp.int32, (tq, tk), 0)
            col = ki * tk + lax.broadcasted_iota(jnp.int32, (tq, tk), 1)
            mask = col <= row
        for h in range(n_heads):
            s = _dot_nt(q_ref[:, h * hd:(h + 1) * hd], k)
            if masked:
                s = jnp.where(mask, s, NEG_INF)
            m_prev = m_ref[h]
            m_new = jnp.maximum(m_prev, jnp.max(s, axis=-1, keepdims=True))
            alpha = jnp.exp2(c * (m_prev - m_new))
            p = jnp.exp2(c * (s - m_new))
            l_ref[h] = alpha * l_ref[h] + jnp.sum(p, axis=-1, keepdims=True)
            acc_ref[h] = alpha * acc_ref[h] + _dot(p, v)
            m_ref[h] = m_new

    below_diag = ki * tk + tk - 1 <= qi * tq
    pl.when(below_diag)(functools.partial(block, False))
    pl.when(jnp.logical_not(below_diag) & (ki * tk <= qi * tq + tq - 1))(functools.partial(block, True))

    @pl.when(ki == pl.num_programs(2) - 1)
    def _():
        for h in range(n_heads):
            o_ref[:, h * dv:(h + 1) * dv] = (acc_ref[h] / l_ref[h]).astype(o_ref.dtype)


def _mla_flash(qcat, kcat, B, T, n_heads, dv, scale):
    M, hd = kcat.shape
    tq = _tile(T, FLASH_TQ, 8)
    tk = _tile(T, FLASH_TK, 8)
    nq, nk = T // tq, T // tk
    kern = functools.partial(_mla_flash_kernel, n_heads=n_heads, tq=tq, tk=tk, hd=hd, dv=dv, scale=scale)

    def kmap(b, qi, ki):
        return (b * nk + jnp.minimum(ki, (qi * tq + tq - 1) // tk), 0)

    return pl.pallas_call(
        kern,
        grid=(B, nq, nk),
        in_specs=[pl.BlockSpec((tq, n_heads * hd), lambda b, qi, ki: (b * nq + qi, 0)),
                  pl.BlockSpec((tk, hd), kmap)],
        out_specs=pl.BlockSpec((tq, n_heads * dv), lambda b, qi, ki: (b * nq + qi, 0)),
        out_shape=jax.ShapeDtypeStruct((M, n_heads * dv), CDT),
        scratch_shapes=[pltpu.VMEM((n_heads, tq, 1), F32),
                        pltpu.VMEM((n_heads, tq, 1), F32),
                        pltpu.VMEM((n_heads, tq, dv), F32)],
        compiler_params=_cparams(("parallel", "parallel", "arbitrary")),
        name="mla_flash",
    )(qcat, kcat)


def _mla_decode_kernel(pt_ref, q_ref, kn_ref, *refs, n_pages_step, n_heads, t_new, dv, rope, scale):
    G = n_pages_step
    ckv_refs = refs[:G]
    kpe_refs = refs[G:2 * G]
    o_ref, m_ref, l_ref, acc_ref = refs[2 * G:]
    g = pl.program_id(1)

    @pl.when(g == 0)
    def _():
        m_ref[...] = jnp.full(m_ref.shape, NEG_INF, F32)
        l_ref[...] = jnp.zeros(l_ref.shape, F32)
        acc_ref[...] = jnp.zeros(acc_ref.shape, F32)

    q = q_ref[0]

    def update(s, v):
        m_prev = m_ref[...]
        m_new = jnp.maximum(m_prev, jnp.max(s, axis=-1, keepdims=True))
        alpha = jnp.exp(m_prev - m_new)
        p = jnp.exp(s - m_new)
        l_ref[...] = alpha * l_ref[...] + jnp.sum(p, axis=-1, keepdims=True)
        acc_ref[...] = alpha * acc_ref[...] + _dot(p, v)
        m_ref[...] = m_new

    ckv = jnp.concatenate([r[0, 0] for r in ckv_refs], axis=0).astype(CDT)
    kpe_t = jnp.concatenate([r[0, 0] for r in kpe_refs], axis=1).astype(CDT)
    s = (_dot_nt(q[:, :dv], ckv) + _dot(q[:, dv:dv + rope], kpe_t)) * scale
    update(s, ckv)

    @pl.when(g == pl.num_programs(1) - 1)
    def _():
        kn = kn_ref[0]
        s2 = _dot_nt(q, kn) * scale
        r, c = s2.shape
        t_row = lax.broadcasted_iota(jnp.int32, (r, c), 0) // n_heads
        col = lax.broadcasted_iota(jnp.int32, (r, c), 1)
        s2 = jnp.where((col <= t_row) & (col < t_new), s2, NEG_INF)
        update(s2, kn[:, :dv])
        o_ref[0] = (acc_ref[...] / l_ref[...]).astype(o_ref.dtype)


def _mla_decode(qcat, kcat, cache_ckv, cache_kpe_t, layer, page_table, B, Ts, n_heads, dv, rope, scale):
    hd = kcat.shape[1]
    n_pages = page_table.shape[1]
    page = cache_ckv.shape[2]
    G = math.gcd(n_pages, PAGES_PER_STEP_MLA)
    NG = n_pages // G
    R = Ts * n_heads
    q3 = qcat.reshape(B, R, hd)
    npad = 16
    kn = jnp.pad(kcat.reshape(B, Ts, hd), ((0, 0), (0, npad - Ts), (0, 0)))
    kern = functools.partial(_mla_decode_kernel, n_pages_step=G, n_heads=n_heads, t_new=Ts, dv=dv, rope=rope,
                             scale=scale)

    def page_map(i):
        return lambda b, g, pt: (layer, pt[b, g * G + i], 0, 0)

    in_specs = [pl.BlockSpec((1, R, hd), lambda b, g, pt: (b, 0, 0)),
                pl.BlockSpec((1, npad, hd), lambda b, g, pt: (b, 0, 0))]
    in_specs += [pl.BlockSpec((1, 1, page, dv), page_map(i)) for i in range(G)]
    in_specs += [pl.BlockSpec((1, 1, rope, page), page_map(i)) for i in range(G)]
    out = pl.pallas_call(
        kern,
        grid_spec=pltpu.PrefetchScalarGridSpec(
            num_scalar_prefetch=1,
            grid=(B, NG),
            in_specs=in_specs,
            out_specs=pl.BlockSpec((1, R, dv), lambda b, g, pt: (b, 0, 0)),
            scratch_shapes=[pltpu.VMEM((R, 1), F32), pltpu.VMEM((R, 1), F32), pltpu.VMEM((R, dv), F32)]),
        out_shape=jax.ShapeDtypeStruct((B, R, dv), CDT),
        compiler_params=_cparams(("parallel", "arbitrary")),
        name="mla_decode",
    )(page_table, q3, kn, *([cache_ckv] * G), *([cache_kpe_t] * G))
    return out.reshape(B * Ts, n_heads * dv)


def _mla_out_kernel(o_ref, wuv_ref, wo_ref, x_ref, gt_ref, y_ref, *, n_heads, dv):
    parts = [_dot(o_ref[:, h * dv:(h + 1) * dv], wuv_ref[h]).astype(CDT) for h in range(n_heads)]
    o = jnp.concatenate(parts, axis=1)
    y = jnp.dot(o, wo_ref[...], preferred_element_type=F32)
    y_ref[...] = x_ref[...] + gt_ref[0] * y


def _mla_out(o_lat, wuv, wo, x, gt, rows_per_group, n_heads, dv):
    M, D = x.shape
    tm = _tile(M, 512, 8)
    tpg = max(rows_per_group // tm, 1)
    kern = functools.partial(_mla_out_kernel, n_heads=n_heads, dv=dv)
    return pl.pallas_call(
        kern,
        grid=(M // tm,),
        in_specs=[pl.BlockSpec((tm, n_heads * dv), lambda i: (i, 0)),
                  pl.BlockSpec(wuv.shape, lambda i: (0, 0, 0)),
                  pl.BlockSpec(wo.shape, lambda i: (0, 0)),
                  pl.BlockSpec((tm, D), lambda i: (i, 0)),
                  pl.BlockSpec((1, gt.shape[1], D), lambda i: (i // tpg, 0, 0))],
        out_specs=pl.BlockSpec((tm, D), lambda i: (i, 0)),
        out_shape=jax.ShapeDtypeStruct((M, D), F32),
        compiler_params=_cparams(("parallel",)),
        name="mla_out",
    )(o_lat, wuv, wo, x, gt)


def _proj_res_kernel(o_ref, wo_ref, x_ref, gt_ref, y_ref):
    y = jnp.dot(o_ref[...], wo_ref[...], preferred_element_type=F32)
    y_ref[...] = x_ref[...] + gt_ref[0] * y


def _proj_res(o, wo, x, gt, rows_per_group):
    M, D = x.shape
    K = o.shape[1]
    tm = _tile(M, 512, 8)
    tpg = max(rows_per_group // tm, 1)
    return pl.pallas_call(
        _proj_res_kernel,
        grid=(M // tm,),
        in_specs=[pl.BlockSpec((tm, K), lambda i: (i, 0)),
                  pl.BlockSpec(wo.shape, lambda i: (0, 0)),
                  pl.BlockSpec((tm, D), lambda i: (i, 0)),
                  pl.BlockSpec((1, gt.shape[1], D), lambda i: (i // tpg, 0, 0))],
        out_specs=pl.BlockSpec((tm, D), lambda i: (i, 0)),
        out_shape=jax.ShapeDtypeStruct((M, D), F32),
        compiler_params=_cparams(("parallel",)),
        name="proj_res",
    )(o, wo, x, gt)


def _gla_gate_kernel(a_ref, w_ref, b_ref, o_ref):
    z = _dot(a_ref[...], w_ref[...]) + b_ref[...]
    o_ref[...] = (jnp.minimum(z, 0.0) - jnp.log(1.0 + jnp.exp(-jnp.abs(z)))) / GLA_TAU


def _gla_gate(proj, col_block, w_a2p, b_a2):
    M = proj.shape[0]
    N = w_a2p.shape[1]
    tm = _tile(M, 1024, 8)
    return pl.pallas_call(
        _gla_gate_kernel,
        grid=(M // tm,),
        in_specs=[pl.BlockSpec((tm, LANE), lambda i: (i, col_block)),
                  pl.BlockSpec(w_a2p.shape, lambda i: (0, 0)),
                  pl.BlockSpec((1, N), lambda i: (0, 0))],
        out_specs=pl.BlockSpec((tm, N), lambda i: (i, 0)),
        out_shape=jax.ShapeDtypeStruct((M, N), F32),
        compiler_params=_cparams(("parallel",)),
        name="gla_gate",
    )(proj, w_a2p, b_a2.reshape(1, N))


def _cumsum_rows(x):
    C = x.shape[0]
    row = lax.broadcasted_iota(jnp.int32, x.shape, 0)
    if C <= 8:
        out = jnp.zeros_like(x)
        for s in range(C):
            out = out + jnp.where(row >= s, x[s:s + 1], 0.0)
        return out
    sh = 1
    while sh < C:
        x = x + jnp.where(row >= sh, pltpu.roll(x, sh, 0), 0.0)
        sh *= 2
    return x


def _gla_kernel(q_ref, k_ref, v_ref, la_ref, s0_ref, o_ref, sf_ref, st_ref, *, n_heads, dk, dv, sub, qscale):
    c = pl.program_id(1)

    @pl.when(c == 0)
    def _():
        for h in range(n_heads):
            st_ref[h] = s0_ref[0, h].T

    C = q_ref.shape[1]
    nsub = C // sub
    for h in range(n_heads):
        q = q_ref[0, :, h * dk:(h + 1) * dk] * qscale
        k = k_ref[0, :, h * dk:(h + 1) * dk]
        v = v_ref[0, :, h * dv:(h + 1) * dv]
        b = _cumsum_rows(la_ref[0, :, h * dk:(h + 1) * dk])
        st = st_ref[h]
        o_inter = _dot_nt(q * jnp.exp(b), st)
        b_last = b[C - 1:C]
        k_dec = k * jnp.exp(b_last - b)
        st_ref[h] = jnp.exp(b_last) * st + _dot_tn(v, k_dec)
        outs = []
        for i in range(nsub):
            r0 = i * sub
            b_i = b[r0:r0 + sub]
            q_i = q[r0:r0 + sub]
            k_i = k[r0:r0 + sub]
            v_i = v[r0:r0 + sub]
            o_i = o_inter[r0:r0 + sub]
            if i > 0:
                ref_row = b[r0:r0 + 1]
                att = _dot_nt(q_i * jnp.exp(b_i - ref_row), k[:r0] * jnp.exp(ref_row - b[:r0]))
                o_i = o_i + _dot(att, v[:r0])
            t_loc = lax.broadcasted_iota(jnp.int32, (sub, 1), 0)
            for s in range(sub):
                w = jnp.exp(jnp.minimum(b_i - b_i[s:s + 1], 0.0))
                col = jnp.sum(q_i * w * k_i[s:s + 1], axis=-1, keepdims=True)
                o_i = o_i + jnp.where(t_loc >= s, col, 0.0) * v_i[s:s + 1]
            outs.append(o_i)
        o_ref[0, :, h * dv:(h + 1) * dv] = outs[0] if nsub == 1 else jnp.concatenate(outs, axis=0)

    @pl.when(c == pl.num_programs(1) - 1)
    def _():
        for h in range(n_heads):
            sf_ref[0, h] = st_ref[h].T


def _gla_recurrence(proj, log_a, s0, B, T, n_heads, dk, dv):
    C = 64 if T % 64 == 0 else T
    sub = min(16, C)
    nc = T // C
    Np = proj.shape[1]
    p3 = proj.reshape(B * nc, C, Np)
    la3 = log_a.reshape(B * nc, C, n_heads * dk)
    hk, hv = n_heads * dk, n_heads * dv
    assert hv % hk == 0
    kern = functools.partial(_gla_kernel, n_heads=n_heads, dk=dk, dv=dv, sub=sub, qscale=dk ** -0.5)
    o, sf = pl.pallas_call(
        kern,
        grid=(B, nc),
        in_specs=[pl.BlockSpec((1, C, hk), lambda b, c: (b * nc + c, 0, 0)),
                  pl.BlockSpec((1, C, hk), lambda b, c: (b * nc + c, 0, 1)),
                  pl.BlockSpec((1, C, hv), lambda b, c: (b * nc + c, 0, 2 * hk // hv)),
                  pl.BlockSpec((1, C, hk), lambda b, c: (b * nc + c, 0, 0)),
                  pl.BlockSpec((1, n_heads, dk, dv), lambda b, c: (b, 0, 0, 0))],
        out_specs=[pl.BlockSpec((1, C, hv), lambda b, c: (b * nc + c, 0, 0)),
                   pl.BlockSpec((1, n_heads, dk, dv), lambda b, c: (b, 0, 0, 0))],
        out_shape=[jax.ShapeDtypeStruct((B * nc, C, hv), F32),
                   jax.ShapeDtypeStruct((B, n_heads, dk, dv), F32)],
        scratch_shapes=[pltpu.VMEM((n_heads, dv, dk), F32)],
        compiler_params=_cparams(("parallel", "arbitrary")),
        name="gla_recurrence",
    )(p3, p3, p3, la3, s0)
    return o.reshape(B * T, hv), sf


def _gla_out_kernel(o_ref, r_ref, g_ref, wo_ref, x_ref, gt_ref, y_ref, *, n_heads, dv):
    parts = []
    for h in range(n_heads):
        sl = slice(h * dv, (h + 1) * dv)
        parts.append((_rms(o_ref[:, sl], g_ref[...]) * _silu(r_ref[:, sl])).astype(CDT))
    y = jnp.dot(jnp.concatenate(parts, axis=1), wo_ref[...], preferred_element_type=F32)
    y_ref[...] = x_ref[...] + gt_ref[0] * y


def _gla_out(o, proj, r_block, g_o, wo, x, gt, rows_per_group, n_heads, dv):
    M, D = x.shape
    hv = n_heads * dv
    tm = _tile(M, 512, 8)
    tpg = max(rows_per_group // tm, 1)
    kern = functools.partial(_gla_out_kernel, n_heads=n_heads, dv=dv)
    return pl.pallas_call(
        kern,
        grid=(M // tm,),
        in_specs=[pl.BlockSpec((tm, hv), lambda i: (i, 0)),
                  pl.BlockSpec((tm, hv), lambda i: (i, r_block)),
                  pl.BlockSpec((1, dv), lambda i: (0, 0)),
                  pl.BlockSpec(wo.shape, lambda i: (0, 0)),
                  pl.BlockSpec((tm, D), lambda i: (i, 0)),
                  pl.BlockSpec((1, gt.shape[1], D), lambda i: (i // tpg, 0, 0))],
        out_specs=pl.BlockSpec((tm, D), lambda i: (i, 0)),
        out_shape=jax.ShapeDtypeStruct((M, D), F32),
        compiler_params=_cparams(("parallel",)),
        name="gla_out",
    )(o, proj, g_o.reshape(1, dv), wo, x, gt)


def _order_key(score):
    score = jnp.where(score == 0.0, 0.0, score)
    bits = pltpu.bitcast(score, jnp.int32)
    return jnp.where(bits < 0, bits ^ jnp.int32(0x7FFFFFFF), bits)


def _kth_largest_key(key, topk, axes):
    shape = tuple(1 if a in axes else s for a, s in enumerate(key.shape))

    def body(it, t):
        cand = t + lax.shift_left(jnp.int32(1), jnp.int32(31) - it)
        cnt = jnp.sum(jnp.where(key >= cand, 1, 0), axis=axes, keepdims=True)
        return jnp.where(cnt >= topk, cand, t)

    return lax.fori_loop(0, 32, body, jnp.full(shape, INT_MIN, jnp.int32))


def _bias_chain(dist, value_of_bucket):
    val = value_of_bucket(0)
    for j in range(1, N_BUCKETS):
        val = jnp.where(dist >= BUCKET_LO[j], value_of_bucket(j), val)
    return val


def _strict_upper(n):
    a = lax.broadcasted_iota(jnp.int32, (n, n), 0)
    b = lax.broadcasted_iota(jnp.int32, (n, n), 1)
    return jnp.where(a < b, 1.0, 0.0).astype(CDT)


def _dsa_prompt_kernel(rb_ref, q_ref, qi_ref, wq_ref, k_ref, v_ref, kk_ref, o_ref, sel_ref, band_ref, *,
                       n_heads, n_kv, hd, n_idx, di, topk, tq, T, n_cls):
    b_id = pl.program_id(0)
    qt = pl.program_id(1)
    group = n_heads // n_kv

    @pl.when((b_id == 0) & (qt == 0))
    def _():
        i = lax.broadcasted_iota(jnp.int32, (tq, LANE), 0)
        j = lax.broadcasted_iota(jnp.int32, (tq, LANE), 1)
        for h in range(n_heads):
            far = rb_ref[N_BUCKETS - 1, h]
            for w in range(tq // LANE + 1):
                dist = i - j + w * LANE
                band_ref[w, h] = _bias_chain(dist, lambda bk: rb_ref[bk, h]) - far

    def body(nk):
        row_pos = qt * tq + lax.broadcasted_iota(jnp.int32, (tq, nk), 0)
        col_pos = lax.broadcasted_iota(jnp.int32, (tq, nk), 1)
        causal = col_pos <= row_pos

        ki = kk_ref[:nk, :di]
        wi = wq_ref[:, di:di + n_idx] * (n_idx ** -0.5)
        score = jnp.zeros((tq, nk), F32)
        for h in range(n_idx):
            lg = _dot_nt(qi_ref[:, h * di:(h + 1) * di], ki) * (di ** -0.5)
            score = score + wi[:, h:h + 1] * jnp.maximum(lg, 0.0)
        score = jnp.where(causal, score, NEG_INF)
        key = _order_key(score)
        thr = _kth_largest_key(key, topk, (1,))
        gt = key > thr
        eq = key == thr
        n_gt = jnp.sum(jnp.where(gt, 1, 0), axis=1, keepdims=True)
        n_eq = jnp.sum(jnp.where(eq, 1, 0), axis=1, keepdims=True)
        need = topk - n_gt
        tie = jnp.max(n_eq - need) > 0

        @pl.when(jnp.logical_not(tie))
        def _():
            sel_ref[:, :nk] = jnp.where((gt | eq) & causal, 0.0, NEG_INF)

        @pl.when(tie)
        def _():
            upper = _strict_upper(LANE)
            run = jnp.zeros((tq, 1), F32)
            needf = need.astype(F32)
            for kb in range(nk // LANE):
                sl = slice(kb * LANE, (kb + 1) * LANE)
                eqb = jnp.where(eq[:, sl], 1.0, 0.0)
                pre = jnp.dot(eqb.astype(CDT), upper, preferred_element_type=F32) + run
                keep = gt[:, sl] | (eq[:, sl] & (pre < needf))
                sel_ref[:, sl] = jnp.where(keep & causal[:, sl], 0.0, NEG_INF)
                run = run + jnp.sum(eqb, axis=1, keepdims=True)

        scale = hd ** -0.5
        nband = tq // LANE + 1
        nkb = nk // LANE
        first_near = max(nkb - (T // n_cls) // LANE - (nband - 1), 0)
        for g in range(n_kv):
            kg = k_ref[:nk, g * hd:(g + 1) * hd].astype(CDT)
            vg = v_ref[:nk, g * hd:(g + 1) * hd].astype(CDT)
            for r in range(group):
                h = g * group + r
                s = _dot_nt(q_ref[:, h * hd:(h + 1) * hd], kg) * scale + rb_ref[N_BUCKETS - 1, h] + sel_ref[:, :nk]
                parts = [s[:, :first_near * LANE]] if first_near else []
                for kb in range(first_near, nkb):
                    w = qt * (tq // LANE) - kb
                    add = jnp.zeros((tq, LANE), F32)
                    for wv in range(nband):
                        add = jnp.where(w == wv, band_ref[wv, h], add)
                    parts.append(s[:, kb * LANE:(kb + 1) * LANE] + add)
                s = jnp.concatenate(parts, axis=1)
                m = jnp.max(s, axis=-1, keepdims=True)
                p = jnp.exp(s - m)
                l = jnp.sum(p, axis=-1, keepdims=True)
                o_ref[:, h * hd:(h + 1) * hd] = (_dot(p, vg) / l).astype(o_ref.dtype)

    width = T // n_cls
    cls = ((qt + 1) * tq - 1) // width
    for c in range(n_cls):
        pl.when(cls == c)(functools.partial(body, (c + 1) * width))


def _dsa_prompt(proj, rel_bias, B, T, dims, topk):
    n_heads, n_kv, hd, n_idx, di = dims
    M, Np = proj.shape
    tq = LANE
    nqt = T // tq
    hq, hkv, hi = n_heads * hd, n_kv * hd, n_idx * di
    assert hq % hkv == 0 and (hq + 2 * hkv) % hi == 0 and (hq + 2 * hkv + hi) % LANE == 0
    kk_blk = (hq + 2 * hkv + hi) // LANE
    n_cls = math.gcd(T // LANE, DSA_EXTENT_CLASSES)
    kern = functools.partial(_dsa_prompt_kernel, n_heads=n_heads, n_kv=n_kv, hd=hd, n_idx=n_idx, di=di, topk=topk,
                             tq=tq, T=T, n_cls=n_cls)
    return pl.pallas_call(
        kern,
        grid=(B, nqt),
        in_specs=[pl.BlockSpec(memory_space=pltpu.SMEM),
                  pl.BlockSpec((tq, hq), lambda b, t: (b * nqt + t, 0)),
                  pl.BlockSpec((tq, hi), lambda b, t: (b * nqt + t, (hq + 2 * hkv) // hi)),
                  pl.BlockSpec((tq, LANE), lambda b, t: (b * nqt + t, kk_blk)),
                  pl.BlockSpec((T, hkv), lambda b, t: (b, hq // hkv)),
                  pl.BlockSpec((T, hkv), lambda b, t: (b, hq // hkv + 1)),
                  pl.BlockSpec((T, LANE), lambda b, t: (b, kk_blk))],
        out_specs=pl.BlockSpec((tq, hq), lambda b, t: (b * nqt + t, 0)),
        out_shape=jax.ShapeDtypeStruct((M, hq), CDT),
        scratch_shapes=[pltpu.VMEM((tq, T), F32),
                        pltpu.VMEM((tq // LANE + 1, n_heads, tq, LANE), F32)],
        compiler_params=_cparams(("arbitrary", "arbitrary")),
        name="dsa_prompt",
    )(rel_bias, proj, proj, proj, proj, proj, proj)


def _dsa_select_kernel(pt_ref, qi_ref, wi_ref, kn_ref, *refs, n_pages_step, n_idx, di, topk, t_new):
    G = n_pages_step
    ki_refs = refs[:G]
    mask_ref, sc_ref = refs[G:]
    g = pl.program_id(1)
    NG = pl.num_programs(1)
    R = qi_ref.shape[1] // n_idx
    qi = qi_ref[0]
    wi = wi_ref[0] * (n_idx ** -0.5)

    def head_sum(lg):
        w = wi * jnp.maximum(lg * (di ** -0.5), 0.0)
        sc = w[0:R]
        for h in range(1, n_idx):
            sc = sc + w[h * R:(h + 1) * R]
        return sc

    past = head_sum(_dot(qi, jnp.concatenate([r[0, 0] for r in ki_refs], axis=1)))
    GW = past.shape[1]
    sc_ref[g] = past

    @pl.when(g == NG - 1)
    def _():
        new = head_sum(_dot_nt(qi, kn_ref[0][:, :di]))
        t_row = lax.broadcasted_iota(jnp.int32, new.shape, 0) % t_new
        col = lax.broadcasted_iota(jnp.int32, new.shape, 1)
        new = jnp.where((col <= t_row) & (col < t_new), new, NEG_INF)
        sc_ref[NG] = jnp.concatenate([new, jnp.full((R, GW - LANE), NEG_INF, F32)], axis=1)
        score = sc_ref[...]
        valid = score > NEG_INF
        key = _order_key(score)
        thr = _kth_largest_key(key, topk, (0, 2))
        gt = key > thr
        eq = key == thr
        n_gt = jnp.sum(jnp.where(gt, 1, 0), axis=(0, 2), keepdims=True)
        n_eq = jnp.sum(jnp.where(eq, 1, 0), axis=(0, 2), keepdims=True)
        need = topk - n_gt
        tie = jnp.max(n_eq - need) > 0

        @pl.when(jnp.logical_not(tie))
        def _():
            mask_ref[0] = jnp.where((gt | eq) & valid, 0.0, NEG_INF)

        @pl.when(tie)
        def _():
            upper = _strict_upper(LANE)
            needf = need[0].astype(F32)

            def blk(gi, run):
                k_g = _order_key(sc_ref[gi])
                v_g = sc_ref[gi] > NEG_INF
                for c in range(GW // LANE):
                    sl = slice(c * LANE, (c + 1) * LANE)
                    eqb = jnp.where(k_g[:, sl] == thr[0], 1.0, 0.0)
                    pre = jnp.dot(eqb.astype(CDT), upper, preferred_element_type=F32) + run
                    keep = (k_g[:, sl] > thr[0]) | ((eqb > 0.0) & (pre < needf))
                    mask_ref[0, gi, :, sl] = jnp.where(keep & v_g[:, sl], 0.0, NEG_INF)
                    run = run + jnp.sum(eqb, axis=1, keepdims=True)
                return run

            lax.fori_loop(0, NG + 1, blk, jnp.zeros((R, 1), F32))


def _dsa_select(qi_hm, wi_hm, kn, cache_ki_t, layer, page_table, B, n_idx, di, topk, t_new):
    n_pages = page_table.shape[1]
    page = cache_ki_t.shape[3]
    G = math.gcd(n_pages, PAGES_PER_STEP_SELECT)
    NG = n_pages // G
    GW = G * page
    R = qi_hm.shape[1] // n_idx
    kern = functools.partial(_dsa_select_kernel, n_pages_step=G, n_idx=n_idx, di=di, topk=topk, t_new=t_new)

    def page_map(i):
        return lambda b, g, pt: (layer, pt[b, g * G + i], 0, 0)

    in_specs = [pl.BlockSpec((1, n_idx * R, di), lambda b, g, pt: (b, 0, 0)),
                pl.BlockSpec((1, n_idx * R, 1), lambda b, g, pt: (b, 0, 0)),
                pl.BlockSpec((1, LANE, LANE), lambda b, g, pt: (b, 0, 0))]
    in_specs += [pl.BlockSpec((1, 1, di, page), page_map(i)) for i in range(G)]
    return pl.pallas_call(
        kern,
        grid_spec=pltpu.PrefetchScalarGridSpec(
            num_scalar_prefetch=1,
            grid=(B, NG),
            in_specs=in_specs,
            out_specs=pl.BlockSpec((1, NG + 1, R, GW), lambda b, g, pt: (b, 0, 0, 0)),
            scratch_shapes=[pltpu.VMEM((NG + 1, R, GW), F32)]),
        out_shape=jax.ShapeDtypeStruct((B, NG + 1, R, GW), F32),
        compiler_params=_cparams(("parallel", "arbitrary")),
        name="dsa_select",
    )(page_table, qi_hm, wi_hm, kn, *([cache_ki_t] * G))


def _dsa_decode_kernel(pt_ref, rb_ref, q_ref, kn_ref, vn_ref, mask_ref, mnew_ref, *refs,
                       n_pages_step, n_kv, group, hd, t_new, past_len):
    G = n_pages_step
    k_refs = refs[:G]
    v_refs = refs[G:2 * G]
    o_ref, m_ref, l_ref, acc_ref = refs[2 * G:]
    g = pl.program_id(1)
    NG = pl.num_programs(1)
    R = group * t_new
    scale = hd ** -0.5

    @pl.when(g == 0)
    def _():
        m_ref[...] = jnp.full(m_ref.shape, NEG_INF, F32)
        l_ref[...] = jnp.zeros(l_ref.shape, F32)
        acc_ref[...] = jnp.zeros(acc_ref.shape, F32)

    def head_val(kvh, bucket):
        row = lax.broadcasted_iota(jnp.int32, (R, 1), 0)
        val = jnp.full((R, 1), rb_ref[bucket, kvh * group], F32)
        for r in range(1, group):
            val = jnp.where(row >= r * t_new, rb_ref[bucket, kvh * group + r], val)
        return val

    def update(kvh, s, v):
        m_prev = m_ref[kvh]
        m_new = jnp.maximum(m_prev, jnp.max(s, axis=-1, keepdims=True))
        m_safe = jnp.where(m_new > NEG_INF, m_new, 0.0)
        alpha = jnp.exp(m_prev - m_safe)
        p = jnp.exp(s - m_safe)
        l_ref[kvh] = alpha * l_ref[kvh] + jnp.sum(p, axis=-1, keepdims=True)
        acc_ref[kvh] = alpha * acc_ref[kvh] + _dot(p, v)
        m_ref[kvh] = m_new

    page = k_refs[0].shape[2] // n_kv
    GW = G * page
    mask = mask_ref[0, 0]
    t_row = lax.broadcasted_iota(jnp.int32, (R, GW), 0) % t_new
    col = lax.broadcasted_iota(jnp.int32, (R, GW), 1)
    dist = past_len + t_row - (g * GW + col)
    near = g == NG - 1

    def head_rows(refs_, kvh):
        return jnp.concatenate([r[0, 0, pl.ds(kvh, page, stride=n_kv), :] for r in refs_], axis=0).astype(CDT)

    for kvh in range(n_kv):
        s = _dot_nt(q_ref[0, kvh], head_rows(k_refs, kvh)) * scale
        bias = lax.cond(near,
                        lambda: _bias_chain(dist, functools.partial(head_val, kvh)),
                        lambda: jnp.broadcast_to(head_val(kvh, N_BUCKETS - 1), (R, GW)))
        update(kvh, s + bias + mask, head_rows(v_refs, kvh))

    @pl.when(near)
    def _():
        kn = kn_ref[0].astype(CDT)
        vn = vn_ref[0].astype(CDT)
        mnew = mnew_ref[0, 0][:, :LANE]
        t_r = lax.broadcasted_iota(jnp.int32, (R, LANE), 0) % t_new
        c = lax.broadcasted_iota(jnp.int32, (R, LANE), 1)
        d_new = jnp.maximum(t_r - c, 0)
        for kvh in range(n_kv):
            sl = slice(kvh * hd, (kvh + 1) * hd)
            s = _dot_nt(q_ref[0, kvh], kn[:, sl]) * scale
            bias = _bias_chain(d_new, functools.partial(head_val, kvh))
            update(kvh, s + bias + mnew, vn[:, sl])
            o_ref[0, kvh] = (acc_ref[kvh] / l_ref[kvh]).astype(o_ref.dtype)


def _dsa_decode(q4, kn, vn, mask, rel_bias, cache_k, cache_v, layer, page_table, B, n_kv, group, hd, t_new):
    n_pages = page_table.shape[1]
    page = cache_k.shape[2] // n_kv
    GWs = mask.shape[3]
    G = math.gcd(math.gcd(n_pages, PAGES_PER_STEP_DECODE), GWs // page)
    NG = n_pages // G
    GW = G * page
    ratio = GWs // GW
    R = group * t_new
    kern = functools.partial(_dsa_decode_kernel, n_pages_step=G, n_kv=n_kv, group=group, hd=hd, t_new=t_new,
                             past_len=n_pages * page)

    def page_map(i):
        return lambda b, g, pt: (layer, pt[b, g * G + i], 0, 0)

    in_specs = [pl.BlockSpec(memory_space=pltpu.SMEM),
                pl.BlockSpec((1, n_kv, R, hd), lambda b, g, pt: (b, 0, 0, 0)),
                pl.BlockSpec((1, LANE, n_kv * hd), lambda b, g, pt: (b, 0, 0)),
                pl.BlockSpec((1, LANE, n_kv * hd), lambda b, g, pt: (b, 0, 0)),
                pl.BlockSpec((1, 1, R, GW), lambda b, g, pt: (b, g // ratio, 0, g % ratio)),
                pl.BlockSpec((1, 1, R, GW), lambda b, g, pt: (b, mask.shape[1] - 1, 0, 0))]
    in_specs += [pl.BlockSpec((1, 1, page * n_kv, hd), page_map(i)) for i in range(G)]
    in_specs += [pl.BlockSpec((1, 1, page * n_kv, hd), page_map(i)) for i in range(G)]
    return pl.pallas_call(
        kern,
        grid_spec=pltpu.PrefetchScalarGridSpec(
            num_scalar_prefetch=1,
            grid=(B, NG),
            in_specs=in_specs,
            out_specs=pl.BlockSpec((1, n_kv, R, hd), lambda b, g, pt: (b, 0, 0, 0)),
            scratch_shapes=[pltpu.VMEM((n_kv, R, 1), F32), pltpu.VMEM((n_kv, R, 1), F32),
                            pltpu.VMEM((n_kv, R, hd), F32)]),
        out_shape=jax.ShapeDtypeStruct((B, n_kv, R, hd), CDT),
        compiler_params=_cparams(("parallel", "arbitrary")),
        name="dsa_decode",
    )(page_table, rel_bias, q4, kn, vn, mask, mask, *([cache_k] * G), *([cache_v] * G))


def _ffn_kernel(*refs, seq_tiles, t_seq, conv_w):
    if seq_tiles:
        (x_ref, g_ref, sc_ref, sh_ref, gt_ref, wg_ref, wv_ref, wd_ref, cw_ref, cb_ref,
         y_ref, tail_ref, h_ref, acc_ref, stash_ref) = refs
    else:
        (x_ref, g_ref, sc_ref, sh_ref, gt_ref, wg_ref, wv_ref, wd_ref, cw_ref, cb_ref, p1_ref, p2_ref,
         y_ref, tail_ref, h_ref, acc_ref) = refs
    i = pl.program_id(0)
    j = pl.program_id(1)

    @pl.when(j == 0)
    def _():
        h = _rms(x_ref[...], g_ref[...]) * (1.0 + sc_ref[0]) + sh_ref[0]
        h_ref[...] = h.astype(CDT)
        acc_ref[...] = jnp.zeros(acc_ref.shape, F32)

    gate = jnp.dot(h_ref[...], wg_ref[...], preferred_element_type=F32)
    val = jnp.dot(h_ref[...], wv_ref[...], preferred_element_type=F32)
    tm = gate.shape[0]
    row = lax.broadcasted_iota(jnp.int32, gate.shape, 0)
    g1 = pltpu.roll(gate, 1, 0)
    g2 = pltpu.roll(gate, 2, 0)
    if seq_tiles:
        @pl.when((i == 0) & (j == 0))
        def _():
            stash_ref[...] = jnp.zeros(stash_ref.shape, F32)

        prev = jnp.where(i % seq_tiles == 0, 0.0, stash_ref[j])
        g1 = jnp.where(row == 0, prev[7:8], g1)
        g2 = jnp.where(row == 0, prev[6:7], jnp.where(row == 1, prev[7:8], g2))
        stash_ref[j] = gate[tm - 8:]
        tail_ref[0] = gate[tm - 8:]
    else:
        t = row % t_seq
        g1 = jnp.where(t == 0, p1_ref[...], g1)
        g2 = jnp.where(t < 2, p2_ref[...], g2)
        tail_ref[...] = gate
    cw = cw_ref[...]
    conv = cw[0:1] * g2 + cw[1:2] * g1 + cw[2:3] * gate + cb_ref[...]
    act = (_silu(conv) * val).astype(CDT)
    acc_ref[...] += jnp.dot(act, wd_ref[...], preferred_element_type=F32)

    @pl.when(j == pl.num_programs(1) - 1)
    def _():
        y_ref[...] = x_ref[...] + gt_ref[0] * acc_ref[...]


def _ffn(x, g, sc, sh, gt, w_up, w_down, conv_w, conv_b, rows_per_group, t_seq, prev=None):
    M, D = x.shape
    Fd = w_down.shape[0]
    assert conv_w.shape[0] == 3
    tf = _tile(Fd, FFN_TILE_F)
    nf = Fd // tf
    cw = jnp.pad(conv_w, ((0, 8 - conv_w.shape[0]), (0, 0)))
    cb = conv_b.reshape(1, Fd)
    seq_mode = prev is None
    if seq_mode:
        tm = _tile(t_seq, 512, 8)
        assert t_seq % tm == 0 and tm >= 8
        seq_tiles = t_seq // tm
    else:
        tm = M
        seq_tiles = 0
    nt = M // tm
    tpg = max(rows_per_group // tm, 1)
    mod = lambda m: pl.BlockSpec((1, m.shape[1], D), lambda i, j: (i // tpg, 0, 0))
    in_specs = [pl.BlockSpec((tm, D), lambda i, j: (i, 0)),
                pl.BlockSpec((1, D), lambda i, j: (0, 0)),
                mod(sc), mod(sh), mod(gt),
                pl.BlockSpec((D, tf), lambda i, j: (0, j)),
                pl.BlockSpec((D, tf), lambda i, j: (0, nf + j)),
                pl.BlockSpec((tf, D), lambda i, j: (j, 0)),
                pl.BlockSpec((8, tf), lambda i, j: (0, j)),
                pl.BlockSpec((1, tf), lambda i, j: (0, j))]
    args = [x, g.reshape(1, D), sc, sh, gt, w_up, w_up, w_down, cw, cb]
    scratch = [pltpu.VMEM((tm, D), CDT), pltpu.VMEM((tm, D), F32)]
    if seq_mode:
        tail_shape = jax.ShapeDtypeStruct((nt, 8, Fd), F32)
        tail_spec = pl.BlockSpec((1, 8, tf), lambda i, j: (i, 0, j))
        scratch.append(pltpu.VMEM((nf, 8, tf), F32))
    else:
        p1 = jnp.concatenate([prev[:, 1:2], jnp.zeros_like(prev[:, :1]).repeat(t_seq - 1, axis=1)], axis=1)
        p2 = jnp.concatenate([prev[:, 0:2], jnp.zeros_like(prev[:, :1]).repeat(t_seq - 2, axis=1)], axis=1)
        args += [p1.reshape(M, Fd), p2.reshape(M, Fd)]
        in_specs += [pl.BlockSpec((tm, tf), lambda i, j: (i, j))] * 2
        tail_shape = jax.ShapeDtypeStruct((M, Fd), F32)
        tail_spec = pl.BlockSpec((tm, tf), lambda i, j: (i, j))
    kern = functools.partial(_ffn_kernel, seq_tiles=seq_tiles, t_seq=t_seq, conv_w=conv_w.shape[0])
    return pl.pallas_call(
        kern,
        grid=(nt, nf),
        in_specs=in_specs,
        out_specs=[pl.BlockSpec((tm, D), lambda i, j: (i, 0)), tail_spec],
        out_shape=[jax.ShapeDtypeStruct((M, D), F32), tail_shape],
        scratch_shapes=scratch,
        compiler_params=_cparams(("arbitrary", "arbitrary")),
        name="ffn",
    )(*args)


def _final_norm_kernel(x_ref, g_ref, o_ref):
    o_ref[...] = _rms(x_ref[...], g_ref[...])


def _final_norm(x, g):
    M, D = x.shape
    tm = _tile(M, 1024, 8)
    return pl.pallas_call(
        _final_norm_kernel,
        grid=(M // tm,),
        in_specs=[pl.BlockSpec((tm, D), lambda i: (i, 0)), pl.BlockSpec((1, D), lambda i: (0, 0))],
        out_specs=pl.BlockSpec((tm, D), lambda i: (i, 0)),
        out_shape=jax.ShapeDtypeStruct((M, D), F32),
        compiler_params=_cparams(("parallel",)),
        name="final_norm",
    )(x, g.reshape(1, D))


def _rope_tables(pos, rope, n_heads, reps):
    half = rope // 2
    inv = ROPE_BASE ** (-jnp.arange(half, dtype=F32) / half)
    ang = pos.astype(F32)[:, None] * inv[None, :]
    cos = jnp.cos(ang)
    sin = jnp.sin(ang)
    cosk = jnp.concatenate([cos, cos], axis=1)
    sink = jnp.concatenate([-sin, sin], axis=1)
    tabs = (jnp.tile(cosk, (1, n_heads)), jnp.tile(sink, (1, n_heads)), cosk, sink)
    return tuple(jnp.tile(t, (reps, 1)) for t in tabs)


def _swap_halves(w, rope):
    half = rope // 2
    return jnp.concatenate([w[..., half:], w[..., :half]], axis=-1)


def _pad_cols(w, n):
    return jnp.pad(w, ((0, 0), (0, n - w.shape[1])))


def kernel(x_prompt, x_sample, cache_mla_ckv, cache_mla_kpe, state_gla, cache_dsa_k, cache_dsa_v, cache_dsa_kidx,
           state_ffn_conv, page_table, c_prompt, c_sample, ada_w, ada_b, norm1_g, norm2_g, final_g, mla_w_in,
           mla_g_q, mla_g_kv, mla_w_uq, mla_w_uk, mla_w_uv, mla_w_o, gla_w_in, gla_w_a2, gla_b_a2, gla_g_o,
           gla_w_o, dsa_w_in, dsa_w_o, rel_bias, ffn_w_up, ffn_conv_w, ffn_conv_b, ffn_w_down):
    Bp, Tp, D = x_prompt.shape
    Bs, Ts, _ = x_sample.shape
    depth = ada_w.shape[0]
    n_mod = ada_w.shape[2] // D
    n_pages, page = page_table.shape[1], cache_mla_ckv.shape[2]
    past_len = n_pages * page
    cache_kpe_t = jnp.swapaxes(cache_mla_kpe, 2, 3)
    cache_ki_t = jnp.swapaxes(cache_dsa_kidx, 2, 3)

    q_lora, mla_h, qk_dim = mla_w_uq.shape[1:]
    kv_lora, _, nope = mla_w_uk.shape[1:]
    mla_v = mla_w_uv.shape[3]
    rope = qk_dim - nope
    gla_h = state_gla.shape[2]
    gla_dk, gla_dv = state_gla.shape[3:]
    gla_rank = gla_w_a2.shape[1]
    dsa_kv, dsa_hd = cache_dsa_k.shape[3:]
    dsa_di = cache_dsa_kidx.shape[3]
    dsa_h = dsa_w_o.shape[1] // dsa_hd
    dsa_hi = (dsa_w_in.shape[2] - (dsa_h + 2 * dsa_kv) * dsa_hd - dsa_di) // (dsa_di + 1)
    assert kv_lora % LANE == 0 and q_lora % LANE == 0 and rope <= LANE

    mod_all = _ada_mod(jnp.concatenate([c_prompt, c_sample], axis=0), ada_w, ada_b)

    def mods(l, sample):
        m = mod_all[l, Bp:] if sample else mod_all[l, :Bp]
        parts = [m[:, i * D:(i + 1) * D] for i in range(n_mod)]
        if sample:
            return [jnp.repeat(p, Ts, axis=0).reshape(1, Bs * Ts, D) for p in parts]
        return [p.reshape(Bp, 1, D) for p in parts]

    mla_w = []
    for j in range(mla_w_in.shape[0]):
        w_in = mla_w_in[j]
        kcol = q_lora + kv_lora
        w_in_ext = jnp.concatenate([w_in, _swap_halves(w_in[:, kcol:kcol + rope], rope)], axis=1)
        w_in_ext = _pad_cols(w_in_ext, -(-w_in_ext.shape[1] // LANE) * LANE).astype(CDT)
        uq = mla_w_uq[j]
        wqn = uq[:, :, :nope].reshape(q_lora, mla_h * nope).astype(CDT)
        wqp = uq[:, :, nope:].reshape(q_lora, mla_h * rope).astype(CDT)
        wqs = _swap_halves(uq[:, :, nope:], rope).reshape(q_lora, mla_h * rope).astype(CDT)
        wukT = jnp.transpose(mla_w_uk[j], (1, 2, 0)).astype(CDT)
        wuv = jnp.transpose(mla_w_uv[j], (1, 0, 2)).astype(CDT)
        mla_w.append((w_in_ext, wqn, wqp, wqs, wukT, wuv, mla_w_o[j].astype(CDT)))
    gla_np = -(-gla_w_in.shape[2] // LANE) * LANE
    gla_w = []
    for j in range(gla_w_in.shape[0]):
        w_a2p = jnp.pad(gla_w_a2[j], ((0, LANE - gla_rank), (0, 0))).astype(CDT)
        gla_w.append((_pad_cols(gla_w_in[j], gla_np).astype(CDT), w_a2p, gla_w_o[j].astype(CDT)))
    dsa_np = -(-dsa_w_in.shape[2] // LANE) * LANE
    dsa_w = [(_pad_cols(dsa_w_in[j], dsa_np).astype(CDT), dsa_w_o[j].astype(CDT)) for j in range(dsa_w_in.shape[0])]
    ffn_up = ffn_w_up.astype(CDT)
    ffn_down = ffn_w_down.astype(CDT)

    mla_dims = (mla_h, q_lora, kv_lora, rope, nope)
    mla_scale = qk_dim ** -0.5

    def trunk(x3, sample):
        B, T, _ = x3.shape
        M = B * T
        x = x3.reshape(M, D)
        rpg = M if sample else T
        pos = (past_len if sample else 0) + jnp.arange(T, dtype=jnp.int32)
        tabs = _rope_tables(pos, rope, mla_h, B if sample else 1)
        outs = dict(mla_ckv=[], mla_kpe=[], gla=[], dsa_k=[], dsa_v=[], dsa_ki=[], conv=[])
        for l in range(depth):
            sh1, sc1, gt1, sh2, sc2, gt2 = mods(l, sample)
            j = l // 3
            if l % 3 == 0:
                w_in_ext, wqn, wqp, wqs, wukT, wuv, wo = mla_w[j]
                p = _nm_linear(x, norm1_g[l], sc1, sh1, w_in_ext, rpg)
                ckv, kpe, kcat, qcat = _mla_prep(p, mla_g_q[j], mla_g_kv[j], tabs, wqn, wqp, wqs, wukT, mla_dims)
                if sample:
                    o_lat = _mla_decode(qcat, kcat, cache_mla_ckv, cache_kpe_t, j, page_table, B, T, mla_h,
                                        kv_lora, rope, mla_scale)
                else:
                    o_lat = _mla_flash(qcat, kcat, B, T, mla_h, kv_lora, mla_scale)
                x = _mla_out(o_lat, wuv, wo, x, gt1, rpg, mla_h, kv_lora)
                outs["mla_ckv"].append(ckv.reshape(B, T, kv_lora))
                outs["mla_kpe"].append(kpe.reshape(B, T, rope))
            elif l % 3 == 1:
                w_in_p, w_a2p, wo = gla_w[j]
                hk, hv = gla_h * gla_dk, gla_h * gla_dv
                p = _nm_linear(x, norm1_g[l], sc1, sh1, w_in_p, rpg, tn_target=640)
                log_a = _gla_gate(p, (2 * hk + 2 * hv) // LANE, w_a2p, gla_b_a2[j])
                s0 = state_gla[j] if sample else jnp.zeros((B, gla_h, gla_dk, gla_dv), F32)
                o, s_fin = _gla_recurrence(p, log_a, s0, B, T, gla_h, gla_dk, gla_dv)
                x = _gla_out(o, p, (2 * hk + hv) // hv, gla_g_o[j], wo, x, gt1, rpg, gla_h, gla_dv)
                outs["gla"].append(s_fin)
            else:
                w_in_p, wo = dsa_w[j]
                hq, hkv, hi = dsa_h * dsa_hd, dsa_kv * dsa_hd, dsa_hi * dsa_di
                p = _nm_linear(x, norm1_g[l], sc1, sh1, w_in_p, rpg, tn_target=896)
                k_new = p[:, hq:hq + hkv]
                v_new = p[:, hq + hkv:hq + 2 * hkv]
                kk = p[:, hq + 2 * hkv + hi:]
                L_keys = (past_len if sample else 0) + T
                topk = min(DSA_TOPK, L_keys // 4)
                if sample:
                    group = dsa_h // dsa_kv
                    dup = lambda a: jnp.concatenate([a.reshape(B, T, dsa_hi, -1)] * group, axis=1)
                    qi_hm = dup(p[:, hq + 2 * hkv:hq + 2 * hkv + hi]).transpose(0, 2, 1, 3)
                    qi_hm = qi_hm.reshape(B, dsa_hi * group * T, dsa_di)
                    wi_hm = dup(kk[:, dsa_di:dsa_di + dsa_hi]).transpose(0, 2, 1, 3).reshape(B, dsa_hi * group * T, 1)
                    padn = lambda a: jnp.pad(a.reshape(B, T, -1), ((0, 0), (0, LANE - T), (0, 0)))
                    mask = _dsa_select(qi_hm, wi_hm, padn(kk), cache_ki_t, j, page_table, B, dsa_hi, dsa_di, topk, T)
                    q4 = p[:, :hq].reshape(B, T, dsa_kv, group, dsa_hd).transpose(0, 2, 3, 1, 4)
                    q4 = q4.reshape(B, dsa_kv, group * T, dsa_hd)
                    ck = cache_dsa_k.reshape(*cache_dsa_k.shape[:2], page * dsa_kv, dsa_hd)
                    cv = cache_dsa_v.reshape(*cache_dsa_v.shape[:2], page * dsa_kv, dsa_hd)
                    o4 = _dsa_decode(q4, padn(k_new), padn(v_new), mask, rel_bias, ck, cv, j, page_table, B,
                                     dsa_kv, group, dsa_hd, T)
                    o = o4.reshape(B, dsa_kv, group, T, dsa_hd).transpose(0, 3, 1, 2, 4).reshape(M, hq)
                else:
                    o = _dsa_prompt(p, rel_bias, B, T, (dsa_h, dsa_kv, dsa_hd, dsa_hi, dsa_di), topk)
                x = _proj_res(o, wo, x, gt1, rpg)
                outs["dsa_k"].append(k_new.reshape(B, T, dsa_kv, dsa_hd))
                outs["dsa_v"].append(v_new.reshape(B, T, dsa_kv, dsa_hd))
                outs["dsa_ki"].append(kk[:, :dsa_di].reshape(B, T, dsa_di))
            if sample:
                x, tail = _ffn(x, norm2_g[l], sc2, sh2, gt2, ffn_up[l], ffn_down[l], ffn_conv_w[l], ffn_conv_b[l],
                               rpg, T, prev=state_ffn_conv[l])
                outs["conv"].append(tail.reshape(B, T, -1)[:, T - 2:])
            else:
                x, tail = _ffn(x, norm2_g[l], sc2, sh2, gt2, ffn_up[l], ffn_down[l], ffn_conv_w[l], ffn_conv_b[l],
                               rpg, T)
                nt = tail.shape[0] // B
                outs["conv"].append(tail.reshape(B, nt, 8, -1)[:, nt - 1, 6:8])
        y = _final_norm(x, final_g).reshape(B, T, D)
        return (y, jnp.stack(outs["mla_ckv"]), jnp.stack(outs["mla_kpe"]), jnp.stack(outs["gla"]),
                jnp.stack(outs["dsa_k"]), jnp.stack(outs["dsa_v"]), jnp.stack(outs["dsa_ki"]),
                jnp.stack(outs["conv"]))

    rp = trunk(x_prompt, False)
    rs = trunk(x_sample, True)
    return (rp[0], rs[0]) + tuple(rp[1:]) + tuple(rs[1:])
```

```python
import functools
import math

import numpy as np
import jax
import jax.numpy as jnp
from jax import lax
from jax.experimental import pallas as pl
from jax.experimental.pallas import tpu as pltpu

F32 = jnp.float32
CDT = jnp.bfloat16
EPS = 1e-6
ROPE_BASE = 10000.0
GLA_TAU = 16.0
N_BUCKETS = 32
MAX_DISTANCE = 128
DSA_TOPK = 256
LANE = 128
NEG_INF = float("-inf")
INT_MIN = -2 ** 31
VMEM_LIMIT = 56 * 1024 * 1024
PAGES_PER_STEP_MLA = 32
PAGES_PER_STEP_SELECT = 32
PAGES_PER_STEP_DECODE = 16
FFN_TILE_F = 1408
DSA_EXTENT_CLASSES = 4
DSA_TQ = 256
NM_LINEAR_TN = 4096
FLASH_TQ = 512
FLASH_TK = 512
FLASH_ROW_CHUNK = 64


def _bucket_thresholds():
    d = np.arange(0, 4 * MAX_DISTANCE)
    exact = N_BUCKETS // 2
    lr = np.log(np.maximum(d, 1).astype(np.float32) / np.float32(exact)) / np.float32(math.log(MAX_DISTANCE / exact))
    large = np.minimum(exact + (lr * np.float32(N_BUCKETS - exact)).astype(np.int32), N_BUCKETS - 1)
    b = np.where(d < exact, d, large)
    return [int(np.argmax(b >= j)) for j in range(N_BUCKETS)]


BUCKET_LO = _bucket_thresholds()


def _cparams(sem, vmem=VMEM_LIMIT):
    return pltpu.CompilerParams(dimension_semantics=sem, vmem_limit_bytes=vmem)


def _tile(n, target, mult=LANE):
    if n <= target:
        return n
    t = (target // mult) * mult
    while t > mult and n % t:
        t -= mult
    assert n % t == 0, (n, target)
    return t


def _dot(a, b):
    return jnp.dot(a.astype(CDT), b.astype(CDT), preferred_element_type=F32)


def _dot_nt(a, b):
    return lax.dot_general(a.astype(CDT), b.astype(CDT), (((1,), (1,)), ((), ())), preferred_element_type=F32)


def _dot_tn(a, b):
    return lax.dot_general(a.astype(CDT), b.astype(CDT), (((0,), (0,)), ((), ())), preferred_element_type=F32)


def _rms(x, g):
    return x * lax.rsqrt(jnp.mean(x * x, axis=-1, keepdims=True) + EPS) * g


def _silu(x):
    return x * jax.nn.sigmoid(x)


def _ada_kernel(c_ref, w_ref, b_ref, o_ref):
    ca = _silu(c_ref[...])
    o_ref[0] = _dot(ca, w_ref[0]) + b_ref[0]


def _ada_mod(c, ada_w, ada_b):
    L, D, N = ada_w.shape
    R = c.shape[0]
    tn = _tile(N, 1536)
    return pl.pallas_call(
        _ada_kernel,
        grid=(L, N // tn),
        in_specs=[pl.BlockSpec((R, D), lambda l, j: (0, 0)),
                  pl.BlockSpec((1, D, tn), lambda l, j: (l, 0, j)),
                  pl.BlockSpec((1, 1, tn), lambda l, j: (l, 0, j))],
        out_specs=pl.BlockSpec((1, R, tn), lambda l, j: (l, 0, j)),
        out_shape=jax.ShapeDtypeStruct((L, R, N), F32),
        compiler_params=_cparams(("parallel", "parallel")),
        name="ada_mod",
    )(c, ada_w, ada_b.reshape(L, 1, N))


def _nm_linear_kernel(x_ref, g_ref, sc_ref, sh_ref, w_ref, o_ref, *rest):
    h_ref = rest[-1]

    @pl.when(pl.program_id(1) == 0)
    def _():
        h = _rms(x_ref[...], g_ref[...]) * (1.0 + sc_ref[0]) + sh_ref[0]
        h_ref[...] = h.astype(CDT)

    o = jnp.dot(h_ref[...], w_ref[...], preferred_element_type=F32)
    o_ref[...] = o
    if len(rest) == 2:
        rest[0][...] = o.astype(CDT)


def _mod_spec(mod, tm, rows_per_group):
    G, R, D = mod.shape
    tpg = max(rows_per_group // tm, 1)
    return pl.BlockSpec((1, R, D), lambda i, j: (i // tpg, 0, 0))


def _nm_linear(x, g, sc, sh, w, rows_per_group, tm_target=512, tn_target=1024, with_cdt_copy=False):
    M, D = x.shape
    N = w.shape[1]
    tm = _tile(M, tm_target, 8)
    tn = _tile(N, tn_target)
    out_spec = pl.BlockSpec((tm, tn), lambda i, j: (i, j))
    out_shape = jax.ShapeDtypeStruct((M, N), F32)
    if with_cdt_copy:
        out_spec = [out_spec, out_spec]
        out_shape = [out_shape, jax.ShapeDtypeStruct((M, N), CDT)]
    return pl.pallas_call(
        _nm_linear_kernel,
        grid=(M // tm, N // tn),
        in_specs=[pl.BlockSpec((tm, D), lambda i, j: (i, 0)),
                  pl.BlockSpec((1, D), lambda i, j: (0, 0)),
                  _mod_spec(sc, tm, rows_per_group),
                  _mod_spec(sh, tm, rows_per_group),
                  pl.BlockSpec((D, tn), lambda i, j: (0, j))],
        out_specs=out_spec,
        out_shape=out_shape,
        scratch_shapes=[pltpu.VMEM((tm, D), CDT)],
        compiler_params=_cparams(("parallel", "arbitrary")),
        name="nm_linear",
    )(x, g.reshape(1, D), sc, sh, w)


def _mla_prep_kernel(p_ref, gq_ref, gkv_ref, cq_ref, sq_ref, ck_ref, sk_ref, wqn_ref, wqp_ref, wqs_ref, wuk_ref,
                     ckv_ref, kpe_ref, kcat_ref, qcat_ref, *, n_heads, q_lora, kv_lora, rope, nope):
    p = p_ref[...]
    cq = p[:, :q_lora]
    ckv = p[:, q_lora:q_lora + kv_lora]
    kpe = p[:, q_lora + kv_lora:q_lora + kv_lora + rope]
    kpe_sw = p[:, q_lora + kv_lora + rope:q_lora + kv_lora + 2 * rope]
    cqn = _rms(cq, gq_ref[...]).astype(CDT)
    q_nope = jnp.dot(cqn, wqn_ref[...], preferred_element_type=F32)
    q_pe = (jnp.dot(cqn, wqp_ref[...], preferred_element_type=F32) * cq_ref[...]
            + jnp.dot(cqn, wqs_ref[...], preferred_element_type=F32) * sq_ref[...])
    ckv_n = _rms(ckv, gkv_ref[...])
    kpe_r = kpe * ck_ref[...] + kpe_sw * sk_ref[...]
    ckv_ref[...] = ckv_n
    kpe_ref[...] = kpe_r
    tm = p.shape[0]
    hd = kv_lora + LANE
    pad = jnp.zeros((tm, LANE - rope), CDT)
    kcat_ref[...] = jnp.concatenate([ckv_n.astype(CDT), kpe_r.astype(CDT), pad], axis=1)
    for h in range(n_heads):
        q_lat = _dot(q_nope[:, h * nope:(h + 1) * nope], wuk_ref[h])
        qcat_ref[:, h * hd:(h + 1) * hd] = jnp.concatenate(
            [q_lat.astype(CDT), q_pe[:, h * rope:(h + 1) * rope].astype(CDT), pad], axis=1)


def _mla_prep(p, g_q, g_kv, tabs, wqn, wqp, wqs, wukT, dims):
    n_heads, q_lora, kv_lora, rope, nope = dims
    M, NP = p.shape
    cosq, sinq, cosk, sink = tabs
    tm = _tile(M, 256, 8)
    ntab = cosq.shape[0] // tm
    hd = kv_lora + LANE
    row = lambda i: (i, 0)
    tab = lambda i: (i % ntab, 0)
    full2 = lambda i: (0, 0)
    full3 = lambda i: (0, 0, 0)
    kern = functools.partial(_mla_prep_kernel, n_heads=n_heads, q_lora=q_lora, kv_lora=kv_lora, rope=rope, nope=nope)
    return pl.pallas_call(
        kern,
        grid=(M // tm,),
        in_specs=[pl.BlockSpec((tm, NP), row),
                  pl.BlockSpec((1, q_lora), full2),
                  pl.BlockSpec((1, kv_lora), full2),
                  pl.BlockSpec((tm, n_heads * rope), tab),
                  pl.BlockSpec((tm, n_heads * rope), tab),
                  pl.BlockSpec((tm, rope), tab),
                  pl.BlockSpec((tm, rope), tab),
                  pl.BlockSpec(wqn.shape, full2),
                  pl.BlockSpec(wqp.shape, full2),
                  pl.BlockSpec(wqs.shape, full2),
                  pl.BlockSpec(wukT.shape, full3)],
        out_specs=[pl.BlockSpec((tm, kv_lora), row),
                   pl.BlockSpec((tm, rope), row),
                   pl.BlockSpec((tm, hd), row),
                   pl.BlockSpec((tm, n_heads * hd), row)],
        out_shape=[jax.ShapeDtypeStruct((M, kv_lora), F32),
                   jax.ShapeDtypeStruct((M, rope), F32),
                   jax.ShapeDtypeStruct((M, hd), CDT),
                   jax.ShapeDtypeStruct((M, n_heads * hd), CDT)],
        compiler_params=_cparams(("parallel",)),
        name="mla_prep",
    )(p, g_q.reshape(1, -1), g_kv.reshape(1, -1), cosq, sinq, cosk, sink, wqn, wqp, wqs, wukT)


def _mla_flash_kernel(q_ref, k_ref, o_ref, m_ref, l_ref, acc_ref, s2_ref, p2_ref, pm_ref, ps_ref, *,
                      n_heads, tq, tk, hd, dv, scale, rc):
    qi = pl.program_id(1)
    ki = pl.program_id(2)

    @pl.when(ki == 0)
    def _():
        m_ref[...] = jnp.full(m_ref.shape, NEG_INF, F32)
        l_ref[...] = jnp.zeros(l_ref.shape, F32)
        acc_ref[...] = jnp.zeros(acc_ref.shape, F32)

    c = scale * math.log2(math.e)

    def block(masked):
        k = k_ref[...]
        v = k[:, :dv]
        tiles = lambda a: [a[:, t * LANE:(t + 1) * LANE] for t in range(a.shape[1] // LANE)]

        def chunk_logits(s_ref, r):
            s = s_ref[r * rc:(r + 1) * rc, :]
            if masked:
                row = qi * tq + r * rc + lax.broadcasted_iota(jnp.int32, (rc, tk), 0)
                col = ki * tk + lax.broadcasted_iota(jnp.int32, (rc, tk), 1)
                s = jnp.where(col <= row, s, NEG_INF)
            return s

        for h in range(n_heads):
            b = h % 2
            s_ref, p_ref = s2_ref.at[b], p2_ref.at[b]
            s_ref[...] = _dot_nt(q_ref[:, h * hd:(h + 1) * hd], k)
            for r in range(tq // rc):
                pm_ref[b, r * rc:(r + 1) * rc, :] = functools.reduce(jnp.maximum, tiles(chunk_logits(s_ref, r)))
            m_prev = m_ref[h]
            m_new = jnp.maximum(m_prev, jnp.broadcast_to(jnp.max(pm_ref[b], axis=-1, keepdims=True), (tq, LANE)))
            alpha = jnp.exp2(c * (m_prev - m_new))
            m_ref[h] = m_new
            pm_ref[b] = m_new
            for r in range(tq // rc):
                mb = pm_ref[b, r * rc:(r + 1) * rc, :]
                p = [jnp.exp2(c * (st - mb)) for st in tiles(chunk_logits(s_ref, r))]
                ps_ref[b, r * rc:(r + 1) * rc, :] = functools.reduce(jnp.add, p)
                p_ref[r * rc:(r + 1) * rc, :] = jnp.concatenate(p, axis=1).astype(CDT)
            l_blk = jnp.broadcast_to(jnp.sum(ps_ref[b], axis=-1, keepdims=True), (tq, LANE))
            l_ref[h] = alpha * l_ref[h] + l_blk
            pv = jnp.dot(p_ref[...], v, preferred_element_type=F32)
            acc_ref[h] = jnp.concatenate([alpha] * (dv // LANE), axis=1) * acc_ref[h] + pv

    below_diag = ki * tk + tk - 1 <= qi * tq
    pl.when(below_diag)(functools.partial(block, False))
    pl.when(jnp.logical_not(below_diag) & (ki * tk <= qi * tq + tq - 1))(functools.partial(block, True))

    @pl.when(ki == pl.num_programs(2) - 1)
    def _():
        for h in range(n_heads):
            l = jnp.concatenate([l_ref[h]] * (dv // LANE), axis=1)
            o_ref[:, h * dv:(h + 1) * dv] = (acc_ref[h] / l).astype(o_ref.dtype)


def _mla_flash(qcat, kcat, B, T, n_heads, dv, scale):
    M, hd = kcat.shape
    tq = _tile(T, FLASH_TQ, 8)
    tk = _tile(T, FLASH_TK, 8)
    nq, nk = T // tq, T // tk
    assert tk % LANE == 0 and dv % LANE == 0
    rc = math.gcd(tq, FLASH_ROW_CHUNK)
    kern = functools.partial(_mla_flash_kernel, n_heads=n_heads, tq=tq, tk=tk, hd=hd, dv=dv, scale=scale, rc=rc)

    def kmap(b, qi, ki):
        return (b * nk + jnp.minimum(ki, (qi * tq + tq - 1) // tk), 0)

    return pl.pallas_call(
        kern,
        grid=(B, nq, nk),
        in_specs=[pl.BlockSpec((tq, n_heads * hd), lambda b, qi, ki: (b * nq + qi, 0)),
                  pl.BlockSpec((tk, hd), kmap)],
        out_specs=pl.BlockSpec((tq, n_heads * dv), lambda b, qi, ki: (b * nq + qi, 0)),
        out_shape=jax.ShapeDtypeStruct((M, n_heads * dv), CDT),
        scratch_shapes=[pltpu.VMEM((n_heads, tq, LANE), F32),
                        pltpu.VMEM((n_heads, tq, LANE), F32),
                        pltpu.VMEM((n_heads, tq, dv), F32),
                        pltpu.VMEM((2, tq, tk), F32),
                        pltpu.VMEM((2, tq, tk), CDT),
                        pltpu.VMEM((2, tq, LANE), F32),
                        pltpu.VMEM((2, tq, LANE), F32)],
        compiler_params=_cparams(("parallel", "parallel", "arbitrary")),
        name="mla_flash",
    )(qcat, kcat)


def _mla_decode_kernel(pt_ref, q_ref, kn_ref, *refs, n_pages_step, n_heads, t_new, dv, rope, scale):
    G = n_pages_step
    ckv_refs = refs[:G]
    kpe_refs = refs[G:2 * G]
    o_ref, m_ref, l_ref, acc_ref = refs[2 * G:]
    g = pl.program_id(1)

    @pl.when(g == 0)
    def _():
        m_ref[...] = jnp.full(m_ref.shape, NEG_INF, F32)
        l_ref[...] = jnp.zeros(l_ref.shape, F32)
        acc_ref[...] = jnp.zeros(acc_ref.shape, F32)

    q = q_ref[0]

    def update(s, v):
        m_prev = m_ref[...]
        m_new = jnp.maximum(m_prev, jnp.max(s, axis=-1, keepdims=True))
        alpha = jnp.exp(m_prev - m_new)
        p = jnp.exp(s - m_new)
        l_ref[...] = alpha * l_ref[...] + jnp.sum(p, axis=-1, keepdims=True)
        acc_ref[...] = alpha * acc_ref[...] + _dot(p, v)
        m_ref[...] = m_new

    ckv = jnp.concatenate([r[0, 0] for r in ckv_refs], axis=0).astype(CDT)
    kpe_t = jnp.concatenate([r[0, 0] for r in kpe_refs], axis=1).astype(CDT)
    s = (_dot_nt(q[:, :dv], ckv) + _dot(q[:, dv:dv + rope], kpe_t)) * scale
    update(s, ckv)

    @pl.when(g == pl.num_programs(1) - 1)
    def _():
        kn = kn_ref[0]
        s2 = _dot_nt(q, kn) * scale
        r, c = s2.shape
        t_row = lax.broadcasted_iota(jnp.int32, (r, c), 0) // n_heads
        col = lax.broadcasted_iota(jnp.int32, (r, c), 1)
        s2 = jnp.where((col <= t_row) & (col < t_new), s2, NEG_INF)
        update(s2, kn[:, :dv])
        o_ref[0] = (acc_ref[...] / l_ref[...]).astype(o_ref.dtype)


def _mla_decode(qcat, kcat, cache_ckv, cache_kpe_t, layer, page_table, B, Ts, n_heads, dv, rope, scale):
    hd = kcat.shape[1]
    n_pages = page_table.shape[1]
    page = cache_ckv.shape[2]
    G = math.gcd(n_pages, PAGES_PER_STEP_MLA)
    NG = n_pages // G
    R = Ts * n_heads
    q3 = qcat.reshape(B, R, hd)
    npad = 16
    kn = jnp.pad(kcat.reshape(B, Ts, hd), ((0, 0), (0, npad - Ts), (0, 0)))
    kern = functools.partial(_mla_decode_kernel, n_pages_step=G, n_heads=n_heads, t_new=Ts, dv=dv, rope=rope,
                             scale=scale)

    def page_map(i):
        return lambda b, g, pt: (layer, pt[b, g * G + i], 0, 0)

    in_specs = [pl.BlockSpec((1, R, hd), lambda b, g, pt: (b, 0, 0)),
                pl.BlockSpec((1, npad, hd), lambda b, g, pt: (b, 0, 0))]
    in_specs += [pl.BlockSpec((1, 1, page, dv), page_map(i)) for i in range(G)]
    in_specs += [pl.BlockSpec((1, 1, rope, page), page_map(i)) for i in range(G)]
    out = pl.pallas_call(
        kern,
        grid_spec=pltpu.PrefetchScalarGridSpec(
            num_scalar_prefetch=1,
            grid=(B, NG),
            in_specs=in_specs,
            out_specs=pl.BlockSpec((1, R, dv), lambda b, g, pt: (b, 0, 0)),
            scratch_shapes=[pltpu.VMEM((R, 1), F32), pltpu.VMEM((R, 1), F32), pltpu.VMEM((R, dv), F32)]),
        out_shape=jax.ShapeDtypeStruct((B, R, dv), CDT),
        compiler_params=_cparams(("parallel", "arbitrary")),
        name="mla_decode",
    )(page_table, q3, kn, *([cache_ckv] * G), *([cache_kpe_t] * G))
    return out.reshape(B * Ts, n_heads * dv)


def _mla_out_kernel(o_ref, wuv_ref, wo_ref, x_ref, gt_ref, y_ref, *, n_heads, dv):
    parts = [_dot(o_ref[:, h * dv:(h + 1) * dv], wuv_ref[h]).astype(CDT) for h in range(n_heads)]
    o = jnp.concatenate(parts, axis=1)
    y = jnp.dot(o, wo_ref[...], preferred_element_type=F32)
    y_ref[...] = x_ref[...] + gt_ref[0] * y


def _mla_out(o_lat, wuv, wo, x, gt, rows_per_group, n_heads, dv):
    M, D = x.shape
    tm = _tile(M, 512, 8)
    tpg = max(rows_per_group // tm, 1)
    kern = functools.partial(_mla_out_kernel, n_heads=n_heads, dv=dv)
    return pl.pallas_call(
        kern,
        grid=(M // tm,),
        in_specs=[pl.BlockSpec((tm, n_heads * dv), lambda i: (i, 0)),
                  pl.BlockSpec(wuv.shape, lambda i: (0, 0, 0)),
                  pl.BlockSpec(wo.shape, lambda i: (0, 0)),
                  pl.BlockSpec((tm, D), lambda i: (i, 0)),
                  pl.BlockSpec((1, gt.shape[1], D), lambda i: (i // tpg, 0, 0))],
        out_specs=pl.BlockSpec((tm, D), lambda i: (i, 0)),
        out_shape=jax.ShapeDtypeStruct((M, D), F32),
        compiler_params=_cparams(("parallel",)),
        name="mla_out",
    )(o_lat, wuv, wo, x, gt)


def _proj_res_kernel(o_ref, wo_ref, x_ref, gt_ref, y_ref):
    y = jnp.dot(o_ref[...], wo_ref[...], preferred_element_type=F32)
    y_ref[...] = x_ref[...] + gt_ref[0] * y


def _proj_res(o, wo, x, gt, rows_per_group):
    M, D = x.shape
    K = o.shape[1]
    tm = _tile(M, 512, 8)
    tpg = max(rows_per_group // tm, 1)
    return pl.pallas_call(
        _proj_res_kernel,
        grid=(M // tm,),
        in_specs=[pl.BlockSpec((tm, K), lambda i: (i, 0)),
                  pl.BlockSpec(wo.shape, lambda i: (0, 0)),
                  pl.BlockSpec((tm, D), lambda i: (i, 0)),
                  pl.BlockSpec((1, gt.shape[1], D), lambda i: (i // tpg, 0, 0))],
        out_specs=pl.BlockSpec((tm, D), lambda i: (i, 0)),
        out_shape=jax.ShapeDtypeStruct((M, D), F32),
        compiler_params=_cparams(("parallel",)),
        name="proj_res",
    )(o, wo, x, gt)


def _gla_gate_kernel(a_ref, w_ref, b_ref, o_ref):
    z = _dot(a_ref[...], w_ref[...]) + b_ref[...]
    o_ref[...] = (jnp.minimum(z, 0.0) - jnp.log(1.0 + jnp.exp(-jnp.abs(z)))) / GLA_TAU


def _gla_gate(proj, col_block, w_a2p, b_a2):
    M = proj.shape[0]
    N = w_a2p.shape[1]
    tm = _tile(M, 1024, 8)
    return pl.pallas_call(
        _gla_gate_kernel,
        grid=(M // tm,),
        in_specs=[pl.BlockSpec((tm, LANE), lambda i: (i, col_block)),
                  pl.BlockSpec(w_a2p.shape, lambda i: (0, 0)),
                  pl.BlockSpec((1, N), lambda i: (0, 0))],
        out_specs=pl.BlockSpec((tm, N), lambda i: (i, 0)),
        out_shape=jax.ShapeDtypeStruct((M, N), F32),
        compiler_params=_cparams(("parallel",)),
        name="gla_gate",
    )(proj, w_a2p, b_a2.reshape(1, N))


def _cumsum_rows(x):
    C = x.shape[0]
    row = lax.broadcasted_iota(jnp.int32, x.shape, 0)
    if C <= 8:
        out = jnp.zeros_like(x)
        for s in range(C):
            out = out + jnp.where(row >= s, x[s:s + 1], 0.0)
        return out
    sh = 1
    while sh < C:
        x = x + jnp.where(row >= sh, pltpu.roll(x, sh, 0), 0.0)
        sh *= 2
    return x


def _gla_kernel(q_ref, k_ref, v_ref, la_ref, s0_ref, o_ref, sf_ref, st_ref, *, n_heads, dk, dv, sub, qscale):
    c = pl.program_id(1)

    @pl.when(c == 0)
    def _():
        for h in range(n_heads):
            st_ref[h] = s0_ref[0, h].T

    C = q_ref.shape[1]
    nsub = C // sub
    for h in range(n_heads):
        q = q_ref[0, :, h * dk:(h + 1) * dk] * qscale
        k = k_ref[0, :, h * dk:(h + 1) * dk]
        v = v_ref[0, :, h * dv:(h + 1) * dv]
        b = _cumsum_rows(la_ref[0, :, h * dk:(h + 1) * dk])
        st = st_ref[h]
        o_inter = _dot_nt(q * jnp.exp(b), st)
        b_last = b[C - 1:C]
        k_dec = k * jnp.exp(b_last - b)
        st_ref[h] = jnp.exp(b_last) * st + _dot_tn(v, k_dec)
        outs = []
        for i in range(nsub):
            r0 = i * sub
            b_i = b[r0:r0 + sub]
            q_i = q[r0:r0 + sub]
            k_i = k[r0:r0 + sub]
            v_i = v[r0:r0 + sub]
            o_i = o_inter[r0:r0 + sub]
            if i > 0:
                ref_row = b[r0:r0 + 1]
                att = _dot_nt(q_i * jnp.exp(b_i - ref_row), k[:r0] * jnp.exp(ref_row - b[:r0]))
                o_i = o_i + _dot(att, v[:r0])
            t_loc = lax.broadcasted_iota(jnp.int32, (sub, 1), 0)
            for s in range(sub):
                w = jnp.exp(jnp.minimum(b_i - b_i[s:s + 1], 0.0))
                col = jnp.sum(q_i * w * k_i[s:s + 1], axis=-1, keepdims=True)
                o_i = o_i + jnp.where(t_loc >= s, col, 0.0) * v_i[s:s + 1]
            outs.append(o_i)
        o_ref[0, :, h * dv:(h + 1) * dv] = outs[0] if nsub == 1 else jnp.concatenate(outs, axis=0)

    @pl.when(c == pl.num_programs(1) - 1)
    def _():
        for h in range(n_heads):
            sf_ref[0, h] = st_ref[h].T


def _gla_recurrence(proj, log_a, s0, B, T, n_heads, dk, dv):
    C = 64 if T % 64 == 0 else T
    sub = min(16, C)
    nc = T // C
    Np = proj.shape[1]
    p3 = proj.reshape(B * nc, C, Np)
    la3 = log_a.reshape(B * nc, C, n_heads * dk)
    hk, hv = n_heads * dk, n_heads * dv
    assert hv % hk == 0
    kern = functools.partial(_gla_kernel, n_heads=n_heads, dk=dk, dv=dv, sub=sub, qscale=dk ** -0.5)
    o, sf = pl.pallas_call(
        kern,
        grid=(B, nc),
        in_specs=[pl.BlockSpec((1, C, hk), lambda b, c: (b * nc + c, 0, 0)),
                  pl.BlockSpec((1, C, hk), lambda b, c: (b * nc + c, 0, 1)),
                  pl.BlockSpec((1, C, hv), lambda b, c: (b * nc + c, 0, 2 * hk // hv)),
                  pl.BlockSpec((1, C, hk), lambda b, c: (b * nc + c, 0, 0)),
                  pl.BlockSpec((1, n_heads, dk, dv), lambda b, c: (b, 0, 0, 0))],
        out_specs=[pl.BlockSpec((1, C, hv), lambda b, c: (b * nc + c, 0, 0)),
                   pl.BlockSpec((1, n_heads, dk, dv), lambda b, c: (b, 0, 0, 0))],
        out_shape=[jax.ShapeDtypeStruct((B * nc, C, hv), F32),
                   jax.ShapeDtypeStruct((B, n_heads, dk, dv), F32)],
        scratch_shapes=[pltpu.VMEM((n_heads, dv, dk), F32)],
        compiler_params=_cparams(("parallel", "arbitrary")),
        name="gla_recurrence",
    )(p3, p3, p3, la3, s0)
    return o.reshape(B * T, hv), sf


def _gla_out_kernel(o_ref, r_ref, g_ref, wo_ref, x_ref, gt_ref, y_ref, *, n_heads, dv):
    parts = []
    for h in range(n_heads):
        sl = slice(h * dv, (h + 1) * dv)
        parts.append((_rms(o_ref[:, sl], g_ref[...]) * _silu(r_ref[:, sl])).astype(CDT))
    y = jnp.dot(jnp.concatenate(parts, axis=1), wo_ref[...], preferred_element_type=F32)
    y_ref[...] = x_ref[...] + gt_ref[0] * y


def _gla_out(o, proj, r_block, g_o, wo, x, gt, rows_per_group, n_heads, dv):
    M, D = x.shape
    hv = n_heads * dv
    tm = _tile(M, 512, 8)
    tpg = max(rows_per_group // tm, 1)
    kern = functools.partial(_gla_out_kernel, n_heads=n_heads, dv=dv)
    return pl.pallas_call(
        kern,
        grid=(M // tm,),
        in_specs=[pl.BlockSpec((tm, hv), lambda i: (i, 0)),
                  pl.BlockSpec((tm, hv), lambda i: (i, r_block)),
                  pl.BlockSpec((1, dv), lambda i: (0, 0)),
                  pl.BlockSpec(wo.shape, lambda i: (0, 0)),
                  pl.BlockSpec((tm, D), lambda i: (i, 0)),
                  pl.BlockSpec((1, gt.shape[1], D), lambda i: (i // tpg, 0, 0))],
        out_specs=pl.BlockSpec((tm, D), lambda i: (i, 0)),
        out_shape=jax.ShapeDtypeStruct((M, D), F32),
        compiler_params=_cparams(("parallel",)),
        name="gla_out",
    )(o, proj, g_o.reshape(1, dv), wo, x, gt)


def _order_key(score):
    score = jnp.where(score == 0.0, 0.0, score)
    bits = pltpu.bitcast(score, jnp.int32)
    return jnp.where(bits < 0, bits ^ jnp.int32(0x7FFFFFFF), bits)


def _kth_largest_key(key, topk, axes):
    shape = tuple(1 if a in axes else s for a, s in enumerate(key.shape))

    def body(it, t):
        cand = t + lax.shift_left(jnp.int32(1), jnp.int32(31) - it)
        cnt = jnp.sum(jnp.where(key >= cand, 1, 0), axis=axes, keepdims=True)
        return jnp.where(cnt >= topk, cand, t)

    return lax.fori_loop(0, 32, body, jnp.full(shape, INT_MIN, jnp.int32))


def _bias_chain(dist, value_of_bucket):
    val = value_of_bucket(0)
    for j in range(1, N_BUCKETS):
        val = jnp.where(dist >= BUCKET_LO[j], value_of_bucket(j), val)
    return val


def _strict_upper(n):
    a = lax.broadcasted_iota(jnp.int32, (n, n), 0)
    b = lax.broadcasted_iota(jnp.int32, (n, n), 1)
    return jnp.where(a < b, 1.0, 0.0).astype(CDT)


def _dsa_prompt_kernel(rb_ref, q_ref, qi_ref, wq_ref, k_ref, v_ref, kk_ref, o_ref, sel_ref, band_ref, u2_ref, p2_ref,
                       pm_ref, ps_ref, *,
                       n_heads, n_kv, hd, n_idx, di, topk, tq, T, n_cls):
    b_id = pl.program_id(0)
    qt = pl.program_id(1)
    group = n_heads // n_kv

    @pl.when((b_id == 0) & (qt == 0))
    def _():
        i = lax.broadcasted_iota(jnp.int32, (tq, LANE), 0)
        j = lax.broadcasted_iota(jnp.int32, (tq, LANE), 1)
        for h in range(n_heads):
            far = rb_ref[N_BUCKETS - 1, h]
            for w in range(tq // LANE + 1):
                dist = i - j + (w - (tq // LANE - 1)) * LANE
                band_ref[w, h] = (_bias_chain(dist, lambda bk: rb_ref[bk, h]) - far) * (hd ** 0.5)

    def body(nk):
        row_pos = qt * tq + lax.broadcasted_iota(jnp.int32, (tq, nk), 0)
        col_pos = lax.broadcasted_iota(jnp.int32, (tq, nk), 1)
        causal = col_pos <= row_pos

        ki = kk_ref[:nk, :di]
        wi = wq_ref[:, di:di + n_idx] * (n_idx ** -0.5 * di ** -0.5)
        score = jnp.zeros((tq, nk), F32)
        for h in range(n_idx):
            lg = _dot_nt(qi_ref[:, h * di:(h + 1) * di], ki)
            score = score + wi[:, h:h + 1] * jnp.maximum(lg, 0.0)
        score = jnp.where(causal, score, NEG_INF)
        key = _order_key(score)
        thr = _kth_largest_key(key, topk, (1,))
        gt = key > thr
        eq = key == thr
        n_gt = jnp.sum(jnp.where(gt, 1, 0), axis=1, keepdims=True)
        n_eq = jnp.sum(jnp.where(eq, 1, 0), axis=1, keepdims=True)
        need = topk - n_gt
        tie = jnp.max(n_eq - need) > 0

        @pl.when(jnp.logical_not(tie))
        def _():
            sel_ref[:, :nk] = jnp.where((gt | eq) & causal, 0.0, NEG_INF)

        @pl.when(tie)
        def _():
            upper = _strict_upper(LANE)
            run = jnp.zeros((tq, 1), F32)
            needf = need.astype(F32)
            for kb in range(nk // LANE):
                sl = slice(kb * LANE, (kb + 1) * LANE)
                eqb = jnp.where(eq[:, sl], 1.0, 0.0)
                pre = jnp.dot(eqb.astype(CDT), upper, preferred_element_type=F32) + run
                keep = gt[:, sl] | (eq[:, sl] & (pre < needf))
                sel_ref[:, sl] = jnp.where(keep & causal[:, sl], 0.0, NEG_INF)
                run = run + jnp.sum(eqb, axis=1, keepdims=True)

        scale = hd ** -0.5
        nband = tq // LANE + 1
        nkb = nk // LANE
        first_near = max(nkb - (T // n_cls) // LANE - tq // LANE, 0)
        c = scale * math.log2(math.e)
        rc = 8 * math.gcd(tq // 8, 1 << (max(32 * LANE // nk, 1).bit_length() - 1))
        tiles = lambda a: [a[:, t * LANE:(t + 1) * LANE] for t in range(a.shape[1] // LANE)]
        for g in range(n_kv):
            kg = k_ref[:nk, g * hd:(g + 1) * hd].astype(CDT)
            vg = v_ref[:nk, g * hd:(g + 1) * hd].astype(CDT)
            for r in range(group):
                h = g * group + r
                b = h % 2
                u_ref, p_ref = u2_ref.at[b], p2_ref.at[b]
                u_ref[:, :nk] = _dot_nt(q_ref[:, h * hd:(h + 1) * hd], kg) + sel_ref[:, :nk]
                for kb in range(first_near, nkb):
                    w = qt * (tq // LANE) - kb + (tq // LANE - 1)
                    add = jnp.zeros((tq, LANE), F32)
                    for wv in range(nband):
                        add = jnp.where(w == wv, band_ref[wv, h], add)
                    u_ref[:, kb * LANE:(kb + 1) * LANE] += add
                for rr in range(tq // rc):
                    rows = slice(rr * rc, (rr + 1) * rc)
                    pm_ref[b, rows, :] = functools.reduce(jnp.maximum, tiles(u_ref[rows, :nk]))
                pm_ref[b] = jnp.broadcast_to(jnp.max(pm_ref[b], axis=-1, keepdims=True), (tq, LANE))
                for rr in range(tq // rc):
                    rows = slice(rr * rc, (rr + 1) * rc)
                    mb = pm_ref[b, rows, :]
                    p = [jnp.exp2(c * (ut - mb)) for ut in tiles(u_ref[rows, :nk])]
                    ps_ref[b, rows, :] = functools.reduce(jnp.add, p)
                    p_ref[rows, :nk] = jnp.concatenate(p, axis=1).astype(CDT)
                l = jnp.broadcast_to(jnp.sum(ps_ref[b], axis=-1, keepdims=True), (tq, LANE))
                o = jnp.dot(p_ref[:, :nk], vg, preferred_element_type=F32)
                o_ref[:, h * hd:(h + 1) * hd] = (o / jnp.concatenate([l] * (hd // LANE), axis=1)).astype(o_ref.dtype)

    width = T // n_cls
    cls = ((qt + 1) * tq - 1) // width
    for c in range(n_cls):
        pl.when(cls == c)(functools.partial(body, (c + 1) * width))


def _dsa_prompt(proj, proj_c, rel_bias, B, T, dims, topk):
    n_heads, n_kv, hd, n_idx, di = dims
    M, Np = proj.shape
    tq = math.gcd(T, DSA_TQ)
    assert tq % LANE == 0
    nqt = T // tq
    hq, hkv, hi = n_heads * hd, n_kv * hd, n_idx * di
    assert hq % hkv == 0 and (hq + 2 * hkv) % hi == 0 and (hq + 2 * hkv + hi) % LANE == 0
    kk_blk = (hq + 2 * hkv + hi) // LANE
    n_cls = math.gcd(T // LANE, DSA_EXTENT_CLASSES)
    kern = functools.partial(_dsa_prompt_kernel, n_heads=n_heads, n_kv=n_kv, hd=hd, n_idx=n_idx, di=di, topk=topk,
                             tq=tq, T=T, n_cls=n_cls)
    return pl.pallas_call(
        kern,
        grid=(B, nqt),
        in_specs=[pl.BlockSpec(memory_space=pltpu.SMEM),
                  pl.BlockSpec((tq, hq), lambda b, t: (b * nqt + t, 0)),
                  pl.BlockSpec((tq, hi), lambda b, t: (b * nqt + t, (hq + 2 * hkv) // hi)),
                  pl.BlockSpec((tq, LANE), lambda b, t: (b * nqt + t, kk_blk)),
                  pl.BlockSpec((T, hkv), lambda b, t: (b, hq // hkv)),
                  pl.BlockSpec((T, hkv), lambda b, t: (b, hq // hkv + 1)),
                  pl.BlockSpec((T, LANE), lambda b, t: (b, kk_blk))],
        out_specs=pl.BlockSpec((tq, hq), lambda b, t: (b * nqt + t, 0)),
        out_shape=jax.ShapeDtypeStruct((M, hq), CDT),
        scratch_shapes=[pltpu.VMEM((tq, T), F32),
                        pltpu.VMEM((tq // LANE + 1, n_heads, tq, LANE), F32),
                        pltpu.VMEM((2, tq, T), F32),
                        pltpu.VMEM((2, tq, T), CDT),
                        pltpu.VMEM((2, tq, LANE), F32),
                        pltpu.VMEM((2, tq, LANE), F32)],
        compiler_params=_cparams(("arbitrary", "arbitrary")),
        name="dsa_prompt",
    )(rel_bias, proj_c, proj_c, proj, proj_c, proj_c, proj_c)


def _dsa_select_kernel(pt_ref, qi_ref, wi_ref, kn_ref, *refs, n_pages_step, n_idx, di, topk, t_new):
    G = n_pages_step
    ki_refs = refs[:G]
    mask_ref, sc_ref = refs[G:]
    g = pl.program_id(1)
    NG = pl.num_programs(1)
    R = qi_ref.shape[1] // n_idx
    qi = qi_ref[0]
    wi = wi_ref[0] * (n_idx ** -0.5)

    def head_sum(lg):
        w = wi * jnp.maximum(lg * (di ** -0.5), 0.0)
        sc = w[0:R]
        for h in range(1, n_idx):
            sc = sc + w[h * R:(h + 1) * R]
        return sc

    past = head_sum(_dot(qi, jnp.concatenate([r[0, 0] for r in ki_refs], axis=1)))
    GW = past.shape[1]
    sc_ref[g] = past

    @pl.when(g == NG - 1)
    def _():
        new = head_sum(_dot_nt(qi, kn_ref[0][:, :di]))
        t_row = lax.broadcasted_iota(jnp.int32, new.shape, 0) % t_new
        col = lax.broadcasted_iota(jnp.int32, new.shape, 1)
        new = jnp.where((col <= t_row) & (col < t_new), new, NEG_INF)
        sc_ref[NG] = jnp.concatenate([new, jnp.full((R, GW - LANE), NEG_INF, F32)], axis=1)
        score = sc_ref[...]
        valid = score > NEG_INF
        key = _order_key(score)
        thr = _kth_largest_key(key, topk, (0, 2))
        gt = key > thr
        eq = key == thr
        n_gt = jnp.sum(jnp.where(gt, 1, 0), axis=(0, 2), keepdims=True)
        n_eq = jnp.sum(jnp.where(eq, 1, 0), axis=(0, 2), keepdims=True)
        need = topk - n_gt
        tie = jnp.max(n_eq - need) > 0

        @pl.when(jnp.logical_not(tie))
        def _():
            mask_ref[0] = jnp.where((gt | eq) & valid, 0.0, NEG_INF)

        @pl.when(tie)
        def _():
            upper = _strict_upper(LANE)
            needf = need[0].astype(F32)

            def blk(gi, run):
                k_g = _order_key(sc_ref[gi])
                v_g = sc_ref[gi] > NEG_INF
                for c in range(GW // LANE):
                    sl = slice(c * LANE, (c + 1) * LANE)
                    eqb = jnp.where(k_g[:, sl] == thr[0], 1.0, 0.0)
                    pre = jnp.dot(eqb.astype(CDT), upper, preferred_element_type=F32) + run
                    keep = (k_g[:, sl] > thr[0]) | ((eqb > 0.0) & (pre < needf))
                    mask_ref[0, gi, :, sl] = jnp.where(keep & v_g[:, sl], 0.0, NEG_INF)
                    run = run + jnp.sum(eqb, axis=1, keepdims=True)
                return run

            lax.fori_loop(0, NG + 1, blk, jnp.zeros((R, 1), F32))


def _dsa_select(qi_hm, wi_hm, kn, cache_ki_t, layer, page_table, B, n_idx, di, topk, t_new):
    n_pages = page_table.shape[1]
    page = cache_ki_t.shape[3]
    G = math.gcd(n_pages, PAGES_PER_STEP_SELECT)
    NG = n_pages // G
    GW = G * page
    R = qi_hm.shape[1] // n_idx
    kern = functools.partial(_dsa_select_kernel, n_pages_step=G, n_idx=n_idx, di=di, topk=topk, t_new=t_new)

    def page_map(i):
        return lambda b, g, pt: (layer, pt[b, g * G + i], 0, 0)

    in_specs = [pl.BlockSpec((1, n_idx * R, di), lambda b, g, pt: (b, 0, 0)),
                pl.BlockSpec((1, n_idx * R, 1), lambda b, g, pt: (b, 0, 0)),
                pl.BlockSpec((1, LANE, LANE), lambda b, g, pt: (b, 0, 0))]
    in_specs += [pl.BlockSpec((1, 1, di, page), page_map(i)) for i in range(G)]
    return pl.pallas_call(
        kern,
        grid_spec=pltpu.PrefetchScalarGridSpec(
            num_scalar_prefetch=1,
            grid=(B, NG),
            in_specs=in_specs,
            out_specs=pl.BlockSpec((1, NG + 1, R, GW), lambda b, g, pt: (b, 0, 0, 0)),
            scratch_shapes=[pltpu.VMEM((NG + 1, R, GW), F32)]),
        out_shape=jax.ShapeDtypeStruct((B, NG + 1, R, GW), F32),
        compiler_params=_cparams(("parallel", "arbitrary")),
        name="dsa_select",
    )(page_table, qi_hm, wi_hm, kn, *([cache_ki_t] * G))


def _dsa_decode_kernel(pt_ref, rb_ref, q_ref, kn_ref, vn_ref, mask_ref, mnew_ref, *refs,
                       n_pages_step, n_kv, group, hd, t_new, past_len):
    G = n_pages_step
    k_refs = refs[:G]
    v_refs = refs[G:2 * G]
    o_ref, m_ref, l_ref, acc_ref = refs[2 * G:]
    g = pl.program_id(1)
    NG = pl.num_programs(1)
    R = group * t_new
    scale = hd ** -0.5

    @pl.when(g == 0)
    def _():
        m_ref[...] = jnp.full(m_ref.shape, NEG_INF, F32)
        l_ref[...] = jnp.zeros(l_ref.shape, F32)
        acc_ref[...] = jnp.zeros(acc_ref.shape, F32)

    def head_val(kvh, bucket):
        row = lax.broadcasted_iota(jnp.int32, (R, 1), 0)
        val = jnp.full((R, 1), rb_ref[bucket, kvh * group], F32)
        for r in range(1, group):
            val = jnp.where(row >= r * t_new, rb_ref[bucket, kvh * group + r], val)
        return val

    def update(kvh, s, v):
        m_prev = m_ref[kvh]
        m_new = jnp.maximum(m_prev, jnp.max(s, axis=-1, keepdims=True))
        m_safe = jnp.where(m_new > NEG_INF, m_new, 0.0)
        alpha = jnp.exp(m_prev - m_safe)
        p = jnp.exp(s - m_safe)
        l_ref[kvh] = alpha * l_ref[kvh] + jnp.sum(p, axis=-1, keepdims=True)
        acc_ref[kvh] = alpha * acc_ref[kvh] + _dot(p, v)
        m_ref[kvh] = m_new

    page = k_refs[0].shape[2] // n_kv
    GW = G * page
    mask = mask_ref[0, 0]
    t_row = lax.broadcasted_iota(jnp.int32, (R, GW), 0) % t_new
    col = lax.broadcasted_iota(jnp.int32, (R, GW), 1)
    dist = past_len + t_row - (g * GW + col)
    near = g == NG - 1

    def head_rows(refs_, kvh):
        return jnp.concatenate([r[0, 0, pl.ds(kvh, page, stride=n_kv), :] for r in refs_], axis=0).astype(CDT)

    for kvh in range(n_kv):
        s = _dot_nt(q_ref[0, kvh], head_rows(k_refs, kvh)) * scale
        bias = lax.cond(near,
                        lambda: _bias_chain(dist, functools.partial(head_val, kvh)),
                        lambda: jnp.broadcast_to(head_val(kvh, N_BUCKETS - 1), (R, GW)))
        update(kvh, s + bias + mask, head_rows(v_refs, kvh))

    @pl.when(near)
    def _():
        kn = kn_ref[0].astype(CDT)
        vn = vn_ref[0].astype(CDT)
        mnew = mnew_ref[0, 0][:, :LANE]
        t_r = lax.broadcasted_iota(jnp.int32, (R, LANE), 0) % t_new
        c = lax.broadcasted_iota(jnp.int32, (R, LANE), 1)
        d_new = jnp.maximum(t_r - c, 0)
        for kvh in range(n_kv):
            sl = slice(kvh * hd, (kvh + 1) * hd)
            s = _dot_nt(q_ref[0, kvh], kn[:, sl]) * scale
            bias = _bias_chain(d_new, functools.partial(head_val, kvh))
            update(kvh, s + bias + mnew, vn[:, sl])
            o_ref[0, kvh] = (acc_ref[kvh] / l_ref[kvh]).astype(o_ref.dtype)


def _dsa_decode(q4, kn, vn, mask, rel_bias, cache_k, cache_v, layer, page_table, B, n_kv, group, hd, t_new):
    n_pages = page_table.shape[1]
    page = cache_k.shape[2] // n_kv
    GWs = mask.shape[3]
    G = math.gcd(math.gcd(n_pages, PAGES_PER_STEP_DECODE), GWs // page)
    NG = n_pages // G
    GW = G * page
    ratio = GWs // GW
    R = group * t_new
    kern = functools.partial(_dsa_decode_kernel, n_pages_step=G, n_kv=n_kv, group=group, hd=hd, t_new=t_new,
                             past_len=n_pages * page)

    def page_map(i):
        return lambda b, g, pt: (layer, pt[b, g * G + i], 0, 0)

    in_specs = [pl.BlockSpec(memory_space=pltpu.SMEM),
                pl.BlockSpec((1, n_kv, R, hd), lambda b, g, pt: (b, 0, 0, 0)),
                pl.BlockSpec((1, LANE, n_kv * hd), lambda b, g, pt: (b, 0, 0)),
                pl.BlockSpec((1, LANE, n_kv * hd), lambda b, g, pt: (b, 0, 0)),
                pl.BlockSpec((1, 1, R, GW), lambda b, g, pt: (b, g // ratio, 0, g % ratio)),
                pl.BlockSpec((1, 1, R, GW), lambda b, g, pt: (b, mask.shape[1] - 1, 0, 0))]
    in_specs += [pl.BlockSpec((1, 1, page * n_kv, hd), page_map(i)) for i in range(G)]
    in_specs += [pl.BlockSpec((1, 1, page * n_kv, hd), page_map(i)) for i in range(G)]
    return pl.pallas_call(
        kern,
        grid_spec=pltpu.PrefetchScalarGridSpec(
            num_scalar_prefetch=1,
            grid=(B, NG),
            in_specs=in_specs,
            out_specs=pl.BlockSpec((1, n_kv, R, hd), lambda b, g, pt: (b, 0, 0, 0)),
            scratch_shapes=[pltpu.VMEM((n_kv, R, 1), F32), pltpu.VMEM((n_kv, R, 1), F32),
                            pltpu.VMEM((n_kv, R, hd), F32)]),
        out_shape=jax.ShapeDtypeStruct((B, n_kv, R, hd), CDT),
        compiler_params=_cparams(("parallel", "arbitrary")),
        name="dsa_decode",
    )(page_table, rel_bias, q4, kn, vn, mask, mask, *([cache_k] * G), *([cache_v] * G))


def _ffn_kernel(*refs, seq_tiles, t_seq, conv_w):
    if seq_tiles:
        (x_ref, g_ref, sc_ref, sh_ref, gt_ref, wg_ref, wv_ref, wd_ref, cw_ref, cb_ref,
         y_ref, tail_ref, h_ref, acc_ref, stash_ref) = refs
    else:
        (x_ref, g_ref, sc_ref, sh_ref, gt_ref, wg_ref, wv_ref, wd_ref, cw_ref, cb_ref, p1_ref, p2_ref,
         y_ref, tail_ref, h_ref, acc_ref) = refs
    i = pl.program_id(0)
    j = pl.program_id(1)

    @pl.when(j == 0)
    def _():
        h = _rms(x_ref[...], g_ref[...]) * (1.0 + sc_ref[0]) + sh_ref[0]
        h_ref[...] = h.astype(CDT)
        acc_ref[...] = jnp.zeros(acc_ref.shape, F32)

    gate = jnp.dot(h_ref[...], wg_ref[...], preferred_element_type=F32)
    val = jnp.dot(h_ref[...], wv_ref[...], preferred_element_type=F32)
    tm = gate.shape[0]
    row = lax.broadcasted_iota(jnp.int32, gate.shape, 0)
    g1 = pltpu.roll(gate, 1, 0)
    g2 = pltpu.roll(gate, 2, 0)
    if seq_tiles:
        @pl.when((i == 0) & (j == 0))
        def _():
            stash_ref[...] = jnp.zeros(stash_ref.shape, F32)

        prev = jnp.where(i % seq_tiles == 0, 0.0, stash_ref[j])
        g1 = jnp.where(row == 0, prev[7:8], g1)
        g2 = jnp.where(row == 0, prev[6:7], jnp.where(row == 1, prev[7:8], g2))
        stash_ref[j] = gate[tm - 8:]
        tail_ref[0] = gate[tm - 8:]
    else:
        t = row % t_seq
        g1 = jnp.where(t == 0, p1_ref[...], g1)
        g2 = jnp.where(t < 2, p2_ref[...], g2)
        tail_ref[...] = gate
    cw = cw_ref[...]
    conv = cw[0:1] * g2 + cw[1:2] * g1 + cw[2:3] * gate + cb_ref[...]
    act = (_silu(conv) * val).astype(CDT)
    acc_ref[...] += jnp.dot(act, wd_ref[...], preferred_element_type=F32)

    @pl.when(j == pl.num_programs(1) - 1)
    def _():
        y_ref[...] = x_ref[...] + gt_ref[0] * acc_ref[...]


def _ffn(x, g, sc, sh, gt, w_up, w_down, conv_w, conv_b, rows_per_group, t_seq, prev=None):
    M, D = x.shape
    Fd = w_down.shape[0]
    assert conv_w.shape[0] == 3
    tf = _tile(Fd, FFN_TILE_F)
    nf = Fd // tf
    cw = jnp.pad(conv_w, ((0, 8 - conv_w.shape[0]), (0, 0)))
    cb = conv_b.reshape(1, Fd)
    seq_mode = prev is None
    if seq_mode:
        tm = _tile(t_seq, 512, 8)
        assert t_seq % tm == 0 and tm >= 8
        seq_tiles = t_seq // tm
    else:
        tm = M
        seq_tiles = 0
    nt = M // tm
    tpg = max(rows_per_group // tm, 1)
    mod = lambda m: pl.BlockSpec((1, m.shape[1], D), lambda i, j: (i // tpg, 0, 0))
    in_specs = [pl.BlockSpec((tm, D), lambda i, j: (i, 0)),
                pl.BlockSpec((1, D), lambda i, j: (0, 0)),
                mod(sc), mod(sh), mod(gt),
                pl.BlockSpec((D, tf), lambda i, j: (0, j)),
                pl.BlockSpec((D, tf), lambda i, j: (0, nf + j)),
                pl.BlockSpec((tf, D), lambda i, j: (j, 0)),
                pl.BlockSpec((8, tf), lambda i, j: (0, j)),
                pl.BlockSpec((1, tf), lambda i, j: (0, j))]
    args = [x, g.reshape(1, D), sc, sh, gt, w_up, w_up, w_down, cw, cb]
    scratch = [pltpu.VMEM((tm, D), CDT), pltpu.VMEM((tm, D), F32)]
    if seq_mode:
        tail_shape = jax.ShapeDtypeStruct((nt, 8, Fd), F32)
        tail_spec = pl.BlockSpec((1, 8, tf), lambda i, j: (i, 0, j))
        scratch.append(pltpu.VMEM((nf, 8, tf), F32))
    else:
        p1 = jnp.concatenate([prev[:, 1:2], jnp.zeros_like(prev[:, :1]).repeat(t_seq - 1, axis=1)], axis=1)
        p2 = jnp.concatenate([prev[:, 0:2], jnp.zeros_like(prev[:, :1]).repeat(t_seq - 2, axis=1)], axis=1)
        args += [p1.reshape(M, Fd), p2.reshape(M, Fd)]
        in_specs += [pl.BlockSpec((tm, tf), lambda i, j: (i, j))] * 2
        tail_shape = jax.ShapeDtypeStruct((M, Fd), F32)
        tail_spec = pl.BlockSpec((tm, tf), lambda i, j: (i, j))
    kern = functools.partial(_ffn_kernel, seq_tiles=seq_tiles, t_seq=t_seq, conv_w=conv_w.shape[0])
    return pl.pallas_call(
        kern,
        grid=(nt, nf),
        in_specs=in_specs,
        out_specs=[pl.BlockSpec((tm, D), lambda i, j: (i, 0)), tail_spec],
        out_shape=[jax.ShapeDtypeStruct((M, D), F32), tail_shape],
        scratch_shapes=scratch,
        compiler_params=_cparams(("arbitrary", "arbitrary")),
        name="ffn",
    )(*args)


def _final_norm_kernel(x_ref, g_ref, o_ref):
    o_ref[...] = _rms(x_ref[...], g_ref[...])


def _final_norm(x, g):
    M, D = x.shape
    tm = _tile(M, 1024, 8)
    return pl.pallas_call(
        _final_norm_kernel,
        grid=(M // tm,),
        in_specs=[pl.BlockSpec((tm, D), lambda i: (i, 0)), pl.BlockSpec((1, D), lambda i: (0, 0))],
        out_specs=pl.BlockSpec((tm, D), lambda i: (i, 0)),
        out_shape=jax.ShapeDtypeStruct((M, D), F32),
        compiler_params=_cparams(("parallel",)),
        name="final_norm",
    )(x, g.reshape(1, D))


def _rope_tables(pos, rope, n_heads, reps):
    half = rope // 2
    inv = ROPE_BASE ** (-jnp.arange(half, dtype=F32) / half)
    ang = pos.astype(F32)[:, None] * inv[None, :]
    cos = jnp.cos(ang)
    sin = jnp.sin(ang)
    cosk = jnp.concatenate([cos, cos], axis=1)
    sink = jnp.concatenate([-sin, sin], axis=1)
    tabs = (jnp.tile(cosk, (1, n_heads)), jnp.tile(sink, (1, n_heads)), cosk, sink)
    return tuple(jnp.tile(t, (reps, 1)) for t in tabs)


def _swap_halves(w, rope):
    half = rope // 2
    return jnp.concatenate([w[..., half:], w[..., :half]], axis=-1)


def _pad_cols(w, n):
    return jnp.pad(w, ((0, 0), (0, n - w.shape[1])))


def kernel(x_prompt, x_sample, cache_mla_ckv, cache_mla_kpe, state_gla, cache_dsa_k, cache_dsa_v, cache_dsa_kidx,
           state_ffn_conv, page_table, c_prompt, c_sample, ada_w, ada_b, norm1_g, norm2_g, final_g, mla_w_in,
           mla_g_q, mla_g_kv, mla_w_uq, mla_w_uk, mla_w_uv, mla_w_o, gla_w_in, gla_w_a2, gla_b_a2, gla_g_o,
           gla_w_o, dsa_w_in, dsa_w_o, rel_bias, ffn_w_up, ffn_conv_w, ffn_conv_b, ffn_w_down):
    Bp, Tp, D = x_prompt.shape
    Bs, Ts, _ = x_sample.shape
    depth = ada_w.shape[0]
    n_mod = ada_w.shape[2] // D
    n_pages, page = page_table.shape[1], cache_mla_ckv.shape[2]
    past_len = n_pages * page
    cache_kpe_t = jnp.swapaxes(cache_mla_kpe, 2, 3)
    cache_ki_t = jnp.swapaxes(cache_dsa_kidx, 2, 3)

    q_lora, mla_h, qk_dim = mla_w_uq.shape[1:]
    kv_lora, _, nope = mla_w_uk.shape[1:]
    mla_v = mla_w_uv.shape[3]
    rope = qk_dim - nope
    gla_h = state_gla.shape[2]
    gla_dk, gla_dv = state_gla.shape[3:]
    gla_rank = gla_w_a2.shape[1]
    dsa_kv, dsa_hd = cache_dsa_k.shape[3:]
    dsa_di = cache_dsa_kidx.shape[3]
    dsa_h = dsa_w_o.shape[1] // dsa_hd
    dsa_hi = (dsa_w_in.shape[2] - (dsa_h + 2 * dsa_kv) * dsa_hd - dsa_di) // (dsa_di + 1)
    assert kv_lora % LANE == 0 and q_lora % LANE == 0 and rope <= LANE

    mod_all = _ada_mod(jnp.concatenate([c_prompt, c_sample], axis=0), ada_w, ada_b)

    def mods(l, sample):
        m = mod_all[l, Bp:] if sample else mod_all[l, :Bp]
        parts = [m[:, i * D:(i + 1) * D] for i in range(n_mod)]
        if sample:
            return [jnp.repeat(p, Ts, axis=0).reshape(1, Bs * Ts, D) for p in parts]
        return [p.reshape(Bp, 1, D) for p in parts]

    mla_w = []
    for j in range(mla_w_in.shape[0]):
        w_in = mla_w_in[j]
        kcol = q_lora + kv_lora
        w_in_ext = jnp.concatenate([w_in, _swap_halves(w_in[:, kcol:kcol + rope], rope)], axis=1)
        w_in_ext = _pad_cols(w_in_ext, -(-w_in_ext.shape[1] // LANE) * LANE).astype(CDT)
        uq = mla_w_uq[j]
        wqn = uq[:, :, :nope].reshape(q_lora, mla_h * nope).astype(CDT)
        wqp = uq[:, :, nope:].reshape(q_lora, mla_h * rope).astype(CDT)
        wqs = _swap_halves(uq[:, :, nope:], rope).reshape(q_lora, mla_h * rope).astype(CDT)
        wukT = jnp.transpose(mla_w_uk[j], (1, 2, 0)).astype(CDT)
        wuv = jnp.transpose(mla_w_uv[j], (1, 0, 2)).astype(CDT)
        mla_w.append((w_in_ext, wqn, wqp, wqs, wukT, wuv, mla_w_o[j].astype(CDT)))
    gla_np = -(-gla_w_in.shape[2] // LANE) * LANE
    gla_w = []
    for j in range(gla_w_in.shape[0]):
        w_a2p = jnp.pad(gla_w_a2[j], ((0, LANE - gla_rank), (0, 0))).astype(CDT)
        gla_w.append((_pad_cols(gla_w_in[j], gla_np).astype(CDT), w_a2p, gla_w_o[j].astype(CDT)))
    dsa_np = -(-dsa_w_in.shape[2] // LANE) * LANE
    dsa_w = [(_pad_cols(dsa_w_in[j], dsa_np).astype(CDT), dsa_w_o[j].astype(CDT)) for j in range(dsa_w_in.shape[0])]
    ffn_up = ffn_w_up.astype(CDT)
    ffn_down = ffn_w_down.astype(CDT)

    mla_dims = (mla_h, q_lora, kv_lora, rope, nope)
    mla_scale = qk_dim ** -0.5

    def trunk(x3, sample):
        B, T, _ = x3.shape
        M = B * T
        x = x3.reshape(M, D)
        rpg = M if sample else T
        pos = (past_len if sample else 0) + jnp.arange(T, dtype=jnp.int32)
        tabs = _rope_tables(pos, rope, mla_h, B if sample else 1)
        outs = dict(mla_ckv=[], mla_kpe=[], gla=[], dsa_k=[], dsa_v=[], dsa_ki=[], conv=[])
        for l in range(depth):
            sh1, sc1, gt1, sh2, sc2, gt2 = mods(l, sample)
            j = l // 3
            if l % 3 == 0:
                w_in_ext, wqn, wqp, wqs, wukT, wuv, wo = mla_w[j]
                p = _nm_linear(x, norm1_g[l], sc1, sh1, w_in_ext, rpg)
                ckv, kpe, kcat, qcat = _mla_prep(p, mla_g_q[j], mla_g_kv[j], tabs, wqn, wqp, wqs, wukT, mla_dims)
                if sample:
                    o_lat = _mla_decode(qcat, kcat, cache_mla_ckv, cache_kpe_t, j, page_table, B, T, mla_h,
                                        kv_lora, rope, mla_scale)
                else:
                    o_lat = _mla_flash(qcat, kcat, B, T, mla_h, kv_lora, mla_scale)
                x = _mla_out(o_lat, wuv, wo, x, gt1, rpg, mla_h, kv_lora)
                outs["mla_ckv"].append(ckv.reshape(B, T, kv_lora))
                outs["mla_kpe"].append(kpe.reshape(B, T, rope))
            elif l % 3 == 1:
                w_in_p, w_a2p, wo = gla_w[j]
                hk, hv = gla_h * gla_dk, gla_h * gla_dv
                p = _nm_linear(x, norm1_g[l], sc1, sh1, w_in_p, rpg, tn_target=NM_LINEAR_TN)
                log_a = _gla_gate(p, (2 * hk + 2 * hv) // LANE, w_a2p, gla_b_a2[j])
                s0 = state_gla[j] if sample else jnp.zeros((B, gla_h, gla_dk, gla_dv), F32)
                o, s_fin = _gla_recurrence(p, log_a, s0, B, T, gla_h, gla_dk, gla_dv)
                x = _gla_out(o, p, (2 * hk + hv) // hv, gla_g_o[j], wo, x, gt1, rpg, gla_h, gla_dv)
                outs["gla"].append(s_fin)
            else:
                w_in_p, wo = dsa_w[j]
                hq, hkv, hi = dsa_h * dsa_hd, dsa_kv * dsa_hd, dsa_hi * dsa_di
                p = _nm_linear(x, norm1_g[l], sc1, sh1, w_in_p, rpg, tn_target=NM_LINEAR_TN, with_cdt_copy=not sample)
                if not sample:
                    p, p_c = p
                k_new = p[:, hq:hq + hkv]
                v_new = p[:, hq + hkv:hq + 2 * hkv]
                kk = p[:, hq + 2 * hkv + hi:]
                L_keys = (past_len if sample else 0) + T
                topk = min(DSA_TOPK, L_keys // 4)
                if sample:
                    group = dsa_h // dsa_kv
                    dup = lambda a: jnp.concatenate([a.reshape(B, T, dsa_hi, -1)] * group, axis=1)
                    qi_hm = dup(p[:, hq + 2 * hkv:hq + 2 * hkv + hi]).transpose(0, 2, 1, 3)
                    qi_hm = qi_hm.reshape(B, dsa_hi * group * T, dsa_di)
                    wi_hm = dup(kk[:, dsa_di:dsa_di + dsa_hi]).transpose(0, 2, 1, 3).reshape(B, dsa_hi * group * T, 1)
                    padn = lambda a: jnp.pad(a.reshape(B, T, -1), ((0, 0), (0, LANE - T), (0, 0)))
                    mask = _dsa_select(qi_hm, wi_hm, padn(kk), cache_ki_t, j, page_table, B, dsa_hi, dsa_di, topk, T)
                    q4 = p[:, :hq].reshape(B, T, dsa_kv, group, dsa_hd).transpose(0, 2, 3, 1, 4)
                    q4 = q4.reshape(B, dsa_kv, group * T, dsa_hd)
                    ck = cache_dsa_k.reshape(*cache_dsa_k.shape[:2], page * dsa_kv, dsa_hd)
                    cv = cache_dsa_v.reshape(*cache_dsa_v.shape[:2], page * dsa_kv, dsa_hd)
                    o4 = _dsa_decode(q4, padn(k_new), padn(v_new), mask, rel_bias, ck, cv, j, page_table, B,
                                     dsa_kv, group, dsa_hd, T)
                    o = o4.reshape(B, dsa_kv, group, T, dsa_hd).transpose(0, 3, 1, 2, 4).reshape(M, hq)
                else:
                    o = _dsa_prompt(p, p_c, rel_bias, B, T, (dsa_h, dsa_kv, dsa_hd, dsa_hi, dsa_di), topk)
                x = _proj_res(o, wo, x, gt1, rpg)
                outs["dsa_k"].append(k_new.reshape(B, T, dsa_kv, dsa_hd))
                outs["dsa_v"].append(v_new.reshape(B, T, dsa_kv, dsa_hd))
                outs["dsa_ki"].append(kk[:, :dsa_di].reshape(B, T, dsa_di))
            if sample:
                x, tail = _ffn(x, norm2_g[l], sc2, sh2, gt2, ffn_up[l], ffn_down[l], ffn_conv_w[l], ffn_conv_b[l],
                               rpg, T, prev=state_ffn_conv[l])
                outs["conv"].append(tail.reshape(B, T, -1)[:, T - 2:])
            else:
                x, tail = _ffn(x, norm2_g[l], sc2, sh2, gt2, ffn_up[l], ffn_down[l], ffn_conv_w[l], ffn_conv_b[l],
                               rpg, T)
                nt = tail.shape[0] // B
                outs["conv"].append(tail.reshape(B, nt, 8, -1)[:, nt - 1, 6:8])
        y = _final_norm(x, final_g).reshape(B, T, D)
        return (y, jnp.stack(outs["mla_ckv"]), jnp.stack(outs["mla_kpe"]), jnp.stack(outs["gla"]),
                jnp.stack(outs["dsa_k"]), jnp.stack(outs["dsa_v"]), jnp.stack(outs["dsa_ki"]),
                jnp.stack(outs["conv"]))

    rp = trunk(x_prompt, False)
    rs = trunk(x_sample, True)
    return (rp[0], rs[0]) + tuple(rp[1:]) + tuple(rs[1:])
```

```python
import functools
import math

import numpy as np
import jax
import jax.numpy as jnp
from jax import lax
from jax.experimental import pallas as pl
from jax.experimental.pallas import tpu as pltpu

F32 = jnp.float32
CDT = jnp.bfloat16
EPS = 1e-6
ROPE_BASE = 10000.0
GLA_TAU = 16.0
N_BUCKETS = 32
MAX_DISTANCE = 128
DSA_TOPK = 256
LANE = 128
NEG_INF = float("-inf")
INT_MIN = -2 ** 31
VMEM_LIMIT = 56 * 1024 * 1024
PAGES_PER_STEP_MLA = 32
PAGES_PER_STEP_SELECT = 32
PAGES_PER_STEP_DECODE = 16
FFN_TILE_F = 1408
FFN_SUB_F = 256
DSA_EXTENT_CLASSES = 4
DSA_TQ = 128
NM_LINEAR_TN = 4096
FLASH_TQ = 512
FLASH_TK = 512
FLASH_ROW_CHUNK = 64


def _bucket_thresholds():
    d = np.arange(0, 4 * MAX_DISTANCE)
    exact = N_BUCKETS // 2
    lr = np.log(np.maximum(d, 1).astype(np.float32) / np.float32(exact)) / np.float32(math.log(MAX_DISTANCE / exact))
    large = np.minimum(exact + (lr * np.float32(N_BUCKETS - exact)).astype(np.int32), N_BUCKETS - 1)
    b = np.where(d < exact, d, large)
    return [int(np.argmax(b >= j)) for j in range(N_BUCKETS)]


BUCKET_LO = _bucket_thresholds()


def _cparams(sem, vmem=VMEM_LIMIT):
    return pltpu.CompilerParams(dimension_semantics=sem, vmem_limit_bytes=vmem)


def _tile(n, target, mult=LANE):
    if n <= target:
        return n
    t = (target // mult) * mult
    while t > mult and n % t:
        t -= mult
    assert n % t == 0, (n, target)
    return t


def _dot(a, b):
    return jnp.dot(a.astype(CDT), b.astype(CDT), preferred_element_type=F32)


def _dot_nt(a, b):
    return lax.dot_general(a.astype(CDT), b.astype(CDT), (((1,), (1,)), ((), ())), preferred_element_type=F32)


def _dot_tn(a, b):
    return lax.dot_general(a.astype(CDT), b.astype(CDT), (((0,), (0,)), ((), ())), preferred_element_type=F32)


def _rms(x, g):
    return x * lax.rsqrt(jnp.mean(x * x, axis=-1, keepdims=True) + EPS) * g


def _silu(x):
    return x * jax.nn.sigmoid(x)


def _ada_kernel(c_ref, w_ref, b_ref, o_ref):
    ca = _silu(c_ref[...])
    o_ref[0] = _dot(ca, w_ref[0]) + b_ref[0]


def _ada_mod(c, ada_w, ada_b):
    L, D, N = ada_w.shape
    R = c.shape[0]
    tn = _tile(N, 1536)
    return pl.pallas_call(
        _ada_kernel,
        grid=(L, N // tn),
        in_specs=[pl.BlockSpec((R, D), lambda l, j: (0, 0)),
                  pl.BlockSpec((1, D, tn), lambda l, j: (l, 0, j)),
                  pl.BlockSpec((1, 1, tn), lambda l, j: (l, 0, j))],
        out_specs=pl.BlockSpec((1, R, tn), lambda l, j: (l, 0, j)),
        out_shape=jax.ShapeDtypeStruct((L, R, N), F32),
        compiler_params=_cparams(("parallel", "parallel")),
        name="ada_mod",
    )(c, ada_w, ada_b.reshape(L, 1, N))


def _nm_linear_kernel(x_ref, g_ref, sc_ref, sh_ref, w_ref, o_ref, *rest):
    h_ref = rest[-1]

    @pl.when(pl.program_id(1) == 0)
    def _():
        h = _rms(x_ref[...], g_ref[...]) * (1.0 + sc_ref[0]) + sh_ref[0]
        h_ref[...] = h.astype(CDT)

    o = jnp.dot(h_ref[...], w_ref[...], preferred_element_type=F32)
    o_ref[...] = o
    if len(rest) == 2:
        rest[0][...] = o.astype(CDT)


def _mod_spec(mod, tm, rows_per_group):
    G, R, D = mod.shape
    tpg = max(rows_per_group // tm, 1)
    return pl.BlockSpec((1, R, D), lambda i, j: (i // tpg, 0, 0))


def _nm_linear(x, g, sc, sh, w, rows_per_group, tm_target=512, tn_target=1024, with_cdt_copy=False):
    M, D = x.shape
    N = w.shape[1]
    tm = _tile(M, tm_target, 8)
    tn = _tile(N, tn_target)
    out_spec = pl.BlockSpec((tm, tn), lambda i, j: (i, j))
    out_shape = jax.ShapeDtypeStruct((M, N), F32)
    if with_cdt_copy:
        out_spec = [out_spec, out_spec]
        out_shape = [out_shape, jax.ShapeDtypeStruct((M, N), CDT)]
    return pl.pallas_call(
        _nm_linear_kernel,
        grid=(M // tm, N // tn),
        in_specs=[pl.BlockSpec((tm, D), lambda i, j: (i, 0)),
                  pl.BlockSpec((1, D), lambda i, j: (0, 0)),
                  _mod_spec(sc, tm, rows_per_group),
                  _mod_spec(sh, tm, rows_per_group),
                  pl.BlockSpec((D, tn), lambda i, j: (0, j))],
        out_specs=out_spec,
        out_shape=out_shape,
        scratch_shapes=[pltpu.VMEM((tm, D), CDT)],
        compiler_params=_cparams(("parallel", "arbitrary")),
        name="nm_linear",
    )(x, g.reshape(1, D), sc, sh, w)


def _mla_prep_kernel(p_ref, gq_ref, gkv_ref, cq_ref, sq_ref, ck_ref, sk_ref, wqn_ref, wqp_ref, wqs_ref, wuk_ref,
                     ckv_ref, kpe_ref, kcat_ref, qcat_ref, *, n_heads, q_lora, kv_lora, rope, nope):
    p = p_ref[...]
    cq = p[:, :q_lora]
    ckv = p[:, q_lora:q_lora + kv_lora]
    kpe = p[:, q_lora + kv_lora:q_lora + kv_lora + rope]
    kpe_sw = p[:, q_lora + kv_lora + rope:q_lora + kv_lora + 2 * rope]
    cqn = _rms(cq, gq_ref[...]).astype(CDT)
    q_nope = jnp.dot(cqn, wqn_ref[...], preferred_element_type=F32)
    q_pe = (jnp.dot(cqn, wqp_ref[...], preferred_element_type=F32) * cq_ref[...]
            + jnp.dot(cqn, wqs_ref[...], preferred_element_type=F32) * sq_ref[...])
    ckv_n = _rms(ckv, gkv_ref[...])
    kpe_r = kpe * ck_ref[...] + kpe_sw * sk_ref[...]
    ckv_ref[...] = ckv_n
    kpe_ref[...] = kpe_r
    tm = p.shape[0]
    hd = kv_lora + LANE
    pad = jnp.zeros((tm, LANE - rope), CDT)
    kcat_ref[...] = jnp.concatenate([ckv_n.astype(CDT), kpe_r.astype(CDT), pad], axis=1)
    for h in range(n_heads):
        q_lat = _dot(q_nope[:, h * nope:(h + 1) * nope], wuk_ref[h])
        qcat_ref[:, h * hd:(h + 1) * hd] = jnp.concatenate(
            [q_lat.astype(CDT), q_pe[:, h * rope:(h + 1) * rope].astype(CDT), pad], axis=1)


def _mla_prep(p, g_q, g_kv, tabs, wqn, wqp, wqs, wukT, dims):
    n_heads, q_lora, kv_lora, rope, nope = dims
    M, NP = p.shape
    cosq, sinq, cosk, sink = tabs
    tm = _tile(M, 256, 8)
    ntab = cosq.shape[0] // tm
    hd = kv_lora + LANE
    row = lambda i: (i, 0)
    tab = lambda i: (i % ntab, 0)
    full2 = lambda i: (0, 0)
    full3 = lambda i: (0, 0, 0)
    kern = functools.partial(_mla_prep_kernel, n_heads=n_heads, q_lora=q_lora, kv_lora=kv_lora, rope=rope, nope=nope)
    return pl.pallas_call(
        kern,
        grid=(M // tm,),
        in_specs=[pl.BlockSpec((tm, NP), row),
                  pl.BlockSpec((1, q_lora), full2),
                  pl.BlockSpec((1, kv_lora), full2),
                  pl.BlockSpec((tm, n_heads * rope), tab),
                  pl.BlockSpec((tm, n_heads * rope), tab),
                  pl.BlockSpec((tm, rope), tab),
                  pl.BlockSpec((tm, rope), tab),
                  pl.BlockSpec(wqn.shape, full2),
                  pl.BlockSpec(wqp.shape, full2),
                  pl.BlockSpec(wqs.shape, full2),
                  pl.BlockSpec(wukT.shape, full3)],
        out_specs=[pl.BlockSpec((tm, kv_lora), row),
                   pl.BlockSpec((tm, rope), row),
                   pl.BlockSpec((tm, hd), row),
                   pl.BlockSpec((tm, n_heads * hd), row)],
        out_shape=[jax.ShapeDtypeStruct((M, kv_lora), F32),
                   jax.ShapeDtypeStruct((M, rope), F32),
                   jax.ShapeDtypeStruct((M, hd), CDT),
                   jax.ShapeDtypeStruct((M, n_heads * hd), CDT)],
        compiler_params=_cparams(("parallel",)),
        name="mla_prep",
    )(p, g_q.reshape(1, -1), g_kv.reshape(1, -1), cosq, sinq, cosk, sink, wqn, wqp, wqs, wukT)


def _mla_flash_kernel(q_ref, k_ref, o_ref, m_ref, l_ref, acc_ref, s2_ref, p2_ref, pm_ref, ps_ref, *,
                      n_heads, tq, tk, hd, dv, scale, rc):
    qi = pl.program_id(1)
    ki = pl.program_id(2)

    @pl.when(ki == 0)
    def _():
        m_ref[...] = jnp.full(m_ref.shape, NEG_INF, F32)
        l_ref[...] = jnp.zeros(l_ref.shape, F32)
        acc_ref[...] = jnp.zeros(acc_ref.shape, F32)

    c = scale * math.log2(math.e)

    def block(masked):
        k = k_ref[...]
        v = k[:, :dv]
        tiles = lambda a: [a[:, t * LANE:(t + 1) * LANE] for t in range(a.shape[1] // LANE)]

        def chunk_logits(s_ref, r):
            s = s_ref[r * rc:(r + 1) * rc, :]
            if masked:
                row = qi * tq + r * rc + lax.broadcasted_iota(jnp.int32, (rc, tk), 0)
                col = ki * tk + lax.broadcasted_iota(jnp.int32, (rc, tk), 1)
                s = jnp.where(col <= row, s, NEG_INF)
            return s

        for h in range(n_heads):
            b = h % 2
            s_ref, p_ref = s2_ref.at[b], p2_ref.at[b]
            s_ref[...] = _dot_nt(q_ref[:, h * hd:(h + 1) * hd], k)
            for r in range(tq // rc):
                pm_ref[b, r * rc:(r + 1) * rc, :] = functools.reduce(jnp.maximum, tiles(chunk_logits(s_ref, r)))
            m_prev = m_ref[h]
            m_new = jnp.maximum(m_prev, jnp.broadcast_to(jnp.max(pm_ref[b], axis=-1, keepdims=True), (tq, LANE)))
            alpha = jnp.exp2(c * (m_prev - m_new))
            m_ref[h] = m_new
            pm_ref[b] = m_new
            for r in range(tq // rc):
                mb = pm_ref[b, r * rc:(r + 1) * rc, :]
                p = [jnp.exp2(c * (st - mb)) for st in tiles(chunk_logits(s_ref, r))]
                ps_ref[b, r * rc:(r + 1) * rc, :] = functools.reduce(jnp.add, p)
                p_ref[r * rc:(r + 1) * rc, :] = jnp.concatenate(p, axis=1).astype(CDT)
            l_blk = jnp.broadcast_to(jnp.sum(ps_ref[b], axis=-1, keepdims=True), (tq, LANE))
            l_ref[h] = alpha * l_ref[h] + l_blk
            pv = jnp.dot(p_ref[...], v, preferred_element_type=F32)
            acc_ref[h] = jnp.concatenate([alpha] * (dv // LANE), axis=1) * acc_ref[h] + pv

    below_diag = ki * tk + tk - 1 <= qi * tq
    pl.when(below_diag)(functools.partial(block, False))
    pl.when(jnp.logical_not(below_diag) & (ki * tk <= qi * tq + tq - 1))(functools.partial(block, True))

    @pl.when(ki == pl.num_programs(2) - 1)
    def _():
        for h in range(n_heads):
            l = jnp.concatenate([l_ref[h]] * (dv // LANE), axis=1)
            o_ref[:, h * dv:(h + 1) * dv] = (acc_ref[h] / l).astype(o_ref.dtype)


def _mla_flash(qcat, kcat, B, T, n_heads, dv, scale):
    M, hd = kcat.shape
    tq = _tile(T, FLASH_TQ, 8)
    tk = _tile(T, FLASH_TK, 8)
    nq, nk = T // tq, T // tk
    assert tk % LANE == 0 and dv % LANE == 0
    rc = math.gcd(tq, FLASH_ROW_CHUNK)
    kern = functools.partial(_mla_flash_kernel, n_heads=n_heads, tq=tq, tk=tk, hd=hd, dv=dv, scale=scale, rc=rc)

    def kmap(b, qi, ki):
        return (b * nk + jnp.minimum(ki, (qi * tq + tq - 1) // tk), 0)

    return pl.pallas_call(
        kern,
        grid=(B, nq, nk),
        in_specs=[pl.BlockSpec((tq, n_heads * hd), lambda b, qi, ki: (b * nq + qi, 0)),
                  pl.BlockSpec((tk, hd), kmap)],
        out_specs=pl.BlockSpec((tq, n_heads * dv), lambda b, qi, ki: (b * nq + qi, 0)),
        out_shape=jax.ShapeDtypeStruct((M, n_heads * dv), CDT),
        scratch_shapes=[pltpu.VMEM((n_heads, tq, LANE), F32),
                        pltpu.VMEM((n_heads, tq, LANE), F32),
                        pltpu.VMEM((n_heads, tq, dv), F32),
                        pltpu.VMEM((2, tq, tk), F32),
                        pltpu.VMEM((2, tq, tk), CDT),
                        pltpu.VMEM((2, tq, LANE), F32),
                        pltpu.VMEM((2, tq, LANE), F32)],
        compiler_params=_cparams(("parallel", "parallel", "arbitrary")),
        name="mla_flash",
    )(qcat, kcat)


def _mla_decode_kernel(pt_ref, q_ref, kn_ref, *refs, n_pages_step, n_heads, t_new, dv, rope, scale):
    G = n_pages_step
    ckv_refs = refs[:G]
    kpe_refs = refs[G:2 * G]
    o_ref, m_ref, l_ref, acc_ref = refs[2 * G:]
    g = pl.program_id(1)

    @pl.when(g == 0)
    def _():
        m_ref[...] = jnp.full(m_ref.shape, NEG_INF, F32)
        l_ref[...] = jnp.zeros(l_ref.shape, F32)
        acc_ref[...] = jnp.zeros(acc_ref.shape, F32)

    q = q_ref[0]

    def update(s, v):
        m_prev = m_ref[...]
        m_new = jnp.maximum(m_prev, jnp.max(s, axis=-1, keepdims=True))
        alpha = jnp.exp(m_prev - m_new)
        p = jnp.exp(s - m_new)
        l_ref[...] = alpha * l_ref[...] + jnp.sum(p, axis=-1, keepdims=True)
        acc_ref[...] = alpha * acc_ref[...] + _dot(p, v)
        m_ref[...] = m_new

    ckv = jnp.concatenate([r[0, 0] for r in ckv_refs], axis=0).astype(CDT)
    kpe_t = jnp.concatenate([r[0, 0] for r in kpe_refs], axis=1).astype(CDT)
    s = (_dot_nt(q[:, :dv], ckv) + _dot(q[:, dv:dv + rope], kpe_t)) * scale
    update(s, ckv)

    @pl.when(g == pl.num_programs(1) - 1)
    def _():
        kn = kn_ref[0]
        s2 = _dot_nt(q, kn) * scale
        r, c = s2.shape
        t_row = lax.broadcasted_iota(jnp.int32, (r, c), 0) // n_heads
        col = lax.broadcasted_iota(jnp.int32, (r, c), 1)
        s2 = jnp.where((col <= t_row) & (col < t_new), s2, NEG_INF)
        update(s2, kn[:, :dv])
        o_ref[0] = (acc_ref[...] / l_ref[...]).astype(o_ref.dtype)


def _mla_decode(qcat, kcat, cache_ckv, cache_kpe_t, layer, page_table, B, Ts, n_heads, dv, rope, scale):
    hd = kcat.shape[1]
    n_pages = page_table.shape[1]
    page = cache_ckv.shape[2]
    G = math.gcd(n_pages, PAGES_PER_STEP_MLA)
    NG = n_pages // G
    R = Ts * n_heads
    q3 = qcat.reshape(B, R, hd)
    npad = 16
    kn = jnp.pad(kcat.reshape(B, Ts, hd), ((0, 0), (0, npad - Ts), (0, 0)))
    kern = functools.partial(_mla_decode_kernel, n_pages_step=G, n_heads=n_heads, t_new=Ts, dv=dv, rope=rope,
                             scale=scale)

    def page_map(i):
        return lambda b, g, pt: (layer, pt[b, g * G + i], 0, 0)

    in_specs = [pl.BlockSpec((1, R, hd), lambda b, g, pt: (b, 0, 0)),
                pl.BlockSpec((1, npad, hd), lambda b, g, pt: (b, 0, 0))]
    in_specs += [pl.BlockSpec((1, 1, page, dv), page_map(i)) for i in range(G)]
    in_specs += [pl.BlockSpec((1, 1, rope, page), page_map(i)) for i in range(G)]
    out = pl.pallas_call(
        kern,
        grid_spec=pltpu.PrefetchScalarGridSpec(
            num_scalar_prefetch=1,
            grid=(B, NG),
            in_specs=in_specs,
            out_specs=pl.BlockSpec((1, R, dv), lambda b, g, pt: (b, 0, 0)),
            scratch_shapes=[pltpu.VMEM((R, 1), F32), pltpu.VMEM((R, 1), F32), pltpu.VMEM((R, dv), F32)]),
        out_shape=jax.ShapeDtypeStruct((B, R, dv), CDT),
        compiler_params=_cparams(("parallel", "arbitrary")),
        name="mla_decode",
    )(page_table, q3, kn, *([cache_ckv] * G), *([cache_kpe_t] * G))
    return out.reshape(B * Ts, n_heads * dv)


def _mla_out_kernel(o_ref, wuv_ref, wo_ref, x_ref, gt_ref, y_ref, *, n_heads, dv):
    parts = [_dot(o_ref[:, h * dv:(h + 1) * dv], wuv_ref[h]).astype(CDT) for h in range(n_heads)]
    o = jnp.concatenate(parts, axis=1)
    y = jnp.dot(o, wo_ref[...], preferred_element_type=F32)
    y_ref[...] = x_ref[...] + gt_ref[0] * y


def _mla_out(o_lat, wuv, wo, x, gt, rows_per_group, n_heads, dv):
    M, D = x.shape
    tm = _tile(M, 512, 8)
    tpg = max(rows_per_group // tm, 1)
    kern = functools.partial(_mla_out_kernel, n_heads=n_heads, dv=dv)
    return pl.pallas_call(
        kern,
        grid=(M // tm,),
        in_specs=[pl.BlockSpec((tm, n_heads * dv), lambda i: (i, 0)),
                  pl.BlockSpec(wuv.shape, lambda i: (0, 0, 0)),
                  pl.BlockSpec(wo.shape, lambda i: (0, 0)),
                  pl.BlockSpec((tm, D), lambda i: (i, 0)),
                  pl.BlockSpec((1, gt.shape[1], D), lambda i: (i // tpg, 0, 0))],
        out_specs=pl.BlockSpec((tm, D), lambda i: (i, 0)),
        out_shape=jax.ShapeDtypeStruct((M, D), F32),
        compiler_params=_cparams(("parallel",)),
        name="mla_out",
    )(o_lat, wuv, wo, x, gt)


def _proj_res_kernel(o_ref, wo_ref, x_ref, gt_ref, y_ref):
    y = jnp.dot(o_ref[...], wo_ref[...], preferred_element_type=F32)
    y_ref[...] = x_ref[...] + gt_ref[0] * y


def _proj_res(o, wo, x, gt, rows_per_group):
    M, D = x.shape
    K = o.shape[1]
    tm = _tile(M, 512, 8)
    tpg = max(rows_per_group // tm, 1)
    return pl.pallas_call(
        _proj_res_kernel,
        grid=(M // tm,),
        in_specs=[pl.BlockSpec((tm, K), lambda i: (i, 0)),
                  pl.BlockSpec(wo.shape, lambda i: (0, 0)),
                  pl.BlockSpec((tm, D), lambda i: (i, 0)),
                  pl.BlockSpec((1, gt.shape[1], D), lambda i: (i // tpg, 0, 0))],
        out_specs=pl.BlockSpec((tm, D), lambda i: (i, 0)),
        out_shape=jax.ShapeDtypeStruct((M, D), F32),
        compiler_params=_cparams(("parallel",)),
        name="proj_res",
    )(o, wo, x, gt)


def _gla_gate_kernel(a_ref, w_ref, b_ref, o_ref):
    z = _dot(a_ref[...], w_ref[...]) + b_ref[...]
    o_ref[...] = (jnp.minimum(z, 0.0) - jnp.log(1.0 + jnp.exp(-jnp.abs(z)))) / GLA_TAU


def _gla_gate(proj, col_block, w_a2p, b_a2):
    M = proj.shape[0]
    N = w_a2p.shape[1]
    tm = _tile(M, 1024, 8)
    return pl.pallas_call(
        _gla_gate_kernel,
        grid=(M // tm,),
        in_specs=[pl.BlockSpec((tm, LANE), lambda i: (i, col_block)),
                  pl.BlockSpec(w_a2p.shape, lambda i: (0, 0)),
                  pl.BlockSpec((1, N), lambda i: (0, 0))],
        out_specs=pl.BlockSpec((tm, N), lambda i: (i, 0)),
        out_shape=jax.ShapeDtypeStruct((M, N), F32),
        compiler_params=_cparams(("parallel",)),
        name="gla_gate",
    )(proj, w_a2p, b_a2.reshape(1, N))


def _cumsum_rows(x):
    C = x.shape[0]
    row = lax.broadcasted_iota(jnp.int32, x.shape, 0)
    if C <= 8:
        out = jnp.zeros_like(x)
        for s in range(C):
            out = out + jnp.where(row >= s, x[s:s + 1], 0.0)
        return out
    sh = 1
    while sh < C:
        x = x + jnp.where(row >= sh, pltpu.roll(x, sh, 0), 0.0)
        sh *= 2
    return x


def _gla_kernel(q_ref, k_ref, v_ref, la_ref, s0_ref, o_ref, sf_ref, st_ref, *, n_heads, dk, dv, sub, qscale):
    c = pl.program_id(1)

    @pl.when(c == 0)
    def _():
        for h in range(n_heads):
            st_ref[h] = s0_ref[0, h].T

    C = q_ref.shape[1]
    nsub = C // sub
    for h in range(n_heads):
        q = q_ref[0, :, h * dk:(h + 1) * dk] * qscale
        k = k_ref[0, :, h * dk:(h + 1) * dk]
        v = v_ref[0, :, h * dv:(h + 1) * dv]
        b = _cumsum_rows(la_ref[0, :, h * dk:(h + 1) * dk])
        st = st_ref[h]
        o_inter = _dot_nt(q * jnp.exp(b), st)
        b_last = b[C - 1:C]
        k_dec = k * jnp.exp(b_last - b)
        st_ref[h] = jnp.exp(b_last) * st + _dot_tn(v, k_dec)
        outs = []
        for i in range(nsub):
            r0 = i * sub
            b_i = b[r0:r0 + sub]
            q_i = q[r0:r0 + sub]
            k_i = k[r0:r0 + sub]
            v_i = v[r0:r0 + sub]
            o_i = o_inter[r0:r0 + sub]
            if i > 0:
                ref_row = b[r0:r0 + 1]
                att = _dot_nt(q_i * jnp.exp(b_i - ref_row), k[:r0] * jnp.exp(ref_row - b[:r0]))
                o_i = o_i + _dot(att, v[:r0])
            t_loc = lax.broadcasted_iota(jnp.int32, (sub, 1), 0)
            for s in range(sub):
                w = jnp.exp(jnp.minimum(b_i - b_i[s:s + 1], 0.0))
                col = jnp.sum(q_i * w * k_i[s:s + 1], axis=-1, keepdims=True)
                o_i = o_i + jnp.where(t_loc >= s, col, 0.0) * v_i[s:s + 1]
            outs.append(o_i)
        o_ref[0, :, h * dv:(h + 1) * dv] = outs[0] if nsub == 1 else jnp.concatenate(outs, axis=0)

    @pl.when(c == pl.num_programs(1) - 1)
    def _():
        for h in range(n_heads):
            sf_ref[0, h] = st_ref[h].T


def _gla_recurrence(proj, log_a, s0, B, T, n_heads, dk, dv):
    C = 64 if T % 64 == 0 else T
    sub = min(16, C)
    nc = T // C
    Np = proj.shape[1]
    p3 = proj.reshape(B * nc, C, Np)
    la3 = log_a.reshape(B * nc, C, n_heads * dk)
    hk, hv = n_heads * dk, n_heads * dv
    assert hv % hk == 0
    kern = functools.partial(_gla_kernel, n_heads=n_heads, dk=dk, dv=dv, sub=sub, qscale=dk ** -0.5)
    o, sf = pl.pallas_call(
        kern,
        grid=(B, nc),
        in_specs=[pl.BlockSpec((1, C, hk), lambda b, c: (b * nc + c, 0, 0)),
                  pl.BlockSpec((1, C, hk), lambda b, c: (b * nc + c, 0, 1)),
                  pl.BlockSpec((1, C, hv), lambda b, c: (b * nc + c, 0, 2 * hk // hv)),
                  pl.BlockSpec((1, C, hk), lambda b, c: (b * nc + c, 0, 0)),
                  pl.BlockSpec((1, n_heads, dk, dv), lambda b, c: (b, 0, 0, 0))],
        out_specs=[pl.BlockSpec((1, C, hv), lambda b, c: (b * nc + c, 0, 0)),
                   pl.BlockSpec((1, n_heads, dk, dv), lambda b, c: (b, 0, 0, 0))],
        out_shape=[jax.ShapeDtypeStruct((B * nc, C, hv), F32),
                   jax.ShapeDtypeStruct((B, n_heads, dk, dv), F32)],
        scratch_shapes=[pltpu.VMEM((n_heads, dv, dk), F32)],
        compiler_params=_cparams(("parallel", "arbitrary")),
        name="gla_recurrence",
    )(p3, p3, p3, la3, s0)
    return o.reshape(B * T, hv), sf


def _gla_out_kernel(o_ref, r_ref, g_ref, wo_ref, x_ref, gt_ref, y_ref, *, n_heads, dv):
    parts = []
    for h in range(n_heads):
        sl = slice(h * dv, (h + 1) * dv)
        parts.append((_rms(o_ref[:, sl], g_ref[...]) * _silu(r_ref[:, sl])).astype(CDT))
    y = jnp.dot(jnp.concatenate(parts, axis=1), wo_ref[...], preferred_element_type=F32)
    y_ref[...] = x_ref[...] + gt_ref[0] * y


def _gla_out(o, proj, r_block, g_o, wo, x, gt, rows_per_group, n_heads, dv):
    M, D = x.shape
    hv = n_heads * dv
    tm = _tile(M, 512, 8)
    tpg = max(rows_per_group // tm, 1)
    kern = functools.partial(_gla_out_kernel, n_heads=n_heads, dv=dv)
    return pl.pallas_call(
        kern,
        grid=(M // tm,),
        in_specs=[pl.BlockSpec((tm, hv), lambda i: (i, 0)),
                  pl.BlockSpec((tm, hv), lambda i: (i, r_block)),
                  pl.BlockSpec((1, dv), lambda i: (0, 0)),
                  pl.BlockSpec(wo.shape, lambda i: (0, 0)),
                  pl.BlockSpec((tm, D), lambda i: (i, 0)),
                  pl.BlockSpec((1, gt.shape[1], D), lambda i: (i // tpg, 0, 0))],
        out_specs=pl.BlockSpec((tm, D), lambda i: (i, 0)),
        out_shape=jax.ShapeDtypeStruct((M, D), F32),
        compiler_params=_cparams(("parallel",)),
        name="gla_out",
    )(o, proj, g_o.reshape(1, dv), wo, x, gt)


def _order_key(score):
    score = jnp.where(score == 0.0, 0.0, score)
    bits = pltpu.bitcast(score, jnp.int32)
    return jnp.where(bits < 0, bits ^ jnp.int32(0x7FFFFFFF), bits)


def _kth_largest_key(key, topk, axes):
    shape = tuple(1 if a in axes else s for a, s in enumerate(key.shape))

    def count(cand):
        return jnp.sum(jnp.where(key >= cand, 1, 0), axis=axes, keepdims=True)

    def body(it, t):
        hi = lax.shift_left(jnp.int32(1), jnp.int32(31) - 2 * it)
        lo = lax.shift_left(jnp.int32(1), jnp.int32(30) - 2 * it)
        t01, t10, t11 = t + lo, t + hi, t + hi + lo
        c01, c10, c11 = count(t01), count(t10), count(t11)
        return jnp.where(c11 >= topk, t11, jnp.where(c10 >= topk, t10, jnp.where(c01 >= topk, t01, t)))

    return lax.fori_loop(0, 16, body, jnp.full(shape, INT_MIN, jnp.int32))


def _bias_chain(dist, value_of_bucket):
    val = value_of_bucket(0)
    for j in range(1, N_BUCKETS):
        val = jnp.where(dist >= BUCKET_LO[j], value_of_bucket(j), val)
    return val


def _strict_upper(n):
    a = lax.broadcasted_iota(jnp.int32, (n, n), 0)
    b = lax.broadcasted_iota(jnp.int32, (n, n), 1)
    return jnp.where(a < b, 1.0, 0.0).astype(CDT)


def _dsa_prompt_kernel(rb_ref, q_ref, qi_ref, wq_ref, k_ref, v_ref, kk_ref, o_ref, sel_ref, band_ref, *,
                       n_heads, n_kv, hd, n_idx, di, topk, tq, T, n_cls):
    b_id = pl.program_id(0)
    qt = pl.program_id(1)
    group = n_heads // n_kv

    @pl.when((b_id == 0) & (qt == 0))
    def _():
        i = lax.broadcasted_iota(jnp.int32, (tq, LANE), 0)
        j = lax.broadcasted_iota(jnp.int32, (tq, LANE), 1)
        for h in range(n_heads):
            far = rb_ref[N_BUCKETS - 1, h]
            for w in range(tq // LANE + 1):
                dist = i - j + (w - (tq // LANE - 1)) * LANE
                band_ref[w, h] = _bias_chain(dist, lambda bk: rb_ref[bk, h]) - far

    def body(nk):
        row_pos = qt * tq + lax.broadcasted_iota(jnp.int32, (tq, nk), 0)
        col_pos = lax.broadcasted_iota(jnp.int32, (tq, nk), 1)
        causal = col_pos <= row_pos

        ki = kk_ref[:nk, :di]
        wi = wq_ref[:, di:di + n_idx] * (n_idx ** -0.5 * di ** -0.5)
        score = jnp.zeros((tq, nk), F32)
        for h in range(n_idx):
            lg = _dot_nt(qi_ref[:, h * di:(h + 1) * di], ki)
            score = score + wi[:, h:h + 1] * jnp.maximum(lg, 0.0)
        score = jnp.where(causal, score, NEG_INF)
        key = _order_key(score)
        thr = _kth_largest_key(key, topk, (1,))
        gt = key > thr
        eq = key == thr
        n_gt = jnp.sum(jnp.where(gt, 1, 0), axis=1, keepdims=True)
        n_eq = jnp.sum(jnp.where(eq, 1, 0), axis=1, keepdims=True)
        need = topk - n_gt
        tie = jnp.max(n_eq - need) > 0

        @pl.when(jnp.logical_not(tie))
        def _():
            sel_ref[:, :nk] = jnp.where((gt | eq) & causal, 0.0, NEG_INF)

        @pl.when(tie)
        def _():
            upper = _strict_upper(LANE)
            run = jnp.zeros((tq, 1), F32)
            needf = need.astype(F32)
            for kb in range(nk // LANE):
                sl = slice(kb * LANE, (kb + 1) * LANE)
                eqb = jnp.where(eq[:, sl], 1.0, 0.0)
                pre = jnp.dot(eqb.astype(CDT), upper, preferred_element_type=F32) + run
                keep = gt[:, sl] | (eq[:, sl] & (pre < needf))
                sel_ref[:, sl] = jnp.where(keep & causal[:, sl], 0.0, NEG_INF)
                run = run + jnp.sum(eqb, axis=1, keepdims=True)

        scale = hd ** -0.5
        nband = tq // LANE + 1
        nkb = nk // LANE
        first_near = max(nkb - (T // n_cls) // LANE - tq // LANE, 0)
        for g in range(n_kv):
            kg = k_ref[:nk, g * hd:(g + 1) * hd]
            vg = v_ref[:nk, g * hd:(g + 1) * hd]
            for r in range(group):
                h = g * group + r
                s = _dot_nt(q_ref[:, h * hd:(h + 1) * hd], kg) * scale + rb_ref[N_BUCKETS - 1, h] + sel_ref[:, :nk]
                parts = [s[:, :first_near * LANE]] if first_near else []
                for kb in range(first_near, nkb):
                    w = qt * (tq // LANE) - kb + (tq // LANE - 1)
                    add = jnp.zeros((tq, LANE), F32)
                    for wv in range(nband):
                        add = jnp.where(w == wv, band_ref[wv, h], add)
                    parts.append(s[:, kb * LANE:(kb + 1) * LANE] + add)
                s = jnp.concatenate(parts, axis=1)
                m = jnp.max(s, axis=-1, keepdims=True)
                p = jnp.exp(s - m)
                l = jnp.sum(p, axis=-1, keepdims=True)
                o_ref[:, h * hd:(h + 1) * hd] = (_dot(p, vg) / l).astype(o_ref.dtype)

    width = T // n_cls
    cls = ((qt + 1) * tq - 1) // width
    for c in range(n_cls):
        pl.when(cls == c)(functools.partial(body, (c + 1) * width))


def _dsa_prompt(proj, proj_c, rel_bias, B, T, dims, topk):
    n_heads, n_kv, hd, n_idx, di = dims
    M, Np = proj.shape
    tq = math.gcd(T, DSA_TQ)
    assert tq % LANE == 0
    nqt = T // tq
    hq, hkv, hi = n_heads * hd, n_kv * hd, n_idx * di
    assert hq % hkv == 0 and (hq + 2 * hkv) % hi == 0 and (hq + 2 * hkv + hi) % LANE == 0
    kk_blk = (hq + 2 * hkv + hi) // LANE
    n_cls = math.gcd(T // LANE, DSA_EXTENT_CLASSES)
    kern = functools.partial(_dsa_prompt_kernel, n_heads=n_heads, n_kv=n_kv, hd=hd, n_idx=n_idx, di=di, topk=topk,
                             tq=tq, T=T, n_cls=n_cls)
    return pl.pallas_call(
        kern,
        grid=(B, nqt),
        in_specs=[pl.BlockSpec(memory_space=pltpu.SMEM),
                  pl.BlockSpec((tq, hq), lambda b, t: (b * nqt + t, 0)),
                  pl.BlockSpec((tq, hi), lambda b, t: (b * nqt + t, (hq + 2 * hkv) // hi)),
                  pl.BlockSpec((tq, LANE), lambda b, t: (b * nqt + t, kk_blk)),
                  pl.BlockSpec((T, hkv), lambda b, t: (b, hq // hkv)),
                  pl.BlockSpec((T, hkv), lambda b, t: (b, hq // hkv + 1)),
                  pl.BlockSpec((T, LANE), lambda b, t: (b, kk_blk))],
        out_specs=pl.BlockSpec((tq, hq), lambda b, t: (b * nqt + t, 0)),
        out_shape=jax.ShapeDtypeStruct((M, hq), CDT),
        scratch_shapes=[pltpu.VMEM((tq, T), F32),
                        pltpu.VMEM((tq // LANE + 1, n_heads, tq, LANE), F32)],
        compiler_params=_cparams(("arbitrary", "arbitrary")),
        name="dsa_prompt",
    )(rel_bias, proj_c, proj_c, proj, proj_c, proj_c, proj_c)


def _dsa_select_kernel(pt_ref, qi_ref, wi_ref, kn_ref, *refs, n_pages_step, n_idx, di, topk, t_new):
    G = n_pages_step
    ki_refs = refs[:G]
    mask_ref, sc_ref = refs[G:]
    g = pl.program_id(1)
    NG = pl.num_programs(1)
    R = qi_ref.shape[1] // n_idx
    qi = qi_ref[0]
    wi = wi_ref[0] * (n_idx ** -0.5)

    def head_sum(lg):
        w = wi * jnp.maximum(lg * (di ** -0.5), 0.0)
        sc = w[0:R]
        for h in range(1, n_idx):
            sc = sc + w[h * R:(h + 1) * R]
        return sc

    past = head_sum(_dot(qi, jnp.concatenate([r[0, 0] for r in ki_refs], axis=1)))
    GW = past.shape[1]
    sc_ref[g] = past

    @pl.when(g == NG - 1)
    def _():
        new = head_sum(_dot_nt(qi, kn_ref[0][:, :di]))
        t_row = lax.broadcasted_iota(jnp.int32, new.shape, 0) % t_new
        col = lax.broadcasted_iota(jnp.int32, new.shape, 1)
        new = jnp.where((col <= t_row) & (col < t_new), new, NEG_INF)
        sc_ref[NG] = jnp.concatenate([new, jnp.full((R, GW - LANE), NEG_INF, F32)], axis=1)
        score = sc_ref[...]
        valid = score > NEG_INF
        key = _order_key(score)
        thr = _kth_largest_key(key, topk, (0, 2))
        gt = key > thr
        eq = key == thr
        n_gt = jnp.sum(jnp.where(gt, 1, 0), axis=(0, 2), keepdims=True)
        n_eq = jnp.sum(jnp.where(eq, 1, 0), axis=(0, 2), keepdims=True)
        need = topk - n_gt
        tie = jnp.max(n_eq - need) > 0

        @pl.when(jnp.logical_not(tie))
        def _():
            mask_ref[0] = jnp.where((gt | eq) & valid, 0.0, NEG_INF)

        @pl.when(tie)
        def _():
            upper = _strict_upper(LANE)
            needf = need[0].astype(F32)

            def blk(gi, run):
                k_g = _order_key(sc_ref[gi])
                v_g = sc_ref[gi] > NEG_INF
                for c in range(GW // LANE):
                    sl = slice(c * LANE, (c + 1) * LANE)
                    eqb = jnp.where(k_g[:, sl] == thr[0], 1.0, 0.0)
                    pre = jnp.dot(eqb.astype(CDT), upper, preferred_element_type=F32) + run
                    keep = (k_g[:, sl] > thr[0]) | ((eqb > 0.0) & (pre < needf))
                    mask_ref[0, gi, :, sl] = jnp.where(keep & v_g[:, sl], 0.0, NEG_INF)
                    run = run + jnp.sum(eqb, axis=1, keepdims=True)
                return run

            lax.fori_loop(0, NG + 1, blk, jnp.zeros((R, 1), F32))


def _dsa_select(qi_hm, wi_hm, kn, cache_ki_t, layer, page_table, B, n_idx, di, topk, t_new):
    n_pages = page_table.shape[1]
    page = cache_ki_t.shape[3]
    G = math.gcd(n_pages, PAGES_PER_STEP_SELECT)
    NG = n_pages // G
    GW = G * page
    R = qi_hm.shape[1] // n_idx
    kern = functools.partial(_dsa_select_kernel, n_pages_step=G, n_idx=n_idx, di=di, topk=topk, t_new=t_new)

    def page_map(i):
        return lambda b, g, pt: (layer, pt[b, g * G + i], 0, 0)

    in_specs = [pl.BlockSpec((1, n_idx * R, di), lambda b, g, pt: (b, 0, 0)),
                pl.BlockSpec((1, n_idx * R, 1), lambda b, g, pt: (b, 0, 0)),
                pl.BlockSpec((1, LANE, LANE), lambda b, g, pt: (b, 0, 0))]
    in_specs += [pl.BlockSpec((1, 1, di, page), page_map(i)) for i in range(G)]
    return pl.pallas_call(
        kern,
        grid_spec=pltpu.PrefetchScalarGridSpec(
            num_scalar_prefetch=1,
            grid=(B, NG),
            in_specs=in_specs,
            out_specs=pl.BlockSpec((1, NG + 1, R, GW), lambda b, g, pt: (b, 0, 0, 0)),
            scratch_shapes=[pltpu.VMEM((NG + 1, R, GW), F32)]),
        out_shape=jax.ShapeDtypeStruct((B, NG + 1, R, GW), F32),
        compiler_params=_cparams(("parallel", "arbitrary")),
        name="dsa_select",
    )(page_table, qi_hm, wi_hm, kn, *([cache_ki_t] * G))


def _dsa_decode_kernel(pt_ref, rb_ref, q_ref, kn_ref, vn_ref, mask_ref, mnew_ref, *refs,
                       n_pages_step, n_kv, group, hd, t_new, past_len):
    G = n_pages_step
    k_refs = refs[:G]
    v_refs = refs[G:2 * G]
    o_ref, m_ref, l_ref, acc_ref = refs[2 * G:]
    g = pl.program_id(1)
    NG = pl.num_programs(1)
    R = group * t_new
    scale = hd ** -0.5

    @pl.when(g == 0)
    def _():
        m_ref[...] = jnp.full(m_ref.shape, NEG_INF, F32)
        l_ref[...] = jnp.zeros(l_ref.shape, F32)
        acc_ref[...] = jnp.zeros(acc_ref.shape, F32)

    def head_val(kvh, bucket):
        row = lax.broadcasted_iota(jnp.int32, (R, 1), 0)
        val = jnp.full((R, 1), rb_ref[bucket, kvh * group], F32)
        for r in range(1, group):
            val = jnp.where(row >= r * t_new, rb_ref[bucket, kvh * group + r], val)
        return val

    def update(kvh, s, v):
        m_prev = m_ref[kvh]
        m_new = jnp.maximum(m_prev, jnp.max(s, axis=-1, keepdims=True))
        m_safe = jnp.where(m_new > NEG_INF, m_new, 0.0)
        alpha = jnp.exp(m_prev - m_safe)
        p = jnp.exp(s - m_safe)
        l_ref[kvh] = alpha * l_ref[kvh] + jnp.sum(p, axis=-1, keepdims=True)
        acc_ref[kvh] = alpha * acc_ref[kvh] + _dot(p, v)
        m_ref[kvh] = m_new

    page = k_refs[0].shape[2] // n_kv
    GW = G * page
    mask = mask_ref[0, 0]
    t_row = lax.broadcasted_iota(jnp.int32, (R, GW), 0) % t_new
    col = lax.broadcasted_iota(jnp.int32, (R, GW), 1)
    dist = past_len + t_row - (g * GW + col)
    near = g == NG - 1

    def head_rows(refs_, kvh):
        return jnp.concatenate([r[0, 0, pl.ds(kvh, page, stride=n_kv), :] for r in refs_], axis=0).astype(CDT)

    for kvh in range(n_kv):
        s = _dot_nt(q_ref[0, kvh], head_rows(k_refs, kvh)) * scale
        bias = lax.cond(near,
                        lambda: _bias_chain(dist, functools.partial(head_val, kvh)),
                        lambda: jnp.broadcast_to(head_val(kvh, N_BUCKETS - 1), (R, GW)))
        update(kvh, s + bias + mask, head_rows(v_refs, kvh))

    @pl.when(near)
    def _():
        kn = kn_ref[0].astype(CDT)
        vn = vn_ref[0].astype(CDT)
        mnew = mnew_ref[0, 0][:, :LANE]
        t_r = lax.broadcasted_iota(jnp.int32, (R, LANE), 0) % t_new
        c = lax.broadcasted_iota(jnp.int32, (R, LANE), 1)
        d_new = jnp.maximum(t_r - c, 0)
        for kvh in range(n_kv):
            sl = slice(kvh * hd, (kvh + 1) * hd)
            s = _dot_nt(q_ref[0, kvh], kn[:, sl]) * scale
            bias = _bias_chain(d_new, functools.partial(head_val, kvh))
            update(kvh, s + bias + mnew, vn[:, sl])
            o_ref[0, kvh] = (acc_ref[kvh] / l_ref[kvh]).astype(o_ref.dtype)


def _dsa_decode(q4, kn, vn, mask, rel_bias, cache_k, cache_v, layer, page_table, B, n_kv, group, hd, t_new):
    n_pages = page_table.shape[1]
    page = cache_k.shape[2] // n_kv
    GWs = mask.shape[3]
    G = math.gcd(math.gcd(n_pages, PAGES_PER_STEP_DECODE), GWs // page)
    NG = n_pages // G
    GW = G * page
    ratio = GWs // GW
    R = group * t_new
    kern = functools.partial(_dsa_decode_kernel, n_pages_step=G, n_kv=n_kv, group=group, hd=hd, t_new=t_new,
                             past_len=n_pages * page)

    def page_map(i):
        return lambda b, g, pt: (layer, pt[b, g * G + i], 0, 0)

    in_specs = [pl.BlockSpec(memory_space=pltpu.SMEM),
                pl.BlockSpec((1, n_kv, R, hd), lambda b, g, pt: (b, 0, 0, 0)),
                pl.BlockSpec((1, LANE, n_kv * hd), lambda b, g, pt: (b, 0, 0)),
                pl.BlockSpec((1, LANE, n_kv * hd), lambda b, g, pt: (b, 0, 0)),
                pl.BlockSpec((1, 1, R, GW), lambda b, g, pt: (b, g // ratio, 0, g % ratio)),
                pl.BlockSpec((1, 1, R, GW), lambda b, g, pt: (b, mask.shape[1] - 1, 0, 0))]
    in_specs += [pl.BlockSpec((1, 1, page * n_kv, hd), page_map(i)) for i in range(G)]
    in_specs += [pl.BlockSpec((1, 1, page * n_kv, hd), page_map(i)) for i in range(G)]
    return pl.pallas_call(
        kern,
        grid_spec=pltpu.PrefetchScalarGridSpec(
            num_scalar_prefetch=1,
            grid=(B, NG),
            in_specs=in_specs,
            out_specs=pl.BlockSpec((1, n_kv, R, hd), lambda b, g, pt: (b, 0, 0, 0)),
            scratch_shapes=[pltpu.VMEM((n_kv, R, 1), F32), pltpu.VMEM((n_kv, R, 1), F32),
                            pltpu.VMEM((n_kv, R, hd), F32)]),
        out_shape=jax.ShapeDtypeStruct((B, n_kv, R, hd), CDT),
        compiler_params=_cparams(("parallel", "arbitrary")),
        name="dsa_decode",
    )(page_table, rel_bias, q4, kn, vn, mask, mask, *([cache_k] * G), *([cache_v] * G))


def _ffn_kernel(*refs, seq_tiles, t_seq, conv_w):
    if seq_tiles:
        (x_ref, g_ref, sc_ref, sh_ref, gt_ref, wg_ref, wv_ref, wd_ref, cw_ref, cb_ref,
         y_ref, tail_ref, h_ref, acc_ref, act_ref, stash_ref) = refs
    else:
        (x_ref, g_ref, sc_ref, sh_ref, gt_ref, wg_ref, wv_ref, wd_ref, cw_ref, cb_ref, p1_ref, p2_ref,
         y_ref, tail_ref, h_ref, acc_ref, act_ref) = refs
    i = pl.program_id(0)
    j = pl.program_id(1)

    @pl.when(j == 0)
    def _():
        h = _rms(x_ref[...], g_ref[...]) * (1.0 + sc_ref[0]) + sh_ref[0]
        h_ref[...] = h.astype(CDT)
        acc_ref[...] = jnp.zeros(acc_ref.shape, F32)

    if seq_tiles:
        @pl.when((i == 0) & (j == 0))
        def _():
            stash_ref[...] = jnp.zeros(stash_ref.shape, F32)

    tm, tf = h_ref.shape[0], wg_ref.shape[1]
    for c0 in range(0, tf, FFN_SUB_F):
        cs = slice(c0, min(c0 + FFN_SUB_F, tf))
        gate = jnp.dot(h_ref[...], wg_ref[:, cs], preferred_element_type=F32)
        val = jnp.dot(h_ref[...], wv_ref[:, cs], preferred_element_type=F32)
        row = lax.broadcasted_iota(jnp.int32, gate.shape, 0)
        g1 = pltpu.roll(gate, 1, 0)
        g2 = pltpu.roll(gate, 2, 0)
        if seq_tiles:
            prev = jnp.where(i % seq_tiles == 0, 0.0, stash_ref[j, :, cs])
            g1 = jnp.where(row == 0, prev[7:8], g1)
            g2 = jnp.where(row == 0, prev[6:7], jnp.where(row == 1, prev[7:8], g2))
            stash_ref[j, :, cs] = gate[tm - 8:]
            tail_ref[0, :, cs] = gate[tm - 8:]
        else:
            t = row % t_seq
            g1 = jnp.where(t == 0, p1_ref[:, cs], g1)
            g2 = jnp.where(t < 2, p2_ref[:, cs], g2)
            tail_ref[:, cs] = gate
        conv = cw_ref[0:1, cs] * g2 + cw_ref[1:2, cs] * g1 + cw_ref[2:3, cs] * gate + cb_ref[:, cs]
        act_ref[:, cs] = (_silu(conv) * val).astype(CDT)
    acc_ref[...] += jnp.dot(act_ref[...], wd_ref[...], preferred_element_type=F32)

    @pl.when(j == pl.num_programs(1) - 1)
    def _():
        y_ref[...] = x_ref[...] + gt_ref[0] * acc_ref[...]


def _ffn(x, g, sc, sh, gt, w_up, w_down, conv_w, conv_b, rows_per_group, t_seq, prev=None):
    M, D = x.shape
    Fd = w_down.shape[0]
    assert conv_w.shape[0] == 3
    tf = _tile(Fd, FFN_TILE_F)
    nf = Fd // tf
    cw = jnp.pad(conv_w, ((0, 8 - conv_w.shape[0]), (0, 0)))
    cb = conv_b.reshape(1, Fd)
    seq_mode = prev is None
    if seq_mode:
        tm = _tile(t_seq, 512, 8)
        assert t_seq % tm == 0 and tm >= 8
        seq_tiles = t_seq // tm
    else:
        tm = M
        seq_tiles = 0
    nt = M // tm
    tpg = max(rows_per_group // tm, 1)
    mod = lambda m: pl.BlockSpec((1, m.shape[1], D), lambda i, j: (i // tpg, 0, 0))
    in_specs = [pl.BlockSpec((tm, D), lambda i, j: (i, 0)),
                pl.BlockSpec((1, D), lambda i, j: (0, 0)),
                mod(sc), mod(sh), mod(gt),
                pl.BlockSpec((D, tf), lambda i, j: (0, j)),
                pl.BlockSpec((D, tf), lambda i, j: (0, nf + j)),
                pl.BlockSpec((tf, D), lambda i, j: (j, 0)),
                pl.BlockSpec((8, tf), lambda i, j: (0, j)),
                pl.BlockSpec((1, tf), lambda i, j: (0, j))]
    args = [x, g.reshape(1, D), sc, sh, gt, w_up, w_up, w_down, cw, cb]
    scratch = [pltpu.VMEM((tm, D), CDT), pltpu.VMEM((tm, D), F32), pltpu.VMEM((tm, tf), CDT)]
    if seq_mode:
        tail_shape = jax.ShapeDtypeStruct((nt, 8, Fd), F32)
        tail_spec = pl.BlockSpec((1, 8, tf), lambda i, j: (i, 0, j))
        scratch.append(pltpu.VMEM((nf, 8, tf), F32))
    else:
        p1 = jnp.concatenate([prev[:, 1:2], jnp.zeros_like(prev[:, :1]).repeat(t_seq - 1, axis=1)], axis=1)
        p2 = jnp.concatenate([prev[:, 0:2], jnp.zeros_like(prev[:, :1]).repeat(t_seq - 2, axis=1)], axis=1)
        args += [p1.reshape(M, Fd), p2.reshape(M, Fd)]
        in_specs += [pl.BlockSpec((tm, tf), lambda i, j: (i, j))] * 2
        tail_shape = jax.ShapeDtypeStruct((M, Fd), F32)
        tail_spec = pl.BlockSpec((tm, tf), lambda i, j: (i, j))
    kern = functools.partial(_ffn_kernel, seq_tiles=seq_tiles, t_seq=t_seq, conv_w=conv_w.shape[0])
    return pl.pallas_call(
        kern,
        grid=(nt, nf),
        in_specs=in_specs,
        out_specs=[pl.BlockSpec((tm, D), lambda i, j: (i, 0)), tail_spec],
        out_shape=[jax.ShapeDtypeStruct((M, D), F32), tail_shape],
        scratch_shapes=scratch,
        compiler_params=_cparams(("arbitrary", "arbitrary")),
        name="ffn",
    )(*args)


def _final_norm_kernel(x_ref, g_ref, o_ref):
    o_ref[...] = _rms(x_ref[...], g_ref[...])


def _final_norm(x, g):
    M, D = x.shape
    tm = _tile(M, 1024, 8)
    return pl.pallas_call(
        _final_norm_kernel,
        grid=(M // tm,),
        in_specs=[pl.BlockSpec((tm, D), lambda i: (i, 0)), pl.BlockSpec((1, D), lambda i: (0, 0))],
        out_specs=pl.BlockSpec((tm, D), lambda i: (i, 0)),
        out_shape=jax.ShapeDtypeStruct((M, D), F32),
        compiler_params=_cparams(("parallel",)),
        name="final_norm",
    )(x, g.reshape(1, D))


def _rope_tables(pos, rope, n_heads, reps):
    half = rope // 2
    inv = ROPE_BASE ** (-jnp.arange(half, dtype=F32) / half)
    ang = pos.astype(F32)[:, None] * inv[None, :]
    cos = jnp.cos(ang)
    sin = jnp.sin(ang)
    cosk = jnp.concatenate([cos, cos], axis=1)
    sink = jnp.concatenate([-sin, sin], axis=1)
    tabs = (jnp.tile(cosk, (1, n_heads)), jnp.tile(sink, (1, n_heads)), cosk, sink)
    return tuple(jnp.tile(t, (reps, 1)) for t in tabs)


def _swap_halves(w, rope):
    half = rope // 2
    return jnp.concatenate([w[..., half:], w[..., :half]], axis=-1)


def _pad_cols(w, n):
    return jnp.pad(w, ((0, 0), (0, n - w.shape[1])))


def kernel(x_prompt, x_sample, cache_mla_ckv, cache_mla_kpe, state_gla, cache_dsa_k, cache_dsa_v, cache_dsa_kidx,
           state_ffn_conv, page_table, c_prompt, c_sample, ada_w, ada_b, norm1_g, norm2_g, final_g, mla_w_in,
           mla_g_q, mla_g_kv, mla_w_uq, mla_w_uk, mla_w_uv, mla_w_o, gla_w_in, gla_w_a2, gla_b_a2, gla_g_o,
           gla_w_o, dsa_w_in, dsa_w_o, rel_bias, ffn_w_up, ffn_conv_w, ffn_conv_b, ffn_w_down):
    Bp, Tp, D = x_prompt.shape
    Bs, Ts, _ = x_sample.shape
    depth = ada_w.shape[0]
    n_mod = ada_w.shape[2] // D
    n_pages, page = page_table.shape[1], cache_mla_ckv.shape[2]
    past_len = n_pages * page
    cache_kpe_t = jnp.swapaxes(cache_mla_kpe, 2, 3)
    cache_ki_t = jnp.swapaxes(cache_dsa_kidx, 2, 3)

    q_lora, mla_h, qk_dim = mla_w_uq.shape[1:]
    kv_lora, _, nope = mla_w_uk.shape[1:]
    mla_v = mla_w_uv.shape[3]
    rope = qk_dim - nope
    gla_h = state_gla.shape[2]
    gla_dk, gla_dv = state_gla.shape[3:]
    gla_rank = gla_w_a2.shape[1]
    dsa_kv, dsa_hd = cache_dsa_k.shape[3:]
    dsa_di = cache_dsa_kidx.shape[3]
    dsa_h = dsa_w_o.shape[1] // dsa_hd
    dsa_hi = (dsa_w_in.shape[2] - (dsa_h + 2 * dsa_kv) * dsa_hd - dsa_di) // (dsa_di + 1)
    assert kv_lora % LANE == 0 and q_lora % LANE == 0 and rope <= LANE

    mod_all = _ada_mod(jnp.concatenate([c_prompt, c_sample], axis=0), ada_w, ada_b)

    def mods(l, sample):
        m = mod_all[l, Bp:] if sample else mod_all[l, :Bp]
        parts = [m[:, i * D:(i + 1) * D] for i in range(n_mod)]
        if sample:
            return [jnp.repeat(p, Ts, axis=0).reshape(1, Bs * Ts, D) for p in parts]
        return [p.reshape(Bp, 1, D) for p in parts]

    mla_w = []
    for j in range(mla_w_in.shape[0]):
        w_in = mla_w_in[j]
        kcol = q_lora + kv_lora
        w_in_ext = jnp.concatenate([w_in, _swap_halves(w_in[:, kcol:kcol + rope], rope)], axis=1)
        w_in_ext = _pad_cols(w_in_ext, -(-w_in_ext.shape[1] // LANE) * LANE).astype(CDT)
        uq = mla_w_uq[j]
        wqn = uq[:, :, :nope].reshape(q_lora, mla_h * nope).astype(CDT)
        wqp = uq[:, :, nope:].reshape(q_lora, mla_h * rope).astype(CDT)
        wqs = _swap_halves(uq[:, :, nope:], rope).reshape(q_lora, mla_h * rope).astype(CDT)
        wukT = jnp.transpose(mla_w_uk[j], (1, 2, 0)).astype(CDT)
        wuv = jnp.transpose(mla_w_uv[j], (1, 0, 2)).astype(CDT)
        mla_w.append((w_in_ext, wqn, wqp, wqs, wukT, wuv, mla_w_o[j].astype(CDT)))
    gla_np = -(-gla_w_in.shape[2] // LANE) * LANE
    gla_w = []
    for j in range(gla_w_in.shape[0]):
        w_a2p = jnp.pad(gla_w_a2[j], ((0, LANE - gla_rank), (0, 0))).astype(CDT)
        gla_w.append((_pad_cols(gla_w_in[j], gla_np).astype(CDT), w_a2p, gla_w_o[j].astype(CDT)))
    dsa_np = -(-dsa_w_in.shape[2] // LANE) * LANE
    dsa_w = [(_pad_cols(dsa_w_in[j], dsa_np).astype(CDT), dsa_w_o[j].astype(CDT)) for j in range(dsa_w_in.shape[0])]
    ffn_up = ffn_w_up.astype(CDT)
    ffn_down = ffn_w_down.astype(CDT)

    mla_dims = (mla_h, q_lora, kv_lora, rope, nope)
    mla_scale = qk_dim ** -0.5

    def trunk(x3, sample):
        B, T, _ = x3.shape
        M = B * T
        x = x3.reshape(M, D)
        rpg = M if sample else T
        pos = (past_len if sample else 0) + jnp.arange(T, dtype=jnp.int32)
        tabs = _rope_tables(pos, rope, mla_h, B if sample else 1)
        outs = dict(mla_ckv=[], mla_kpe=[], gla=[], dsa_k=[], dsa_v=[], dsa_ki=[], conv=[])
        for l in range(depth):
            sh1, sc1, gt1, sh2, sc2, gt2 = mods(l, sample)
            j = l // 3
            if l % 3 == 0:
                w_in_ext, wqn, wqp, wqs, wukT, wuv, wo = mla_w[j]
                p = _nm_linear(x, norm1_g[l], sc1, sh1, w_in_ext, rpg)
                ckv, kpe, kcat, qcat = _mla_prep(p, mla_g_q[j], mla_g_kv[j], tabs, wqn, wqp, wqs, wukT, mla_dims)
                if sample:
                    o_lat = _mla_decode(qcat, kcat, cache_mla_ckv, cache_kpe_t, j, page_table, B, T, mla_h,
                                        kv_lora, rope, mla_scale)
                else:
                    o_lat = _mla_flash(qcat, kcat, B, T, mla_h, kv_lora, mla_scale)
                x = _mla_out(o_lat, wuv, wo, x, gt1, rpg, mla_h, kv_lora)
                outs["mla_ckv"].append(ckv.reshape(B, T, kv_lora))
                outs["mla_kpe"].append(kpe.reshape(B, T, rope))
            elif l % 3 == 1:
                w_in_p, w_a2p, wo = gla_w[j]
                hk, hv = gla_h * gla_dk, gla_h * gla_dv
                p = _nm_linear(x, norm1_g[l], sc1, sh1, w_in_p, rpg, tn_target=NM_LINEAR_TN)
                log_a = _gla_gate(p, (2 * hk + 2 * hv) // LANE, w_a2p, gla_b_a2[j])
                s0 = state_gla[j] if sample else jnp.zeros((B, gla_h, gla_dk, gla_dv), F32)
                o, s_fin = _gla_recurrence(p, log_a, s0, B, T, gla_h, gla_dk, gla_dv)
                x = _gla_out(o, p, (2 * hk + hv) // hv, gla_g_o[j], wo, x, gt1, rpg, gla_h, gla_dv)
                outs["gla"].append(s_fin)
            else:
                w_in_p, wo = dsa_w[j]
                hq, hkv, hi = dsa_h * dsa_hd, dsa_kv * dsa_hd, dsa_hi * dsa_di
                p = _nm_linear(x, norm1_g[l], sc1, sh1, w_in_p, rpg, tn_target=NM_LINEAR_TN, with_cdt_copy=not sample)
                if not sample:
                    p, p_c = p
                k_new = p[:, hq:hq + hkv]
                v_new = p[:, hq + hkv:hq + 2 * hkv]
                kk = p[:, hq + 2 * hkv + hi:]
                L_keys = (past_len if sample else 0) + T
                topk = min(DSA_TOPK, L_keys // 4)
                if sample:
                    group = dsa_h // dsa_kv
                    dup = lambda a: jnp.concatenate([a.reshape(B, T, dsa_hi, -1)] * group, axis=1)
                    qi_hm = dup(p[:, hq + 2 * hkv:hq + 2 * hkv + hi]).transpose(0, 2, 1, 3)
                    qi_hm = qi_hm.reshape(B, dsa_hi * group * T, dsa_di)
                    wi_hm = dup(kk[:, dsa_di:dsa_di + dsa_hi]).transpose(0, 2, 1, 3).reshape(B, dsa_hi * group * T, 1)
                    padn = lambda a: jnp.pad(a.reshape(B, T, -1), ((0, 0), (0, LANE - T), (0, 0)))
                    mask = _dsa_select(qi_hm, wi_hm, padn(kk), cache_ki_t, j, page_table, B, dsa_hi, dsa_di, topk, T)
                    q4 = p[:, :hq].reshape(B, T, dsa_kv, group, dsa_hd).transpose(0, 2, 3, 1, 4)
                    q4 = q4.reshape(B, dsa_kv, group * T, dsa_hd)
                    ck = cache_dsa_k.reshape(*cache_dsa_k.shape[:2], page * dsa_kv, dsa_hd)
                    cv = cache_dsa_v.reshape(*cache_dsa_v.shape[:2], page * dsa_kv, dsa_hd)
                    o4 = _dsa_decode(q4, padn(k_new), padn(v_new), mask, rel_bias, ck, cv, j, page_table, B,
                                     dsa_kv, group, dsa_hd, T)
                    o = o4.reshape(B, dsa_kv, group, T, dsa_hd).transpose(0, 3, 1, 2, 4).reshape(M, hq)
                else:
                    o = _dsa_prompt(p, p_c, rel_bias, B, T, (dsa_h, dsa_kv, dsa_hd, dsa_hi, dsa_di), topk)
                x = _proj_res(o, wo, x, gt1, rpg)
                outs["dsa_k"].append(k_new.reshape(B, T, dsa_kv, dsa_hd))
                outs["dsa_v"].append(v_new.reshape(B, T, dsa_kv, dsa_hd))
                outs["dsa_ki"].append(kk[:, :dsa_di].reshape(B, T, dsa_di))
            if sample:
                x, tail = _ffn(x, norm2_g[l], sc2, sh2, gt2, ffn_up[l], ffn_down[l], ffn_conv_w[l], ffn_conv_b[l],
                               rpg, T, prev=state_ffn_conv[l])
                outs["conv"].append(tail.reshape(B, T, -1)[:, T - 2:])
            else:
                x, tail = _ffn(x, norm2_g[l], sc2, sh2, gt2, ffn_up[l], ffn_down[l], ffn_conv_w[l], ffn_conv_b[l],
                               rpg, T)
                nt = tail.shape[0] // B
                outs["conv"].append(tail.reshape(B, nt, 8, -1)[:, nt - 1, 6:8])
        y = _final_norm(x, final_g).reshape(B, T, D)
        return (y, jnp.stack(outs["mla_ckv"]), jnp.stack(outs["mla_kpe"]), jnp.stack(outs["gla"]),
                jnp.stack(outs["dsa_k"]), jnp.stack(outs["dsa_v"]), jnp.stack(outs["dsa_ki"]),
                jnp.stack(outs["conv"]))

    rp = trunk(x_prompt, False)
    rs = trunk(x_sample, True)
    return (rp[0], rs[0]) + tuple(rp[1:]) + tuple(rs[1:])
```

```python
import functools
import math

import numpy as np
import jax
import jax.numpy as jnp
from jax import lax
from jax.experimental import pallas as pl
from jax.experimental.pallas import tpu as pltpu

F32 = jnp.float32
CDT = jnp.bfloat16
EPS = 1e-6
ROPE_BASE = 10000.0
GLA_TAU = 16.0
N_BUCKETS = 32
MAX_DISTANCE = 128
DSA_TOPK = 256
LANE = 128
NEG_INF = float("-inf")
INT_MIN = -2 ** 31
VMEM_LIMIT = 56 * 1024 * 1024
PAGES_PER_STEP_MLA = 64
PAGES_PER_STEP_SELECT = 64
PAGES_PER_STEP_DECODE = 32
FFN_TILE_F = 1408
DSA_EXTENT_CLASSES = 4
DSA_TQ = 128
NM_LINEAR_TN = 4096
FLASH_TQ = 512
FLASH_TK = 512
FLASH_ROW_CHUNK = 64


def _bucket_thresholds():
    d = np.arange(0, 4 * MAX_DISTANCE)
    exact = N_BUCKETS // 2
    lr = np.log(np.maximum(d, 1).astype(np.float32) / np.float32(exact)) / np.float32(math.log(MAX_DISTANCE / exact))
    large = np.minimum(exact + (lr * np.float32(N_BUCKETS - exact)).astype(np.int32), N_BUCKETS - 1)
    b = np.where(d < exact, d, large)
    return [int(np.argmax(b >= j)) for j in range(N_BUCKETS)]


BUCKET_LO = _bucket_thresholds()


def _cparams(sem, vmem=VMEM_LIMIT):
    return pltpu.CompilerParams(dimension_semantics=sem, vmem_limit_bytes=vmem)


def _tile(n, target, mult=LANE):
    if n <= target:
        return n
    t = (target // mult) * mult
    while t > mult and n % t:
        t -= mult
    assert n % t == 0, (n, target)
    return t


def _dot(a, b):
    return jnp.dot(a.astype(CDT), b.astype(CDT), preferred_element_type=F32)


def _dot_nt(a, b):
    return lax.dot_general(a.astype(CDT), b.astype(CDT), (((1,), (1,)), ((), ())), preferred_element_type=F32)


def _dot_tn(a, b):
    return lax.dot_general(a.astype(CDT), b.astype(CDT), (((0,), (0,)), ((), ())), preferred_element_type=F32)


def _rms(x, g):
    return x * lax.rsqrt(jnp.mean(x * x, axis=-1, keepdims=True) + EPS) * g


def _silu(x):
    return x * jax.nn.sigmoid(x)


def _ada_kernel(c_ref, w_ref, b_ref, o_ref):
    ca = _silu(c_ref[...])
    o_ref[0] = _dot(ca, w_ref[0]) + b_ref[0]


def _ada_mod(c, ada_w, ada_b):
    L, D, N = ada_w.shape
    R = c.shape[0]
    tn = _tile(N, 1536)
    return pl.pallas_call(
        _ada_kernel,
        grid=(L, N // tn),
        in_specs=[pl.BlockSpec((R, D), lambda l, j: (0, 0)),
                  pl.BlockSpec((1, D, tn), lambda l, j: (l, 0, j)),
                  pl.BlockSpec((1, 1, tn), lambda l, j: (l, 0, j))],
        out_specs=pl.BlockSpec((1, R, tn), lambda l, j: (l, 0, j)),
        out_shape=jax.ShapeDtypeStruct((L, R, N), F32),
        compiler_params=_cparams(("parallel", "parallel")),
        name="ada_mod",
    )(c, ada_w, ada_b.reshape(L, 1, N))


def _nm_linear_kernel(x_ref, g_ref, sc_ref, sh_ref, w_ref, o_ref, *rest):
    h_ref = rest[-1]

    @pl.when(pl.program_id(1) == 0)
    def _():
        h = _rms(x_ref[...], g_ref[...]) * (1.0 + sc_ref[0]) + sh_ref[0]
        h_ref[...] = h.astype(CDT)

    o = jnp.dot(h_ref[...], w_ref[...], preferred_element_type=F32)
    o_ref[...] = o
    if len(rest) == 2:
        rest[0][...] = o.astype(CDT)


def _mod_spec(mod, tm, rows_per_group):
    G, R, D = mod.shape
    tpg = max(rows_per_group // tm, 1)
    return pl.BlockSpec((1, R, D), lambda i, j: (i // tpg, 0, 0))


def _nm_linear(x, g, sc, sh, w, rows_per_group, tm_target=512, tn_target=1024, with_cdt_copy=False):
    M, D = x.shape
    N = w.shape[1]
    tm = _tile(M, tm_target, 8)
    tn = _tile(N, tn_target)
    out_spec = pl.BlockSpec((tm, tn), lambda i, j: (i, j))
    out_shape = jax.ShapeDtypeStruct((M, N), F32)
    if with_cdt_copy:
        out_spec = [out_spec, out_spec]
        out_shape = [out_shape, jax.ShapeDtypeStruct((M, N), CDT)]
    return pl.pallas_call(
        _nm_linear_kernel,
        grid=(M // tm, N // tn),
        in_specs=[pl.BlockSpec((tm, D), lambda i, j: (i, 0)),
                  pl.BlockSpec((1, D), lambda i, j: (0, 0)),
                  _mod_spec(sc, tm, rows_per_group),
                  _mod_spec(sh, tm, rows_per_group),
                  pl.BlockSpec((D, tn), lambda i, j: (0, j))],
        out_specs=out_spec,
        out_shape=out_shape,
        scratch_shapes=[pltpu.VMEM((tm, D), CDT)],
        compiler_params=_cparams(("parallel", "arbitrary")),
        name="nm_linear",
    )(x, g.reshape(1, D), sc, sh, w)


def _mla_prep_kernel(p_ref, gq_ref, gkv_ref, cq_ref, sq_ref, ck_ref, sk_ref, wqn_ref, wqp_ref, wqs_ref, wuk_ref,
                     ckv_ref, kpe_ref, kcat_ref, qcat_ref, *, n_heads, q_lora, kv_lora, rope, nope):
    p = p_ref[...]
    cq = p[:, :q_lora]
    ckv = p[:, q_lora:q_lora + kv_lora]
    kpe = p[:, q_lora + kv_lora:q_lora + kv_lora + rope]
    kpe_sw = p[:, q_lora + kv_lora + rope:q_lora + kv_lora + 2 * rope]
    cqn = _rms(cq, gq_ref[...]).astype(CDT)
    q_nope = jnp.dot(cqn, wqn_ref[...], preferred_element_type=F32)
    q_pe = (jnp.dot(cqn, wqp_ref[...], preferred_element_type=F32) * cq_ref[...]
            + jnp.dot(cqn, wqs_ref[...], preferred_element_type=F32) * sq_ref[...])
    ckv_n = _rms(ckv, gkv_ref[...])
    kpe_r = kpe * ck_ref[...] + kpe_sw * sk_ref[...]
    ckv_ref[...] = ckv_n
    kpe_ref[...] = kpe_r
    tm = p.shape[0]
    hd = kv_lora + LANE
    pad = jnp.zeros((tm, LANE - rope), CDT)
    kcat_ref[...] = jnp.concatenate([ckv_n.astype(CDT), kpe_r.astype(CDT), pad], axis=1)
    for h in range(n_heads):
        q_lat = _dot(q_nope[:, h * nope:(h + 1) * nope], wuk_ref[h])
        qcat_ref[:, h * hd:(h + 1) * hd] = jnp.concatenate(
            [q_lat.astype(CDT), q_pe[:, h * rope:(h + 1) * rope].astype(CDT), pad], axis=1)


def _mla_prep(p, g_q, g_kv, tabs, wqn, wqp, wqs, wukT, dims):
    n_heads, q_lora, kv_lora, rope, nope = dims
    M, NP = p.shape
    cosq, sinq, cosk, sink = tabs
    tm = _tile(M, 256, 8)
    ntab = cosq.shape[0] // tm
    hd = kv_lora + LANE
    row = lambda i: (i, 0)
    tab = lambda i: (i % ntab, 0)
    full2 = lambda i: (0, 0)
    full3 = lambda i: (0, 0, 0)
    kern = functools.partial(_mla_prep_kernel, n_heads=n_heads, q_lora=q_lora, kv_lora=kv_lora, rope=rope, nope=nope)
    return pl.pallas_call(
        kern,
        grid=(M // tm,),
        in_specs=[pl.BlockSpec((tm, NP), row),
                  pl.BlockSpec((1, q_lora), full2),
                  pl.BlockSpec((1, kv_lora), full2),
                  pl.BlockSpec((tm, n_heads * rope), tab),
                  pl.BlockSpec((tm, n_heads * rope), tab),
                  pl.BlockSpec((tm, rope), tab),
                  pl.BlockSpec((tm, rope), tab),
                  pl.BlockSpec(wqn.shape, full2),
                  pl.BlockSpec(wqp.shape, full2),
                  pl.BlockSpec(wqs.shape, full2),
                  pl.BlockSpec(wukT.shape, full3)],
        out_specs=[pl.BlockSpec((tm, kv_lora), row),
                   pl.BlockSpec((tm, rope), row),
                   pl.BlockSpec((tm, hd), row),
                   pl.BlockSpec((tm, n_heads * hd), row)],
        out_shape=[jax.ShapeDtypeStruct((M, kv_lora), F32),
                   jax.ShapeDtypeStruct((M, rope), F32),
                   jax.ShapeDtypeStruct((M, hd), CDT),
                   jax.ShapeDtypeStruct((M, n_heads * hd), CDT)],
        compiler_params=_cparams(("parallel",)),
        name="mla_prep",
    )(p, g_q.reshape(1, -1), g_kv.reshape(1, -1), cosq, sinq, cosk, sink, wqn, wqp, wqs, wukT)


def _mla_flash_kernel(q_ref, k_ref, o_ref, m_ref, l_ref, acc_ref, s2_ref, p2_ref, pm_ref, ps_ref, *,
                      n_heads, tq, tk, hd, dv, scale, rc):
    qi = pl.program_id(1)
    ki = pl.program_id(2)

    @pl.when(ki == 0)
    def _():
        m_ref[...] = jnp.full(m_ref.shape, NEG_INF, F32)
        l_ref[...] = jnp.zeros(l_ref.shape, F32)
        acc_ref[...] = jnp.zeros(acc_ref.shape, F32)

    c = scale * math.log2(math.e)

    def block(masked):
        k = k_ref[...]
        v = k[:, :dv]
        tiles = lambda a: [a[:, t * LANE:(t + 1) * LANE] for t in range(a.shape[1] // LANE)]

        def chunk_logits(s_ref, r):
            s = s_ref[r * rc:(r + 1) * rc, :]
            if masked:
                row = qi * tq + r * rc + lax.broadcasted_iota(jnp.int32, (rc, tk), 0)
                col = ki * tk + lax.broadcasted_iota(jnp.int32, (rc, tk), 1)
                s = jnp.where(col <= row, s, NEG_INF)
            return s

        for h in range(n_heads):
            b = h % 2
            s_ref, p_ref = s2_ref.at[b], p2_ref.at[b]
            s_ref[...] = _dot_nt(q_ref[:, h * hd:(h + 1) * hd], k)
            for r in range(tq // rc):
                pm_ref[b, r * rc:(r + 1) * rc, :] = functools.reduce(jnp.maximum, tiles(chunk_logits(s_ref, r)))
            m_prev = m_ref[h]
            m_new = jnp.maximum(m_prev, jnp.broadcast_to(jnp.max(pm_ref[b], axis=-1, keepdims=True), (tq, LANE)))
            alpha = jnp.exp2(c * (m_prev - m_new))
            m_ref[h] = m_new
            pm_ref[b] = m_new
            for r in range(tq // rc):
                mb = pm_ref[b, r * rc:(r + 1) * rc, :]
                p = [jnp.exp2(c * (st - mb)) for st in tiles(chunk_logits(s_ref, r))]
                ps_ref[b, r * rc:(r + 1) * rc, :] = functools.reduce(jnp.add, p)
                p_ref[r * rc:(r + 1) * rc, :] = jnp.concatenate(p, axis=1).astype(CDT)
            l_blk = jnp.broadcast_to(jnp.sum(ps_ref[b], axis=-1, keepdims=True), (tq, LANE))
            l_ref[h] = alpha * l_ref[h] + l_blk
            pv = jnp.dot(p_ref[...], v, preferred_element_type=F32)
            acc_ref[h] = jnp.concatenate([alpha] * (dv // LANE), axis=1) * acc_ref[h] + pv

    below_diag = ki * tk + tk - 1 <= qi * tq
    pl.when(below_diag)(functools.partial(block, False))
    pl.when(jnp.logical_not(below_diag) & (ki * tk <= qi * tq + tq - 1))(functools.partial(block, True))

    @pl.when(ki == pl.num_programs(2) - 1)
    def _():
        for h in range(n_heads):
            l = jnp.concatenate([l_ref[h]] * (dv // LANE), axis=1)
            o_ref[:, h * dv:(h + 1) * dv] = (acc_ref[h] / l).astype(o_ref.dtype)


def _mla_flash(qcat, kcat, B, T, n_heads, dv, scale):
    M, hd = kcat.shape
    tq = _tile(T, FLASH_TQ, 8)
    tk = _tile(T, FLASH_TK, 8)
    nq, nk = T // tq, T // tk
    assert tk % LANE == 0 and dv % LANE == 0
    rc = math.gcd(tq, FLASH_ROW_CHUNK)
    kern = functools.partial(_mla_flash_kernel, n_heads=n_heads, tq=tq, tk=tk, hd=hd, dv=dv, scale=scale, rc=rc)

    def kmap(b, qi, ki):
        return (b * nk + jnp.minimum(ki, (qi * tq + tq - 1) // tk), 0)

    return pl.pallas_call(
        kern,
        grid=(B, nq, nk),
        in_specs=[pl.BlockSpec((tq, n_heads * hd), lambda b, qi, ki: (b * nq + qi, 0)),
                  pl.BlockSpec((tk, hd), kmap)],
        out_specs=pl.BlockSpec((tq, n_heads * dv), lambda b, qi, ki: (b * nq + qi, 0)),
        out_shape=jax.ShapeDtypeStruct((M, n_heads * dv), CDT),
        scratch_shapes=[pltpu.VMEM((n_heads, tq, LANE), F32),
                        pltpu.VMEM((n_heads, tq, LANE), F32),
                        pltpu.VMEM((n_heads, tq, dv), F32),
                        pltpu.VMEM((2, tq, tk), F32),
                        pltpu.VMEM((2, tq, tk), CDT),
                        pltpu.VMEM((2, tq, LANE), F32),
                        pltpu.VMEM((2, tq, LANE), F32)],
        compiler_params=_cparams(("parallel", "parallel", "arbitrary")),
        name="mla_flash",
    )(qcat, kcat)


def _mla_decode_kernel(pt_ref, q_ref, kn_ref, *refs, n_pages_step, n_heads, t_new, dv, rope, scale):
    G = n_pages_step
    ckv_refs = refs[:G]
    kpe_refs = refs[G:2 * G]
    o_ref, m_ref, l_ref, acc_ref = refs[2 * G:]
    g = pl.program_id(1)

    @pl.when(g == 0)
    def _():
        m_ref[...] = jnp.full(m_ref.shape, NEG_INF, F32)
        l_ref[...] = jnp.zeros(l_ref.shape, F32)
        acc_ref[...] = jnp.zeros(acc_ref.shape, F32)

    q = q_ref[0]

    def update(s, v):
        m_prev = m_ref[...]
        m_new = jnp.maximum(m_prev, jnp.max(s, axis=-1, keepdims=True))
        alpha = jnp.exp(m_prev - m_new)
        p = jnp.exp(s - m_new)
        l_ref[...] = alpha * l_ref[...] + jnp.sum(p, axis=-1, keepdims=True)
        acc_ref[...] = alpha * acc_ref[...] + _dot(p, v)
        m_ref[...] = m_new

    ckv = jnp.concatenate([r[0, 0] for r in ckv_refs], axis=0).astype(CDT)
    kpe_t = jnp.concatenate([r[0, 0] for r in kpe_refs], axis=1).astype(CDT)
    s = (_dot_nt(q[:, :dv], ckv) + _dot(q[:, dv:dv + rope], kpe_t)) * scale
    update(s, ckv)

    @pl.when(g == pl.num_programs(1) - 1)
    def _():
        kn = kn_ref[0]
        s2 = _dot_nt(q, kn) * scale
        r, c = s2.shape
        t_row = lax.broadcasted_iota(jnp.int32, (r, c), 0) // n_heads
        col = lax.broadcasted_iota(jnp.int32, (r, c), 1)
        s2 = jnp.where((col <= t_row) & (col < t_new), s2, NEG_INF)
        update(s2, kn[:, :dv])
        o_ref[0] = (acc_ref[...] / l_ref[...]).astype(o_ref.dtype)


def _mla_decode(qcat, kcat, cache_ckv, cache_kpe_t, layer, page_table, B, Ts, n_heads, dv, rope, scale):
    hd = kcat.shape[1]
    n_pages = page_table.shape[1]
    page = cache_ckv.shape[2]
    G = math.gcd(n_pages, PAGES_PER_STEP_MLA)
    NG = n_pages // G
    R = Ts * n_heads
    q3 = qcat.reshape(B, R, hd)
    npad = 16
    kn = jnp.pad(kcat.reshape(B, Ts, hd), ((0, 0), (0, npad - Ts), (0, 0)))
    kern = functools.partial(_mla_decode_kernel, n_pages_step=G, n_heads=n_heads, t_new=Ts, dv=dv, rope=rope,
                             scale=scale)

    def page_map(i):
        return lambda b, g, pt: (layer, pt[b, g * G + i], 0, 0)

    in_specs = [pl.BlockSpec((1, R, hd), lambda b, g, pt: (b, 0, 0)),
                pl.BlockSpec((1, npad, hd), lambda b, g, pt: (b, 0, 0))]
    in_specs += [pl.BlockSpec((1, 1, page, dv), page_map(i)) for i in range(G)]
    in_specs += [pl.BlockSpec((1, 1, rope, page), page_map(i)) for i in range(G)]
    out = pl.pallas_call(
        kern,
        grid_spec=pltpu.PrefetchScalarGridSpec(
            num_scalar_prefetch=1,
            grid=(B, NG),
            in_specs=in_specs,
            out_specs=pl.BlockSpec((1, R, dv), lambda b, g, pt: (b, 0, 0)),
            scratch_shapes=[pltpu.VMEM((R, 1), F32), pltpu.VMEM((R, 1), F32), pltpu.VMEM((R, dv), F32)]),
        out_shape=jax.ShapeDtypeStruct((B, R, dv), CDT),
        compiler_params=_cparams(("parallel", "arbitrary")),
        name="mla_decode",
    )(page_table, q3, kn, *([cache_ckv] * G), *([cache_kpe_t] * G))
    return out.reshape(B * Ts, n_heads * dv)


def _mla_out_kernel(o_ref, wuv_ref, wo_ref, x_ref, gt_ref, y_ref, *, n_heads, dv):
    parts = [_dot(o_ref[:, h * dv:(h + 1) * dv], wuv_ref[h]).astype(CDT) for h in range(n_heads)]
    o = jnp.concatenate(parts, axis=1)
    y = jnp.dot(o, wo_ref[...], preferred_element_type=F32)
    y_ref[...] = x_ref[...] + gt_ref[0] * y


def _mla_out(o_lat, wuv, wo, x, gt, rows_per_group, n_heads, dv):
    M, D = x.shape
    tm = _tile(M, 512, 8)
    tpg = max(rows_per_group // tm, 1)
    kern = functools.partial(_mla_out_kernel, n_heads=n_heads, dv=dv)
    return pl.pallas_call(
        kern,
        grid=(M // tm,),
        in_specs=[pl.BlockSpec((tm, n_heads * dv), lambda i: (i, 0)),
                  pl.BlockSpec(wuv.shape, lambda i: (0, 0, 0)),
                  pl.BlockSpec(wo.shape, lambda i: (0, 0)),
                  pl.BlockSpec((tm, D), lambda i: (i, 0)),
                  pl.BlockSpec((1, gt.shape[1], D), lambda i: (i // tpg, 0, 0))],
        out_specs=pl.BlockSpec((tm, D), lambda i: (i, 0)),
        out_shape=jax.ShapeDtypeStruct((M, D), F32),
        compiler_params=_cparams(("parallel",)),
        name="mla_out",
    )(o_lat, wuv, wo, x, gt)


def _proj_res_kernel(o_ref, wo_ref, x_ref, gt_ref, y_ref):
    y = jnp.dot(o_ref[...], wo_ref[...], preferred_element_type=F32)
    y_ref[...] = x_ref[...] + gt_ref[0] * y


def _proj_res(o, wo, x, gt, rows_per_group):
    M, D = x.shape
    K = o.shape[1]
    tm = _tile(M, 512, 8)
    tpg = max(rows_per_group // tm, 1)
    return pl.pallas_call(
        _proj_res_kernel,
        grid=(M // tm,),
        in_specs=[pl.BlockSpec((tm, K), lambda i: (i, 0)),
                  pl.BlockSpec(wo.shape, lambda i: (0, 0)),
                  pl.BlockSpec((tm, D), lambda i: (i, 0)),
                  pl.BlockSpec((1, gt.shape[1], D), lambda i: (i // tpg, 0, 0))],
        out_specs=pl.BlockSpec((tm, D), lambda i: (i, 0)),
        out_shape=jax.ShapeDtypeStruct((M, D), F32),
        compiler_params=_cparams(("parallel",)),
        name="proj_res",
    )(o, wo, x, gt)


def _gla_gate_kernel(a_ref, w_ref, b_ref, o_ref):
    z = _dot(a_ref[...], w_ref[...]) + b_ref[...]
    o_ref[...] = (jnp.minimum(z, 0.0) - jnp.log(1.0 + jnp.exp(-jnp.abs(z)))) / GLA_TAU


def _gla_gate(proj, col_block, w_a2p, b_a2):
    M = proj.shape[0]
    N = w_a2p.shape[1]
    tm = _tile(M, 1024, 8)
    return pl.pallas_call(
        _gla_gate_kernel,
        grid=(M // tm,),
        in_specs=[pl.BlockSpec((tm, LANE), lambda i: (i, col_block)),
                  pl.BlockSpec(w_a2p.shape, lambda i: (0, 0)),
                  pl.BlockSpec((1, N), lambda i: (0, 0))],
        out_specs=pl.BlockSpec((tm, N), lambda i: (i, 0)),
        out_shape=jax.ShapeDtypeStruct((M, N), F32),
        compiler_params=_cparams(("parallel",)),
        name="gla_gate",
    )(proj, w_a2p, b_a2.reshape(1, N))


def _cumsum_rows(x):
    C = x.shape[0]
    row = lax.broadcasted_iota(jnp.int32, x.shape, 0)
    if C <= 8:
        out = jnp.zeros_like(x)
        for s in range(C):
            out = out + jnp.where(row >= s, x[s:s + 1], 0.0)
        return out
    sh = 1
    while sh < C:
        x = x + jnp.where(row >= sh, pltpu.roll(x, sh, 0), 0.0)
        sh *= 2
    return x


def _gla_kernel(q_ref, k_ref, v_ref, la_ref, s0_ref, o_ref, sf_ref, st_ref, *, n_heads, dk, dv, sub, qscale):
    c = pl.program_id(1)

    @pl.when(c == 0)
    def _():
        for h in range(n_heads):
            st_ref[h] = s0_ref[0, h].T

    C = q_ref.shape[1]
    nsub = C // sub
    for h in range(n_heads):
        q = q_ref[0, :, h * dk:(h + 1) * dk] * qscale
        k = k_ref[0, :, h * dk:(h + 1) * dk]
        v = v_ref[0, :, h * dv:(h + 1) * dv]
        b = _cumsum_rows(la_ref[0, :, h * dk:(h + 1) * dk])
        st = st_ref[h]
        o_inter = _dot_nt(q * jnp.exp(b), st)
        b_last = b[C - 1:C]
        k_dec = k * jnp.exp(b_last - b)
        st_ref[h] = jnp.exp(b_last) * st + _dot_tn(v, k_dec)
        outs = []
        for i in range(nsub):
            r0 = i * sub
            b_i = b[r0:r0 + sub]
            q_i = q[r0:r0 + sub]
            k_i = k[r0:r0 + sub]
            v_i = v[r0:r0 + sub]
            o_i = o_inter[r0:r0 + sub]
            if i > 0:
                ref_row = b[r0:r0 + 1]
                att = _dot_nt(q_i * jnp.exp(b_i - ref_row), k[:r0] * jnp.exp(ref_row - b[:r0]))
                o_i = o_i + _dot(att, v[:r0])
            t_loc = lax.broadcasted_iota(jnp.int32, (sub, 1), 0)
            for s in range(sub):
                w = jnp.exp(jnp.minimum(b_i - b_i[s:s + 1], 0.0))
                col = jnp.sum(q_i * w * k_i[s:s + 1], axis=-1, keepdims=True)
                o_i = o_i + jnp.where(t_loc >= s, col, 0.0) * v_i[s:s + 1]
            outs.append(o_i)
        o_ref[0, :, h * dv:(h + 1) * dv] = outs[0] if nsub == 1 else jnp.concatenate(outs, axis=0)

    @pl.when(c == pl.num_programs(1) - 1)
    def _():
        for h in range(n_heads):
            sf_ref[0, h] = st_ref[h].T


def _gla_recurrence(proj, log_a, s0, B, T, n_heads, dk, dv):
    C = 64 if T % 64 == 0 else T
    sub = min(16, C)
    nc = T // C
    Np = proj.shape[1]
    p3 = proj.reshape(B * nc, C, Np)
    la3 = log_a.reshape(B * nc, C, n_heads * dk)
    hk, hv = n_heads * dk, n_heads * dv
    assert hv % hk == 0
    kern = functools.partial(_gla_kernel, n_heads=n_heads, dk=dk, dv=dv, sub=sub, qscale=dk ** -0.5)
    o, sf = pl.pallas_call(
        kern,
        grid=(B, nc),
        in_specs=[pl.BlockSpec((1, C, hk), lambda b, c: (b * nc + c, 0, 0)),
                  pl.BlockSpec((1, C, hk), lambda b, c: (b * nc + c, 0, 1)),
                  pl.BlockSpec((1, C, hv), lambda b, c: (b * nc + c, 0, 2 * hk // hv)),
                  pl.BlockSpec((1, C, hk), lambda b, c: (b * nc + c, 0, 0)),
                  pl.BlockSpec((1, n_heads, dk, dv), lambda b, c: (b, 0, 0, 0))],
        out_specs=[pl.BlockSpec((1, C, hv), lambda b, c: (b * nc + c, 0, 0)),
                   pl.BlockSpec((1, n_heads, dk, dv), lambda b, c: (b, 0, 0, 0))],
        out_shape=[jax.ShapeDtypeStruct((B * nc, C, hv), F32),
                   jax.ShapeDtypeStruct((B, n_heads, dk, dv), F32)],
        scratch_shapes=[pltpu.VMEM((n_heads, dv, dk), F32)],
        compiler_params=_cparams(("parallel", "arbitrary")),
        name="gla_recurrence",
    )(p3, p3, p3, la3, s0)
    return o.reshape(B * T, hv), sf


def _gla_out_kernel(o_ref, r_ref, g_ref, wo_ref, x_ref, gt_ref, y_ref, *, n_heads, dv):
    parts = []
    for h in range(n_heads):
        sl = slice(h * dv, (h + 1) * dv)
        parts.append((_rms(o_ref[:, sl], g_ref[...]) * _silu(r_ref[:, sl])).astype(CDT))
    y = jnp.dot(jnp.concatenate(parts, axis=1), wo_ref[...], preferred_element_type=F32)
    y_ref[...] = x_ref[...] + gt_ref[0] * y


def _gla_out(o, proj, r_block, g_o, wo, x, gt, rows_per_group, n_heads, dv):
    M, D = x.shape
    hv = n_heads * dv
    tm = _tile(M, 512, 8)
    tpg = max(rows_per_group // tm, 1)
    kern = functools.partial(_gla_out_kernel, n_heads=n_heads, dv=dv)
    return pl.pallas_call(
        kern,
        grid=(M // tm,),
        in_specs=[pl.BlockSpec((tm, hv), lambda i: (i, 0)),
                  pl.BlockSpec((tm, hv), lambda i: (i, r_block)),
                  pl.BlockSpec((1, dv), lambda i: (0, 0)),
                  pl.BlockSpec(wo.shape, lambda i: (0, 0)),
                  pl.BlockSpec((tm, D), lambda i: (i, 0)),
                  pl.BlockSpec((1, gt.shape[1], D), lambda i: (i // tpg, 0, 0))],
        out_specs=pl.BlockSpec((tm, D), lambda i: (i, 0)),
        out_shape=jax.ShapeDtypeStruct((M, D), F32),
        compiler_params=_cparams(("parallel",)),
        name="gla_out",
    )(o, proj, g_o.reshape(1, dv), wo, x, gt)


def _order_key(score):
    score = jnp.where(score == 0.0, 0.0, score)
    bits = pltpu.bitcast(score, jnp.int32)
    return jnp.where(bits < 0, bits ^ jnp.int32(0x7FFFFFFF), bits)


def _sum_rows(x, chains=8):
    n = x.shape[0]
    if n % (8 * chains):
        return jnp.sum(x, axis=0, keepdims=True)
    part = jnp.sum(x.reshape(chains, n // chains, *x.shape[1:]), axis=1)
    return jnp.sum(part, axis=0, keepdims=True)


def _kth_largest_key(key, topk, axes, two_bit_steps):
    shape = tuple(1 if a in axes else s for a, s in enumerate(key.shape))

    def count(cand):
        hit = jnp.where(key >= cand, 1, 0)
        return _sum_rows(hit) if axes == (0,) else jnp.sum(hit, axis=axes, keepdims=True)

    def body1(it, t):
        cand = t + lax.shift_left(jnp.int32(1), jnp.int32(31) - it)
        return jnp.where(count(cand) >= topk, cand, t)

    def body2(it, t):
        hi = lax.shift_left(jnp.int32(1), jnp.int32(31) - 2 * it)
        lo = lax.shift_left(jnp.int32(1), jnp.int32(30) - 2 * it)
        t01, t10, t11 = t + lo, t + hi, t + hi + lo
        c01, c10, c11 = count(t01), count(t10), count(t11)
        return jnp.where(c11 >= topk, t11, jnp.where(c10 >= topk, t10, jnp.where(c01 >= topk, t01, t)))

    init = jnp.full(shape, INT_MIN, jnp.int32)
    return lax.fori_loop(0, 16, body2, init) if two_bit_steps else lax.fori_loop(0, 32, body1, init)


def _bias_chain(dist, value_of_bucket):
    val = value_of_bucket(0)
    for j in range(1, N_BUCKETS):
        val = jnp.where(dist >= BUCKET_LO[j], value_of_bucket(j), val)
    return val


def _strict_upper(n):
    a = lax.broadcasted_iota(jnp.int32, (n, n), 0)
    b = lax.broadcasted_iota(jnp.int32, (n, n), 1)
    return jnp.where(a < b, 1.0, 0.0).astype(CDT)


def _strict_lower(n):
    a = lax.broadcasted_iota(jnp.int32, (n, n), 0)
    b = lax.broadcasted_iota(jnp.int32, (n, n), 1)
    return jnp.where(b < a, 1.0, 0.0).astype(CDT)


def _dsa_prompt_kernel(rb_ref, q_ref, qi_ref, wq_ref, k_ref, v_ref, kk_ref, o_ref, sel_ref, band_ref, selt_ref, *,
                       n_heads, n_kv, hd, n_idx, di, topk, tq, T, n_cls):
    b_id = pl.program_id(0)
    qt = pl.program_id(1)
    group = n_heads // n_kv

    @pl.when((b_id == 0) & (qt == 0))
    def _():
        i = lax.broadcasted_iota(jnp.int32, (tq, LANE), 0)
        j = lax.broadcasted_iota(jnp.int32, (tq, LANE), 1)
        for h in range(n_heads):
            far = rb_ref[N_BUCKETS - 1, h]
            for w in range(tq // LANE + 1):
                dist = i - j + (w - (tq // LANE - 1)) * LANE
                band_ref[w, h] = _bias_chain(dist, lambda bk: rb_ref[bk, h]) - far

    def body(nk):
        key_pos = lax.broadcasted_iota(jnp.int32, (nk, tq), 0)
        q_pos = qt * tq + lax.broadcasted_iota(jnp.int32, (nk, tq), 1)
        causal = key_pos <= q_pos

        ki = kk_ref[:nk, :di]
        wi_t = jnp.transpose(wq_ref[...])[di:di + n_idx] * (n_idx ** -0.5 * di ** -0.5)
        score = jnp.zeros((nk, tq), F32)
        for h in range(n_idx):
            lg = _dot_nt(ki, qi_ref[:, h * di:(h + 1) * di])
            score = score + wi_t[h:h + 1] * jnp.maximum(lg, 0.0)
        score = jnp.where(causal, score, NEG_INF)
        key = _order_key(score)
        thr = _kth_largest_key(key, topk, (0,), two_bit_steps=False)
        gt = key > thr
        eq = key == thr
        n_gt = _sum_rows(jnp.where(gt, 1, 0))
        n_eq = _sum_rows(jnp.where(eq, 1, 0))
        need = topk - n_gt
        tie = jnp.max(n_eq - need) > 0

        @pl.when(jnp.logical_not(tie))
        def _():
            selt_ref[:nk, :] = jnp.where((gt | eq) & causal, 0.0, NEG_INF)

        @pl.when(tie)
        def _():
            lower = _strict_lower(LANE)
            run = jnp.zeros((1, tq), F32)
            needf = need.astype(F32)
            for kb in range(nk // LANE):
                sl = slice(kb * LANE, (kb + 1) * LANE)
                eqb = jnp.where(eq[sl], 1.0, 0.0)
                pre = jnp.dot(lower, eqb.astype(CDT), preferred_element_type=F32) + run
                keep = gt[sl] | (eq[sl] & (pre < needf))
                selt_ref[sl, :] = jnp.where(keep & causal[sl], 0.0, NEG_INF)
                run = run + jnp.sum(eqb, axis=0, keepdims=True)

        for kb in range(nk // LANE):
            sel_ref[:, kb * LANE:(kb + 1) * LANE] = jnp.transpose(selt_ref[kb * LANE:(kb + 1) * LANE, :])

        scale = hd ** -0.5
        nband = tq // LANE + 1
        nkb = nk // LANE
        first_near = max(nkb - (T // n_cls) // LANE - tq // LANE, 0)
        for g in range(n_kv):
            kg = k_ref[:nk, g * hd:(g + 1) * hd]
            vg = v_ref[:nk, g * hd:(g + 1) * hd]
            for r in range(group):
                h = g * group + r
                s = _dot_nt(q_ref[:, h * hd:(h + 1) * hd], kg) * scale + rb_ref[N_BUCKETS - 1, h] + sel_ref[:, :nk]
                parts = [s[:, :first_near * LANE]] if first_near else []
                for kb in range(first_near, nkb):
                    w = qt * (tq // LANE) - kb + (tq // LANE - 1)
                    add = jnp.zeros((tq, LANE), F32)
                    for wv in range(nband):
                        add = jnp.where(w == wv, band_ref[wv, h], add)
                    parts.append(s[:, kb * LANE:(kb + 1) * LANE] + add)
                s = jnp.concatenate(parts, axis=1)
                m = jnp.max(s, axis=-1, keepdims=True)
                p = jnp.exp(s - m)
                l = jnp.sum(p, axis=-1, keepdims=True)
                o_ref[:, h * hd:(h + 1) * hd] = (_dot(p, vg) / l).astype(o_ref.dtype)

    width = T // n_cls
    cls = ((qt + 1) * tq - 1) // width
    for c in range(n_cls):
        pl.when(cls == c)(functools.partial(body, (c + 1) * width))


def _dsa_prompt(proj, proj_c, rel_bias, B, T, dims, topk):
    n_heads, n_kv, hd, n_idx, di = dims
    M, Np = proj.shape
    tq = math.gcd(T, DSA_TQ)
    assert tq % LANE == 0
    nqt = T // tq
    hq, hkv, hi = n_heads * hd, n_kv * hd, n_idx * di
    assert hq % hkv == 0 and (hq + 2 * hkv) % hi == 0 and (hq + 2 * hkv + hi) % LANE == 0
    kk_blk = (hq + 2 * hkv + hi) // LANE
    n_cls = math.gcd(T // LANE, DSA_EXTENT_CLASSES)
    kern = functools.partial(_dsa_prompt_kernel, n_heads=n_heads, n_kv=n_kv, hd=hd, n_idx=n_idx, di=di, topk=topk,
                             tq=tq, T=T, n_cls=n_cls)
    return pl.pallas_call(
        kern,
        grid=(B, nqt),
        in_specs=[pl.BlockSpec(memory_space=pltpu.SMEM),
                  pl.BlockSpec((tq, hq), lambda b, t: (b * nqt + t, 0)),
                  pl.BlockSpec((tq, hi), lambda b, t: (b * nqt + t, (hq + 2 * hkv) // hi)),
                  pl.BlockSpec((tq, LANE), lambda b, t: (b * nqt + t, kk_blk)),
                  pl.BlockSpec((T, hkv), lambda b, t: (b, hq // hkv)),
                  pl.BlockSpec((T, hkv), lambda b, t: (b, hq // hkv + 1)),
                  pl.BlockSpec((T, LANE), lambda b, t: (b, kk_blk))],
        out_specs=pl.BlockSpec((tq, hq), lambda b, t: (b * nqt + t, 0)),
        out_shape=jax.ShapeDtypeStruct((M, hq), CDT),
        scratch_shapes=[pltpu.VMEM((tq, T), F32),
                        pltpu.VMEM((tq // LANE + 1, n_heads, tq, LANE), F32),
                        pltpu.VMEM((T, tq), F32)],
        compiler_params=_cparams(("arbitrary", "arbitrary")),
        name="dsa_prompt",
    )(rel_bias, proj_c, proj_c, proj, proj_c, proj_c, proj_c)


def _dsa_select_kernel(pt_ref, qi_ref, wi_ref, kn_ref, *refs, n_pages_step, n_idx, di, topk, t_new):
    G = n_pages_step
    ki_refs = refs[:G]
    mask_ref, sc_ref = refs[G:]
    g = pl.program_id(1)
    NG = pl.num_programs(1)
    R = qi_ref.shape[1] // n_idx
    qi = qi_ref[0]
    wi = wi_ref[0] * (n_idx ** -0.5)

    def head_sum(lg):
        w = wi * jnp.maximum(lg * (di ** -0.5), 0.0)
        sc = w[0:R]
        for h in range(1, n_idx):
            sc = sc + w[h * R:(h + 1) * R]
        return sc

    past = head_sum(_dot(qi, jnp.concatenate([r[0, 0] for r in ki_refs], axis=1)))
    GW = past.shape[1]
    sc_ref[g] = past

    @pl.when(g == NG - 1)
    def _():
        new = head_sum(_dot_nt(qi, kn_ref[0][:, :di]))
        t_row = lax.broadcasted_iota(jnp.int32, new.shape, 0) % t_new
        col = lax.broadcasted_iota(jnp.int32, new.shape, 1)
        new = jnp.where((col <= t_row) & (col < t_new), new, NEG_INF)
        sc_ref[NG] = jnp.concatenate([new, jnp.full((R, GW - LANE), NEG_INF, F32)], axis=1)
        score = sc_ref[...]
        valid = score > NEG_INF
        key = _order_key(score)
        thr = _kth_largest_key(key, topk, (0, 2), two_bit_steps=True)
        gt = key > thr
        eq = key == thr
        n_gt = jnp.sum(jnp.where(gt, 1, 0), axis=(0, 2), keepdims=True)
        n_eq = jnp.sum(jnp.where(eq, 1, 0), axis=(0, 2), keepdims=True)
        need = topk - n_gt
        tie = jnp.max(n_eq - need) > 0

        @pl.when(jnp.logical_not(tie))
        def _():
            mask_ref[0] = jnp.where((gt | eq) & valid, 0.0, NEG_INF)

        @pl.when(tie)
        def _():
            upper = _strict_upper(LANE)
            needf = need[0].astype(F32)

            def blk(gi, run):
                k_g = _order_key(sc_ref[gi])
                v_g = sc_ref[gi] > NEG_INF
                for c in range(GW // LANE):
                    sl = slice(c * LANE, (c + 1) * LANE)
                    eqb = jnp.where(k_g[:, sl] == thr[0], 1.0, 0.0)
                    pre = jnp.dot(eqb.astype(CDT), upper, preferred_element_type=F32) + run
                    keep = (k_g[:, sl] > thr[0]) | ((eqb > 0.0) & (pre < needf))
                    mask_ref[0, gi, :, sl] = jnp.where(keep & v_g[:, sl], 0.0, NEG_INF)
                    run = run + jnp.sum(eqb, axis=1, keepdims=True)
                return run

            lax.fori_loop(0, NG + 1, blk, jnp.zeros((R, 1), F32))


def _dsa_select(qi_hm, wi_hm, kn, cache_ki_t, layer, page_table, B, n_idx, di, topk, t_new):
    n_pages = page_table.shape[1]
    page = cache_ki_t.shape[3]
    G = math.gcd(n_pages, PAGES_PER_STEP_SELECT)
    NG = n_pages // G
    GW = G * page
    R = qi_hm.shape[1] // n_idx
    kern = functools.partial(_dsa_select_kernel, n_pages_step=G, n_idx=n_idx, di=di, topk=topk, t_new=t_new)

    def page_map(i):
        return lambda b, g, pt: (layer, pt[b, g * G + i], 0, 0)

    in_specs = [pl.BlockSpec((1, n_idx * R, di), lambda b, g, pt: (b, 0, 0)),
                pl.BlockSpec((1, n_idx * R, 1), lambda b, g, pt: (b, 0, 0)),
                pl.BlockSpec((1, LANE, LANE), lambda b, g, pt: (b, 0, 0))]
    in_specs += [pl.BlockSpec((1, 1, di, page), page_map(i)) for i in range(G)]
    return pl.pallas_call(
        kern,
        grid_spec=pltpu.PrefetchScalarGridSpec(
            num_scalar_prefetch=1,
            grid=(B, NG),
            in_specs=in_specs,
            out_specs=pl.BlockSpec((1, NG + 1, R, GW), lambda b, g, pt: (b, 0, 0, 0)),
            scratch_shapes=[pltpu.VMEM((NG + 1, R, GW), F32)]),
        out_shape=jax.ShapeDtypeStruct((B, NG + 1, R, GW), F32),
        compiler_params=_cparams(("parallel", "arbitrary")),
        name="dsa_select",
    )(page_table, qi_hm, wi_hm, kn, *([cache_ki_t] * G))


def _dsa_decode_kernel(pt_ref, rb_ref, q_ref, kn_ref, vn_ref, mask_ref, mnew_ref, *refs,
                       n_pages_step, n_kv, group, hd, t_new, past_len):
    G = n_pages_step
    k_refs = refs[:G]
    v_refs = refs[G:2 * G]
    o_ref, m_ref, l_ref, acc_ref = refs[2 * G:]
    g = pl.program_id(1)
    NG = pl.num_programs(1)
    R = group * t_new
    scale = hd ** -0.5

    @pl.when(g == 0)
    def _():
        m_ref[...] = jnp.full(m_ref.shape, NEG_INF, F32)
        l_ref[...] = jnp.zeros(l_ref.shape, F32)
        acc_ref[...] = jnp.zeros(acc_ref.shape, F32)

    def head_val(kvh, bucket):
        row = lax.broadcasted_iota(jnp.int32, (R, 1), 0)
        val = jnp.full((R, 1), rb_ref[bucket, kvh * group], F32)
        for r in range(1, group):
            val = jnp.where(row >= r * t_new, rb_ref[bucket, kvh * group + r], val)
        return val

    def update(kvh, s, v):
        m_prev = m_ref[kvh]
        m_new = jnp.maximum(m_prev, jnp.max(s, axis=-1, keepdims=True))
        m_safe = jnp.where(m_new > NEG_INF, m_new, 0.0)
        alpha = jnp.exp(m_prev - m_safe)
        p = jnp.exp(s - m_safe)
        l_ref[kvh] = alpha * l_ref[kvh] + jnp.sum(p, axis=-1, keepdims=True)
        acc_ref[kvh] = alpha * acc_ref[kvh] + _dot(p, v)
        m_ref[kvh] = m_new

    page = k_refs[0].shape[2] // n_kv
    GW = G * page
    mask = mask_ref[0, 0]
    t_row = lax.broadcasted_iota(jnp.int32, (R, GW), 0) % t_new
    col = lax.broadcasted_iota(jnp.int32, (R, GW), 1)
    dist = past_len + t_row - (g * GW + col)
    near = g == NG - 1

    def head_rows(refs_, kvh):
        return jnp.concatenate([r[0, 0, pl.ds(kvh, page, stride=n_kv), :] for r in refs_], axis=0).astype(CDT)

    for kvh in range(n_kv):
        s = _dot_nt(q_ref[0, kvh], head_rows(k_refs, kvh)) * scale
        bias = lax.cond(near,
                        lambda: _bias_chain(dist, functools.partial(head_val, kvh)),
                        lambda: jnp.broadcast_to(head_val(kvh, N_BUCKETS - 1), (R, GW)))
        update(kvh, s + bias + mask, head_rows(v_refs, kvh))

    @pl.when(near)
    def _():
        kn = kn_ref[0].astype(CDT)
        vn = vn_ref[0].astype(CDT)
        mnew = mnew_ref[0, 0][:, :LANE]
        t_r = lax.broadcasted_iota(jnp.int32, (R, LANE), 0) % t_new
        c = lax.broadcasted_iota(jnp.int32, (R, LANE), 1)
        d_new = jnp.maximum(t_r - c, 0)
        for kvh in range(n_kv):
            sl = slice(kvh * hd, (kvh + 1) * hd)
            s = _dot_nt(q_ref[0, kvh], kn[:, sl]) * scale
            bias = _bias_chain(d_new, functools.partial(head_val, kvh))
            update(kvh, s + bias + mnew, vn[:, sl])
            o_ref[0, kvh] = (acc_ref[kvh] / l_ref[kvh]).astype(o_ref.dtype)


def _dsa_decode(q4, kn, vn, mask, rel_bias, cache_k, cache_v, layer, page_table, B, n_kv, group, hd, t_new):
    n_pages = page_table.shape[1]
    page = cache_k.shape[2] // n_kv
    GWs = mask.shape[3]
    G = math.gcd(math.gcd(n_pages, PAGES_PER_STEP_DECODE), GWs // page)
    NG = n_pages // G
    GW = G * page
    ratio = GWs // GW
    R = group * t_new
    kern = functools.partial(_dsa_decode_kernel, n_pages_step=G, n_kv=n_kv, group=group, hd=hd, t_new=t_new,
                             past_len=n_pages * page)

    def page_map(i):
        return lambda b, g, pt: (layer, pt[b, g * G + i], 0, 0)

    in_specs = [pl.BlockSpec(memory_space=pltpu.SMEM),
                pl.BlockSpec((1, n_kv, R, hd), lambda b, g, pt: (b, 0, 0, 0)),
                pl.BlockSpec((1, LANE, n_kv * hd), lambda b, g, pt: (b, 0, 0)),
                pl.BlockSpec((1, LANE, n_kv * hd), lambda b, g, pt: (b, 0, 0)),
                pl.BlockSpec((1, 1, R, GW), lambda b, g, pt: (b, g // ratio, 0, g % ratio)),
                pl.BlockSpec((1, 1, R, GW), lambda b, g, pt: (b, mask.shape[1] - 1, 0, 0))]
    in_specs += [pl.BlockSpec((1, 1, page * n_kv, hd), page_map(i)) for i in range(G)]
    in_specs += [pl.BlockSpec((1, 1, page * n_kv, hd), page_map(i)) for i in range(G)]
    return pl.pallas_call(
        kern,
        grid_spec=pltpu.PrefetchScalarGridSpec(
            num_scalar_prefetch=1,
            grid=(B, NG),
            in_specs=in_specs,
            out_specs=pl.BlockSpec((1, n_kv, R, hd), lambda b, g, pt: (b, 0, 0, 0)),
            scratch_shapes=[pltpu.VMEM((n_kv, R, 1), F32), pltpu.VMEM((n_kv, R, 1), F32),
                            pltpu.VMEM((n_kv, R, hd), F32)]),
        out_shape=jax.ShapeDtypeStruct((B, n_kv, R, hd), CDT),
        compiler_params=_cparams(("parallel", "arbitrary")),
        name="dsa_decode",
    )(page_table, rel_bias, q4, kn, vn, mask, mask, *([cache_k] * G), *([cache_v] * G))


def _ffn_kernel(*refs, seq_tiles, t_seq, conv_w):
    if seq_tiles:
        (x_ref, g_ref, sc_ref, sh_ref, gt_ref, wg_ref, wv_ref, wd_ref, cw_ref, cb_ref,
         y_ref, tail_ref, h_ref, acc_ref, stash_ref) = refs
    else:
        (x_ref, g_ref, sc_ref, sh_ref, gt_ref, wg_ref, wv_ref, wd_ref, cw_ref, cb_ref, p1_ref, p2_ref,
         y_ref, tail_ref, h_ref, acc_ref) = refs
    i = pl.program_id(0)
    j = pl.program_id(1)

    @pl.when(j == 0)
    def _():
        h = _rms(x_ref[...], g_ref[...]) * (1.0 + sc_ref[0]) + sh_ref[0]
        h_ref[...] = h.astype(CDT)
        acc_ref[...] = jnp.zeros(acc_ref.shape, F32)

    if seq_tiles:
        @pl.when((i == 0) & (j == 0))
        def _():
            stash_ref[...] = jnp.zeros(stash_ref.shape, F32)

    gate = jnp.dot(h_ref[...], wg_ref[...], preferred_element_type=F32)
    val = jnp.dot(h_ref[...], wv_ref[...], preferred_element_type=F32)
    tm = gate.shape[0]
    row = lax.broadcasted_iota(jnp.int32, gate.shape, 0)
    g1 = pltpu.roll(gate, 1, 0)
    g2 = pltpu.roll(gate, 2, 0)
    if seq_tiles:
        prev = jnp.where(i % seq_tiles == 0, 0.0, stash_ref[j])
        g1 = jnp.where(row == 0, prev[7:8], g1)
        g2 = jnp.where(row == 0, prev[6:7], jnp.where(row == 1, prev[7:8], g2))
        stash_ref[j] = gate[tm - 8:]
        tail_ref[0] = gate[tm - 8:]
    else:
        t = row % t_seq
        g1 = jnp.where(t == 0, p1_ref[...], g1)
        g2 = jnp.where(t < 2, p2_ref[...], g2)
        tail_ref[...] = gate
    cw = cw_ref[...]
    conv = cw[0:1] * g2 + cw[1:2] * g1 + cw[2:3] * gate + cb_ref[...]
    act = (_silu(conv) * val).astype(CDT)
    acc_ref[...] += jnp.dot(act, wd_ref[...], preferred_element_type=F32)

    @pl.when(j == pl.num_programs(1) - 1)
    def _():
        y_ref[...] = x_ref[...] + gt_ref[0] * acc_ref[...]


def _ffn(x, g, sc, sh, gt, w_up, w_down, conv_w, conv_b, rows_per_group, t_seq, prev=None):
    M, D = x.shape
    Fd = w_down.shape[0]
    assert conv_w.shape[0] == 3
    tf = _tile(Fd, FFN_TILE_F)
    nf = Fd // tf
    cw = jnp.pad(conv_w, ((0, 8 - conv_w.shape[0]), (0, 0)))
    cb = conv_b.reshape(1, Fd)
    seq_mode = prev is None
    if seq_mode:
        tm = _tile(t_seq, 512, 8)
        assert t_seq % tm == 0 and tm >= 8
        seq_tiles = t_seq // tm
    else:
        tm = M
        seq_tiles = 0
    nt = M // tm
    tpg = max(rows_per_group // tm, 1)
    mod = lambda m: pl.BlockSpec((1, m.shape[1], D), lambda i, j: (i // tpg, 0, 0))
    in_specs = [pl.BlockSpec((tm, D), lambda i, j: (i, 0)),
                pl.BlockSpec((1, D), lambda i, j: (0, 0)),
                mod(sc), mod(sh), mod(gt),
                pl.BlockSpec((D, tf), lambda i, j: (0, j)),
                pl.BlockSpec((D, tf), lambda i, j: (0, nf + j)),
                pl.BlockSpec((tf, D), lambda i, j: (j, 0)),
                pl.BlockSpec((8, tf), lambda i, j: (0, j)),
                pl.BlockSpec((1, tf), lambda i, j: (0, j))]
    args = [x, g.reshape(1, D), sc, sh, gt, w_up, w_up, w_down, cw, cb]
    scratch = [pltpu.VMEM((tm, D), CDT), pltpu.VMEM((tm, D), F32)]
    if seq_mode:
        tail_shape = jax.ShapeDtypeStruct((nt, 8, Fd), F32)
        tail_spec = pl.BlockSpec((1, 8, tf), lambda i, j: (i, 0, j))
        scratch.append(pltpu.VMEM((nf, 8, tf), F32))
    else:
        p1 = jnp.concatenate([prev[:, 1:2], jnp.zeros_like(prev[:, :1]).repeat(t_seq - 1, axis=1)], axis=1)
        p2 = jnp.concatenate([prev[:, 0:2], jnp.zeros_like(prev[:, :1]).repeat(t_seq - 2, axis=1)], axis=1)
        args += [p1.reshape(M, Fd), p2.reshape(M, Fd)]
        in_specs += [pl.BlockSpec((tm, tf), lambda i, j: (i, j))] * 2
        tail_shape = jax.ShapeDtypeStruct((M, Fd), F32)
        tail_spec = pl.BlockSpec((tm, tf), lambda i, j: (i, j))
    kern = functools.partial(_ffn_kernel, seq_tiles=seq_tiles, t_seq=t_seq, conv_w=conv_w.shape[0])
    return pl.pallas_call(
        kern,
        grid=(nt, nf),
        in_specs=in_specs,
        out_specs=[pl.BlockSpec((tm, D), lambda i, j: (i, 0)), tail_spec],
        out_shape=[jax.ShapeDtypeStruct((M, D), F32), tail_shape],
        scratch_shapes=scratch,
        compiler_params=_cparams(("arbitrary", "arbitrary")),
        name="ffn",
    )(*args)


def _final_norm_kernel(x_ref, g_ref, o_ref):
    o_ref[...] = _rms(x_ref[...], g_ref[...])


def _final_norm(x, g):
    M, D = x.shape
    tm = _tile(M, 1024, 8)
    return pl.pallas_call(
        _final_norm_kernel,
        grid=(M // tm,),
        in_specs=[pl.BlockSpec((tm, D), lambda i: (i, 0)), pl.BlockSpec((1, D), lambda i: (0, 0))],
        out_specs=pl.BlockSpec((tm, D), lambda i: (i, 0)),
        out_shape=jax.ShapeDtypeStruct((M, D), F32),
        compiler_params=_cparams(("parallel",)),
        name="final_norm",
    )(x, g.reshape(1, D))


def _rope_tables(pos, rope, n_heads, reps):
    half = rope // 2
    inv = ROPE_BASE ** (-jnp.arange(half, dtype=F32) / half)
    ang = pos.astype(F32)[:, None] * inv[None, :]
    cos = jnp.cos(ang)
    sin = jnp.sin(ang)
    cosk = jnp.concatenate([cos, cos], axis=1)
    sink = jnp.concatenate([-sin, sin], axis=1)
    tabs = (jnp.tile(cosk, (1, n_heads)), jnp.tile(sink, (1, n_heads)), cosk, sink)
    return tuple(jnp.tile(t, (reps, 1)) for t in tabs)


def _swap_halves(w, rope):
    half = rope // 2
    return jnp.concatenate([w[..., half:], w[..., :half]], axis=-1)


def _pad_cols(w, n):
    return jnp.pad(w, ((0, 0), (0, n - w.shape[1])))


def kernel(x_prompt, x_sample, cache_mla_ckv, cache_mla_kpe, state_gla, cache_dsa_k, cache_dsa_v, cache_dsa_kidx,
           state_ffn_conv, page_table, c_prompt, c_sample, ada_w, ada_b, norm1_g, norm2_g, final_g, mla_w_in,
           mla_g_q, mla_g_kv, mla_w_uq, mla_w_uk, mla_w_uv, mla_w_o, gla_w_in, gla_w_a2, gla_b_a2, gla_g_o,
           gla_w_o, dsa_w_in, dsa_w_o, rel_bias, ffn_w_up, ffn_conv_w, ffn_conv_b, ffn_w_down):
    Bp, Tp, D = x_prompt.shape
    Bs, Ts, _ = x_sample.shape
    depth = ada_w.shape[0]
    n_mod = ada_w.shape[2] // D
    n_pages, page = page_table.shape[1], cache_mla_ckv.shape[2]
    past_len = n_pages * page
    cache_kpe_t = jnp.swapaxes(cache_mla_kpe, 2, 3)
    cache_ki_t = jnp.swapaxes(cache_dsa_kidx, 2, 3)

    q_lora, mla_h, qk_dim = mla_w_uq.shape[1:]
    kv_lora, _, nope = mla_w_uk.shape[1:]
    mla_v = mla_w_uv.shape[3]
    rope = qk_dim - nope
    gla_h = state_gla.shape[2]
    gla_dk, gla_dv = state_gla.shape[3:]
    gla_rank = gla_w_a2.shape[1]
    dsa_kv, dsa_hd = cache_dsa_k.shape[3:]
    dsa_di = cache_dsa_kidx.shape[3]
    dsa_h = dsa_w_o.shape[1] // dsa_hd
    dsa_hi = (dsa_w_in.shape[2] - (dsa_h + 2 * dsa_kv) * dsa_hd - dsa_di) // (dsa_di + 1)
    assert kv_lora % LANE == 0 and q_lora % LANE == 0 and rope <= LANE

    mod_all = _ada_mod(jnp.concatenate([c_prompt, c_sample], axis=0), ada_w, ada_b)

    def mods(l, sample):
        m = mod_all[l, Bp:] if sample else mod_all[l, :Bp]
        parts = [m[:, i * D:(i + 1) * D] for i in range(n_mod)]
        if sample:
            return [jnp.repeat(p, Ts, axis=0).reshape(1, Bs * Ts, D) for p in parts]
        return [p.reshape(Bp, 1, D) for p in parts]

    mla_w = []
    for j in range(mla_w_in.shape[0]):
        w_in = mla_w_in[j]
        kcol = q_lora + kv_lora
        w_in_ext = jnp.concatenate([w_in, _swap_halves(w_in[:, kcol:kcol + rope], rope)], axis=1)
        w_in_ext = _pad_cols(w_in_ext, -(-w_in_ext.shape[1] // LANE) * LANE).astype(CDT)
        uq = mla_w_uq[j]
        wqn = uq[:, :, :nope].reshape(q_lora, mla_h * nope).astype(CDT)
        wqp = uq[:, :, nope:].reshape(q_lora, mla_h * rope).astype(CDT)
        wqs = _swap_halves(uq[:, :, nope:], rope).reshape(q_lora, mla_h * rope).astype(CDT)
        wukT = jnp.transpose(mla_w_uk[j], (1, 2, 0)).astype(CDT)
        wuv = jnp.transpose(mla_w_uv[j], (1, 0, 2)).astype(CDT)
        mla_w.append((w_in_ext, wqn, wqp, wqs, wukT, wuv, mla_w_o[j].astype(CDT)))
    gla_np = -(-gla_w_in.shape[2] // LANE) * LANE
    gla_w = []
    for j in range(gla_w_in.shape[0]):
        w_a2p = jnp.pad(gla_w_a2[j], ((0, LANE - gla_rank), (0, 0))).astype(CDT)
        gla_w.append((_pad_cols(gla_w_in[j], gla_np).astype(CDT), w_a2p, gla_w_o[j].astype(CDT)))
    dsa_np = -(-dsa_w_in.shape[2] // LANE) * LANE
    dsa_w = [(_pad_cols(dsa_w_in[j], dsa_np).astype(CDT), dsa_w_o[j].astype(CDT)) for j in range(dsa_w_in.shape[0])]
    ffn_up = ffn_w_up.astype(CDT)
    ffn_down = ffn_w_down.astype(CDT)

    mla_dims = (mla_h, q_lora, kv_lora, rope, nope)
    mla_scale = qk_dim ** -0.5

    def trunk(x3, sample):
        B, T, _ = x3.shape
        M = B * T
        x = x3.reshape(M, D)
        rpg = M if sample else T
        pos = (past_len if sample else 0) + jnp.arange(T, dtype=jnp.int32)
        tabs = _rope_tables(pos, rope, mla_h, B if sample else 1)
        outs = dict(mla_ckv=[], mla_kpe=[], gla=[], dsa_k=[], dsa_v=[], dsa_ki=[], conv=[])
        for l in range(depth):
            sh1, sc1, gt1, sh2, sc2, gt2 = mods(l, sample)
            j = l // 3
            if l % 3 == 0:
                w_in_ext, wqn, wqp, wqs, wukT, wuv, wo = mla_w[j]
                p = _nm_linear(x, norm1_g[l], sc1, sh1, w_in_ext, rpg)
                ckv, kpe, kcat, qcat = _mla_prep(p, mla_g_q[j], mla_g_kv[j], tabs, wqn, wqp, wqs, wukT, mla_dims)
                if sample:
                    o_lat = _mla_decode(qcat, kcat, cache_mla_ckv, cache_kpe_t, j, page_table, B, T, mla_h,
                                        kv_lora, rope, mla_scale)
                else:
                    o_lat = _mla_flash(qcat, kcat, B, T, mla_h, kv_lora, mla_scale)
                x = _mla_out(o_lat, wuv, wo, x, gt1, rpg, mla_h, kv_lora)
                outs["mla_ckv"].append(ckv.reshape(B, T, kv_lora))
                outs["mla_kpe"].append(kpe.reshape(B, T, rope))
            elif l % 3 == 1:
                w_in_p, w_a2p, wo = gla_w[j]
                hk, hv = gla_h * gla_dk, gla_h * gla_dv
                p = _nm_linear(x, norm1_g[l], sc1, sh1, w_in_p, rpg, tn_target=NM_LINEAR_TN)
                log_a = _gla_gate(p, (2 * hk + 2 * hv) // LANE, w_a2p, gla_b_a2[j])
                s0 = state_gla[j] if sample else jnp.zeros((B, gla_h, gla_dk, gla_dv), F32)
                o, s_fin = _gla_recurrence(p, log_a, s0, B, T, gla_h, gla_dk, gla_dv)
                x = _gla_out(o, p, (2 * hk + hv) // hv, gla_g_o[j], wo, x, gt1, rpg, gla_h, gla_dv)
                outs["gla"].append(s_fin)
            else:
                w_in_p, wo = dsa_w[j]
                hq, hkv, hi = dsa_h * dsa_hd, dsa_kv * dsa_hd, dsa_hi * dsa_di
                p = _nm_linear(x, norm1_g[l], sc1, sh1, w_in_p, rpg, tn_target=NM_LINEAR_TN, with_cdt_copy=not sample)
                if not sample:
                    p, p_c = p
                k_new = p[:, hq:hq + hkv]
                v_new = p[:, hq + hkv:hq + 2 * hkv]
                kk = p[:, hq + 2 * hkv + hi:]
                L_keys = (past_len if sample else 0) + T
                topk = min(DSA_TOPK, L_keys // 4)
                if sample:
                    group = dsa_h // dsa_kv
                    dup = lambda a: jnp.concatenate([a.reshape(B, T, dsa_hi, -1)] * group, axis=1)
                    qi_hm = dup(p[:, hq + 2 * hkv:hq + 2 * hkv + hi]).transpose(0, 2, 1, 3)
                    qi_hm = qi_hm.reshape(B, dsa_hi * group * T, dsa_di)
                    wi_hm = dup(kk[:, dsa_di:dsa_di + dsa_hi]).transpose(0, 2, 1, 3).reshape(B, dsa_hi * group * T, 1)
                    padn = lambda a: jnp.pad(a.reshape(B, T, -1), ((0, 0), (0, LANE - T), (0, 0)))
                    mask = _dsa_select(qi_hm, wi_hm, padn(kk), cache_ki_t, j, page_table, B, dsa_hi, dsa_di, topk, T)
                    q4 = p[:, :hq].reshape(B, T, dsa_kv, group, dsa_hd).transpose(0, 2, 3, 1, 4)
                    q4 = q4.reshape(B, dsa_kv, group * T, dsa_hd)
                    ck = cache_dsa_k.reshape(*cache_dsa_k.shape[:2], page * dsa_kv, dsa_hd)
                    cv = cache_dsa_v.reshape(*cache_dsa_v.shape[:2], page * dsa_kv, dsa_hd)
                    o4 = _dsa_decode(q4, padn(k_new), padn(v_new), mask, rel_bias, ck, cv, j, page_table, B,
                                     dsa_kv, group, dsa_hd, T)
                    o = o4.reshape(B, dsa_kv, group, T, dsa_hd).transpose(0, 3, 1, 2, 4).reshape(M, hq)
                else:
                    o = _dsa_prompt(p, p_c, rel_bias, B, T, (dsa_h, dsa_kv, dsa_hd, dsa_hi, dsa_di), topk)
                x = _proj_res(o, wo, x, gt1, rpg)
                outs["dsa_k"].append(k_new.reshape(B, T, dsa_kv, dsa_hd))
                outs["dsa_v"].append(v_new.reshape(B, T, dsa_kv, dsa_hd))
                outs["dsa_ki"].append(kk[:, :dsa_di].reshape(B, T, dsa_di))
            if sample:
                x, tail = _ffn(x, norm2_g[l], sc2, sh2, gt2, ffn_up[l], ffn_down[l], ffn_conv_w[l], ffn_conv_b[l],
                               rpg, T, prev=state_ffn_conv[l])
                outs["conv"].append(tail.reshape(B, T, -1)[:, T - 2:])
            else:
                x, tail = _ffn(x, norm2_g[l], sc2, sh2, gt2, ffn_up[l], ffn_down[l], ffn_conv_w[l], ffn_conv_b[l],
                               rpg, T)
                nt = tail.shape[0] // B
                outs["conv"].append(tail.reshape(B, nt, 8, -1)[:, nt - 1, 6:8])
        y = _final_norm(x, final_g).reshape(B, T, D)
        return (y, jnp.stack(outs["mla_ckv"]), jnp.stack(outs["mla_kpe"]), jnp.stack(outs["gla"]),
                jnp.stack(outs["dsa_k"]), jnp.stack(outs["dsa_v"]), jnp.stack(outs["dsa_ki"]),
                jnp.stack(outs["conv"]))

    rp = trunk(x_prompt, False)
    rs = trunk(x_sample, True)
    return (rp[0], rs[0]) + tuple(rp[1:]) + tuple(rs[1:])
```

```python
import functools
import math

import numpy as np
import jax
import jax.numpy as jnp
from jax import lax
from jax.experimental import pallas as pl
from jax.experimental.pallas import tpu as pltpu

F32 = jnp.float32
CDT = jnp.bfloat16
EPS = 1e-6
ROPE_BASE = 10000.0
GLA_TAU = 16.0
N_BUCKETS = 32
MAX_DISTANCE = 128
DSA_TOPK = 256
LANE = 128
NEG_INF = float("-inf")
INT_MIN = -2 ** 31
VMEM_LIMIT = 56 * 1024 * 1024
PAGES_PER_STEP_MLA = 64
SEQS_PER_STEP_MLA = 2
PAGES_PER_STEP_SELECT = 64
PAGES_PER_STEP_DECODE = 32
FFN_TILE_F = 1408
DSA_EXTENT_CLASSES = 4
DSA_TQ = 128
NM_LINEAR_TN = 4096
FLASH_TQ = 512
FLASH_TK = 512
FLASH_ROW_CHUNK = 64


def _bucket_thresholds():
    d = np.arange(0, 4 * MAX_DISTANCE)
    exact = N_BUCKETS // 2
    lr = np.log(np.maximum(d, 1).astype(np.float32) / np.float32(exact)) / np.float32(math.log(MAX_DISTANCE / exact))
    large = np.minimum(exact + (lr * np.float32(N_BUCKETS - exact)).astype(np.int32), N_BUCKETS - 1)
    b = np.where(d < exact, d, large)
    return [int(np.argmax(b >= j)) for j in range(N_BUCKETS)]


BUCKET_LO = _bucket_thresholds()


def _cparams(sem, vmem=VMEM_LIMIT):
    return pltpu.CompilerParams(dimension_semantics=sem, vmem_limit_bytes=vmem)


def _tile(n, target, mult=LANE):
    if n <= target:
        return n
    t = (target // mult) * mult
    while t > mult and n % t:
        t -= mult
    assert n % t == 0, (n, target)
    return t


def _dot(a, b):
    return jnp.dot(a.astype(CDT), b.astype(CDT), preferred_element_type=F32)


def _dot_nt(a, b):
    return lax.dot_general(a.astype(CDT), b.astype(CDT), (((1,), (1,)), ((), ())), preferred_element_type=F32)


def _dot_tn(a, b):
    return lax.dot_general(a.astype(CDT), b.astype(CDT), (((0,), (0,)), ((), ())), preferred_element_type=F32)


def _rms(x, g):
    return x * lax.rsqrt(jnp.mean(x * x, axis=-1, keepdims=True) + EPS) * g


def _silu(x):
    return x * jax.nn.sigmoid(x)


def _ada_kernel(c_ref, w_ref, b_ref, o_ref):
    ca = _silu(c_ref[...])
    o_ref[0] = _dot(ca, w_ref[0]) + b_ref[0]


def _ada_mod(c, ada_w, ada_b):
    L, D, N = ada_w.shape
    R = c.shape[0]
    tn = _tile(N, 1536)
    return pl.pallas_call(
        _ada_kernel,
        grid=(L, N // tn),
        in_specs=[pl.BlockSpec((R, D), lambda l, j: (0, 0)),
                  pl.BlockSpec((1, D, tn), lambda l, j: (l, 0, j)),
                  pl.BlockSpec((1, 1, tn), lambda l, j: (l, 0, j))],
        out_specs=pl.BlockSpec((1, R, tn), lambda l, j: (l, 0, j)),
        out_shape=jax.ShapeDtypeStruct((L, R, N), F32),
        compiler_params=_cparams(("parallel", "parallel")),
        name="ada_mod",
    )(c, ada_w, ada_b.reshape(L, 1, N))


def _nm_linear_kernel(x_ref, g_ref, sc_ref, sh_ref, w_ref, o_ref, *rest):
    h_ref = rest[-1]

    @pl.when(pl.program_id(1) == 0)
    def _():
        h = _rms(x_ref[...], g_ref[...]) * (1.0 + sc_ref[0]) + sh_ref[0]
        h_ref[...] = h.astype(CDT)

    o = jnp.dot(h_ref[...], w_ref[...], preferred_element_type=F32)
    o_ref[...] = o
    if len(rest) == 2:
        rest[0][...] = o.astype(CDT)


def _mod_spec(mod, tm, rows_per_group):
    G, R, D = mod.shape
    tpg = max(rows_per_group // tm, 1)
    return pl.BlockSpec((1, R, D), lambda i, j: (i // tpg, 0, 0))


def _nm_linear(x, g, sc, sh, w, rows_per_group, tm_target=512, tn_target=1024, with_cdt_copy=False):
    M, D = x.shape
    N = w.shape[1]
    tm = _tile(M, tm_target, 8)
    tn = _tile(N, tn_target)
    out_spec = pl.BlockSpec((tm, tn), lambda i, j: (i, j))
    out_shape = jax.ShapeDtypeStruct((M, N), F32)
    if with_cdt_copy:
        out_spec = [out_spec, out_spec]
        out_shape = [out_shape, jax.ShapeDtypeStruct((M, N), CDT)]
    return pl.pallas_call(
        _nm_linear_kernel,
        grid=(M // tm, N // tn),
        in_specs=[pl.BlockSpec((tm, D), lambda i, j: (i, 0)),
                  pl.BlockSpec((1, D), lambda i, j: (0, 0)),
                  _mod_spec(sc, tm, rows_per_group),
                  _mod_spec(sh, tm, rows_per_group),
                  pl.BlockSpec((D, tn), lambda i, j: (0, j))],
        out_specs=out_spec,
        out_shape=out_shape,
        scratch_shapes=[pltpu.VMEM((tm, D), CDT)],
        compiler_params=_cparams(("parallel", "arbitrary")),
        name="nm_linear",
    )(x, g.reshape(1, D), sc, sh, w)


def _mla_prep_kernel(p_ref, gq_ref, gkv_ref, cq_ref, sq_ref, ck_ref, sk_ref, wqn_ref, wqp_ref, wqs_ref, wuk_ref,
                     ckv_ref, kpe_ref, kcat_ref, qcat_ref, *, n_heads, q_lora, kv_lora, rope, nope):
    p = p_ref[...]
    cq = p[:, :q_lora]
    ckv = p[:, q_lora:q_lora + kv_lora]
    kpe = p[:, q_lora + kv_lora:q_lora + kv_lora + rope]
    kpe_sw = p[:, q_lora + kv_lora + rope:q_lora + kv_lora + 2 * rope]
    cqn = _rms(cq, gq_ref[...]).astype(CDT)
    q_nope = jnp.dot(cqn, wqn_ref[...], preferred_element_type=F32)
    q_pe = (jnp.dot(cqn, wqp_ref[...], preferred_element_type=F32) * cq_ref[...]
            + jnp.dot(cqn, wqs_ref[...], preferred_element_type=F32) * sq_ref[...])
    ckv_n = _rms(ckv, gkv_ref[...])
    kpe_r = kpe * ck_ref[...] + kpe_sw * sk_ref[...]
    ckv_ref[...] = ckv_n
    kpe_ref[...] = kpe_r
    tm = p.shape[0]
    hd = kv_lora + LANE
    pad = jnp.zeros((tm, LANE - rope), CDT)
    kcat_ref[...] = jnp.concatenate([ckv_n.astype(CDT), kpe_r.astype(CDT), pad], axis=1)
    for h in range(n_heads):
        q_lat = _dot(q_nope[:, h * nope:(h + 1) * nope], wuk_ref[h])
        qcat_ref[:, h * hd:(h + 1) * hd] = jnp.concatenate(
            [q_lat.astype(CDT), q_pe[:, h * rope:(h + 1) * rope].astype(CDT), pad], axis=1)


def _mla_prep(p, g_q, g_kv, tabs, wqn, wqp, wqs, wukT, dims):
    n_heads, q_lora, kv_lora, rope, nope = dims
    M, NP = p.shape
    cosq, sinq, cosk, sink = tabs
    tm = _tile(M, 256, 8)
    ntab = cosq.shape[0] // tm
    hd = kv_lora + LANE
    row = lambda i: (i, 0)
    tab = lambda i: (i % ntab, 0)
    full2 = lambda i: (0, 0)
    full3 = lambda i: (0, 0, 0)
    kern = functools.partial(_mla_prep_kernel, n_heads=n_heads, q_lora=q_lora, kv_lora=kv_lora, rope=rope, nope=nope)
    return pl.pallas_call(
        kern,
        grid=(M // tm,),
        in_specs=[pl.BlockSpec((tm, NP), row),
                  pl.BlockSpec((1, q_lora), full2),
                  pl.BlockSpec((1, kv_lora), full2),
                  pl.BlockSpec((tm, n_heads * rope), tab),
                  pl.BlockSpec((tm, n_heads * rope), tab),
                  pl.BlockSpec((tm, rope), tab),
                  pl.BlockSpec((tm, rope), tab),
                  pl.BlockSpec(wqn.shape, full2),
                  pl.BlockSpec(wqp.shape, full2),
                  pl.BlockSpec(wqs.shape, full2),
                  pl.BlockSpec(wukT.shape, full3)],
        out_specs=[pl.BlockSpec((tm, kv_lora), row),
                   pl.BlockSpec((tm, rope), row),
                   pl.BlockSpec((tm, hd), row),
                   pl.BlockSpec((tm, n_heads * hd), row)],
        out_shape=[jax.ShapeDtypeStruct((M, kv_lora), F32),
                   jax.ShapeDtypeStruct((M, rope), F32),
                   jax.ShapeDtypeStruct((M, hd), CDT),
                   jax.ShapeDtypeStruct((M, n_heads * hd), CDT)],
        compiler_params=_cparams(("parallel",)),
        name="mla_prep",
    )(p, g_q.reshape(1, -1), g_kv.reshape(1, -1), cosq, sinq, cosk, sink, wqn, wqp, wqs, wukT)


def _mla_flash_kernel(q_ref, k_ref, o_ref, m_ref, l_ref, acc_ref, s2_ref, p2_ref, pm_ref, ps_ref, *,
                      n_heads, tq, tk, hd, dv, scale, rc):
    qi = pl.program_id(1)
    ki = pl.program_id(2)

    @pl.when(ki == 0)
    def _():
        m_ref[...] = jnp.full(m_ref.shape, NEG_INF, F32)
        l_ref[...] = jnp.zeros(l_ref.shape, F32)
        acc_ref[...] = jnp.zeros(acc_ref.shape, F32)

    c = scale * math.log2(math.e)

    def block(masked):
        k = k_ref[...]
        v = k[:, :dv]
        tiles = lambda a: [a[:, t * LANE:(t + 1) * LANE] for t in range(a.shape[1] // LANE)]

        def chunk_logits(s_ref, r):
            s = s_ref[r * rc:(r + 1) * rc, :]
            if masked:
                row = qi * tq + r * rc + lax.broadcasted_iota(jnp.int32, (rc, tk), 0)
                col = ki * tk + lax.broadcasted_iota(jnp.int32, (rc, tk), 1)
                s = jnp.where(col <= row, s, NEG_INF)
            return s

        for h in range(n_heads):
            b = h % 2
            s_ref, p_ref = s2_ref.at[b], p2_ref.at[b]
            s_ref[...] = _dot_nt(q_ref[:, h * hd:(h + 1) * hd], k)
            for r in range(tq // rc):
                pm_ref[b, r * rc:(r + 1) * rc, :] = functools.reduce(jnp.maximum, tiles(chunk_logits(s_ref, r)))
            m_prev = m_ref[h]
            m_new = jnp.maximum(m_prev, jnp.broadcast_to(jnp.max(pm_ref[b], axis=-1, keepdims=True), (tq, LANE)))
            alpha = jnp.exp2(c * (m_prev - m_new))
            m_ref[h] = m_new
            pm_ref[b] = m_new
            for r in range(tq // rc):
                mb = pm_ref[b, r * rc:(r + 1) * rc, :]
                p = [jnp.exp2(c * (st - mb)) for st in tiles(chunk_logits(s_ref, r))]
                ps_ref[b, r * rc:(r + 1) * rc, :] = functools.reduce(jnp.add, p)
                p_ref[r * rc:(r + 1) * rc, :] = jnp.concatenate(p, axis=1).astype(CDT)
            l_blk = jnp.broadcast_to(jnp.sum(ps_ref[b], axis=-1, keepdims=True), (tq, LANE))
            l_ref[h] = alpha * l_ref[h] + l_blk
            pv = jnp.dot(p_ref[...], v, preferred_element_type=F32)
            acc_ref[h] = jnp.concatenate([alpha] * (dv // LANE), axis=1) * acc_ref[h] + pv

    below_diag = ki * tk + tk - 1 <= qi * tq
    pl.when(below_diag)(functools.partial(block, False))
    pl.when(jnp.logical_not(below_diag) & (ki * tk <= qi * tq + tq - 1))(functools.partial(block, True))

    @pl.when(ki == pl.num_programs(2) - 1)
    def _():
        for h in range(n_heads):
            l = jnp.concatenate([l_ref[h]] * (dv // LANE), axis=1)
            o_ref[:, h * dv:(h + 1) * dv] = (acc_ref[h] / l).astype(o_ref.dtype)


def _mla_flash(qcat, kcat, B, T, n_heads, dv, scale):
    M, hd = kcat.shape
    tq = _tile(T, FLASH_TQ, 8)
    tk = _tile(T, FLASH_TK, 8)
    nq, nk = T // tq, T // tk
    assert tk % LANE == 0 and dv % LANE == 0
    rc = math.gcd(tq, FLASH_ROW_CHUNK)
    kern = functools.partial(_mla_flash_kernel, n_heads=n_heads, tq=tq, tk=tk, hd=hd, dv=dv, scale=scale, rc=rc)

    def kmap(b, qi, ki):
        return (b * nk + jnp.minimum(ki, (qi * tq + tq - 1) // tk), 0)

    return pl.pallas_call(
        kern,
        grid=(B, nq, nk),
        in_specs=[pl.BlockSpec((tq, n_heads * hd), lambda b, qi, ki: (b * nq + qi, 0)),
                  pl.BlockSpec((tk, hd), kmap)],
        out_specs=pl.BlockSpec((tq, n_heads * dv), lambda b, qi, ki: (b * nq + qi, 0)),
        out_shape=jax.ShapeDtypeStruct((M, n_heads * dv), CDT),
        scratch_shapes=[pltpu.VMEM((n_heads, tq, LANE), F32),
                        pltpu.VMEM((n_heads, tq, LANE), F32),
                        pltpu.VMEM((n_heads, tq, dv), F32),
                        pltpu.VMEM((2, tq, tk), F32),
                        pltpu.VMEM((2, tq, tk), CDT),
                        pltpu.VMEM((2, tq, LANE), F32),
                        pltpu.VMEM((2, tq, LANE), F32)],
        compiler_params=_cparams(("parallel", "parallel", "arbitrary")),
        name="mla_flash",
    )(qcat, kcat)


def _mla_decode_kernel(pt_ref, q_ref, kn_ref, *refs, n_pages_step, n_seq, n_heads, t_new, dv, rope, scale):
    G = n_pages_step
    ckv_refs = refs[:n_seq * G]
    kpe_refs = refs[n_seq * G:2 * n_seq * G]
    o_ref, m_ref, l_ref, acc_ref = refs[2 * n_seq * G:]
    g = pl.program_id(1)

    @pl.when(g == 0)
    def _():
        m_ref[...] = jnp.full(m_ref.shape, NEG_INF, F32)
        l_ref[...] = jnp.zeros(l_ref.shape, F32)
        acc_ref[...] = jnp.zeros(acc_ref.shape, F32)

    def update(sb, s, v):
        m_prev = m_ref[sb]
        m_new = jnp.maximum(m_prev, jnp.max(s, axis=-1, keepdims=True))
        alpha = jnp.exp(m_prev - m_new)
        p = jnp.exp(s - m_new)
        l_ref[sb] = alpha * l_ref[sb] + jnp.sum(p, axis=-1, keepdims=True)
        acc_ref[sb] = alpha * acc_ref[sb] + _dot(p, v)
        m_ref[sb] = m_new

    for sb in range(n_seq):
        q = q_ref[sb]
        ckv = jnp.concatenate([r[0, 0] for r in ckv_refs[sb * G:(sb + 1) * G]], axis=0).astype(CDT)
        kpe_t = jnp.concatenate([r[0, 0] for r in kpe_refs[sb * G:(sb + 1) * G]], axis=1).astype(CDT)
        s = (_dot_nt(q[:, :dv], ckv) + _dot(q[:, dv:dv + rope], kpe_t)) * scale
        update(sb, s, ckv)

    @pl.when(g == pl.num_programs(1) - 1)
    def _():
        for sb in range(n_seq):
            q = q_ref[sb]
            kn = kn_ref[sb]
            s2 = _dot_nt(q, kn) * scale
            r, c = s2.shape
            t_row = lax.broadcasted_iota(jnp.int32, (r, c), 0) // n_heads
            col = lax.broadcasted_iota(jnp.int32, (r, c), 1)
            s2 = jnp.where((col <= t_row) & (col < t_new), s2, NEG_INF)
            update(sb, s2, kn[:, :dv])
            o_ref[sb] = (acc_ref[sb] / l_ref[sb]).astype(o_ref.dtype)


def _mla_decode(qcat, kcat, cache_ckv, cache_kpe_t, layer, page_table, B, Ts, n_heads, dv, rope, scale):
    hd = kcat.shape[1]
    n_pages = page_table.shape[1]
    page = cache_ckv.shape[2]
    SB = math.gcd(B, SEQS_PER_STEP_MLA)
    G = math.gcd(n_pages, PAGES_PER_STEP_MLA // SB)
    NG = n_pages // G
    R = Ts * n_heads
    q3 = qcat.reshape(B, R, hd)
    npad = 16
    kn = jnp.pad(kcat.reshape(B, Ts, hd), ((0, 0), (0, npad - Ts), (0, 0)))
    kern = functools.partial(_mla_decode_kernel, n_pages_step=G, n_seq=SB, n_heads=n_heads, t_new=Ts, dv=dv,
                             rope=rope, scale=scale)

    def page_map(sb, i):
        return lambda b, g, pt: (layer, pt[b * SB + sb, g * G + i], 0, 0)

    in_specs = [pl.BlockSpec((SB, R, hd), lambda b, g, pt: (b, 0, 0)),
                pl.BlockSpec((SB, npad, hd), lambda b, g, pt: (b, 0, 0))]
    in_specs += [pl.BlockSpec((1, 1, page, dv), page_map(sb, i)) for sb in range(SB) for i in range(G)]
    in_specs += [pl.BlockSpec((1, 1, rope, page), page_map(sb, i)) for sb in range(SB) for i in range(G)]
    out = pl.pallas_call(
        kern,
        grid_spec=pltpu.PrefetchScalarGridSpec(
            num_scalar_prefetch=1,
            grid=(B // SB, NG),
            in_specs=in_specs,
            out_specs=pl.BlockSpec((SB, R, dv), lambda b, g, pt: (b, 0, 0)),
            scratch_shapes=[pltpu.VMEM((SB, R, 1), F32), pltpu.VMEM((SB, R, 1), F32),
                            pltpu.VMEM((SB, R, dv), F32)]),
        out_shape=jax.ShapeDtypeStruct((B, R, dv), CDT),
        compiler_params=_cparams(("parallel", "arbitrary")),
        name="mla_decode",
    )(page_table, q3, kn, *([cache_ckv] * (SB * G)), *([cache_kpe_t] * (SB * G)))
    return out.reshape(B * Ts, n_heads * dv)


def _mla_out_kernel(o_ref, wuv_ref, wo_ref, x_ref, gt_ref, y_ref, *, n_heads, dv):
    parts = [_dot(o_ref[:, h * dv:(h + 1) * dv], wuv_ref[h]).astype(CDT) for h in range(n_heads)]
    o = jnp.concatenate(parts, axis=1)
    y = jnp.dot(o, wo_ref[...], preferred_element_type=F32)
    y_ref[...] = x_ref[...] + gt_ref[0] * y


def _mla_out(o_lat, wuv, wo, x, gt, rows_per_group, n_heads, dv):
    M, D = x.shape
    tm = _tile(M, 512, 8)
    tpg = max(rows_per_group // tm, 1)
    kern = functools.partial(_mla_out_kernel, n_heads=n_heads, dv=dv)
    return pl.pallas_call(
        kern,
        grid=(M // tm,),
        in_specs=[pl.BlockSpec((tm, n_heads * dv), lambda i: (i, 0)),
                  pl.BlockSpec(wuv.shape, lambda i: (0, 0, 0)),
                  pl.BlockSpec(wo.shape, lambda i: (0, 0)),
                  pl.BlockSpec((tm, D), lambda i: (i, 0)),
                  pl.BlockSpec((1, gt.shape[1], D), lambda i: (i // tpg, 0, 0))],
        out_specs=pl.BlockSpec((tm, D), lambda i: (i, 0)),
        out_shape=jax.ShapeDtypeStruct((M, D), F32),
        compiler_params=_cparams(("parallel",)),
        name="mla_out",
    )(o_lat, wuv, wo, x, gt)


def _proj_res_kernel(o_ref, wo_ref, x_ref, gt_ref, y_ref):
    y = jnp.dot(o_ref[...], wo_ref[...], preferred_element_type=F32)
    y_ref[...] = x_ref[...] + gt_ref[0] * y


def _proj_res(o, wo, x, gt, rows_per_group):
    M, D = x.shape
    K = o.shape[1]
    tm = _tile(M, 512, 8)
    tpg = max(rows_per_group // tm, 1)
    return pl.pallas_call(
        _proj_res_kernel,
        grid=(M // tm,),
        in_specs=[pl.BlockSpec((tm, K), lambda i: (i, 0)),
                  pl.BlockSpec(wo.shape, lambda i: (0, 0)),
                  pl.BlockSpec((tm, D), lambda i: (i, 0)),
                  pl.BlockSpec((1, gt.shape[1], D), lambda i: (i // tpg, 0, 0))],
        out_specs=pl.BlockSpec((tm, D), lambda i: (i, 0)),
        out_shape=jax.ShapeDtypeStruct((M, D), F32),
        compiler_params=_cparams(("parallel",)),
        name="proj_res",
    )(o, wo, x, gt)


def _gla_gate_kernel(a_ref, w_ref, b_ref, o_ref):
    z = _dot(a_ref[...], w_ref[...]) + b_ref[...]
    o_ref[...] = (jnp.minimum(z, 0.0) - jnp.log(1.0 + jnp.exp(-jnp.abs(z)))) / GLA_TAU


def _gla_gate(proj, col_block, w_a2p, b_a2):
    M = proj.shape[0]
    N = w_a2p.shape[1]
    tm = _tile(M, 1024, 8)
    return pl.pallas_call(
        _gla_gate_kernel,
        grid=(M // tm,),
        in_specs=[pl.BlockSpec((tm, LANE), lambda i: (i, col_block)),
                  pl.BlockSpec(w_a2p.shape, lambda i: (0, 0)),
                  pl.BlockSpec((1, N), lambda i: (0, 0))],
        out_specs=pl.BlockSpec((tm, N), lambda i: (i, 0)),
        out_shape=jax.ShapeDtypeStruct((M, N), F32),
        compiler_params=_cparams(("parallel",)),
        name="gla_gate",
    )(proj, w_a2p, b_a2.reshape(1, N))


def _cumsum_rows(x):
    C = x.shape[0]
    row = lax.broadcasted_iota(jnp.int32, x.shape, 0)
    if C <= 8:
        out = jnp.zeros_like(x)
        for s in range(C):
            out = out + jnp.where(row >= s, x[s:s + 1], 0.0)
        return out
    sh = 1
    while sh < C:
        x = x + jnp.where(row >= sh, pltpu.roll(x, sh, 0), 0.0)
        sh *= 2
    return x


def _gla_kernel(q_ref, k_ref, v_ref, la_ref, s0_ref, o_ref, sf_ref, st_ref, *, n_heads, dk, dv, sub, qscale):
    c = pl.program_id(1)

    @pl.when(c == 0)
    def _():
        for h in range(n_heads):
            st_ref[h] = s0_ref[0, h].T

    C = q_ref.shape[1]
    nsub = C // sub
    for h in range(n_heads):
        q = q_ref[0, :, h * dk:(h + 1) * dk] * qscale
        k = k_ref[0, :, h * dk:(h + 1) * dk]
        v = v_ref[0, :, h * dv:(h + 1) * dv]
        b = _cumsum_rows(la_ref[0, :, h * dk:(h + 1) * dk])
        st = st_ref[h]
        o_inter = _dot_nt(q * jnp.exp(b), st)
        b_last = b[C - 1:C]
        k_dec = k * jnp.exp(b_last - b)
        st_ref[h] = jnp.exp(b_last) * st + _dot_tn(v, k_dec)
        outs = []
        for i in range(nsub):
            r0 = i * sub
            b_i = b[r0:r0 + sub]
            q_i = q[r0:r0 + sub]
            k_i = k[r0:r0 + sub]
            v_i = v[r0:r0 + sub]
            o_i = o_inter[r0:r0 + sub]
            if i > 0:
                ref_row = b[r0:r0 + 1]
                att = _dot_nt(q_i * jnp.exp(b_i - ref_row), k[:r0] * jnp.exp(ref_row - b[:r0]))
                o_i = o_i + _dot(att, v[:r0])
            t_loc = lax.broadcasted_iota(jnp.int32, (sub, 1), 0)
            for s in range(sub):
                w = jnp.exp(jnp.minimum(b_i - b_i[s:s + 1], 0.0))
                col = jnp.sum(q_i * w * k_i[s:s + 1], axis=-1, keepdims=True)
                o_i = o_i + jnp.where(t_loc >= s, col, 0.0) * v_i[s:s + 1]
            outs.append(o_i)
        o_ref[0, :, h * dv:(h + 1) * dv] = outs[0] if nsub == 1 else jnp.concatenate(outs, axis=0)

    @pl.when(c == pl.num_programs(1) - 1)
    def _():
        for h in range(n_heads):
            sf_ref[0, h] = st_ref[h].T


def _gla_recurrence(proj, log_a, s0, B, T, n_heads, dk, dv):
    C = 64 if T % 64 == 0 else T
    sub = min(16, C)
    nc = T // C
    Np = proj.shape[1]
    p3 = proj.reshape(B * nc, C, Np)
    la3 = log_a.reshape(B * nc, C, n_heads * dk)
    hk, hv = n_heads * dk, n_heads * dv
    assert hv % hk == 0
    kern = functools.partial(_gla_kernel, n_heads=n_heads, dk=dk, dv=dv, sub=sub, qscale=dk ** -0.5)
    o, sf = pl.pallas_call(
        kern,
        grid=(B, nc),
        in_specs=[pl.BlockSpec((1, C, hk), lambda b, c: (b * nc + c, 0, 0)),
                  pl.BlockSpec((1, C, hk), lambda b, c: (b * nc + c, 0, 1)),
                  pl.BlockSpec((1, C, hv), lambda b, c: (b * nc + c, 0, 2 * hk // hv)),
                  pl.BlockSpec((1, C, hk), lambda b, c: (b * nc + c, 0, 0)),
                  pl.BlockSpec((1, n_heads, dk, dv), lambda b, c: (b, 0, 0, 0))],
        out_specs=[pl.BlockSpec((1, C, hv), lambda b, c: (b * nc + c, 0, 0)),
                   pl.BlockSpec((1, n_heads, dk, dv), lambda b, c: (b, 0, 0, 0))],
        out_shape=[jax.ShapeDtypeStruct((B * nc, C, hv), F32),
                   jax.ShapeDtypeStruct((B, n_heads, dk, dv), F32)],
        scratch_shapes=[pltpu.VMEM((n_heads, dv, dk), F32)],
        compiler_params=_cparams(("parallel", "arbitrary")),
        name="gla_recurrence",
    )(p3, p3, p3, la3, s0)
    return o.reshape(B * T, hv), sf


def _gla_out_kernel(o_ref, r_ref, g_ref, wo_ref, x_ref, gt_ref, y_ref, *, n_heads, dv):
    parts = []
    for h in range(n_heads):
        sl = slice(h * dv, (h + 1) * dv)
        parts.append((_rms(o_ref[:, sl], g_ref[...]) * _silu(r_ref[:, sl])).astype(CDT))
    y = jnp.dot(jnp.concatenate(parts, axis=1), wo_ref[...], preferred_element_type=F32)
    y_ref[...] = x_ref[...] + gt_ref[0] * y


def _gla_out(o, proj, r_block, g_o, wo, x, gt, rows_per_group, n_heads, dv):
    M, D = x.shape
    hv = n_heads * dv
    tm = _tile(M, 512, 8)
    tpg = max(rows_per_group // tm, 1)
    kern = functools.partial(_gla_out_kernel, n_heads=n_heads, dv=dv)
    return pl.pallas_call(
        kern,
        grid=(M // tm,),
        in_specs=[pl.BlockSpec((tm, hv), lambda i: (i, 0)),
                  pl.BlockSpec((tm, hv), lambda i: (i, r_block)),
                  pl.BlockSpec((1, dv), lambda i: (0, 0)),
                  pl.BlockSpec(wo.shape, lambda i: (0, 0)),
                  pl.BlockSpec((tm, D), lambda i: (i, 0)),
                  pl.BlockSpec((1, gt.shape[1], D), lambda i: (i // tpg, 0, 0))],
        out_specs=pl.BlockSpec((tm, D), lambda i: (i, 0)),
        out_shape=jax.ShapeDtypeStruct((M, D), F32),
        compiler_params=_cparams(("parallel",)),
        name="gla_out",
    )(o, proj, g_o.reshape(1, dv), wo, x, gt)


def _order_key(score):
    score = jnp.where(score == 0.0, 0.0, score)
    bits = pltpu.bitcast(score, jnp.int32)
    return jnp.where(bits < 0, bits ^ jnp.int32(0x7FFFFFFF), bits)


def _sum_rows(x, chains=8):
    n = x.shape[0]
    if n % (8 * chains):
        return jnp.sum(x, axis=0, keepdims=True)
    part = jnp.sum(x.reshape(chains, n // chains, *x.shape[1:]), axis=1)
    return jnp.sum(part, axis=0, keepdims=True)


def _kth_largest_key(key, topk, axes, two_bit_steps):
    shape = tuple(1 if a in axes else s for a, s in enumerate(key.shape))

    def count(cand):
        hit = jnp.where(key >= cand, 1, 0)
        if axes == (0,):
            return _sum_rows(hit)
        if axes == (0, 2) and key.shape[2] % LANE == 0:
            parts = [hit[a, :, t * LANE:(t + 1) * LANE] for a in range(key.shape[0]) for t in range(key.shape[2] // LANE)]
            while len(parts) > 1:
                parts = [parts[i] + parts[i + 1] for i in range(0, len(parts) - 1, 2)] + parts[len(parts) & ~1:]
            return jnp.sum(parts[0], axis=-1, keepdims=True)[None]
        return jnp.sum(hit, axis=axes, keepdims=True)

    def body1(it, t):
        cand = t + lax.shift_left(jnp.int32(1), jnp.int32(31) - it)
        return jnp.where(count(cand) >= topk, cand, t)

    def body2(it, t):
        hi = lax.shift_left(jnp.int32(1), jnp.int32(31) - 2 * it)
        lo = lax.shift_left(jnp.int32(1), jnp.int32(30) - 2 * it)
        t01, t10, t11 = t + lo, t + hi, t + hi + lo
        c01, c10, c11 = count(t01), count(t10), count(t11)
        return jnp.where(c11 >= topk, t11, jnp.where(c10 >= topk, t10, jnp.where(c01 >= topk, t01, t)))

    init = jnp.full(shape, INT_MIN, jnp.int32)
    return lax.fori_loop(0, 16, body2, init) if two_bit_steps else lax.fori_loop(0, 32, body1, init)


def _bias_chain(dist, value_of_bucket):
    val = value_of_bucket(0)
    for j in range(1, N_BUCKETS):
        val = jnp.where(dist >= BUCKET_LO[j], value_of_bucket(j), val)
    return val


def _strict_upper(n):
    a = lax.broadcasted_iota(jnp.int32, (n, n), 0)
    b = lax.broadcasted_iota(jnp.int32, (n, n), 1)
    return jnp.where(a < b, 1.0, 0.0).astype(CDT)


def _strict_lower(n):
    a = lax.broadcasted_iota(jnp.int32, (n, n), 0)
    b = lax.broadcasted_iota(jnp.int32, (n, n), 1)
    return jnp.where(b < a, 1.0, 0.0).astype(CDT)


def _dsa_prompt_kernel(rb_ref, q_ref, qi_ref, wq_ref, k_ref, v_ref, kk_ref, o_ref, sel_ref, band_ref, selt_ref, *,
                       n_heads, n_kv, hd, n_idx, di, topk, tq, T, n_cls):
    b_id = pl.program_id(0)
    qt = pl.program_id(1)
    group = n_heads // n_kv

    @pl.when((b_id == 0) & (qt == 0))
    def _():
        i = lax.broadcasted_iota(jnp.int32, (tq, LANE), 0)
        j = lax.broadcasted_iota(jnp.int32, (tq, LANE), 1)
        for h in range(n_heads):
            far = rb_ref[N_BUCKETS - 1, h]
            for w in range(tq // LANE + 1):
                dist = i - j + (w - (tq // LANE - 1)) * LANE
                band_ref[w, h] = _bias_chain(dist, lambda bk: rb_ref[bk, h]) - far

    def body(nk):
        key_pos = lax.broadcasted_iota(jnp.int32, (nk, tq), 0)
        q_pos = qt * tq + lax.broadcasted_iota(jnp.int32, (nk, tq), 1)
        causal = key_pos <= q_pos

        ki = kk_ref[:nk, :di]
        wi_t = jnp.transpose(wq_ref[...])[di:di + n_idx] * (n_idx ** -0.5 * di ** -0.5)
        score = jnp.zeros((nk, tq), F32)
        for h in range(n_idx):
            lg = _dot_nt(ki, qi_ref[:, h * di:(h + 1) * di])
            score = score + wi_t[h:h + 1] * jnp.maximum(lg, 0.0)
        score = jnp.where(causal, score, NEG_INF)
        key = _order_key(score)
        thr = _kth_largest_key(key, topk, (0,), two_bit_steps=False)
        gt = key > thr
        eq = key == thr
        n_gt = _sum_rows(jnp.where(gt, 1, 0))
        n_eq = _sum_rows(jnp.where(eq, 1, 0))
        need = topk - n_gt
        tie = jnp.max(n_eq - need) > 0

        @pl.when(jnp.logical_not(tie))
        def _():
            selt_ref[:nk, :] = jnp.where((gt | eq) & causal, 0.0, NEG_INF)

        @pl.when(tie)
        def _():
            lower = _strict_lower(LANE)
            run = jnp.zeros((1, tq), F32)
            needf = need.astype(F32)
            for kb in range(nk // LANE):
                sl = slice(kb * LANE, (kb + 1) * LANE)
                eqb = jnp.where(eq[sl], 1.0, 0.0)
                pre = jnp.dot(lower, eqb.astype(CDT), preferred_element_type=F32) + run
                keep = gt[sl] | (eq[sl] & (pre < needf))
                selt_ref[sl, :] = jnp.where(keep & causal[sl], 0.0, NEG_INF)
                run = run + jnp.sum(eqb, axis=0, keepdims=True)

        for kb in range(nk // LANE):
            sel_ref[:, kb * LANE:(kb + 1) * LANE] = jnp.transpose(selt_ref[kb * LANE:(kb + 1) * LANE, :])

        scale = hd ** -0.5
        nband = tq // LANE + 1
        nkb = nk // LANE
        first_near = max(nkb - (T // n_cls) // LANE - tq // LANE, 0)
        for g in range(n_kv):
            kg = k_ref[:nk, g * hd:(g + 1) * hd]
            vg = v_ref[:nk, g * hd:(g + 1) * hd]
            for r in range(group):
                h = g * group + r
                s = _dot_nt(q_ref[:, h * hd:(h + 1) * hd], kg) * scale + rb_ref[N_BUCKETS - 1, h] + sel_ref[:, :nk]
                parts = [s[:, :first_near * LANE]] if first_near else []
                for kb in range(first_near, nkb):
                    w = qt * (tq // LANE) - kb + (tq // LANE - 1)
                    add = jnp.zeros((tq, LANE), F32)
                    for wv in range(nband):
                        add = jnp.where(w == wv, band_ref[wv, h], add)
                    parts.append(s[:, kb * LANE:(kb + 1) * LANE] + add)
                s = jnp.concatenate(parts, axis=1)
                m = jnp.max(s, axis=-1, keepdims=True)
                p = jnp.exp(s - m)
                l = jnp.sum(p, axis=-1, keepdims=True)
                o_ref[:, h * hd:(h + 1) * hd] = (_dot(p, vg) / l).astype(o_ref.dtype)

    width = T // n_cls
    cls = ((qt + 1) * tq - 1) // width
    for c in range(n_cls):
        pl.when(cls == c)(functools.partial(body, (c + 1) * width))


def _dsa_prompt(proj, proj_c, rel_bias, B, T, dims, topk):
    n_heads, n_kv, hd, n_idx, di = dims
    M, Np = proj.shape
    tq = math.gcd(T, DSA_TQ)
    assert tq % LANE == 0
    nqt = T // tq
    hq, hkv, hi = n_heads * hd, n_kv * hd, n_idx * di
    assert hq % hkv == 0 and (hq + 2 * hkv) % hi == 0 and (hq + 2 * hkv + hi) % LANE == 0
    kk_blk = (hq + 2 * hkv + hi) // LANE
    n_cls = math.gcd(T // LANE, DSA_EXTENT_CLASSES)
    kern = functools.partial(_dsa_prompt_kernel, n_heads=n_heads, n_kv=n_kv, hd=hd, n_idx=n_idx, di=di, topk=topk,
                             tq=tq, T=T, n_cls=n_cls)
    return pl.pallas_call(
        kern,
        grid=(B, nqt),
        in_specs=[pl.BlockSpec(memory_space=pltpu.SMEM),
                  pl.BlockSpec((tq, hq), lambda b, t: (b * nqt + t, 0)),
                  pl.BlockSpec((tq, hi), lambda b, t: (b * nqt + t, (hq + 2 * hkv) // hi)),
                  pl.BlockSpec((tq, LANE), lambda b, t: (b * nqt + t, kk_blk)),
                  pl.BlockSpec((T, hkv), lambda b, t: (b, hq // hkv)),
                  pl.BlockSpec((T, hkv), lambda b, t: (b, hq // hkv + 1)),
                  pl.BlockSpec((T, LANE), lambda b, t: (b, kk_blk))],
        out_specs=pl.BlockSpec((tq, hq), lambda b, t: (b * nqt + t, 0)),
        out_shape=jax.ShapeDtypeStruct((M, hq), CDT),
        scratch_shapes=[pltpu.VMEM((tq, T), F32),
                        pltpu.VMEM((tq // LANE + 1, n_heads, tq, LANE), F32),
                        pltpu.VMEM((T, tq), F32)],
        compiler_params=_cparams(("arbitrary", "arbitrary")),
        name="dsa_prompt",
    )(rel_bias, proj_c, proj_c, proj, proj_c, proj_c, proj_c)


def _dsa_select_kernel(pt_ref, qi_ref, wi_ref, kn_ref, *refs, n_pages_step, n_idx, di, topk, t_new):
    G = n_pages_step
    ki_refs = refs[:G]
    mask_ref, sc_ref = refs[G:]
    g = pl.program_id(1)
    NG = pl.num_programs(1)
    R = qi_ref.shape[1] // n_idx
    qi = qi_ref[0]
    wi = wi_ref[0] * (n_idx ** -0.5)

    def head_sum(lg):
        w = wi * jnp.maximum(lg * (di ** -0.5), 0.0)
        sc = w[0:R]
        for h in range(1, n_idx):
            sc = sc + w[h * R:(h + 1) * R]
        return sc

    past = head_sum(_dot(qi, jnp.concatenate([r[0, 0] for r in ki_refs], axis=1)))
    GW = past.shape[1]
    sc_ref[g] = past

    @pl.when(g == NG - 1)
    def _():
        new = head_sum(_dot_nt(qi, kn_ref[0][:, :di]))
        t_row = lax.broadcasted_iota(jnp.int32, new.shape, 0) % t_new
        col = lax.broadcasted_iota(jnp.int32, new.shape, 1)
        new = jnp.where((col <= t_row) & (col < t_new), new, NEG_INF)
        sc_ref[NG] = jnp.concatenate([new, jnp.full((R, GW - LANE), NEG_INF, F32)], axis=1)
        score = sc_ref[...]
        valid = score > NEG_INF
        key = _order_key(score)
        thr = _kth_largest_key(key, topk, (0, 2), two_bit_steps=True)
        gt = key > thr
        eq = key == thr
        n_gt = jnp.sum(jnp.where(gt, 1, 0), axis=(0, 2), keepdims=True)
        n_eq = jnp.sum(jnp.where(eq, 1, 0), axis=(0, 2), keepdims=True)
        need = topk - n_gt
        tie = jnp.max(n_eq - need) > 0

        @pl.when(jnp.logical_not(tie))
        def _():
            mask_ref[0] = jnp.where((gt | eq) & valid, 0.0, NEG_INF)

        @pl.when(tie)
        def _():
            upper = _strict_upper(LANE)
            needf = need[0].astype(F32)

            def blk(gi, run):
                k_g = _order_key(sc_ref[gi])
                v_g = sc_ref[gi] > NEG_INF
                for c in range(GW // LANE):
                    sl = slice(c * LANE, (c + 1) * LANE)
                    eqb = jnp.where(k_g[:, sl] == thr[0], 1.0, 0.0)
                    pre = jnp.dot(eqb.astype(CDT), upper, preferred_element_type=F32) + run
                    keep = (k_g[:, sl] > thr[0]) | ((eqb > 0.0) & (pre < needf))
                    mask_ref[0, gi, :, sl] = jnp.where(keep & v_g[:, sl], 0.0, NEG_INF)
                    run = run + jnp.sum(eqb, axis=1, keepdims=True)
                return run

            lax.fori_loop(0, NG + 1, blk, jnp.zeros((R, 1), F32))


def _dsa_select(qi_hm, wi_hm, kn, cache_ki_t, layer, page_table, B, n_idx, di, topk, t_new):
    n_pages = page_table.shape[1]
    page = cache_ki_t.shape[3]
    G = math.gcd(n_pages, PAGES_PER_STEP_SELECT)
    NG = n_pages // G
    GW = G * page
    R = qi_hm.shape[1] // n_idx
    kern = functools.partial(_dsa_select_kernel, n_pages_step=G, n_idx=n_idx, di=di, topk=topk, t_new=t_new)

    def page_map(i):
        return lambda b, g, pt: (layer, pt[b, g * G + i], 0, 0)

    in_specs = [pl.BlockSpec((1, n_idx * R, di), lambda b, g, pt: (b, 0, 0)),
                pl.BlockSpec((1, n_idx * R, 1), lambda b, g, pt: (b, 0, 0)),
                pl.BlockSpec((1, LANE, LANE), lambda b, g, pt: (b, 0, 0))]
    in_specs += [pl.BlockSpec((1, 1, di, page), page_map(i)) for i in range(G)]
    return pl.pallas_call(
        kern,
        grid_spec=pltpu.PrefetchScalarGridSpec(
            num_scalar_prefetch=1,
            grid=(B, NG),
            in_specs=in_specs,
            out_specs=pl.BlockSpec((1, NG + 1, R, GW), lambda b, g, pt: (b, 0, 0, 0)),
            scratch_shapes=[pltpu.VMEM((NG + 1, R, GW), F32)]),
        out_shape=jax.ShapeDtypeStruct((B, NG + 1, R, GW), F32),
        compiler_params=_cparams(("parallel", "arbitrary")),
        name="dsa_select",
    )(page_table, qi_hm, wi_hm, kn, *([cache_ki_t] * G))


def _dsa_decode_kernel(pt_ref, rb_ref, q_ref, kn_ref, vn_ref, mask_ref, mnew_ref, *refs,
                       n_pages_step, n_kv, group, hd, t_new, past_len):
    G = n_pages_step
    k_refs = refs[:G]
    v_refs = refs[G:2 * G]
    o_ref, m_ref, l_ref, acc_ref = refs[2 * G:]
    g = pl.program_id(1)
    NG = pl.num_programs(1)
    R = group * t_new
    scale = hd ** -0.5

    @pl.when(g == 0)
    def _():
        m_ref[...] = jnp.full(m_ref.shape, NEG_INF, F32)
        l_ref[...] = jnp.zeros(l_ref.shape, F32)
        acc_ref[...] = jnp.zeros(acc_ref.shape, F32)

    def head_val(kvh, bucket):
        row = lax.broadcasted_iota(jnp.int32, (R, 1), 0)
        val = jnp.full((R, 1), rb_ref[bucket, kvh * group], F32)
        for r in range(1, group):
            val = jnp.where(row >= r * t_new, rb_ref[bucket, kvh * group + r], val)
        return val

    def update(kvh, s, v):
        m_prev = m_ref[kvh]
        m_new = jnp.maximum(m_prev, jnp.max(s, axis=-1, keepdims=True))
        m_safe = jnp.where(m_new > NEG_INF, m_new, 0.0)
        alpha = jnp.exp(m_prev - m_safe)
        p = jnp.exp(s - m_safe)
        l_ref[kvh] = alpha * l_ref[kvh] + jnp.sum(p, axis=-1, keepdims=True)
        acc_ref[kvh] = alpha * acc_ref[kvh] + _dot(p, v)
        m_ref[kvh] = m_new

    page = k_refs[0].shape[2] // n_kv
    GW = G * page
    mask = mask_ref[0, 0]
    t_row = lax.broadcasted_iota(jnp.int32, (R, GW), 0) % t_new
    col = lax.broadcasted_iota(jnp.int32, (R, GW), 1)
    dist = past_len + t_row - (g * GW + col)
    near = g == NG - 1

    def head_rows(refs_, kvh):
        return jnp.concatenate([r[0, 0, pl.ds(kvh, page, stride=n_kv), :] for r in refs_], axis=0).astype(CDT)

    for kvh in range(n_kv):
        s = _dot_nt(q_ref[0, kvh], head_rows(k_refs, kvh)) * scale
        bias = lax.cond(near,
                        lambda: _bias_chain(dist, functools.partial(head_val, kvh)),
                        lambda: jnp.broadcast_to(head_val(kvh, N_BUCKETS - 1), (R, GW)))
        update(kvh, s + bias + mask, head_rows(v_refs, kvh))

    @pl.when(near)
    def _():
        kn = kn_ref[0].astype(CDT)
        vn = vn_ref[0].astype(CDT)
        mnew = mnew_ref[0, 0][:, :LANE]
        t_r = lax.broadcasted_iota(jnp.int32, (R, LANE), 0) % t_new
        c = lax.broadcasted_iota(jnp.int32, (R, LANE), 1)
        d_new = jnp.maximum(t_r - c, 0)
        for kvh in range(n_kv):
            sl = slice(kvh * hd, (kvh + 1) * hd)
            s = _dot_nt(q_ref[0, kvh], kn[:, sl]) * scale
            bias = _bias_chain(d_new, functools.partial(head_val, kvh))
            update(kvh, s + bias + mnew, vn[:, sl])
            o_ref[0, kvh] = (acc_ref[kvh] / l_ref[kvh]).astype(o_ref.dtype)


def _dsa_decode(q4, kn, vn, mask, rel_bias, cache_k, cache_v, layer, page_table, B, n_kv, group, hd, t_new):
    n_pages = page_table.shape[1]
    page = cache_k.shape[2] // n_kv
    GWs = mask.shape[3]
    G = math.gcd(math.gcd(n_pages, PAGES_PER_STEP_DECODE), GWs // page)
    NG = n_pages // G
    GW = G * page
    ratio = GWs // GW
    R = group * t_new
    kern = functools.partial(_dsa_decode_kernel, n_pages_step=G, n_kv=n_kv, group=group, hd=hd, t_new=t_new,
                             past_len=n_pages * page)

    def page_map(i):
        return lambda b, g, pt: (layer, pt[b, g * G + i], 0, 0)

    in_specs = [pl.BlockSpec(memory_space=pltpu.SMEM),
                pl.BlockSpec((1, n_kv, R, hd), lambda b, g, pt: (b, 0, 0, 0)),
                pl.BlockSpec((1, LANE, n_kv * hd), lambda b, g, pt: (b, 0, 0)),
                pl.BlockSpec((1, LANE, n_kv * hd), lambda b, g, pt: (b, 0, 0)),
                pl.BlockSpec((1, 1, R, GW), lambda b, g, pt: (b, g // ratio, 0, g % ratio)),
                pl.BlockSpec((1, 1, R, GW), lambda b, g, pt: (b, mask.shape[1] - 1, 0, 0))]
    in_specs += [pl.BlockSpec((1, 1, page * n_kv, hd), page_map(i)) for i in range(G)]
    in_specs += [pl.BlockSpec((1, 1, page * n_kv, hd), page_map(i)) for i in range(G)]
    return pl.pallas_call(
        kern,
        grid_spec=pltpu.PrefetchScalarGridSpec(
            num_scalar_prefetch=1,
            grid=(B, NG),
            in_specs=in_specs,
            out_specs=pl.BlockSpec((1, n_kv, R, hd), lambda b, g, pt: (b, 0, 0, 0)),
            scratch_shapes=[pltpu.VMEM((n_kv, R, 1), F32), pltpu.VMEM((n_kv, R, 1), F32),
                            pltpu.VMEM((n_kv, R, hd), F32)]),
        out_shape=jax.ShapeDtypeStruct((B, n_kv, R, hd), CDT),
        compiler_params=_cparams(("parallel", "arbitrary")),
        name="dsa_decode",
    )(page_table, rel_bias, q4, kn, vn, mask, mask, *([cache_k] * G), *([cache_v] * G))


def _ffn_kernel(*refs, seq_tiles, t_seq, conv_w):
    if seq_tiles:
        (x_ref, g_ref, sc_ref, sh_ref, gt_ref, wg_ref, wv_ref, wd_ref, cw_ref, cb_ref,
         y_ref, tail_ref, h_ref, acc_ref, stash_ref) = refs
    else:
        (x_ref, g_ref, sc_ref, sh_ref, gt_ref, wg_ref, wv_ref, wd_ref, cw_ref, cb_ref, p1_ref, p2_ref,
         y_ref, tail_ref, h_ref, acc_ref) = refs
    i = pl.program_id(0)
    j = pl.program_id(1)

    @pl.when(j == 0)
    def _():
        h = _rms(x_ref[...], g_ref[...]) * (1.0 + sc_ref[0]) + sh_ref[0]
        h_ref[...] = h.astype(CDT)
        acc_ref[...] = jnp.zeros(acc_ref.shape, F32)

    if seq_tiles:
        @pl.when((i == 0) & (j == 0))
        def _():
            stash_ref[...] = jnp.zeros(stash_ref.shape, F32)

    gate = jnp.dot(h_ref[...], wg_ref[...], preferred_element_type=F32)
    val = jnp.dot(h_ref[...], wv_ref[...], preferred_element_type=F32)
    tm = gate.shape[0]
    row = lax.broadcasted_iota(jnp.int32, gate.shape, 0)
    g1 = pltpu.roll(gate, 1, 0)
    g2 = pltpu.roll(gate, 2, 0)
    if seq_tiles:
        prev = jnp.where(i % seq_tiles == 0, 0.0, stash_ref[j])
        g1 = jnp.where(row == 0, prev[7:8], g1)
        g2 = jnp.where(row == 0, prev[6:7], jnp.where(row == 1, prev[7:8], g2))
        stash_ref[j] = gate[tm - 8:]
        tail_ref[0] = gate[tm - 8:]
    else:
        t = row % t_seq
        g1 = jnp.where(t == 0, p1_ref[...], g1)
        g2 = jnp.where(t < 2, p2_ref[...], g2)
        tail_ref[...] = gate
    cw = cw_ref[...]
    conv = cw[0:1] * g2 + cw[1:2] * g1 + cw[2:3] * gate + cb_ref[...]
    act = (_silu(conv) * val).astype(CDT)
    acc_ref[...] += jnp.dot(act, wd_ref[...], preferred_element_type=F32)

    @pl.when(j == pl.num_programs(1) - 1)
    def _():
        y_ref[...] = x_ref[...] + gt_ref[0] * acc_ref[...]


def _ffn(x, g, sc, sh, gt, w_up, w_down, conv_w, conv_b, rows_per_group, t_seq, prev=None):
    M, D = x.shape
    Fd = w_down.shape[0]
    assert conv_w.shape[0] == 3
    tf = _tile(Fd, FFN_TILE_F)
    nf = Fd // tf
    cw = jnp.pad(conv_w, ((0, 8 - conv_w.shape[0]), (0, 0)))
    cb = conv_b.reshape(1, Fd)
    seq_mode = prev is None
    if seq_mode:
        tm = _tile(t_seq, 512, 8)
        assert t_seq % tm == 0 and tm >= 8
        seq_tiles = t_seq // tm
    else:
        tm = M
        seq_tiles = 0
    nt = M // tm
    tpg = max(rows_per_group // tm, 1)
    mod = lambda m: pl.BlockSpec((1, m.shape[1], D), lambda i, j: (i // tpg, 0, 0))
    in_specs = [pl.BlockSpec((tm, D), lambda i, j: (i, 0)),
                pl.BlockSpec((1, D), lambda i, j: (0, 0)),
                mod(sc), mod(sh), mod(gt),
                pl.BlockSpec((D, tf), lambda i, j: (0, j)),
                pl.BlockSpec((D, tf), lambda i, j: (0, nf + j)),
                pl.BlockSpec((tf, D), lambda i, j: (j, 0)),
                pl.BlockSpec((8, tf), lambda i, j: (0, j)),
                pl.BlockSpec((1, tf), lambda i, j: (0, j))]
    args = [x, g.reshape(1, D), sc, sh, gt, w_up, w_up, w_down, cw, cb]
    scratch = [pltpu.VMEM((tm, D), CDT), pltpu.VMEM((tm, D), F32)]
    if seq_mode:
        tail_shape = jax.ShapeDtypeStruct((nt, 8, Fd), F32)
        tail_spec = pl.BlockSpec((1, 8, tf), lambda i, j: (i, 0, j))
        scratch.append(pltpu.VMEM((nf, 8, tf), F32))
    else:
        p1 = jnp.concatenate([prev[:, 1:2], jnp.zeros_like(prev[:, :1]).repeat(t_seq - 1, axis=1)], axis=1)
        p2 = jnp.concatenate([prev[:, 0:2], jnp.zeros_like(prev[:, :1]).repeat(t_seq - 2, axis=1)], axis=1)
        args += [p1.reshape(M, Fd), p2.reshape(M, Fd)]
        in_specs += [pl.BlockSpec((tm, tf), lambda i, j: (i, j))] * 2
        tail_shape = jax.ShapeDtypeStruct((M, Fd), F32)
        tail_spec = pl.BlockSpec((tm, tf), lambda i, j: (i, j))
    kern = functools.partial(_ffn_kernel, seq_tiles=seq_tiles, t_seq=t_seq, conv_w=conv_w.shape[0])
    return pl.pallas_call(
        kern,
        grid=(nt, nf),
        in_specs=in_specs,
        out_specs=[pl.BlockSpec((tm, D), lambda i, j: (i, 0)), tail_spec],
        out_shape=[jax.ShapeDtypeStruct((M, D), F32), tail_shape],
        scratch_shapes=scratch,
        compiler_params=_cparams(("arbitrary", "arbitrary")),
        name="ffn",
    )(*args)


def _final_norm_kernel(x_ref, g_ref, o_ref):
    o_ref[...] = _rms(x_ref[...], g_ref[...])


def _final_norm(x, g):
    M, D = x.shape
    tm = _tile(M, 1024, 8)
    return pl.pallas_call(
        _final_norm_kernel,
        grid=(M // tm,),
        in_specs=[pl.BlockSpec((tm, D), lambda i: (i, 0)), pl.BlockSpec((1, D), lambda i: (0, 0))],
        out_specs=pl.BlockSpec((tm, D), lambda i: (i, 0)),
        out_shape=jax.ShapeDtypeStruct((M, D), F32),
        compiler_params=_cparams(("parallel",)),
        name="final_norm",
    )(x, g.reshape(1, D))


def _rope_tables(pos, rope, n_heads, reps):
    half = rope // 2
    inv = ROPE_BASE ** (-jnp.arange(half, dtype=F32) / half)
    ang = pos.astype(F32)[:, None] * inv[None, :]
    cos = jnp.cos(ang)
    sin = jnp.sin(ang)
    cosk = jnp.concatenate([cos, cos], axis=1)
    sink = jnp.concatenate([-sin, sin], axis=1)
    tabs = (jnp.tile(cosk, (1, n_heads)), jnp.tile(sink, (1, n_heads)), cosk, sink)
    return tuple(jnp.tile(t, (reps, 1)) for t in tabs)


def _swap_halves(w, rope):
    half = rope // 2
    return jnp.concatenate([w[..., half:], w[..., :half]], axis=-1)


def _pad_cols(w, n):
    return jnp.pad(w, ((0, 0), (0, n - w.shape[1])))


def kernel(x_prompt, x_sample, cache_mla_ckv, cache_mla_kpe, state_gla, cache_dsa_k, cache_dsa_v, cache_dsa_kidx,
           state_ffn_conv, page_table, c_prompt, c_sample, ada_w, ada_b, norm1_g, norm2_g, final_g, mla_w_in,
           mla_g_q, mla_g_kv, mla_w_uq, mla_w_uk, mla_w_uv, mla_w_o, gla_w_in, gla_w_a2, gla_b_a2, gla_g_o,
           gla_w_o, dsa_w_in, dsa_w_o, rel_bias, ffn_w_up, ffn_conv_w, ffn_conv_b, ffn_w_down):
    Bp, Tp, D = x_prompt.shape
    Bs, Ts, _ = x_sample.shape
    depth = ada_w.shape[0]
    n_mod = ada_w.shape[2] // D
    n_pages, page = page_table.shape[1], cache_mla_ckv.shape[2]
    past_len = n_pages * page
    cache_kpe_t = jnp.swapaxes(cache_mla_kpe, 2, 3)
    cache_ki_t = jnp.swapaxes(cache_dsa_kidx, 2, 3)

    q_lora, mla_h, qk_dim = mla_w_uq.shape[1:]
    kv_lora, _, nope = mla_w_uk.shape[1:]
    mla_v = mla_w_uv.shape[3]
    rope = qk_dim - nope
    gla_h = state_gla.shape[2]
    gla_dk, gla_dv = state_gla.shape[3:]
    gla_rank = gla_w_a2.shape[1]
    dsa_kv, dsa_hd = cache_dsa_k.shape[3:]
    dsa_di = cache_dsa_kidx.shape[3]
    dsa_h = dsa_w_o.shape[1] // dsa_hd
    dsa_hi = (dsa_w_in.shape[2] - (dsa_h + 2 * dsa_kv) * dsa_hd - dsa_di) // (dsa_di + 1)
    assert kv_lora % LANE == 0 and q_lora % LANE == 0 and rope <= LANE

    mod_all = _ada_mod(jnp.concatenate([c_prompt, c_sample], axis=0), ada_w, ada_b)

    def mods(l, sample):
        m = mod_all[l, Bp:] if sample else mod_all[l, :Bp]
        parts = [m[:, i * D:(i + 1) * D] for i in range(n_mod)]
        if sample:
            return [jnp.repeat(p, Ts, axis=0).reshape(1, Bs * Ts, D) for p in parts]
        return [p.reshape(Bp, 1, D) for p in parts]

    mla_w = []
    for j in range(mla_w_in.shape[0]):
        w_in = mla_w_in[j]
        kcol = q_lora + kv_lora
        w_in_ext = jnp.concatenate([w_in, _swap_halves(w_in[:, kcol:kcol + rope], rope)], axis=1)
        w_in_ext = _pad_cols(w_in_ext, -(-w_in_ext.shape[1] // LANE) * LANE).astype(CDT)
        uq = mla_w_uq[j]
        wqn = uq[:, :, :nope].reshape(q_lora, mla_h * nope).astype(CDT)
        wqp = uq[:, :, nope:].reshape(q_lora, mla_h * rope).astype(CDT)
        wqs = _swap_halves(uq[:, :, nope:], rope).reshape(q_lora, mla_h * rope).astype(CDT)
        wukT = jnp.transpose(mla_w_uk[j], (1, 2, 0)).astype(CDT)
        wuv = jnp.transpose(mla_w_uv[j], (1, 0, 2)).astype(CDT)
        mla_w.append((w_in_ext, wqn, wqp, wqs, wukT, wuv, mla_w_o[j].astype(CDT)))
    gla_np = -(-gla_w_in.shape[2] // LANE) * LANE
    gla_w = []
    for j in range(gla_w_in.shape[0]):
        w_a2p = jnp.pad(gla_w_a2[j], ((0, LANE - gla_rank), (0, 0))).astype(CDT)
        gla_w.append((_pad_cols(gla_w_in[j], gla_np).astype(CDT), w_a2p, gla_w_o[j].astype(CDT)))
    dsa_np = -(-dsa_w_in.shape[2] // LANE) * LANE
    dsa_w = [(_pad_cols(dsa_w_in[j], dsa_np).astype(CDT), dsa_w_o[j].astype(CDT)) for j in range(dsa_w_in.shape[0])]
    ffn_up = ffn_w_up.astype(CDT)
    ffn_down = ffn_w_down.astype(CDT)

    mla_dims = (mla_h, q_lora, kv_lora, rope, nope)
    mla_scale = qk_dim ** -0.5

    def trunk(x3, sample):
        B, T, _ = x3.shape
        M = B * T
        x = x3.reshape(M, D)
        rpg = M if sample else T
        pos = (past_len if sample else 0) + jnp.arange(T, dtype=jnp.int32)
        tabs = _rope_tables(pos, rope, mla_h, B if sample else 1)
        outs = dict(mla_ckv=[], mla_kpe=[], gla=[], dsa_k=[], dsa_v=[], dsa_ki=[], conv=[])
        for l in range(depth):
            sh1, sc1, gt1, sh2, sc2, gt2 = mods(l, sample)
            j = l // 3
            if l % 3 == 0:
                w_in_ext, wqn, wqp, wqs, wukT, wuv, wo = mla_w[j]
                p = _nm_linear(x, norm1_g[l], sc1, sh1, w_in_ext, rpg)
                ckv, kpe, kcat, qcat = _mla_prep(p, mla_g_q[j], mla_g_kv[j], tabs, wqn, wqp, wqs, wukT, mla_dims)
                if sample:
                    o_lat = _mla_decode(qcat, kcat, cache_mla_ckv, cache_kpe_t, j, page_table, B, T, mla_h,
                                        kv_lora, rope, mla_scale)
                else:
                    o_lat = _mla_flash(qcat, kcat, B, T, mla_h, kv_lora, mla_scale)
                x = _mla_out(o_lat, wuv, wo, x, gt1, rpg, mla_h, kv_lora)
                outs["mla_ckv"].append(ckv.reshape(B, T, kv_lora))
                outs["mla_kpe"].append(kpe.reshape(B, T, rope))
            elif l % 3 == 1:
                w_in_p, w_a2p, wo = gla_w[j]
                hk, hv = gla_h * gla_dk, gla_h * gla_dv
                p = _nm_linear(x, norm1_g[l], sc1, sh1, w_in_p, rpg, tn_target=NM_LINEAR_TN)
                log_a = _gla_gate(p, (2 * hk + 2 * hv) // LANE, w_a2p, gla_b_a2[j])
                s0 = state_gla[j] if sample else jnp.zeros((B, gla_h, gla_dk, gla_dv), F32)
                o, s_fin = _gla_recurrence(p, log_a, s0, B, T, gla_h, gla_dk, gla_dv)
                x = _gla_out(o, p, (2 * hk + hv) // hv, gla_g_o[j], wo, x, gt1, rpg, gla_h, gla_dv)
                outs["gla"].append(s_fin)
            else:
                w_in_p, wo = dsa_w[j]
                hq, hkv, hi = dsa_h * dsa_hd, dsa_kv * dsa_hd, dsa_hi * dsa_di
                p = _nm_linear(x, norm1_g[l], sc1, sh1, w_in_p, rpg, tn_target=NM_LINEAR_TN, with_cdt_copy=not sample)
                if not sample:
                    p, p_c = p
                k_new = p[:, hq:hq + hkv]
                v_new = p[:, hq + hkv:hq + 2 * hkv]
                kk = p[:, hq + 2 * hkv + hi:]
                L_keys = (past_len if sample else 0) + T
                topk = min(DSA_TOPK, L_keys // 4)
                if sample:
                    group = dsa_h // dsa_kv
                    dup = lambda a: jnp.concatenate([a.reshape(B, T, dsa_hi, -1)] * group, axis=1)
                    qi_hm = dup(p[:, hq + 2 * hkv:hq + 2 * hkv + hi]).transpose(0, 2, 1, 3)
                    qi_hm = qi_hm.reshape(B, dsa_hi * group * T, dsa_di)
                    wi_hm = dup(kk[:, dsa_di:dsa_di + dsa_hi]).transpose(0, 2, 1, 3).reshape(B, dsa_hi * group * T, 1)
                    padn = lambda a: jnp.pad(a.reshape(B, T, -1), ((0, 0), (0, LANE - T), (0, 0)))
                    mask = _dsa_select(qi_hm, wi_hm, padn(kk), cache_ki_t, j, page_table, B, dsa_hi, dsa_di, topk, T)
                    q4 = p[:, :hq].reshape(B, T, dsa_kv, group, dsa_hd).transpose(0, 2, 3, 1, 4)
                    q4 = q4.reshape(B, dsa_kv, group * T, dsa_hd)
                    ck = cache_dsa_k.reshape(*cache_dsa_k.shape[:2], page * dsa_kv, dsa_hd)
                    cv = cache_dsa_v.reshape(*cache_dsa_v.shape[:2], page * dsa_kv, dsa_hd)
                    o4 = _dsa_decode(q4, padn(k_new), padn(v_new), mask, rel_bias, ck, cv, j, page_table, B,
                                     dsa_kv, group, dsa_hd, T)
                    o = o4.reshape(B, dsa_kv, group, T, dsa_hd).transpose(0, 3, 1, 2, 4).reshape(M, hq)
                else:
                    o = _dsa_prompt(p, p_c, rel_bias, B, T, (dsa_h, dsa_kv, dsa_hd, dsa_hi, dsa_di), topk)
                x = _proj_res(o, wo, x, gt1, rpg)
                outs["dsa_k"].append(k_new.reshape(B, T, dsa_kv, dsa_hd))
                outs["dsa_v"].append(v_new.reshape(B, T, dsa_kv, dsa_hd))
                outs["dsa_ki"].append(kk[:, :dsa_di].reshape(B, T, dsa_di))
            if sample:
                x, tail = _ffn(x, norm2_g[l], sc2, sh2, gt2, ffn_up[l], ffn_down[l], ffn_conv_w[l], ffn_conv_b[l],
                               rpg, T, prev=state_ffn_conv[l])
                outs["conv"].append(tail.reshape(B, T, -1)[:, T - 2:])
            else:
                x, tail = _ffn(x, norm2_g[l], sc2, sh2, gt2, ffn_up[l], ffn_down[l], ffn_conv_w[l], ffn_conv_b[l],
                               rpg, T)
                nt = tail.shape[0] // B
                outs["conv"].append(tail.reshape(B, nt, 8, -1)[:, nt - 1, 6:8])
        y = _final_norm(x, final_g).reshape(B, T, D)
        return (y, jnp.stack(outs["mla_ckv"]), jnp.stack(outs["mla_kpe"]), jnp.stack(outs["gla"]),
                jnp.stack(outs["dsa_k"]), jnp.stack(outs["dsa_v"]), jnp.stack(outs["dsa_ki"]),
                jnp.stack(outs["conv"]))

    rp = trunk(x_prompt, False)
    rs = trunk(x_sample, True)
    return (rp[0], rs[0]) + tuple(rp[1:]) + tuple(rs[1:])
```

```python
import functools
import math

import numpy as np
import jax
import jax.numpy as jnp
from jax import lax
from jax.experimental import pallas as pl
from jax.experimental.pallas import tpu as pltpu

F32 = jnp.float32
CDT = jnp.bfloat16
EPS = 1e-6
ROPE_BASE = 10000.0
GLA_TAU = 16.0
N_BUCKETS = 32
MAX_DISTANCE = 128
DSA_TOPK = 256
LANE = 128
NEG_INF = float("-inf")
INT_MIN = -2 ** 31
VMEM_LIMIT = 56 * 1024 * 1024
PAGES_PER_STEP_MLA = 64
SEQS_PER_STEP_MLA = 4
SEQS_PER_STEP_SELECT = 2
PAGES_PER_STEP_SELECT = 64
PAGES_PER_STEP_DECODE = 32
FFN_TILE_F = 1408
DSA_EXTENT_CLASSES = 4
DSA_TQ = 128
NM_LINEAR_TN = 4096
FLASH_TQ = 512
FLASH_TK = 512
FLASH_ROW_CHUNK = 64


def _bucket_thresholds():
    d = np.arange(0, 4 * MAX_DISTANCE)
    exact = N_BUCKETS // 2
    lr = np.log(np.maximum(d, 1).astype(np.float32) / np.float32(exact)) / np.float32(math.log(MAX_DISTANCE / exact))
    large = np.minimum(exact + (lr * np.float32(N_BUCKETS - exact)).astype(np.int32), N_BUCKETS - 1)
    b = np.where(d < exact, d, large)
    return [int(np.argmax(b >= j)) for j in range(N_BUCKETS)]


BUCKET_LO = _bucket_thresholds()


def _cparams(sem, vmem=VMEM_LIMIT):
    return pltpu.CompilerParams(dimension_semantics=sem, vmem_limit_bytes=vmem)


def _tile(n, target, mult=LANE):
    if n <= target:
        return n
    t = (target // mult) * mult
    while t > mult and n % t:
        t -= mult
    assert n % t == 0, (n, target)
    return t


def _dot(a, b):
    return jnp.dot(a.astype(CDT), b.astype(CDT), preferred_element_type=F32)


def _dot_nt(a, b):
    return lax.dot_general(a.astype(CDT), b.astype(CDT), (((1,), (1,)), ((), ())), preferred_element_type=F32)


def _dot_tn(a, b):
    return lax.dot_general(a.astype(CDT), b.astype(CDT), (((0,), (0,)), ((), ())), preferred_element_type=F32)


def _rms(x, g):
    return x * lax.rsqrt(jnp.mean(x * x, axis=-1, keepdims=True) + EPS) * g


def _silu(x):
    return x * jax.nn.sigmoid(x)


def _ada_kernel(c_ref, w_ref, b_ref, o_ref):
    ca = _silu(c_ref[...])
    o_ref[0] = _dot(ca, w_ref[0]) + b_ref[0]


def _ada_mod(c, ada_w, ada_b):
    L, D, N = ada_w.shape
    R = c.shape[0]
    tn = _tile(N, 1536)
    return pl.pallas_call(
        _ada_kernel,
        grid=(L, N // tn),
        in_specs=[pl.BlockSpec((R, D), lambda l, j: (0, 0)),
                  pl.BlockSpec((1, D, tn), lambda l, j: (l, 0, j)),
                  pl.BlockSpec((1, 1, tn), lambda l, j: (l, 0, j))],
        out_specs=pl.BlockSpec((1, R, tn), lambda l, j: (l, 0, j)),
        out_shape=jax.ShapeDtypeStruct((L, R, N), F32),
        compiler_params=_cparams(("parallel", "parallel")),
        name="ada_mod",
    )(c, ada_w, ada_b.reshape(L, 1, N))


def _nm_linear_kernel(x_ref, g_ref, sc_ref, sh_ref, w_ref, o_ref, *rest):
    h_ref = rest[-1]

    @pl.when(pl.program_id(1) == 0)
    def _():
        h = _rms(x_ref[...], g_ref[...]) * (1.0 + sc_ref[0]) + sh_ref[0]
        h_ref[...] = h.astype(CDT)

    o = jnp.dot(h_ref[...], w_ref[...], preferred_element_type=F32)
    o_ref[...] = o
    if len(rest) == 2:
        rest[0][...] = o.astype(CDT)


def _mod_spec(mod, tm, rows_per_group):
    G, R, D = mod.shape
    tpg = max(rows_per_group // tm, 1)
    return pl.BlockSpec((1, R, D), lambda i, j: (i // tpg, 0, 0))


def _nm_linear(x, g, sc, sh, w, rows_per_group, tm_target=512, tn_target=1024, with_cdt_copy=False):
    M, D = x.shape
    N = w.shape[1]
    tm = _tile(M, tm_target, 8)
    tn = _tile(N, tn_target)
    out_spec = pl.BlockSpec((tm, tn), lambda i, j: (i, j))
    out_shape = jax.ShapeDtypeStruct((M, N), F32)
    if with_cdt_copy:
        out_spec = [out_spec, out_spec]
        out_shape = [out_shape, jax.ShapeDtypeStruct((M, N), CDT)]
    return pl.pallas_call(
        _nm_linear_kernel,
        grid=(M // tm, N // tn),
        in_specs=[pl.BlockSpec((tm, D), lambda i, j: (i, 0)),
                  pl.BlockSpec((1, D), lambda i, j: (0, 0)),
                  _mod_spec(sc, tm, rows_per_group),
                  _mod_spec(sh, tm, rows_per_group),
                  pl.BlockSpec((D, tn), lambda i, j: (0, j))],
        out_specs=out_spec,
        out_shape=out_shape,
        scratch_shapes=[pltpu.VMEM((tm, D), CDT)],
        compiler_params=_cparams(("parallel", "arbitrary")),
        name="nm_linear",
    )(x, g.reshape(1, D), sc, sh, w)


def _mla_prep_kernel(p_ref, gq_ref, gkv_ref, cq_ref, sq_ref, ck_ref, sk_ref, wqn_ref, wqp_ref, wqs_ref, wuk_ref,
                     ckv_ref, kpe_ref, kcat_ref, qcat_ref, *, n_heads, q_lora, kv_lora, rope, nope):
    p = p_ref[...]
    cq = p[:, :q_lora]
    ckv = p[:, q_lora:q_lora + kv_lora]
    kpe = p[:, q_lora + kv_lora:q_lora + kv_lora + rope]
    kpe_sw = p[:, q_lora + kv_lora + rope:q_lora + kv_lora + 2 * rope]
    cqn = _rms(cq, gq_ref[...]).astype(CDT)
    q_nope = jnp.dot(cqn, wqn_ref[...], preferred_element_type=F32)
    q_pe = (jnp.dot(cqn, wqp_ref[...], preferred_element_type=F32) * cq_ref[...]
            + jnp.dot(cqn, wqs_ref[...], preferred_element_type=F32) * sq_ref[...])
    ckv_n = _rms(ckv, gkv_ref[...])
    kpe_r = kpe * ck_ref[...] + kpe_sw * sk_ref[...]
    ckv_ref[...] = ckv_n
    kpe_ref[...] = kpe_r
    tm = p.shape[0]
    hd = kv_lora + LANE
    pad = jnp.zeros((tm, LANE - rope), CDT)
    kcat_ref[...] = jnp.concatenate([ckv_n.astype(CDT), kpe_r.astype(CDT), pad], axis=1)
    for h in range(n_heads):
        q_lat = _dot(q_nope[:, h * nope:(h + 1) * nope], wuk_ref[h])
        qcat_ref[:, h * hd:(h + 1) * hd] = jnp.concatenate(
            [q_lat.astype(CDT), q_pe[:, h * rope:(h + 1) * rope].astype(CDT), pad], axis=1)


def _mla_prep(p, g_q, g_kv, tabs, wqn, wqp, wqs, wukT, dims):
    n_heads, q_lora, kv_lora, rope, nope = dims
    M, NP = p.shape
    cosq, sinq, cosk, sink = tabs
    tm = _tile(M, 256, 8)
    ntab = cosq.shape[0] // tm
    hd = kv_lora + LANE
    row = lambda i: (i, 0)
    tab = lambda i: (i % ntab, 0)
    full2 = lambda i: (0, 0)
    full3 = lambda i: (0, 0, 0)
    kern = functools.partial(_mla_prep_kernel, n_heads=n_heads, q_lora=q_lora, kv_lora=kv_lora, rope=rope, nope=nope)
    return pl.pallas_call(
        kern,
        grid=(M // tm,),
        in_specs=[pl.BlockSpec((tm, NP), row),
                  pl.BlockSpec((1, q_lora), full2),
                  pl.BlockSpec((1, kv_lora), full2),
                  pl.BlockSpec((tm, n_heads * rope), tab),
                  pl.BlockSpec((tm, n_heads * rope), tab),
                  pl.BlockSpec((tm, rope), tab),
                  pl.BlockSpec((tm, rope), tab),
                  pl.BlockSpec(wqn.shape, full2),
                  pl.BlockSpec(wqp.shape, full2),
                  pl.BlockSpec(wqs.shape, full2),
                  pl.BlockSpec(wukT.shape, full3)],
        out_specs=[pl.BlockSpec((tm, kv_lora), row),
                   pl.BlockSpec((tm, rope), row),
                   pl.BlockSpec((tm, hd), row),
                   pl.BlockSpec((tm, n_heads * hd), row)],
        out_shape=[jax.ShapeDtypeStruct((M, kv_lora), F32),
                   jax.ShapeDtypeStruct((M, rope), F32),
                   jax.ShapeDtypeStruct((M, hd), CDT),
                   jax.ShapeDtypeStruct((M, n_heads * hd), CDT)],
        compiler_params=_cparams(("parallel",)),
        name="mla_prep",
    )(p, g_q.reshape(1, -1), g_kv.reshape(1, -1), cosq, sinq, cosk, sink, wqn, wqp, wqs, wukT)


def _mla_flash_kernel(q_ref, k_ref, o_ref, m_ref, l_ref, acc_ref, s2_ref, p2_ref, pm_ref, ps_ref, *,
                      n_heads, tq, tk, hd, dv, scale, rc):
    qi = pl.program_id(1)
    ki = pl.program_id(2)

    @pl.when(ki == 0)
    def _():
        m_ref[...] = jnp.full(m_ref.shape, NEG_INF, F32)
        l_ref[...] = jnp.zeros(l_ref.shape, F32)
        acc_ref[...] = jnp.zeros(acc_ref.shape, F32)

    c = scale * math.log2(math.e)

    def block(masked):
        k = k_ref[...]
        v = k[:, :dv]
        tiles = lambda a: [a[:, t * LANE:(t + 1) * LANE] for t in range(a.shape[1] // LANE)]

        def chunk_logits(s_ref, r):
            s = s_ref[r * rc:(r + 1) * rc, :]
            if masked:
                row = qi * tq + r * rc + lax.broadcasted_iota(jnp.int32, (rc, tk), 0)
                col = ki * tk + lax.broadcasted_iota(jnp.int32, (rc, tk), 1)
                s = jnp.where(col <= row, s, NEG_INF)
            return s

        for h in range(n_heads):
            b = h % 2
            s_ref, p_ref = s2_ref.at[b], p2_ref.at[b]
            s_ref[...] = _dot_nt(q_ref[:, h * hd:(h + 1) * hd], k)
            for r in range(tq // rc):
                pm_ref[b, r * rc:(r + 1) * rc, :] = functools.reduce(jnp.maximum, tiles(chunk_logits(s_ref, r)))
            m_prev = m_ref[h]
            m_new = jnp.maximum(m_prev, jnp.broadcast_to(jnp.max(pm_ref[b], axis=-1, keepdims=True), (tq, LANE)))
            alpha = jnp.exp2(c * (m_prev - m_new))
            m_ref[h] = m_new
            pm_ref[b] = m_new
            for r in range(tq // rc):
                mb = pm_ref[b, r * rc:(r + 1) * rc, :]
                p = [jnp.exp2(c * (st - mb)) for st in tiles(chunk_logits(s_ref, r))]
                ps_ref[b, r * rc:(r + 1) * rc, :] = functools.reduce(jnp.add, p)
                p_ref[r * rc:(r + 1) * rc, :] = jnp.concatenate(p, axis=1).astype(CDT)
            l_blk = jnp.broadcast_to(jnp.sum(ps_ref[b], axis=-1, keepdims=True), (tq, LANE))
            l_ref[h] = alpha * l_ref[h] + l_blk
            pv = jnp.dot(p_ref[...], v, preferred_element_type=F32)
            acc_ref[h] = jnp.concatenate([alpha] * (dv // LANE), axis=1) * acc_ref[h] + pv

    below_diag = ki * tk + tk - 1 <= qi * tq
    pl.when(below_diag)(functools.partial(block, False))
    pl.when(jnp.logical_not(below_diag) & (ki * tk <= qi * tq + tq - 1))(functools.partial(block, True))

    @pl.when(ki == pl.num_programs(2) - 1)
    def _():
        for h in range(n_heads):
            l = jnp.concatenate([l_ref[h]] * (dv // LANE), axis=1)
            o_ref[:, h * dv:(h + 1) * dv] = (acc_ref[h] / l).astype(o_ref.dtype)


def _mla_flash(qcat, kcat, B, T, n_heads, dv, scale):
    M, hd = kcat.shape
    tq = _tile(T, FLASH_TQ, 8)
    tk = _tile(T, FLASH_TK, 8)
    nq, nk = T // tq, T // tk
    assert tk % LANE == 0 and dv % LANE == 0
    rc = math.gcd(tq, FLASH_ROW_CHUNK)
    kern = functools.partial(_mla_flash_kernel, n_heads=n_heads, tq=tq, tk=tk, hd=hd, dv=dv, scale=scale, rc=rc)

    def kmap(b, qi, ki):
        return (b * nk + jnp.minimum(ki, (qi * tq + tq - 1) // tk), 0)

    return pl.pallas_call(
        kern,
        grid=(B, nq, nk),
        in_specs=[pl.BlockSpec((tq, n_heads * hd), lambda b, qi, ki: (b * nq + qi, 0)),
                  pl.BlockSpec((tk, hd), kmap)],
        out_specs=pl.BlockSpec((tq, n_heads * dv), lambda b, qi, ki: (b * nq + qi, 0)),
        out_shape=jax.ShapeDtypeStruct((M, n_heads * dv), CDT),
        scratch_shapes=[pltpu.VMEM((n_heads, tq, LANE), F32),
                        pltpu.VMEM((n_heads, tq, LANE), F32),
                        pltpu.VMEM((n_heads, tq, dv), F32),
                        pltpu.VMEM((2, tq, tk), F32),
                        pltpu.VMEM((2, tq, tk), CDT),
                        pltpu.VMEM((2, tq, LANE), F32),
                        pltpu.VMEM((2, tq, LANE), F32)],
        compiler_params=_cparams(("parallel", "parallel", "arbitrary")),
        name="mla_flash",
    )(qcat, kcat)


def _mla_decode_kernel(pt_ref, q_ref, kn_ref, *refs, n_pages_step, n_seq, n_heads, t_new, dv, rope, scale):
    G = n_pages_step
    ckv_refs = refs[:n_seq * G]
    kpe_refs = refs[n_seq * G:2 * n_seq * G]
    o_ref, m_ref, l_ref, acc_ref = refs[2 * n_seq * G:]
    g = pl.program_id(1)

    @pl.when(g == 0)
    def _():
        m_ref[...] = jnp.full(m_ref.shape, NEG_INF, F32)
        l_ref[...] = jnp.zeros(l_ref.shape, F32)
        acc_ref[...] = jnp.zeros(acc_ref.shape, F32)

    def update(sb, s, v):
        m_prev = m_ref[sb]
        m_new = jnp.maximum(m_prev, jnp.max(s, axis=-1, keepdims=True))
        alpha = jnp.exp(m_prev - m_new)
        p = jnp.exp(s - m_new)
        l_ref[sb] = alpha * l_ref[sb] + jnp.sum(p, axis=-1, keepdims=True)
        acc_ref[sb] = alpha * acc_ref[sb] + _dot(p, v)
        m_ref[sb] = m_new

    for sb in range(n_seq):
        q = q_ref[sb]
        ckv = jnp.concatenate([r[0, 0] for r in ckv_refs[sb * G:(sb + 1) * G]], axis=0).astype(CDT)
        kpe_t = jnp.concatenate([r[0, 0] for r in kpe_refs[sb * G:(sb + 1) * G]], axis=1).astype(CDT)
        s = (_dot_nt(q[:, :dv], ckv) + _dot(q[:, dv:dv + rope], kpe_t)) * scale
        update(sb, s, ckv)

    @pl.when(g == pl.num_programs(1) - 1)
    def _():
        for sb in range(n_seq):
            q = q_ref[sb]
            kn = kn_ref[sb]
            s2 = _dot_nt(q, kn) * scale
            r, c = s2.shape
            t_row = lax.broadcasted_iota(jnp.int32, (r, c), 0) // n_heads
            col = lax.broadcasted_iota(jnp.int32, (r, c), 1)
            s2 = jnp.where((col <= t_row) & (col < t_new), s2, NEG_INF)
            update(sb, s2, kn[:, :dv])
            o_ref[sb] = (acc_ref[sb] / l_ref[sb]).astype(o_ref.dtype)


def _mla_decode(qcat, kcat, cache_ckv, cache_kpe_t, layer, page_table, B, Ts, n_heads, dv, rope, scale):
    hd = kcat.shape[1]
    n_pages = page_table.shape[1]
    page = cache_ckv.shape[2]
    SB = math.gcd(B, SEQS_PER_STEP_MLA)
    G = math.gcd(n_pages, PAGES_PER_STEP_MLA // SB)
    NG = n_pages // G
    R = Ts * n_heads
    q3 = qcat.reshape(B, R, hd)
    npad = 16
    kn = jnp.pad(kcat.reshape(B, Ts, hd), ((0, 0), (0, npad - Ts), (0, 0)))
    kern = functools.partial(_mla_decode_kernel, n_pages_step=G, n_seq=SB, n_heads=n_heads, t_new=Ts, dv=dv,
                             rope=rope, scale=scale)

    def page_map(sb, i):
        return lambda b, g, pt: (layer, pt[b * SB + sb, g * G + i], 0, 0)

    in_specs = [pl.BlockSpec((SB, R, hd), lambda b, g, pt: (b, 0, 0)),
                pl.BlockSpec((SB, npad, hd), lambda b, g, pt: (b, 0, 0))]
    in_specs += [pl.BlockSpec((1, 1, page, dv), page_map(sb, i)) for sb in range(SB) for i in range(G)]
    in_specs += [pl.BlockSpec((1, 1, rope, page), page_map(sb, i)) for sb in range(SB) for i in range(G)]
    out = pl.pallas_call(
        kern,
        grid_spec=pltpu.PrefetchScalarGridSpec(
            num_scalar_prefetch=1,
            grid=(B // SB, NG),
            in_specs=in_specs,
            out_specs=pl.BlockSpec((SB, R, dv), lambda b, g, pt: (b, 0, 0)),
            scratch_shapes=[pltpu.VMEM((SB, R, 1), F32), pltpu.VMEM((SB, R, 1), F32),
                            pltpu.VMEM((SB, R, dv), F32)]),
        out_shape=jax.ShapeDtypeStruct((B, R, dv), CDT),
        compiler_params=_cparams(("parallel", "arbitrary")),
        name="mla_decode",
    )(page_table, q3, kn, *([cache_ckv] * (SB * G)), *([cache_kpe_t] * (SB * G)))
    return out.reshape(B * Ts, n_heads * dv)


def _mla_out_kernel(o_ref, wuv_ref, wo_ref, x_ref, gt_ref, y_ref, *, n_heads, dv):
    parts = [_dot(o_ref[:, h * dv:(h + 1) * dv], wuv_ref[h]).astype(CDT) for h in range(n_heads)]
    o = jnp.concatenate(parts, axis=1)
    y = jnp.dot(o, wo_ref[...], preferred_element_type=F32)
    y_ref[...] = x_ref[...] + gt_ref[0] * y


def _mla_out(o_lat, wuv, wo, x, gt, rows_per_group, n_heads, dv):
    M, D = x.shape
    tm = _tile(M, 512, 8)
    tpg = max(rows_per_group // tm, 1)
    kern = functools.partial(_mla_out_kernel, n_heads=n_heads, dv=dv)
    return pl.pallas_call(
        kern,
        grid=(M // tm,),
        in_specs=[pl.BlockSpec((tm, n_heads * dv), lambda i: (i, 0)),
                  pl.BlockSpec(wuv.shape, lambda i: (0, 0, 0)),
                  pl.BlockSpec(wo.shape, lambda i: (0, 0)),
                  pl.BlockSpec((tm, D), lambda i: (i, 0)),
                  pl.BlockSpec((1, gt.shape[1], D), lambda i: (i // tpg, 0, 0))],
        out_specs=pl.BlockSpec((tm, D), lambda i: (i, 0)),
        out_shape=jax.ShapeDtypeStruct((M, D), F32),
        compiler_params=_cparams(("parallel",)),
        name="mla_out",
    )(o_lat, wuv, wo, x, gt)


def _proj_res_kernel(o_ref, wo_ref, x_ref, gt_ref, y_ref):
    y = jnp.dot(o_ref[...], wo_ref[...], preferred_element_type=F32)
    y_ref[...] = x_ref[...] + gt_ref[0] * y


def _proj_res(o, wo, x, gt, rows_per_group):
    M, D = x.shape
    K = o.shape[1]
    tm = _tile(M, 512, 8)
    tpg = max(rows_per_group // tm, 1)
    return pl.pallas_call(
        _proj_res_kernel,
        grid=(M // tm,),
        in_specs=[pl.BlockSpec((tm, K), lambda i: (i, 0)),
                  pl.BlockSpec(wo.shape, lambda i: (0, 0)),
                  pl.BlockSpec((tm, D), lambda i: (i, 0)),
                  pl.BlockSpec((1, gt.shape[1], D), lambda i: (i // tpg, 0, 0))],
        out_specs=pl.BlockSpec((tm, D), lambda i: (i, 0)),
        out_shape=jax.ShapeDtypeStruct((M, D), F32),
        compiler_params=_cparams(("parallel",)),
        name="proj_res",
    )(o, wo, x, gt)


def _gla_gate_kernel(a_ref, w_ref, b_ref, o_ref):
    z = _dot(a_ref[...], w_ref[...]) + b_ref[...]
    o_ref[...] = (jnp.minimum(z, 0.0) - jnp.log(1.0 + jnp.exp(-jnp.abs(z)))) / GLA_TAU


def _gla_gate(proj, col_block, w_a2p, b_a2):
    M = proj.shape[0]
    N = w_a2p.shape[1]
    tm = _tile(M, 1024, 8)
    return pl.pallas_call(
        _gla_gate_kernel,
        grid=(M // tm,),
        in_specs=[pl.BlockSpec((tm, LANE), lambda i: (i, col_block)),
                  pl.BlockSpec(w_a2p.shape, lambda i: (0, 0)),
                  pl.BlockSpec((1, N), lambda i: (0, 0))],
        out_specs=pl.BlockSpec((tm, N), lambda i: (i, 0)),
        out_shape=jax.ShapeDtypeStruct((M, N), F32),
        compiler_params=_cparams(("parallel",)),
        name="gla_gate",
    )(proj, w_a2p, b_a2.reshape(1, N))


def _cumsum_rows(x):
    C = x.shape[0]
    row = lax.broadcasted_iota(jnp.int32, x.shape, 0)
    if C <= 8:
        out = jnp.zeros_like(x)
        for s in range(C):
            out = out + jnp.where(row >= s, x[s:s + 1], 0.0)
        return out
    sh = 1
    while sh < C:
        x = x + jnp.where(row >= sh, pltpu.roll(x, sh, 0), 0.0)
        sh *= 2
    return x


def _gla_kernel(q_ref, k_ref, v_ref, la_ref, s0_ref, o_ref, sf_ref, st_ref, *, n_heads, dk, dv, sub, qscale):
    c = pl.program_id(1)

    @pl.when(c == 0)
    def _():
        for h in range(n_heads):
            st_ref[h] = s0_ref[0, h].T

    C = q_ref.shape[1]
    nsub = C // sub
    for h in range(n_heads):
        q = q_ref[0, :, h * dk:(h + 1) * dk] * qscale
        k = k_ref[0, :, h * dk:(h + 1) * dk]
        v = v_ref[0, :, h * dv:(h + 1) * dv]
        b = _cumsum_rows(la_ref[0, :, h * dk:(h + 1) * dk])
        st = st_ref[h]
        o_inter = _dot_nt(q * jnp.exp(b), st)
        b_last = b[C - 1:C]
        k_dec = k * jnp.exp(b_last - b)
        st_ref[h] = jnp.exp(b_last) * st + _dot_tn(v, k_dec)
        outs = []
        for i in range(nsub):
            r0 = i * sub
            b_i = b[r0:r0 + sub]
            q_i = q[r0:r0 + sub]
            k_i = k[r0:r0 + sub]
            v_i = v[r0:r0 + sub]
            o_i = o_inter[r0:r0 + sub]
            if i > 0:
                ref_row = b[r0:r0 + 1]
                att = _dot_nt(q_i * jnp.exp(b_i - ref_row), k[:r0] * jnp.exp(ref_row - b[:r0]))
                o_i = o_i + _dot(att, v[:r0])
            t_loc = lax.broadcasted_iota(jnp.int32, (sub, 1), 0)
            for s in range(sub):
                w = jnp.exp(jnp.minimum(b_i - b_i[s:s + 1], 0.0))
                col = jnp.sum(q_i * w * k_i[s:s + 1], axis=-1, keepdims=True)
                o_i = o_i + jnp.where(t_loc >= s, col, 0.0) * v_i[s:s + 1]
            outs.append(o_i)
        o_ref[0, :, h * dv:(h + 1) * dv] = outs[0] if nsub == 1 else jnp.concatenate(outs, axis=0)

    @pl.when(c == pl.num_programs(1) - 1)
    def _():
        for h in range(n_heads):
            sf_ref[0, h] = st_ref[h].T


def _gla_recurrence(proj, log_a, s0, B, T, n_heads, dk, dv):
    C = 64 if T % 64 == 0 else T
    sub = min(16, C)
    nc = T // C
    Np = proj.shape[1]
    p3 = proj.reshape(B * nc, C, Np)
    la3 = log_a.reshape(B * nc, C, n_heads * dk)
    hk, hv = n_heads * dk, n_heads * dv
    assert hv % hk == 0
    kern = functools.partial(_gla_kernel, n_heads=n_heads, dk=dk, dv=dv, sub=sub, qscale=dk ** -0.5)
    o, sf = pl.pallas_call(
        kern,
        grid=(B, nc),
        in_specs=[pl.BlockSpec((1, C, hk), lambda b, c: (b * nc + c, 0, 0)),
                  pl.BlockSpec((1, C, hk), lambda b, c: (b * nc + c, 0, 1)),
                  pl.BlockSpec((1, C, hv), lambda b, c: (b * nc + c, 0, 2 * hk // hv)),
                  pl.BlockSpec((1, C, hk), lambda b, c: (b * nc + c, 0, 0)),
                  pl.BlockSpec((1, n_heads, dk, dv), lambda b, c: (b, 0, 0, 0))],
        out_specs=[pl.BlockSpec((1, C, hv), lambda b, c: (b * nc + c, 0, 0)),
                   pl.BlockSpec((1, n_heads, dk, dv), lambda b, c: (b, 0, 0, 0))],
        out_shape=[jax.ShapeDtypeStruct((B * nc, C, hv), F32),
                   jax.ShapeDtypeStruct((B, n_heads, dk, dv), F32)],
        scratch_shapes=[pltpu.VMEM((n_heads, dv, dk), F32)],
        compiler_params=_cparams(("parallel", "arbitrary")),
        name="gla_recurrence",
    )(p3, p3, p3, la3, s0)
    return o.reshape(B * T, hv), sf


def _gla_out_kernel(o_ref, r_ref, g_ref, wo_ref, x_ref, gt_ref, y_ref, *, n_heads, dv):
    parts = []
    for h in range(n_heads):
        sl = slice(h * dv, (h + 1) * dv)
        parts.append((_rms(o_ref[:, sl], g_ref[...]) * _silu(r_ref[:, sl])).astype(CDT))
    y = jnp.dot(jnp.concatenate(parts, axis=1), wo_ref[...], preferred_element_type=F32)
    y_ref[...] = x_ref[...] + gt_ref[0] * y


def _gla_out(o, proj, r_block, g_o, wo, x, gt, rows_per_group, n_heads, dv):
    M, D = x.shape
    hv = n_heads * dv
    tm = _tile(M, 512, 8)
    tpg = max(rows_per_group // tm, 1)
    kern = functools.partial(_gla_out_kernel, n_heads=n_heads, dv=dv)
    return pl.pallas_call(
        kern,
        grid=(M // tm,),
        in_specs=[pl.BlockSpec((tm, hv), lambda i: (i, 0)),
                  pl.BlockSpec((tm, hv), lambda i: (i, r_block)),
                  pl.BlockSpec((1, dv), lambda i: (0, 0)),
                  pl.BlockSpec(wo.shape, lambda i: (0, 0)),
                  pl.BlockSpec((tm, D), lambda i: (i, 0)),
                  pl.BlockSpec((1, gt.shape[1], D), lambda i: (i // tpg, 0, 0))],
        out_specs=pl.BlockSpec((tm, D), lambda i: (i, 0)),
        out_shape=jax.ShapeDtypeStruct((M, D), F32),
        compiler_params=_cparams(("parallel",)),
        name="gla_out",
    )(o, proj, g_o.reshape(1, dv), wo, x, gt)


def _order_key(score):
    score = jnp.where(score == 0.0, 0.0, score)
    bits = pltpu.bitcast(score, jnp.int32)
    return jnp.where(bits < 0, bits ^ jnp.int32(0x7FFFFFFF), bits)


def _sum_rows(x, chains=8):
    n = x.shape[0]
    if n % (8 * chains):
        return jnp.sum(x, axis=0, keepdims=True)
    part = jnp.sum(x.reshape(chains, n // chains, *x.shape[1:]), axis=1)
    return jnp.sum(part, axis=0, keepdims=True)


def _kth_largest_key(key, topk, axes, two_bit_steps):
    shape = tuple(1 if a in axes else s for a, s in enumerate(key.shape))

    def count(cand):
        hit = jnp.where(key >= cand, 1, 0)
        if axes == (0,):
            return _sum_rows(hit)
        if axes == (0, 2) and key.shape[2] % LANE == 0:
            parts = [hit[a, :, t * LANE:(t + 1) * LANE] for a in range(key.shape[0]) for t in range(key.shape[2] // LANE)]
            while len(parts) > 1:
                parts = [parts[i] + parts[i + 1] for i in range(0, len(parts) - 1, 2)] + parts[len(parts) & ~1:]
            return jnp.sum(parts[0], axis=-1, keepdims=True)[None]
        return jnp.sum(hit, axis=axes, keepdims=True)

    def body1(it, t):
        cand = t + lax.shift_left(jnp.int32(1), jnp.int32(31) - it)
        return jnp.where(count(cand) >= topk, cand, t)

    def body2(it, t):
        hi = lax.shift_left(jnp.int32(1), jnp.int32(31) - 2 * it)
        lo = lax.shift_left(jnp.int32(1), jnp.int32(30) - 2 * it)
        t01, t10, t11 = t + lo, t + hi, t + hi + lo
        c01, c10, c11 = count(t01), count(t10), count(t11)
        return jnp.where(c11 >= topk, t11, jnp.where(c10 >= topk, t10, jnp.where(c01 >= topk, t01, t)))

    init = jnp.full(shape, INT_MIN, jnp.int32)
    return lax.fori_loop(0, 16, body2, init) if two_bit_steps else lax.fori_loop(0, 32, body1, init)


def _bias_chain(dist, value_of_bucket):
    val = value_of_bucket(0)
    for j in range(1, N_BUCKETS):
        val = jnp.where(dist >= BUCKET_LO[j], value_of_bucket(j), val)
    return val


def _strict_upper(n):
    a = lax.broadcasted_iota(jnp.int32, (n, n), 0)
    b = lax.broadcasted_iota(jnp.int32, (n, n), 1)
    return jnp.where(a < b, 1.0, 0.0).astype(CDT)


def _strict_lower(n):
    a = lax.broadcasted_iota(jnp.int32, (n, n), 0)
    b = lax.broadcasted_iota(jnp.int32, (n, n), 1)
    return jnp.where(b < a, 1.0, 0.0).astype(CDT)


def _dsa_prompt_kernel(rb_ref, q_ref, qi_ref, wq_ref, k_ref, v_ref, kk_ref, o_ref, sel_ref, band_ref, selt_ref, *,
                       n_heads, n_kv, hd, n_idx, di, topk, tq, T, n_cls):
    b_id = pl.program_id(0)
    qt = pl.program_id(1)
    group = n_heads // n_kv

    @pl.when((b_id == 0) & (qt == 0))
    def _():
        i = lax.broadcasted_iota(jnp.int32, (tq, LANE), 0)
        j = lax.broadcasted_iota(jnp.int32, (tq, LANE), 1)
        for h in range(n_heads):
            far = rb_ref[N_BUCKETS - 1, h]
            for w in range(tq // LANE + 1):
                dist = i - j + (w - (tq // LANE - 1)) * LANE
                band_ref[w, h] = _bias_chain(dist, lambda bk: rb_ref[bk, h]) - far

    def body(nk):
        key_pos = lax.broadcasted_iota(jnp.int32, (nk, tq), 0)
        q_pos = qt * tq + lax.broadcasted_iota(jnp.int32, (nk, tq), 1)
        causal = key_pos <= q_pos

        ki = kk_ref[:nk, :di]
        wi_t = jnp.transpose(wq_ref[...])[di:di + n_idx] * (n_idx ** -0.5 * di ** -0.5)
        score = jnp.zeros((nk, tq), F32)
        for h in range(n_idx):
            lg = _dot_nt(ki, qi_ref[:, h * di:(h + 1) * di])
            score = score + wi_t[h:h + 1] * jnp.maximum(lg, 0.0)
        score = jnp.where(causal, score, NEG_INF)
        key = _order_key(score)
        thr = _kth_largest_key(key, topk, (0,), two_bit_steps=False)
        gt = key > thr
        eq = key == thr
        n_gt = _sum_rows(jnp.where(gt, 1, 0))
        n_eq = _sum_rows(jnp.where(eq, 1, 0))
        need = topk - n_gt
        tie = jnp.max(n_eq - need) > 0

        @pl.when(jnp.logical_not(tie))
        def _():
            selt_ref[:nk, :] = jnp.where((gt | eq) & causal, 0.0, NEG_INF)

        @pl.when(tie)
        def _():
            lower = _strict_lower(LANE)
            run = jnp.zeros((1, tq), F32)
            needf = need.astype(F32)
            for kb in range(nk // LANE):
                sl = slice(kb * LANE, (kb + 1) * LANE)
                eqb = jnp.where(eq[sl], 1.0, 0.0)
                pre = jnp.dot(lower, eqb.astype(CDT), preferred_element_type=F32) + run
                keep = gt[sl] | (eq[sl] & (pre < needf))
                selt_ref[sl, :] = jnp.where(keep & causal[sl], 0.0, NEG_INF)
                run = run + jnp.sum(eqb, axis=0, keepdims=True)

        for kb in range(nk // LANE):
            sel_ref[:, kb * LANE:(kb + 1) * LANE] = jnp.transpose(selt_ref[kb * LANE:(kb + 1) * LANE, :])

        scale = hd ** -0.5
        nband = tq // LANE + 1
        nkb = nk // LANE
        first_near = max(nkb - (T // n_cls) // LANE - tq // LANE, 0)
        for g in range(n_kv):
            kg = k_ref[:nk, g * hd:(g + 1) * hd]
            vg = v_ref[:nk, g * hd:(g + 1) * hd]
            for r in range(group):
                h = g * group + r
                s = _dot_nt(q_ref[:, h * hd:(h + 1) * hd], kg) * scale + rb_ref[N_BUCKETS - 1, h] + sel_ref[:, :nk]
                parts = [s[:, :first_near * LANE]] if first_near else []
                for kb in range(first_near, nkb):
                    w = qt * (tq // LANE) - kb + (tq // LANE - 1)
                    add = jnp.zeros((tq, LANE), F32)
                    for wv in range(nband):
                        add = jnp.where(w == wv, band_ref[wv, h], add)
                    parts.append(s[:, kb * LANE:(kb + 1) * LANE] + add)
                s = jnp.concatenate(parts, axis=1)
                m = jnp.max(s, axis=-1, keepdims=True)
                p = jnp.exp(s - m)
                l = jnp.sum(p, axis=-1, keepdims=True)
                o_ref[:, h * hd:(h + 1) * hd] = (_dot(p, vg) / l).astype(o_ref.dtype)

    width = T // n_cls
    cls = ((qt + 1) * tq - 1) // width
    for c in range(n_cls):
        pl.when(cls == c)(functools.partial(body, (c + 1) * width))


def _dsa_prompt(proj, proj_c, rel_bias, B, T, dims, topk):
    n_heads, n_kv, hd, n_idx, di = dims
    M, Np = proj.shape
    tq = math.gcd(T, DSA_TQ)
    assert tq % LANE == 0
    nqt = T // tq
    hq, hkv, hi = n_heads * hd, n_kv * hd, n_idx * di
    assert hq % hkv == 0 and (hq + 2 * hkv) % hi == 0 and (hq + 2 * hkv + hi) % LANE == 0
    kk_blk = (hq + 2 * hkv + hi) // LANE
    n_cls = math.gcd(T // LANE, DSA_EXTENT_CLASSES)
    kern = functools.partial(_dsa_prompt_kernel, n_heads=n_heads, n_kv=n_kv, hd=hd, n_idx=n_idx, di=di, topk=topk,
                             tq=tq, T=T, n_cls=n_cls)
    return pl.pallas_call(
        kern,
        grid=(B, nqt),
        in_specs=[pl.BlockSpec(memory_space=pltpu.SMEM),
                  pl.BlockSpec((tq, hq), lambda b, t: (b * nqt + t, 0)),
                  pl.BlockSpec((tq, hi), lambda b, t: (b * nqt + t, (hq + 2 * hkv) // hi)),
                  pl.BlockSpec((tq, LANE), lambda b, t: (b * nqt + t, kk_blk)),
                  pl.BlockSpec((T, hkv), lambda b, t: (b, hq // hkv)),
                  pl.BlockSpec((T, hkv), lambda b, t: (b, hq // hkv + 1)),
                  pl.BlockSpec((T, LANE), lambda b, t: (b, kk_blk))],
        out_specs=pl.BlockSpec((tq, hq), lambda b, t: (b * nqt + t, 0)),
        out_shape=jax.ShapeDtypeStruct((M, hq), CDT),
        scratch_shapes=[pltpu.VMEM((tq, T), F32),
                        pltpu.VMEM((tq // LANE + 1, n_heads, tq, LANE), F32),
                        pltpu.VMEM((T, tq), F32)],
        compiler_params=_cparams(("arbitrary", "arbitrary")),
        name="dsa_prompt",
    )(rel_bias, proj_c, proj_c, proj, proj_c, proj_c, proj_c)


def _dsa_select_kernel(pt_ref, qi_ref, wi_ref, kn_ref, *refs, n_pages_step, n_seq, n_idx, di, topk, t_new):
    G = n_pages_step
    ki_refs = refs[:n_seq * G]
    mask_ref, sc_ref = refs[n_seq * G:]
    g = pl.program_id(1)
    NG = pl.num_programs(1)
    R = qi_ref.shape[1] // n_idx
    RT = n_seq * R
    GW = G * ki_refs[0].shape[3]

    def head_sum(sb, lg):
        wi = wi_ref[sb] * (n_idx ** -0.5)
        w = wi * jnp.maximum(lg * (di ** -0.5), 0.0)
        sc = w[0:R]
        for h in range(1, n_idx):
            sc = sc + w[h * R:(h + 1) * R]
        return sc

    for sb in range(n_seq):
        keys_t = jnp.concatenate([r[0, 0] for r in ki_refs[sb * G:(sb + 1) * G]], axis=1)
        sc_ref[g, sb * R:(sb + 1) * R, :] = head_sum(sb, _dot(qi_ref[sb], keys_t))

    @pl.when(g == NG - 1)
    def _():
        for sb in range(n_seq):
            new = head_sum(sb, _dot_nt(qi_ref[sb], kn_ref[sb][:, :di]))
            t_row = lax.broadcasted_iota(jnp.int32, new.shape, 0) % t_new
            col = lax.broadcasted_iota(jnp.int32, new.shape, 1)
            new = jnp.where((col <= t_row) & (col < t_new), new, NEG_INF)
            sc_ref[NG, sb * R:(sb + 1) * R, :] = jnp.concatenate([new, jnp.full((R, GW - LANE), NEG_INF, F32)], axis=1)
        score = sc_ref[...]
        valid = score > NEG_INF
        key = _order_key(score)
        thr = _kth_largest_key(key, topk, (0, 2), two_bit_steps=True)
        gt = key > thr
        eq = key == thr
        n_gt = jnp.sum(jnp.where(gt, 1, 0), axis=(0, 2), keepdims=True)
        n_eq = jnp.sum(jnp.where(eq, 1, 0), axis=(0, 2), keepdims=True)
        need = topk - n_gt
        tie = jnp.max(n_eq - need) > 0

        @pl.when(jnp.logical_not(tie))
        def _():
            mask_ref[0] = jnp.where((gt | eq) & valid, 0.0, NEG_INF)

        @pl.when(tie)
        def _():
            upper = _strict_upper(LANE)
            needf = need[0].astype(F32)

            def blk(gi, run):
                k_g = _order_key(sc_ref[gi])
                v_g = sc_ref[gi] > NEG_INF
                for c in range(GW // LANE):
                    sl = slice(c * LANE, (c + 1) * LANE)
                    eqb = jnp.where(k_g[:, sl] == thr[0], 1.0, 0.0)
                    pre = jnp.dot(eqb.astype(CDT), upper, preferred_element_type=F32) + run
                    keep = (k_g[:, sl] > thr[0]) | ((eqb > 0.0) & (pre < needf))
                    mask_ref[0, gi, :, sl] = jnp.where(keep & v_g[:, sl], 0.0, NEG_INF)
                    run = run + jnp.sum(eqb, axis=1, keepdims=True)
                return run

            lax.fori_loop(0, NG + 1, blk, jnp.zeros((RT, 1), F32))


def _dsa_select(qi_hm, wi_hm, kn, cache_ki_t, layer, page_table, B, n_idx, di, topk, t_new):
    n_pages = page_table.shape[1]
    page = cache_ki_t.shape[3]
    SB = math.gcd(B, SEQS_PER_STEP_SELECT)
    G = math.gcd(n_pages, PAGES_PER_STEP_SELECT)
    NG = n_pages // G
    GW = G * page
    R = qi_hm.shape[1] // n_idx
    kern = functools.partial(_dsa_select_kernel, n_pages_step=G, n_seq=SB, n_idx=n_idx, di=di, topk=topk,
                             t_new=t_new)

    def page_map(sb, i):
        return lambda b, g, pt: (layer, pt[b * SB + sb, g * G + i], 0, 0)

    in_specs = [pl.BlockSpec((SB, n_idx * R, di), lambda b, g, pt: (b, 0, 0)),
                pl.BlockSpec((SB, n_idx * R, 1), lambda b, g, pt: (b, 0, 0)),
                pl.BlockSpec((SB, LANE, LANE), lambda b, g, pt: (b, 0, 0))]
    in_specs += [pl.BlockSpec((1, 1, di, page), page_map(sb, i)) for sb in range(SB) for i in range(G)]
    return pl.pallas_call(
        kern,
        grid_spec=pltpu.PrefetchScalarGridSpec(
            num_scalar_prefetch=1,
            grid=(B // SB, NG),
            in_specs=in_specs,
            out_specs=pl.BlockSpec((1, NG + 1, SB * R, GW), lambda b, g, pt: (b, 0, 0, 0)),
            scratch_shapes=[pltpu.VMEM((NG + 1, SB * R, GW), F32)]),
        out_shape=jax.ShapeDtypeStruct((B // SB, NG + 1, SB * R, GW), F32),
        compiler_params=_cparams(("parallel", "arbitrary")),
        name="dsa_select",
    )(page_table, qi_hm, wi_hm, kn, *([cache_ki_t] * (SB * G)))


def _dsa_decode_kernel(pt_ref, rb_ref, q_ref, kn_ref, vn_ref, mask_ref, mnew_ref, *refs,
                       n_pages_step, n_kv, group, hd, t_new, past_len):
    G = n_pages_step
    k_refs = refs[:G]
    v_refs = refs[G:2 * G]
    o_ref, m_ref, l_ref, acc_ref = refs[2 * G:]
    g = pl.program_id(1)
    NG = pl.num_programs(1)
    R = group * t_new
    scale = hd ** -0.5

    @pl.when(g == 0)
    def _():
        m_ref[...] = jnp.full(m_ref.shape, NEG_INF, F32)
        l_ref[...] = jnp.zeros(l_ref.shape, F32)
        acc_ref[...] = jnp.zeros(acc_ref.shape, F32)

    def head_val(kvh, bucket):
        row = lax.broadcasted_iota(jnp.int32, (R, 1), 0)
        val = jnp.full((R, 1), rb_ref[bucket, kvh * group], F32)
        for r in range(1, group):
            val = jnp.where(row >= r * t_new, rb_ref[bucket, kvh * group + r], val)
        return val

    def update(kvh, s, v):
        m_prev = m_ref[kvh]
        m_new = jnp.maximum(m_prev, jnp.max(s, axis=-1, keepdims=True))
        m_safe = jnp.where(m_new > NEG_INF, m_new, 0.0)
        alpha = jnp.exp(m_prev - m_safe)
        p = jnp.exp(s - m_safe)
        l_ref[kvh] = alpha * l_ref[kvh] + jnp.sum(p, axis=-1, keepdims=True)
        acc_ref[kvh] = alpha * acc_ref[kvh] + _dot(p, v)
        m_ref[kvh] = m_new

    page = k_refs[0].shape[2] // n_kv
    GW = G * page
    mask = mask_ref[0, 0]
    t_row = lax.broadcasted_iota(jnp.int32, (R, GW), 0) % t_new
    col = lax.broadcasted_iota(jnp.int32, (R, GW), 1)
    dist = past_len + t_row - (g * GW + col)
    near = g == NG - 1

    def head_rows(refs_, kvh):
        return jnp.concatenate([r[0, 0, pl.ds(kvh, page, stride=n_kv), :] for r in refs_], axis=0).astype(CDT)

    for kvh in range(n_kv):
        s = _dot_nt(q_ref[0, kvh], head_rows(k_refs, kvh)) * scale
        bias = lax.cond(near,
                        lambda: _bias_chain(dist, functools.partial(head_val, kvh)),
                        lambda: jnp.broadcast_to(head_val(kvh, N_BUCKETS - 1), (R, GW)))
        update(kvh, s + bias + mask, head_rows(v_refs, kvh))

    @pl.when(near)
    def _():
        kn = kn_ref[0].astype(CDT)
        vn = vn_ref[0].astype(CDT)
        mnew = mnew_ref[0, 0][:, :LANE]
        t_r = lax.broadcasted_iota(jnp.int32, (R, LANE), 0) % t_new
        c = lax.broadcasted_iota(jnp.int32, (R, LANE), 1)
        d_new = jnp.maximum(t_r - c, 0)
        for kvh in range(n_kv):
            sl = slice(kvh * hd, (kvh + 1) * hd)
            s = _dot_nt(q_ref[0, kvh], kn[:, sl]) * scale
            bias = _bias_chain(d_new, functools.partial(head_val, kvh))
            update(kvh, s + bias + mnew, vn[:, sl])
            o_ref[0, kvh] = (acc_ref[kvh] / l_ref[kvh]).astype(o_ref.dtype)


def _dsa_decode(q4, kn, vn, mask, rel_bias, cache_k, cache_v, layer, page_table, B, n_kv, group, hd, t_new):
    n_pages = page_table.shape[1]
    page = cache_k.shape[2] // n_kv
    GWs = mask.shape[3]
    G = math.gcd(math.gcd(n_pages, PAGES_PER_STEP_DECODE), GWs // page)
    NG = n_pages // G
    GW = G * page
    ratio = GWs // GW
    R = group * t_new
    SB = mask.shape[2] // R
    kern = functools.partial(_dsa_decode_kernel, n_pages_step=G, n_kv=n_kv, group=group, hd=hd, t_new=t_new,
                             past_len=n_pages * page)

    def page_map(i):
        return lambda b, g, pt: (layer, pt[b, g * G + i], 0, 0)

    in_specs = [pl.BlockSpec(memory_space=pltpu.SMEM),
                pl.BlockSpec((1, n_kv, R, hd), lambda b, g, pt: (b, 0, 0, 0)),
                pl.BlockSpec((1, LANE, n_kv * hd), lambda b, g, pt: (b, 0, 0)),
                pl.BlockSpec((1, LANE, n_kv * hd), lambda b, g, pt: (b, 0, 0)),
                pl.BlockSpec((1, 1, R, GW), lambda b, g, pt: (b // SB, g // ratio, b % SB, g % ratio)),
                pl.BlockSpec((1, 1, R, GW), lambda b, g, pt: (b // SB, mask.shape[1] - 1, b % SB, 0))]
    in_specs += [pl.BlockSpec((1, 1, page * n_kv, hd), page_map(i)) for i in range(G)]
    in_specs += [pl.BlockSpec((1, 1, page * n_kv, hd), page_map(i)) for i in range(G)]
    return pl.pallas_call(
        kern,
        grid_spec=pltpu.PrefetchScalarGridSpec(
            num_scalar_prefetch=1,
            grid=(B, NG),
            in_specs=in_specs,
            out_specs=pl.BlockSpec((1, n_kv, R, hd), lambda b, g, pt: (b, 0, 0, 0)),
            scratch_shapes=[pltpu.VMEM((n_kv, R, 1), F32), pltpu.VMEM((n_kv, R, 1), F32),
                            pltpu.VMEM((n_kv, R, hd), F32)]),
        out_shape=jax.ShapeDtypeStruct((B, n_kv, R, hd), CDT),
        compiler_params=_cparams(("parallel", "arbitrary")),
        name="dsa_decode",
    )(page_table, rel_bias, q4, kn, vn, mask, mask, *([cache_k] * G), *([cache_v] * G))


def _ffn_kernel(*refs, seq_tiles, t_seq, conv_w):
    if seq_tiles:
        (x_ref, g_ref, sc_ref, sh_ref, gt_ref, wg_ref, wv_ref, wd_ref, cw_ref, cb_ref,
         y_ref, tail_ref, h_ref, acc_ref, stash_ref) = refs
    else:
        (x_ref, g_ref, sc_ref, sh_ref, gt_ref, wg_ref, wv_ref, wd_ref, cw_ref, cb_ref, p1_ref, p2_ref,
         y_ref, tail_ref, h_ref, acc_ref) = refs
    i = pl.program_id(0)
    j = pl.program_id(1)

    @pl.when(j == 0)
    def _():
        h = _rms(x_ref[...], g_ref[...]) * (1.0 + sc_ref[0]) + sh_ref[0]
        h_ref[...] = h.astype(CDT)
        acc_ref[...] = jnp.zeros(acc_ref.shape, F32)

    if seq_tiles:
        @pl.when((i == 0) & (j == 0))
        def _():
            stash_ref[...] = jnp.zeros(stash_ref.shape, F32)

    gate = jnp.dot(h_ref[...], wg_ref[...], preferred_element_type=F32)
    val = jnp.dot(h_ref[...], wv_ref[...], preferred_element_type=F32)
    tm = gate.shape[0]
    row = lax.broadcasted_iota(jnp.int32, gate.shape, 0)
    g1 = pltpu.roll(gate, 1, 0)
    g2 = pltpu.roll(gate, 2, 0)
    if seq_tiles:
        prev = jnp.where(i % seq_tiles == 0, 0.0, stash_ref[j])
        g1 = jnp.where(row == 0, prev[7:8], g1)
        g2 = jnp.where(row == 0, prev[6:7], jnp.where(row == 1, prev[7:8], g2))
        stash_ref[j] = gate[tm - 8:]
        tail_ref[0] = gate[tm - 8:]
    else:
        t = row % t_seq
        g1 = jnp.where(t == 0, p1_ref[...], g1)
        g2 = jnp.where(t < 2, p2_ref[...], g2)
        tail_ref[...] = gate
    cw = cw_ref[...]
    conv = cw[0:1] * g2 + cw[1:2] * g1 + cw[2:3] * gate + cb_ref[...]
    act = (_silu(conv) * val).astype(CDT)
    acc_ref[...] += jnp.dot(act, wd_ref[...], preferred_element_type=F32)

    @pl.when(j == pl.num_programs(1) - 1)
    def _():
        y_ref[...] = x_ref[...] + gt_ref[0] * acc_ref[...]


def _ffn(x, g, sc, sh, gt, w_up, w_down, conv_w, conv_b, rows_per_group, t_seq, prev=None):
    M, D = x.shape
    Fd = w_down.shape[0]
    assert conv_w.shape[0] == 3
    tf = _tile(Fd, FFN_TILE_F)
    nf = Fd // tf
    cw = jnp.pad(conv_w, ((0, 8 - conv_w.shape[0]), (0, 0)))
    cb = conv_b.reshape(1, Fd)
    seq_mode = prev is None
    if seq_mode:
        tm = _tile(t_seq, 512, 8)
        assert t_seq % tm == 0 and tm >= 8
        seq_tiles = t_seq // tm
    else:
        tm = M
        seq_tiles = 0
    nt = M // tm
    tpg = max(rows_per_group // tm, 1)
    mod = lambda m: pl.BlockSpec((1, m.shape[1], D), lambda i, j: (i // tpg, 0, 0))
    in_specs = [pl.BlockSpec((tm, D), lambda i, j: (i, 0)),
                pl.BlockSpec((1, D), lambda i, j: (0, 0)),
                mod(sc), mod(sh), mod(gt),
                pl.BlockSpec((D, tf), lambda i, j: (0, j)),
                pl.BlockSpec((D, tf), lambda i, j: (0, nf + j)),
                pl.BlockSpec((tf, D), lambda i, j: (j, 0)),
                pl.BlockSpec((8, tf), lambda i, j: (0, j)),
                pl.BlockSpec((1, tf), lambda i, j: (0, j))]
    args = [x, g.reshape(1, D), sc, sh, gt, w_up, w_up, w_down, cw, cb]
    scratch = [pltpu.VMEM((tm, D), CDT), pltpu.VMEM((tm, D), F32)]
    if seq_mode:
        tail_shape = jax.ShapeDtypeStruct((nt, 8, Fd), F32)
        tail_spec = pl.BlockSpec((1, 8, tf), lambda i, j: (i, 0, j))
        scratch.append(pltpu.VMEM((nf, 8, tf), F32))
    else:
        p1 = jnp.concatenate([prev[:, 1:2], jnp.zeros_like(prev[:, :1]).repeat(t_seq - 1, axis=1)], axis=1)
        p2 = jnp.concatenate([prev[:, 0:2], jnp.zeros_like(prev[:, :1]).repeat(t_seq - 2, axis=1)], axis=1)
        args += [p1.reshape(M, Fd), p2.reshape(M, Fd)]
        in_specs += [pl.BlockSpec((tm, tf), lambda i, j: (i, j))] * 2
        tail_shape = jax.ShapeDtypeStruct((M, Fd), F32)
        tail_spec = pl.BlockSpec((tm, tf), lambda i, j: (i, j))
    kern = functools.partial(_ffn_kernel, seq_tiles=seq_tiles, t_seq=t_seq, conv_w=conv_w.shape[0])
    return pl.pallas_call(
        kern,
        grid=(nt, nf),
        in_specs=in_specs,
        out_specs=[pl.BlockSpec((tm, D), lambda i, j: (i, 0)), tail_spec],
        out_shape=[jax.ShapeDtypeStruct((M, D), F32), tail_shape],
        scratch_shapes=scratch,
        compiler_params=_cparams(("arbitrary", "arbitrary")),
        name="ffn",
    )(*args)


def _final_norm_kernel(x_ref, g_ref, o_ref):
    o_ref[...] = _rms(x_ref[...], g_ref[...])


def _final_norm(x, g):
    M, D = x.shape
    tm = _tile(M, 1024, 8)
    return pl.pallas_call(
        _final_norm_kernel,
        grid=(M // tm,),
        in_specs=[pl.BlockSpec((tm, D), lambda i: (i, 0)), pl.BlockSpec((1, D), lambda i: (0, 0))],
        out_specs=pl.BlockSpec((tm, D), lambda i: (i, 0)),
        out_shape=jax.ShapeDtypeStruct((M, D), F32),
        compiler_params=_cparams(("parallel",)),
        name="final_norm",
    )(x, g.reshape(1, D))


def _rope_tables(pos, rope, n_heads, reps):
    half = rope // 2
    inv = ROPE_BASE ** (-jnp.arange(half, dtype=F32) / half)
    ang = pos.astype(F32)[:, None] * inv[None, :]
    cos = jnp.cos(ang)
    sin = jnp.sin(ang)
    cosk = jnp.concatenate([cos, cos], axis=1)
    sink = jnp.concatenate([-sin, sin], axis=1)
    tabs = (jnp.tile(cosk, (1, n_heads)), jnp.tile(sink, (1, n_heads)), cosk, sink)
    return tuple(jnp.tile(t, (reps, 1)) for t in tabs)


def _swap_halves(w, rope):
    half = rope // 2
    return jnp.concatenate([w[..., half:], w[..., :half]], axis=-1)


def _pad_cols(w, n):
    return jnp.pad(w, ((0, 0), (0, n - w.shape[1])))


def kernel(x_prompt, x_sample, cache_mla_ckv, cache_mla_kpe, state_gla, cache_dsa_k, cache_dsa_v, cache_dsa_kidx,
           state_ffn_conv, page_table, c_prompt, c_sample, ada_w, ada_b, norm1_g, norm2_g, final_g, mla_w_in,
           mla_g_q, mla_g_kv, mla_w_uq, mla_w_uk, mla_w_uv, mla_w_o, gla_w_in, gla_w_a2, gla_b_a2, gla_g_o,
           gla_w_o, dsa_w_in, dsa_w_o, rel_bias, ffn_w_up, ffn_conv_w, ffn_conv_b, ffn_w_down):
    Bp, Tp, D = x_prompt.shape
    Bs, Ts, _ = x_sample.shape
    depth = ada_w.shape[0]
    n_mod = ada_w.shape[2] // D
    n_pages, page = page_table.shape[1], cache_mla_ckv.shape[2]
    past_len = n_pages * page
    cache_kpe_t = jnp.swapaxes(cache_mla_kpe, 2, 3)
    cache_ki_t = jnp.swapaxes(cache_dsa_kidx, 2, 3)

    q_lora, mla_h, qk_dim = mla_w_uq.shape[1:]
    kv_lora, _, nope = mla_w_uk.shape[1:]
    mla_v = mla_w_uv.shape[3]
    rope = qk_dim - nope
    gla_h = state_gla.shape[2]
    gla_dk, gla_dv = state_gla.shape[3:]
    gla_rank = gla_w_a2.shape[1]
    dsa_kv, dsa_hd = cache_dsa_k.shape[3:]
    dsa_di = cache_dsa_kidx.shape[3]
    dsa_h = dsa_w_o.shape[1] // dsa_hd
    dsa_hi = (dsa_w_in.shape[2] - (dsa_h + 2 * dsa_kv) * dsa_hd - dsa_di) // (dsa_di + 1)
    assert kv_lora % LANE == 0 and q_lora % LANE == 0 and rope <= LANE

    mod_all = _ada_mod(jnp.concatenate([c_prompt, c_sample], axis=0), ada_w, ada_b)

    def mods(l, sample):
        m = mod_all[l, Bp:] if sample else mod_all[l, :Bp]
        parts = [m[:, i * D:(i + 1) * D] for i in range(n_mod)]
        if sample:
            return [jnp.repeat(p, Ts, axis=0).reshape(1, Bs * Ts, D) for p in parts]
        return [p.reshape(Bp, 1, D) for p in parts]

    mla_w = []
    for j in range(mla_w_in.shape[0]):
        w_in = mla_w_in[j]
        kcol = q_lora + kv_lora
        w_in_ext = jnp.concatenate([w_in, _swap_halves(w_in[:, kcol:kcol + rope], rope)], axis=1)
        w_in_ext = _pad_cols(w_in_ext, -(-w_in_ext.shape[1] // LANE) * LANE).astype(CDT)
        uq = mla_w_uq[j]
        wqn = uq[:, :, :nope].reshape(q_lora, mla_h * nope).astype(CDT)
        wqp = uq[:, :, nope:].reshape(q_lora, mla_h * rope).astype(CDT)
        wqs = _swap_halves(uq[:, :, nope:], rope).reshape(q_lora, mla_h * rope).astype(CDT)
        wukT = jnp.transpose(mla_w_uk[j], (1, 2, 0)).astype(CDT)
        wuv = jnp.transpose(mla_w_uv[j], (1, 0, 2)).astype(CDT)
        mla_w.append((w_in_ext, wqn, wqp, wqs, wukT, wuv, mla_w_o[j].astype(CDT)))
    gla_np = -(-gla_w_in.shape[2] // LANE) * LANE
    gla_w = []
    for j in range(gla_w_in.shape[0]):
        w_a2p = jnp.pad(gla_w_a2[j], ((0, LANE - gla_rank), (0, 0))).astype(CDT)
        gla_w.append((_pad_cols(gla_w_in[j], gla_np).astype(CDT), w_a2p, gla_w_o[j].astype(CDT)))
    dsa_np = -(-dsa_w_in.shape[2] // LANE) * LANE
    dsa_w = [(_pad_cols(dsa_w_in[j], dsa_np).astype(CDT), dsa_w_o[j].astype(CDT)) for j in range(dsa_w_in.shape[0])]
    ffn_up = ffn_w_up.astype(CDT)
    ffn_down = ffn_w_down.astype(CDT)

    mla_dims = (mla_h, q_lora, kv_lora, rope, nope)
    mla_scale = qk_dim ** -0.5

    def trunk(x3, sample):
        B, T, _ = x3.shape
        M = B * T
        x = x3.reshape(M, D)
        rpg = M if sample else T
        pos = (past_len if sample else 0) + jnp.arange(T, dtype=jnp.int32)
        tabs = _rope_tables(pos, rope, mla_h, B if sample else 1)
        outs = dict(mla_ckv=[], mla_kpe=[], gla=[], dsa_k=[], dsa_v=[], dsa_ki=[], conv=[])
        for l in range(depth):
            sh1, sc1, gt1, sh2, sc2, gt2 = mods(l, sample)
            j = l // 3
            if l % 3 == 0:
                w_in_ext, wqn, wqp, wqs, wukT, wuv, wo = mla_w[j]
                p = _nm_linear(x, norm1_g[l], sc1, sh1, w_in_ext, rpg)
                ckv, kpe, kcat, qcat = _mla_prep(p, mla_g_q[j], mla_g_kv[j], tabs, wqn, wqp, wqs, wukT, mla_dims)
                if sample:
                    o_lat = _mla_decode(qcat, kcat, cache_mla_ckv, cache_kpe_t, j, page_table, B, T, mla_h,
                                        kv_lora, rope, mla_scale)
                else:
                    o_lat = _mla_flash(qcat, kcat, B, T, mla_h, kv_lora, mla_scale)
                x = _mla_out(o_lat, wuv, wo, x, gt1, rpg, mla_h, kv_lora)
                outs["mla_ckv"].append(ckv.reshape(B, T, kv_lora))
                outs["mla_kpe"].append(kpe.reshape(B, T, rope))
            elif l % 3 == 1:
                w_in_p, w_a2p, wo = gla_w[j]
                hk, hv = gla_h * gla_dk, gla_h * gla_dv
                p = _nm_linear(x, norm1_g[l], sc1, sh1, w_in_p, rpg, tn_target=NM_LINEAR_TN)
                log_a = _gla_gate(p, (2 * hk + 2 * hv) // LANE, w_a2p, gla_b_a2[j])
                s0 = state_gla[j] if sample else jnp.zeros((B, gla_h, gla_dk, gla_dv), F32)
                o, s_fin = _gla_recurrence(p, log_a, s0, B, T, gla_h, gla_dk, gla_dv)
                x = _gla_out(o, p, (2 * hk + hv) // hv, gla_g_o[j], wo, x, gt1, rpg, gla_h, gla_dv)
                outs["gla"].append(s_fin)
            else:
                w_in_p, wo = dsa_w[j]
                hq, hkv, hi = dsa_h * dsa_hd, dsa_kv * dsa_hd, dsa_hi * dsa_di
                p = _nm_linear(x, norm1_g[l], sc1, sh1, w_in_p, rpg, tn_target=NM_LINEAR_TN, with_cdt_copy=not sample)
                if not sample:
                    p, p_c = p
                k_new = p[:, hq:hq + hkv]
                v_new = p[:, hq + hkv:hq + 2 * hkv]
                kk = p[:, hq + 2 * hkv + hi:]
                L_keys = (past_len if sample else 0) + T
                topk = min(DSA_TOPK, L_keys // 4)
                if sample:
                    group = dsa_h // dsa_kv
                    dup = lambda a: jnp.concatenate([a.reshape(B, T, dsa_hi, -1)] * group, axis=1)
                    qi_hm = dup(p[:, hq + 2 * hkv:hq + 2 * hkv + hi]).transpose(0, 2, 1, 3)
                    qi_hm = qi_hm.reshape(B, dsa_hi * group * T, dsa_di)
                    wi_hm = dup(kk[:, dsa_di:dsa_di + dsa_hi]).transpose(0, 2, 1, 3).reshape(B, dsa_hi * group * T, 1)
                    padn = lambda a: jnp.pad(a.reshape(B, T, -1), ((0, 0), (0, LANE - T), (0, 0)))
                    mask = _dsa_select(qi_hm, wi_hm, padn(kk), cache_ki_t, j, page_table, B, dsa_hi, dsa_di, topk, T)
                    q4 = p[:, :hq].reshape(B, T, dsa_kv, group, dsa_hd).transpose(0, 2, 3, 1, 4)
                    q4 = q4.reshape(B, dsa_kv, group * T, dsa_hd)
                    ck = cache_dsa_k.reshape(*cache_dsa_k.shape[:2], page * dsa_kv, dsa_hd)
                    cv = cache_dsa_v.reshape(*cache_dsa_v.shape[:2], page * dsa_kv, dsa_hd)
                    o4 = _dsa_decode(q4, padn(k_new), padn(v_new), mask, rel_bias, ck, cv, j, page_table, B,
                                     dsa_kv, group, dsa_hd, T)
                    o = o4.reshape(B, dsa_kv, group, T, dsa_hd).transpose(0, 3, 1, 2, 4).reshape(M, hq)
                else:
                    o = _dsa_prompt(p, p_c, rel_bias, B, T, (dsa_h, dsa_kv, dsa_hd, dsa_hi, dsa_di), topk)
                x = _proj_res(o, wo, x, gt1, rpg)
                outs["dsa_k"].append(k_new.reshape(B, T, dsa_kv, dsa_hd))
                outs["dsa_v"].append(v_new.reshape(B, T, dsa_kv, dsa_hd))
                outs["dsa_ki"].append(kk[:, :dsa_di].reshape(B, T, dsa_di))
            if sample:
                x, tail = _ffn(x, norm2_g[l], sc2, sh2, gt2, ffn_up[l], ffn_down[l], ffn_conv_w[l], ffn_conv_b[l],
                               rpg, T, prev=state_ffn_conv[l])
                outs["conv"].append(tail.reshape(B, T, -1)[:, T - 2:])
            else:
                x, tail = _ffn(x, norm2_g[l], sc2, sh2, gt2, ffn_up[l], ffn_down[l], ffn_conv_w[l], ffn_conv_b[l],
                               rpg, T)
                nt = tail.shape[0] // B
                outs["conv"].append(tail.reshape(B, nt, 8, -1)[:, nt - 1, 6:8])
        y = _final_norm(x, final_g).reshape(B, T, D)
        return (y, jnp.stack(outs["mla_ckv"]), jnp.stack(outs["mla_kpe"]), jnp.stack(outs["gla"]),
                jnp.stack(outs["dsa_k"]), jnp.stack(outs["dsa_v"]), jnp.stack(outs["dsa_ki"]),
                jnp.stack(outs["conv"]))

    rp = trunk(x_prompt, False)
    rs = trunk(x_sample, True)
    return (rp[0], rs[0]) + tuple(rp[1:]) + tuple(rs[1:])
```

```python
import functools
import math

import numpy as np
import jax
import jax.numpy as jnp
from jax import lax
from jax.experimental import pallas as pl
from jax.experimental.pallas import tpu as pltpu

F32 = jnp.float32
CDT = jnp.bfloat16
EPS = 1e-6
ROPE_BASE = 10000.0
GLA_TAU = 16.0
N_BUCKETS = 32
MAX_DISTANCE = 128
DSA_TOPK = 256
LANE = 128
NEG_INF = float("-inf")
INT_MIN = -2 ** 31
VMEM_LIMIT = 56 * 1024 * 1024
PAGES_PER_STEP_MLA = 64
SEQS_PER_STEP_MLA = 2
SEQS_PER_STEP_SELECT = 2
PAGES_PER_STEP_SELECT = 64
PAGES_PER_STEP_DECODE = 32
FFN_TILE_F = 1408
DSA_EXTENT_CLASSES = 4
DSA_TQ = 128
NM_LINEAR_TN = 4096
FLASH_TQ = 512
FLASH_TK = 512
FLASH_ROW_CHUNK = 64


def _bucket_thresholds():
    d = np.arange(0, 4 * MAX_DISTANCE)
    exact = N_BUCKETS // 2
    lr = np.log(np.maximum(d, 1).astype(np.float32) / np.float32(exact)) / np.float32(math.log(MAX_DISTANCE / exact))
    large = np.minimum(exact + (lr * np.float32(N_BUCKETS - exact)).astype(np.int32), N_BUCKETS - 1)
    b = np.where(d < exact, d, large)
    return [int(np.argmax(b >= j)) for j in range(N_BUCKETS)]


BUCKET_LO = _bucket_thresholds()


def _cparams(sem, vmem=VMEM_LIMIT):
    return pltpu.CompilerParams(dimension_semantics=sem, vmem_limit_bytes=vmem)


def _tile(n, target, mult=LANE):
    if n <= target:
        return n
    t = (target // mult) * mult
    while t > mult and n % t:
        t -= mult
    assert n % t == 0, (n, target)
    return t


def _dot(a, b):
    return jnp.dot(a.astype(CDT), b.astype(CDT), preferred_element_type=F32)


def _dot_nt(a, b):
    return lax.dot_general(a.astype(CDT), b.astype(CDT), (((1,), (1,)), ((), ())), preferred_element_type=F32)


def _dot_tn(a, b):
    return lax.dot_general(a.astype(CDT), b.astype(CDT), (((0,), (0,)), ((), ())), preferred_element_type=F32)


def _rms(x, g):
    return x * lax.rsqrt(jnp.mean(x * x, axis=-1, keepdims=True) + EPS) * g


def _silu(x):
    return x * jax.nn.sigmoid(x)


def _ada_kernel(c_ref, w_ref, b_ref, o_ref):
    ca = _silu(c_ref[...])
    o_ref[0] = _dot(ca, w_ref[0]) + b_ref[0]


def _ada_mod(c, ada_w, ada_b):
    L, D, N = ada_w.shape
    R = c.shape[0]
    tn = _tile(N, 1536)
    return pl.pallas_call(
        _ada_kernel,
        grid=(L, N // tn),
        in_specs=[pl.BlockSpec((R, D), lambda l, j: (0, 0)),
                  pl.BlockSpec((1, D, tn), lambda l, j: (l, 0, j)),
                  pl.BlockSpec((1, 1, tn), lambda l, j: (l, 0, j))],
        out_specs=pl.BlockSpec((1, R, tn), lambda l, j: (l, 0, j)),
        out_shape=jax.ShapeDtypeStruct((L, R, N), F32),
        compiler_params=_cparams(("parallel", "parallel")),
        name="ada_mod",
    )(c, ada_w, ada_b.reshape(L, 1, N))


def _nm_linear_kernel(x_ref, g_ref, sc_ref, sh_ref, w_ref, o_ref, *rest):
    h_ref = rest[-1]

    @pl.when(pl.program_id(1) == 0)
    def _():
        h = _rms(x_ref[...], g_ref[...]) * (1.0 + sc_ref[0]) + sh_ref[0]
        h_ref[...] = h.astype(CDT)

    o = jnp.dot(h_ref[...], w_ref[...], preferred_element_type=F32)
    o_ref[...] = o
    if len(rest) == 2:
        rest[0][...] = o.astype(CDT)


def _mod_spec(mod, tm, rows_per_group):
    G, R, D = mod.shape
    tpg = max(rows_per_group // tm, 1)
    return pl.BlockSpec((1, R, D), lambda i, j: (i // tpg, 0, 0))


def _nm_linear(x, g, sc, sh, w, rows_per_group, tm_target=512, tn_target=1024, with_cdt_copy=False):
    M, D = x.shape
    N = w.shape[1]
    tm = _tile(M, tm_target, 8)
    tn = _tile(N, tn_target)
    out_spec = pl.BlockSpec((tm, tn), lambda i, j: (i, j))
    out_shape = jax.ShapeDtypeStruct((M, N), F32)
    if with_cdt_copy:
        out_spec = [out_spec, out_spec]
        out_shape = [out_shape, jax.ShapeDtypeStruct((M, N), CDT)]
    return pl.pallas_call(
        _nm_linear_kernel,
        grid=(M // tm, N // tn),
        in_specs=[pl.BlockSpec((tm, D), lambda i, j: (i, 0)),
                  pl.BlockSpec((1, D), lambda i, j: (0, 0)),
                  _mod_spec(sc, tm, rows_per_group),
                  _mod_spec(sh, tm, rows_per_group),
                  pl.BlockSpec((D, tn), lambda i, j: (0, j))],
        out_specs=out_spec,
        out_shape=out_shape,
        scratch_shapes=[pltpu.VMEM((tm, D), CDT)],
        compiler_params=_cparams(("parallel", "arbitrary")),
        name="nm_linear",
    )(x, g.reshape(1, D), sc, sh, w)


def _mla_prep_kernel(p_ref, gq_ref, gkv_ref, cq_ref, sq_ref, ck_ref, sk_ref, wqn_ref, wqp_ref, wqs_ref, wuk_ref,
                     ckv_ref, kpe_ref, kcat_ref, qcat_ref, *, n_heads, q_lora, kv_lora, rope, nope):
    p = p_ref[...]
    cq = p[:, :q_lora]
    ckv = p[:, q_lora:q_lora + kv_lora]
    kpe = p[:, q_lora + kv_lora:q_lora + kv_lora + rope]
    kpe_sw = p[:, q_lora + kv_lora + rope:q_lora + kv_lora + 2 * rope]
    cqn = _rms(cq, gq_ref[...]).astype(CDT)
    q_nope = jnp.dot(cqn, wqn_ref[...], preferred_element_type=F32)
    q_pe = (jnp.dot(cqn, wqp_ref[...], preferred_element_type=F32) * cq_ref[...]
            + jnp.dot(cqn, wqs_ref[...], preferred_element_type=F32) * sq_ref[...])
    ckv_n = _rms(ckv, gkv_ref[...])
    kpe_r = kpe * ck_ref[...] + kpe_sw * sk_ref[...]
    ckv_ref[...] = ckv_n
    kpe_ref[...] = kpe_r
    tm = p.shape[0]
    hd = kv_lora + LANE
    pad = jnp.zeros((tm, LANE - rope), CDT)
    kcat_ref[...] = jnp.concatenate([ckv_n.astype(CDT), kpe_r.astype(CDT), pad], axis=1)
    for h in range(n_heads):
        q_lat = _dot(q_nope[:, h * nope:(h + 1) * nope], wuk_ref[h])
        qcat_ref[:, h * hd:(h + 1) * hd] = jnp.concatenate(
            [q_lat.astype(CDT), q_pe[:, h * rope:(h + 1) * rope].astype(CDT), pad], axis=1)


def _mla_prep(p, g_q, g_kv, tabs, wqn, wqp, wqs, wukT, dims):
    n_heads, q_lora, kv_lora, rope, nope = dims
    M, NP = p.shape
    cosq, sinq, cosk, sink = tabs
    tm = _tile(M, 256, 8)
    ntab = cosq.shape[0] // tm
    hd = kv_lora + LANE
    row = lambda i: (i, 0)
    tab = lambda i: (i % ntab, 0)
    full2 = lambda i: (0, 0)
    full3 = lambda i: (0, 0, 0)
    kern = functools.partial(_mla_prep_kernel, n_heads=n_heads, q_lora=q_lora, kv_lora=kv_lora, rope=rope, nope=nope)
    return pl.pallas_call(
        kern,
        grid=(M // tm,),
        in_specs=[pl.BlockSpec((tm, NP), row),
                  pl.BlockSpec((1, q_lora), full2),
                  pl.BlockSpec((1, kv_lora), full2),
                  pl.BlockSpec((tm, n_heads * rope), tab),
                  pl.BlockSpec((tm, n_heads * rope), tab),
                  pl.BlockSpec((tm, rope), tab),
                  pl.BlockSpec((tm, rope), tab),
                  pl.BlockSpec(wqn.shape, full2),
                  pl.BlockSpec(wqp.shape, full2),
                  pl.BlockSpec(wqs.shape, full2),
                  pl.BlockSpec(wukT.shape, full3)],
        out_specs=[pl.BlockSpec((tm, kv_lora), row),
                   pl.BlockSpec((tm, rope), row),
                   pl.BlockSpec((tm, hd), row),
                   pl.BlockSpec((tm, n_heads * hd), row)],
        out_shape=[jax.ShapeDtypeStruct((M, kv_lora), F32),
                   jax.ShapeDtypeStruct((M, rope), F32),
                   jax.ShapeDtypeStruct((M, hd), CDT),
                   jax.ShapeDtypeStruct((M, n_heads * hd), CDT)],
        compiler_params=_cparams(("parallel",)),
        name="mla_prep",
    )(p, g_q.reshape(1, -1), g_kv.reshape(1, -1), cosq, sinq, cosk, sink, wqn, wqp, wqs, wukT)


def _mla_flash_kernel(q_ref, k_ref, o_ref, m_ref, l_ref, acc_ref, s2_ref, p2_ref, pm_ref, ps_ref, *,
                      n_heads, tq, tk, hd, dv, scale, rc):
    qi = pl.program_id(1)
    ki = pl.program_id(2)

    @pl.when(ki == 0)
    def _():
        m_ref[...] = jnp.full(m_ref.shape, NEG_INF, F32)
        l_ref[...] = jnp.zeros(l_ref.shape, F32)
        acc_ref[...] = jnp.zeros(acc_ref.shape, F32)

    c = scale * math.log2(math.e)

    def block(masked):
        k = k_ref[...]
        v = k[:, :dv]
        tiles = lambda a: [a[:, t * LANE:(t + 1) * LANE] for t in range(a.shape[1] // LANE)]

        def chunk_logits(s_ref, r):
            s = s_ref[r * rc:(r + 1) * rc, :]
            if masked:
                row = qi * tq + r * rc + lax.broadcasted_iota(jnp.int32, (rc, tk), 0)
                col = ki * tk + lax.broadcasted_iota(jnp.int32, (rc, tk), 1)
                s = jnp.where(col <= row, s, NEG_INF)
            return s

        for h in range(n_heads):
            b = h % 2
            s_ref, p_ref = s2_ref.at[b], p2_ref.at[b]
            s_ref[...] = _dot_nt(q_ref[:, h * hd:(h + 1) * hd], k)
            for r in range(tq // rc):
                pm_ref[b, r * rc:(r + 1) * rc, :] = functools.reduce(jnp.maximum, tiles(chunk_logits(s_ref, r)))
            m_prev = m_ref[h]
            m_new = jnp.maximum(m_prev, jnp.broadcast_to(jnp.max(pm_ref[b], axis=-1, keepdims=True), (tq, LANE)))
            alpha = jnp.exp2(c * (m_prev - m_new))
            m_ref[h] = m_new
            pm_ref[b] = m_new
            for r in range(tq // rc):
                mb = pm_ref[b, r * rc:(r + 1) * rc, :]
                p = [jnp.exp2(c * (st - mb)) for st in tiles(chunk_logits(s_ref, r))]
                ps_ref[b, r * rc:(r + 1) * rc, :] = functools.reduce(jnp.add, p)
                p_ref[r * rc:(r + 1) * rc, :] = jnp.concatenate(p, axis=1).astype(CDT)
            l_blk = jnp.broadcast_to(jnp.sum(ps_ref[b], axis=-1, keepdims=True), (tq, LANE))
            l_ref[h] = alpha * l_ref[h] + l_blk
            pv = jnp.dot(p_ref[...], v, preferred_element_type=F32)
            acc_ref[h] = jnp.concatenate([alpha] * (dv // LANE), axis=1) * acc_ref[h] + pv

    below_diag = ki * tk + tk - 1 <= qi * tq
    pl.when(below_diag)(functools.partial(block, False))
    pl.when(jnp.logical_not(below_diag) & (ki * tk <= qi * tq + tq - 1))(functools.partial(block, True))

    @pl.when(ki == pl.num_programs(2) - 1)
    def _():
        for h in range(n_heads):
            l = jnp.concatenate([l_ref[h]] * (dv // LANE), axis=1)
            o_ref[:, h * dv:(h + 1) * dv] = (acc_ref[h] / l).astype(o_ref.dtype)


def _mla_flash(qcat, kcat, B, T, n_heads, dv, scale):
    M, hd = kcat.shape
    tq = _tile(T, FLASH_TQ, 8)
    tk = _tile(T, FLASH_TK, 8)
    nq, nk = T // tq, T // tk
    assert tk % LANE == 0 and dv % LANE == 0
    rc = math.gcd(tq, FLASH_ROW_CHUNK)
    kern = functools.partial(_mla_flash_kernel, n_heads=n_heads, tq=tq, tk=tk, hd=hd, dv=dv, scale=scale, rc=rc)

    def kmap(b, qi, ki):
        return (b * nk + jnp.minimum(ki, (qi * tq + tq - 1) // tk), 0)

    return pl.pallas_call(
        kern,
        grid=(B, nq, nk),
        in_specs=[pl.BlockSpec((tq, n_heads * hd), lambda b, qi, ki: (b * nq + qi, 0)),
                  pl.BlockSpec((tk, hd), kmap)],
        out_specs=pl.BlockSpec((tq, n_heads * dv), lambda b, qi, ki: (b * nq + qi, 0)),
        out_shape=jax.ShapeDtypeStruct((M, n_heads * dv), CDT),
        scratch_shapes=[pltpu.VMEM((n_heads, tq, LANE), F32),
                        pltpu.VMEM((n_heads, tq, LANE), F32),
                        pltpu.VMEM((n_heads, tq, dv), F32),
                        pltpu.VMEM((2, tq, tk), F32),
                        pltpu.VMEM((2, tq, tk), CDT),
                        pltpu.VMEM((2, tq, LANE), F32),
                        pltpu.VMEM((2, tq, LANE), F32)],
        compiler_params=_cparams(("parallel", "parallel", "arbitrary")),
        name="mla_flash",
    )(qcat, kcat)


def _mla_decode_kernel(pt_ref, q_ref, kn_ref, *refs, n_pages_step, n_seq, n_heads, t_new, dv, rope, scale):
    G = n_pages_step
    ckv_refs = refs[:n_seq * G]
    kpe_refs = refs[n_seq * G:2 * n_seq * G]
    o_ref, m_ref, l_ref, acc_ref = refs[2 * n_seq * G:]
    g = pl.program_id(1)

    @pl.when(g == 0)
    def _():
        m_ref[...] = jnp.full(m_ref.shape, NEG_INF, F32)
        l_ref[...] = jnp.zeros(l_ref.shape, F32)
        acc_ref[...] = jnp.zeros(acc_ref.shape, F32)

    def update(sb, s, v):
        m_prev = m_ref[sb]
        m_new = jnp.maximum(m_prev, jnp.max(s, axis=-1, keepdims=True))
        alpha = jnp.exp(m_prev - m_new)
        p = jnp.exp(s - m_new)
        l_ref[sb] = alpha * l_ref[sb] + jnp.sum(p, axis=-1, keepdims=True)
        acc_ref[sb] = alpha * acc_ref[sb] + _dot(p, v)
        m_ref[sb] = m_new

    for sb in range(n_seq):
        q = q_ref[sb]
        ckv = jnp.concatenate([r[0, 0] for r in ckv_refs[sb * G:(sb + 1) * G]], axis=0).astype(CDT)
        kpe_t = jnp.concatenate([r[0, 0] for r in kpe_refs[sb * G:(sb + 1) * G]], axis=1).astype(CDT)
        s = (_dot_nt(q[:, :dv], ckv) + _dot(q[:, dv:dv + rope], kpe_t)) * scale
        update(sb, s, ckv)

    @pl.when(g == pl.num_programs(1) - 1)
    def _():
        for sb in range(n_seq):
            q = q_ref[sb]
            kn = kn_ref[sb]
            s2 = _dot_nt(q, kn) * scale
            r, c = s2.shape
            t_row = lax.broadcasted_iota(jnp.int32, (r, c), 0) // n_heads
            col = lax.broadcasted_iota(jnp.int32, (r, c), 1)
            s2 = jnp.where((col <= t_row) & (col < t_new), s2, NEG_INF)
            update(sb, s2, kn[:, :dv])
            o_ref[sb] = (acc_ref[sb] / l_ref[sb]).astype(o_ref.dtype)


def _mla_decode(qcat, kcat, cache_ckv, cache_kpe_t, layer, page_table, B, Ts, n_heads, dv, rope, scale):
    hd = kcat.shape[1]
    n_pages = page_table.shape[1]
    page = cache_ckv.shape[2]
    SB = math.gcd(B, SEQS_PER_STEP_MLA)
    G = math.gcd(n_pages, PAGES_PER_STEP_MLA // SB)
    NG = n_pages // G
    R = Ts * n_heads
    q3 = qcat.reshape(B, R, hd)
    npad = 16
    kn = jnp.pad(kcat.reshape(B, Ts, hd), ((0, 0), (0, npad - Ts), (0, 0)))
    kern = functools.partial(_mla_decode_kernel, n_pages_step=G, n_seq=SB, n_heads=n_heads, t_new=Ts, dv=dv,
                             rope=rope, scale=scale)

    def page_map(sb, i):
        return lambda b, g, pt: (layer, pt[b * SB + sb, g * G + i], 0, 0)

    in_specs = [pl.BlockSpec((SB, R, hd), lambda b, g, pt: (b, 0, 0)),
                pl.BlockSpec((SB, npad, hd), lambda b, g, pt: (b, 0, 0))]
    in_specs += [pl.BlockSpec((1, 1, page, dv), page_map(sb, i)) for sb in range(SB) for i in range(G)]
    in_specs += [pl.BlockSpec((1, 1, rope, page), page_map(sb, i)) for sb in range(SB) for i in range(G)]
    out = pl.pallas_call(
        kern,
        grid_spec=pltpu.PrefetchScalarGridSpec(
            num_scalar_prefetch=1,
            grid=(B // SB, NG),
            in_specs=in_specs,
            out_specs=pl.BlockSpec((SB, R, dv), lambda b, g, pt: (b, 0, 0)),
            scratch_shapes=[pltpu.VMEM((SB, R, 1), F32), pltpu.VMEM((SB, R, 1), F32),
                            pltpu.VMEM((SB, R, dv), F32)]),
        out_shape=jax.ShapeDtypeStruct((B, R, dv), CDT),
        compiler_params=_cparams(("parallel", "arbitrary")),
        name="mla_decode",
    )(page_table, q3, kn, *([cache_ckv] * (SB * G)), *([cache_kpe_t] * (SB * G)))
    return out.reshape(B * Ts, n_heads * dv)


def _mla_out_kernel(o_ref, wuv_ref, wo_ref, x_ref, gt_ref, y_ref, *, n_heads, dv):
    parts = [_dot(o_ref[:, h * dv:(h + 1) * dv], wuv_ref[h]).astype(CDT) for h in range(n_heads)]
    o = jnp.concatenate(parts, axis=1)
    y = jnp.dot(o, wo_ref[...], preferred_element_type=F32)
    y_ref[...] = x_ref[...] + gt_ref[0] * y


def _mla_out(o_lat, wuv, wo, x, gt, rows_per_group, n_heads, dv):
    M, D = x.shape
    tm = _tile(M, 512, 8)
    tpg = max(rows_per_group // tm, 1)
    kern = functools.partial(_mla_out_kernel, n_heads=n_heads, dv=dv)
    return pl.pallas_call(
        kern,
        grid=(M // tm,),
        in_specs=[pl.BlockSpec((tm, n_heads * dv), lambda i: (i, 0)),
                  pl.BlockSpec(wuv.shape, lambda i: (0, 0, 0)),
                  pl.BlockSpec(wo.shape, lambda i: (0, 0)),
                  pl.BlockSpec((tm, D), lambda i: (i, 0)),
                  pl.BlockSpec((1, gt.shape[1], D), lambda i: (i // tpg, 0, 0))],
        out_specs=pl.BlockSpec((tm, D), lambda i: (i, 0)),
        out_shape=jax.ShapeDtypeStruct((M, D), F32),
        compiler_params=_cparams(("parallel",)),
        name="mla_out",
    )(o_lat, wuv, wo, x, gt)


def _proj_res_kernel(o_ref, wo_ref, x_ref, gt_ref, y_ref):
    y = jnp.dot(o_ref[...], wo_ref[...], preferred_element_type=F32)
    y_ref[...] = x_ref[...] + gt_ref[0] * y


def _proj_res(o, wo, x, gt, rows_per_group):
    M, D = x.shape
    K = o.shape[1]
    tm = _tile(M, 512, 8)
    tpg = max(rows_per_group // tm, 1)
    return pl.pallas_call(
        _proj_res_kernel,
        grid=(M // tm,),
        in_specs=[pl.BlockSpec((tm, K), lambda i: (i, 0)),
                  pl.BlockSpec(wo.shape, lambda i: (0, 0)),
                  pl.BlockSpec((tm, D), lambda i: (i, 0)),
                  pl.BlockSpec((1, gt.shape[1], D), lambda i: (i // tpg, 0, 0))],
        out_specs=pl.BlockSpec((tm, D), lambda i: (i, 0)),
        out_shape=jax.ShapeDtypeStruct((M, D), F32),
        compiler_params=_cparams(("parallel",)),
        name="proj_res",
    )(o, wo, x, gt)


def _gla_gate_kernel(a_ref, w_ref, b_ref, o_ref):
    z = _dot(a_ref[...], w_ref[...]) + b_ref[...]
    o_ref[...] = (jnp.minimum(z, 0.0) - jnp.log(1.0 + jnp.exp(-jnp.abs(z)))) / GLA_TAU


def _gla_gate(proj, col_block, w_a2p, b_a2):
    M = proj.shape[0]
    N = w_a2p.shape[1]
    tm = _tile(M, 1024, 8)
    return pl.pallas_call(
        _gla_gate_kernel,
        grid=(M // tm,),
        in_specs=[pl.BlockSpec((tm, LANE), lambda i: (i, col_block)),
                  pl.BlockSpec(w_a2p.shape, lambda i: (0, 0)),
                  pl.BlockSpec((1, N), lambda i: (0, 0))],
        out_specs=pl.BlockSpec((tm, N), lambda i: (i, 0)),
        out_shape=jax.ShapeDtypeStruct((M, N), F32),
        compiler_params=_cparams(("parallel",)),
        name="gla_gate",
    )(proj, w_a2p, b_a2.reshape(1, N))


def _cumsum_rows(x):
    C = x.shape[0]
    row = lax.broadcasted_iota(jnp.int32, x.shape, 0)
    if C <= 8:
        out = jnp.zeros_like(x)
        for s in range(C):
            out = out + jnp.where(row >= s, x[s:s + 1], 0.0)
        return out
    sh = 1
    while sh < C:
        x = x + jnp.where(row >= sh, pltpu.roll(x, sh, 0), 0.0)
        sh *= 2
    return x


def _gla_kernel(q_ref, k_ref, v_ref, la_ref, s0_ref, o_ref, sf_ref, st_ref, *, n_heads, dk, dv, sub, qscale):
    c = pl.program_id(1)

    @pl.when(c == 0)
    def _():
        for h in range(n_heads):
            st_ref[h] = s0_ref[0, h].T

    C = q_ref.shape[1]
    nsub = C // sub
    for h in range(n_heads):
        q = q_ref[0, :, h * dk:(h + 1) * dk] * qscale
        k = k_ref[0, :, h * dk:(h + 1) * dk]
        v = v_ref[0, :, h * dv:(h + 1) * dv]
        b = _cumsum_rows(la_ref[0, :, h * dk:(h + 1) * dk])
        st = st_ref[h]
        o_inter = _dot_nt(q * jnp.exp(b), st)
        b_last = b[C - 1:C]
        k_dec = k * jnp.exp(b_last - b)
        st_ref[h] = jnp.exp(b_last) * st + _dot_tn(v, k_dec)
        outs = []
        for i in range(nsub):
            r0 = i * sub
            b_i = b[r0:r0 + sub]
            q_i = q[r0:r0 + sub]
            k_i = k[r0:r0 + sub]
            v_i = v[r0:r0 + sub]
            o_i = o_inter[r0:r0 + sub]
            if i > 0:
                ref_row = b[r0:r0 + 1]
                att = _dot_nt(q_i * jnp.exp(b_i - ref_row), k[:r0] * jnp.exp(ref_row - b[:r0]))
                o_i = o_i + _dot(att, v[:r0])
            t_loc = lax.broadcasted_iota(jnp.int32, (sub, 1), 0)
            for s in range(sub):
                w = jnp.exp(jnp.minimum(b_i - b_i[s:s + 1], 0.0))
                col = jnp.sum(q_i * w * k_i[s:s + 1], axis=-1, keepdims=True)
                o_i = o_i + jnp.where(t_loc >= s, col, 0.0) * v_i[s:s + 1]
            outs.append(o_i)
        o_ref[0, :, h * dv:(h + 1) * dv] = outs[0] if nsub == 1 else jnp.concatenate(outs, axis=0)

    @pl.when(c == pl.num_programs(1) - 1)
    def _():
        for h in range(n_heads):
            sf_ref[0, h] = st_ref[h].T


def _gla_recurrence(proj, log_a, s0, B, T, n_heads, dk, dv):
    C = 64 if T % 64 == 0 else T
    sub = min(16, C)
    nc = T // C
    Np = proj.shape[1]
    p3 = proj.reshape(B * nc, C, Np)
    la3 = log_a.reshape(B * nc, C, n_heads * dk)
    hk, hv = n_heads * dk, n_heads * dv
    assert hv % hk == 0
    kern = functools.partial(_gla_kernel, n_heads=n_heads, dk=dk, dv=dv, sub=sub, qscale=dk ** -0.5)
    o, sf = pl.pallas_call(
        kern,
        grid=(B, nc),
        in_specs=[pl.BlockSpec((1, C, hk), lambda b, c: (b * nc + c, 0, 0)),
                  pl.BlockSpec((1, C, hk), lambda b, c: (b * nc + c, 0, 1)),
                  pl.BlockSpec((1, C, hv), lambda b, c: (b * nc + c, 0, 2 * hk // hv)),
                  pl.BlockSpec((1, C, hk), lambda b, c: (b * nc + c, 0, 0)),
                  pl.BlockSpec((1, n_heads, dk, dv), lambda b, c: (b, 0, 0, 0))],
        out_specs=[pl.BlockSpec((1, C, hv), lambda b, c: (b * nc + c, 0, 0)),
                   pl.BlockSpec((1, n_heads, dk, dv), lambda b, c: (b, 0, 0, 0))],
        out_shape=[jax.ShapeDtypeStruct((B * nc, C, hv), F32),
                   jax.ShapeDtypeStruct((B, n_heads, dk, dv), F32)],
        scratch_shapes=[pltpu.VMEM((n_heads, dv, dk), F32)],
        compiler_params=_cparams(("parallel", "arbitrary")),
        name="gla_recurrence",
    )(p3, p3, p3, la3, s0)
    return o.reshape(B * T, hv), sf


def _gla_out_kernel(o_ref, r_ref, g_ref, wo_ref, x_ref, gt_ref, y_ref, *, n_heads, dv):
    parts = []
    for h in range(n_heads):
        sl = slice(h * dv, (h + 1) * dv)
        parts.append((_rms(o_ref[:, sl], g_ref[...]) * _silu(r_ref[:, sl])).astype(CDT))
    y = jnp.dot(jnp.concatenate(parts, axis=1), wo_ref[...], preferred_element_type=F32)
    y_ref[...] = x_ref[...] + gt_ref[0] * y


def _gla_out(o, proj, r_block, g_o, wo, x, gt, rows_per_group, n_heads, dv):
    M, D = x.shape
    hv = n_heads * dv
    tm = _tile(M, 512, 8)
    tpg = max(rows_per_group // tm, 1)
    kern = functools.partial(_gla_out_kernel, n_heads=n_heads, dv=dv)
    return pl.pallas_call(
        kern,
        grid=(M // tm,),
        in_specs=[pl.BlockSpec((tm, hv), lambda i: (i, 0)),
                  pl.BlockSpec((tm, hv), lambda i: (i, r_block)),
                  pl.BlockSpec((1, dv), lambda i: (0, 0)),
                  pl.BlockSpec(wo.shape, lambda i: (0, 0)),
                  pl.BlockSpec((tm, D), lambda i: (i, 0)),
                  pl.BlockSpec((1, gt.shape[1], D), lambda i: (i // tpg, 0, 0))],
        out_specs=pl.BlockSpec((tm, D), lambda i: (i, 0)),
        out_shape=jax.ShapeDtypeStruct((M, D), F32),
        compiler_params=_cparams(("parallel",)),
        name="gla_out",
    )(o, proj, g_o.reshape(1, dv), wo, x, gt)


def _order_key(score):
    score = jnp.where(score == 0.0, 0.0, score)
    bits = pltpu.bitcast(score, jnp.int32)
    return jnp.where(bits < 0, bits ^ jnp.int32(0x7FFFFFFF), bits)


def _sum_rows(x, chains=8):
    n = x.shape[0]
    if n % (8 * chains):
        return jnp.sum(x, axis=0, keepdims=True)
    part = jnp.sum(x.reshape(chains, n // chains, *x.shape[1:]), axis=1)
    return jnp.sum(part, axis=0, keepdims=True)


def _kth_largest_key(key, topk, axes, two_bit_steps):
    shape = tuple(1 if a in axes else s for a, s in enumerate(key.shape))

    def count(cand):
        hit = jnp.where(key >= cand, 1, 0)
        if axes == (0,):
            return _sum_rows(hit)
        if axes == (0, 2) and key.shape[2] % LANE == 0:
            parts = [hit[a, :, t * LANE:(t + 1) * LANE] for a in range(key.shape[0]) for t in range(key.shape[2] // LANE)]
            while len(parts) > 1:
                parts = [parts[i] + parts[i + 1] for i in range(0, len(parts) - 1, 2)] + parts[len(parts) & ~1:]
            return jnp.sum(parts[0], axis=-1, keepdims=True)[None]
        return jnp.sum(hit, axis=axes, keepdims=True)

    def body1(it, t):
        cand = t + lax.shift_left(jnp.int32(1), jnp.int32(31) - it)
        return jnp.where(count(cand) >= topk, cand, t)

    def body2(it, t):
        hi = lax.shift_left(jnp.int32(1), jnp.int32(31) - 2 * it)
        lo = lax.shift_left(jnp.int32(1), jnp.int32(30) - 2 * it)
        t01, t10, t11 = t + lo, t + hi, t + hi + lo
        c01, c10, c11 = count(t01), count(t10), count(t11)
        return jnp.where(c11 >= topk, t11, jnp.where(c10 >= topk, t10, jnp.where(c01 >= topk, t01, t)))

    init = jnp.full(shape, INT_MIN, jnp.int32)
    return lax.fori_loop(0, 16, body2, init) if two_bit_steps else lax.fori_loop(0, 32, body1, init)


def _bias_chain(dist, value_of_bucket):
    val = value_of_bucket(0)
    for j in range(1, N_BUCKETS):
        val = jnp.where(dist >= BUCKET_LO[j], value_of_bucket(j), val)
    return val


def _strict_upper(n):
    a = lax.broadcasted_iota(jnp.int32, (n, n), 0)
    b = lax.broadcasted_iota(jnp.int32, (n, n), 1)
    return jnp.where(a < b, 1.0, 0.0).astype(CDT)


def _strict_lower(n):
    a = lax.broadcasted_iota(jnp.int32, (n, n), 0)
    b = lax.broadcasted_iota(jnp.int32, (n, n), 1)
    return jnp.where(b < a, 1.0, 0.0).astype(CDT)


def _dsa_prompt_kernel(rb_ref, q_ref, qi_ref, wq_ref, k_ref, v_ref, kk_ref, o_ref, sel_ref, band_ref, selt_ref, *,
                       n_heads, n_kv, hd, n_idx, di, topk, tq, T, n_cls):
    b_id = pl.program_id(0)
    qt = pl.program_id(1)
    group = n_heads // n_kv

    @pl.when((b_id == 0) & (qt == 0))
    def _():
        i = lax.broadcasted_iota(jnp.int32, (tq, LANE), 0)
        j = lax.broadcasted_iota(jnp.int32, (tq, LANE), 1)
        for h in range(n_heads):
            far = rb_ref[N_BUCKETS - 1, h]
            for w in range(tq // LANE + 1):
                dist = i - j + (w - (tq // LANE - 1)) * LANE
                band_ref[w, h] = _bias_chain(dist, lambda bk: rb_ref[bk, h]) - far

    def body(nk):
        key_pos = lax.broadcasted_iota(jnp.int32, (nk, tq), 0)
        q_pos = qt * tq + lax.broadcasted_iota(jnp.int32, (nk, tq), 1)
        causal = key_pos <= q_pos

        ki = kk_ref[:nk, :di]
        wi_t = jnp.transpose(wq_ref[...])[di:di + n_idx] * (n_idx ** -0.5 * di ** -0.5)
        score = jnp.zeros((nk, tq), F32)
        for h in range(n_idx):
            lg = _dot_nt(ki, qi_ref[:, h * di:(h + 1) * di])
            score = score + wi_t[h:h + 1] * jnp.maximum(lg, 0.0)
        score = jnp.where(causal, score, NEG_INF)
        key = _order_key(score)
        thr = _kth_largest_key(key, topk, (0,), two_bit_steps=False)
        gt = key > thr
        eq = key == thr
        n_gt = _sum_rows(jnp.where(gt, 1, 0))
        n_eq = _sum_rows(jnp.where(eq, 1, 0))
        need = topk - n_gt
        tie = jnp.max(n_eq - need) > 0

        @pl.when(jnp.logical_not(tie))
        def _():
            selt_ref[:nk, :] = jnp.where((gt | eq) & causal, 0.0, NEG_INF)

        @pl.when(tie)
        def _():
            lower = _strict_lower(LANE)
            run = jnp.zeros((1, tq), F32)
            needf = need.astype(F32)
            for kb in range(nk // LANE):
                sl = slice(kb * LANE, (kb + 1) * LANE)
                eqb = jnp.where(eq[sl], 1.0, 0.0)
                pre = jnp.dot(lower, eqb.astype(CDT), preferred_element_type=F32) + run
                keep = gt[sl] | (eq[sl] & (pre < needf))
                selt_ref[sl, :] = jnp.where(keep & causal[sl], 0.0, NEG_INF)
                run = run + jnp.sum(eqb, axis=0, keepdims=True)

        for kb in range(nk // LANE):
            sel_ref[:, kb * LANE:(kb + 1) * LANE] = jnp.transpose(selt_ref[kb * LANE:(kb + 1) * LANE, :])

        scale = hd ** -0.5
        nband = tq // LANE + 1
        nkb = nk // LANE
        first_near = max(nkb - (T // n_cls) // LANE - tq // LANE, 0)
        for g in range(n_kv):
            kg = k_ref[:nk, g * hd:(g + 1) * hd]
            vg = v_ref[:nk, g * hd:(g + 1) * hd]
            for r in range(group):
                h = g * group + r
                s = _dot_nt(q_ref[:, h * hd:(h + 1) * hd], kg) * scale + rb_ref[N_BUCKETS - 1, h] + sel_ref[:, :nk]
                parts = [s[:, :first_near * LANE]] if first_near else []
                for kb in range(first_near, nkb):
                    w = qt * (tq // LANE) - kb + (tq // LANE - 1)
                    add = jnp.zeros((tq, LANE), F32)
                    for wv in range(nband):
                        add = jnp.where(w == wv, band_ref[wv, h], add)
                    parts.append(s[:, kb * LANE:(kb + 1) * LANE] + add)
                s = jnp.concatenate(parts, axis=1)
                m = jnp.max(s, axis=-1, keepdims=True)
                p = jnp.exp(s - m)
                l = jnp.sum(p, axis=-1, keepdims=True)
                o_ref[:, h * hd:(h + 1) * hd] = (_dot(p, vg) / l).astype(o_ref.dtype)

    width = T // n_cls
    cls = ((qt + 1) * tq - 1) // width
    for c in range(n_cls):
        pl.when(cls == c)(functools.partial(body, (c + 1) * width))


def _dsa_prompt(proj, proj_c, rel_bias, B, T, dims, topk):
    n_heads, n_kv, hd, n_idx, di = dims
    M, Np = proj.shape
    tq = math.gcd(T, DSA_TQ)
    assert tq % LANE == 0
    nqt = T // tq
    hq, hkv, hi = n_heads * hd, n_kv * hd, n_idx * di
    assert hq % hkv == 0 and (hq + 2 * hkv) % hi == 0 and (hq + 2 * hkv + hi) % LANE == 0
    kk_blk = (hq + 2 * hkv + hi) // LANE
    n_cls = math.gcd(T // LANE, DSA_EXTENT_CLASSES)
    kern = functools.partial(_dsa_prompt_kernel, n_heads=n_heads, n_kv=n_kv, hd=hd, n_idx=n_idx, di=di, topk=topk,
                             tq=tq, T=T, n_cls=n_cls)
    return pl.pallas_call(
        kern,
        grid=(B, nqt),
        in_specs=[pl.BlockSpec(memory_space=pltpu.SMEM),
                  pl.BlockSpec((tq, hq), lambda b, t: (b * nqt + t, 0)),
                  pl.BlockSpec((tq, hi), lambda b, t: (b * nqt + t, (hq + 2 * hkv) // hi)),
                  pl.BlockSpec((tq, LANE), lambda b, t: (b * nqt + t, kk_blk)),
                  pl.BlockSpec((T, hkv), lambda b, t: (b, hq // hkv)),
                  pl.BlockSpec((T, hkv), lambda b, t: (b, hq // hkv + 1)),
                  pl.BlockSpec((T, LANE), lambda b, t: (b, kk_blk))],
        out_specs=pl.BlockSpec((tq, hq), lambda b, t: (b * nqt + t, 0)),
        out_shape=jax.ShapeDtypeStruct((M, hq), CDT),
        scratch_shapes=[pltpu.VMEM((tq, T), F32),
                        pltpu.VMEM((tq // LANE + 1, n_heads, tq, LANE), F32),
                        pltpu.VMEM((T, tq), F32)],
        compiler_params=_cparams(("arbitrary", "arbitrary")),
        name="dsa_prompt",
    )(rel_bias, proj_c, proj_c, proj, proj_c, proj_c, proj_c)


def _dsa_select_kernel(pt_ref, qi_ref, wi_ref, kn_ref, *refs, n_pages_step, n_seq, n_idx, di, topk, t_new):
    G = n_pages_step
    ki_refs = refs[:n_seq * G]
    mask_ref, sc_ref = refs[n_seq * G:]
    g = pl.program_id(1)
    NG = pl.num_programs(1)
    R = qi_ref.shape[1] // n_idx
    RT = n_seq * R
    GW = G * ki_refs[0].shape[3]

    def head_sum(sb, lg):
        wi = wi_ref[sb] * (n_idx ** -0.5)
        w = wi * jnp.maximum(lg * (di ** -0.5), 0.0)
        sc = w[0:R]
        for h in range(1, n_idx):
            sc = sc + w[h * R:(h + 1) * R]
        return sc

    for sb in range(n_seq):
        keys_t = jnp.concatenate([r[0, 0] for r in ki_refs[sb * G:(sb + 1) * G]], axis=1)
        sc_ref[g, sb * R:(sb + 1) * R, :] = head_sum(sb, _dot(qi_ref[sb], keys_t))

    @pl.when(g == NG - 1)
    def _():
        for sb in range(n_seq):
            new = head_sum(sb, _dot_nt(qi_ref[sb], kn_ref[sb][:, :di]))
            t_row = lax.broadcasted_iota(jnp.int32, new.shape, 0) % t_new
            col = lax.broadcasted_iota(jnp.int32, new.shape, 1)
            new = jnp.where((col <= t_row) & (col < t_new), new, NEG_INF)
            sc_ref[NG, sb * R:(sb + 1) * R, :] = jnp.concatenate([new, jnp.full((R, GW - LANE), NEG_INF, F32)], axis=1)
        score = sc_ref[...]
        valid = score > NEG_INF
        key = _order_key(score)
        thr = _kth_largest_key(key, topk, (0, 2), two_bit_steps=True)
        gt = key > thr
        eq = key == thr
        n_gt = jnp.sum(jnp.where(gt, 1, 0), axis=(0, 2), keepdims=True)
        n_eq = jnp.sum(jnp.where(eq, 1, 0), axis=(0, 2), keepdims=True)
        need = topk - n_gt
        tie = jnp.max(n_eq - need) > 0

        @pl.when(jnp.logical_not(tie))
        def _():
            mask_ref[0] = jnp.where((gt | eq) & valid, 0.0, NEG_INF)

        @pl.when(tie)
        def _():
            upper = _strict_upper(LANE)
            needf = need[0].astype(F32)

            def blk(gi, run):
                k_g = _order_key(sc_ref[gi])
                v_g = sc_ref[gi] > NEG_INF
                for c in range(GW // LANE):
                    sl = slice(c * LANE, (c + 1) * LANE)
                    eqb = jnp.where(k_g[:, sl] == thr[0], 1.0, 0.0)
                    pre = jnp.dot(eqb.astype(CDT), upper, preferred_element_type=F32) + run
                    keep = (k_g[:, sl] > thr[0]) | ((eqb > 0.0) & (pre < needf))
                    mask_ref[0, gi, :, sl] = jnp.where(keep & v_g[:, sl], 0.0, NEG_INF)
                    run = run + jnp.sum(eqb, axis=1, keepdims=True)
                return run

            lax.fori_loop(0, NG + 1, blk, jnp.zeros((RT, 1), F32))


def _dsa_select(qi_hm, wi_hm, kn, cache_ki_t, layer, page_table, B, n_idx, di, topk, t_new):
    n_pages = page_table.shape[1]
    page = cache_ki_t.shape[3]
    SB = math.gcd(B, SEQS_PER_STEP_SELECT)
    G = math.gcd(n_pages, PAGES_PER_STEP_SELECT)
    NG = n_pages // G
    GW = G * page
    R = qi_hm.shape[1] // n_idx
    kern = functools.partial(_dsa_select_kernel, n_pages_step=G, n_seq=SB, n_idx=n_idx, di=di, topk=topk,
                             t_new=t_new)

    def page_map(sb, i):
        return lambda b, g, pt: (layer, pt[b * SB + sb, g * G + i], 0, 0)

    in_specs = [pl.BlockSpec((SB, n_idx * R, di), lambda b, g, pt: (b, 0, 0)),
                pl.BlockSpec((SB, n_idx * R, 1), lambda b, g, pt: (b, 0, 0)),
                pl.BlockSpec((SB, LANE, LANE), lambda b, g, pt: (b, 0, 0))]
    in_specs += [pl.BlockSpec((1, 1, di, page), page_map(sb, i)) for sb in range(SB) for i in range(G)]
    return pl.pallas_call(
        kern,
        grid_spec=pltpu.PrefetchScalarGridSpec(
            num_scalar_prefetch=1,
            grid=(B // SB, NG),
            in_specs=in_specs,
            out_specs=pl.BlockSpec((1, NG + 1, SB * R, GW), lambda b, g, pt: (b, 0, 0, 0)),
            scratch_shapes=[pltpu.VMEM((NG + 1, SB * R, GW), F32)]),
        out_shape=jax.ShapeDtypeStruct((B // SB, NG + 1, SB * R, GW), F32),
        compiler_params=_cparams(("parallel", "arbitrary")),
        name="dsa_select",
    )(page_table, qi_hm, wi_hm, kn, *([cache_ki_t] * (SB * G)))


def _dsa_decode_kernel(pt_ref, rb_ref, q_ref, kn_ref, vn_ref, mask_ref, mnew_ref, *refs,
                       n_pages_step, n_kv, group, hd, t_new, past_len):
    G = n_pages_step
    k_refs = refs[:G]
    v_refs = refs[G:2 * G]
    o_ref, m_ref, l_ref, acc_ref = refs[2 * G:]
    g = pl.program_id(1)
    NG = pl.num_programs(1)
    R = group * t_new
    scale = hd ** -0.5

    @pl.when(g == 0)
    def _():
        m_ref[...] = jnp.full(m_ref.shape, NEG_INF, F32)
        l_ref[...] = jnp.zeros(l_ref.shape, F32)
        acc_ref[...] = jnp.zeros(acc_ref.shape, F32)

    def head_val(kvh, bucket):
        row = lax.broadcasted_iota(jnp.int32, (R, 1), 0)
        val = jnp.full((R, 1), rb_ref[bucket, kvh * group], F32)
        for r in range(1, group):
            val = jnp.where(row >= r * t_new, rb_ref[bucket, kvh * group + r], val)
        return val

    def update(kvh, s, v):
        m_prev = m_ref[kvh]
        m_new = jnp.maximum(m_prev, jnp.max(s, axis=-1, keepdims=True))
        m_safe = jnp.where(m_new > NEG_INF, m_new, 0.0)
        alpha = jnp.exp(m_prev - m_safe)
        p = jnp.exp(s - m_safe)
        l_ref[kvh] = alpha * l_ref[kvh] + jnp.sum(p, axis=-1, keepdims=True)
        acc_ref[kvh] = alpha * acc_ref[kvh] + _dot(p, v)
        m_ref[kvh] = m_new

    page = k_refs[0].shape[2] // n_kv
    GW = G * page
    mask = mask_ref[0, 0]
    t_row = lax.broadcasted_iota(jnp.int32, (R, GW), 0) % t_new
    col = lax.broadcasted_iota(jnp.int32, (R, GW), 1)
    dist = past_len + t_row - (g * GW + col)
    near = g == NG - 1

    def head_rows(refs_, kvh):
        return jnp.concatenate([r[0, 0, pl.ds(kvh, page, stride=n_kv), :] for r in refs_], axis=0).astype(CDT)

    for kvh in range(n_kv):
        s = _dot_nt(q_ref[0, kvh], head_rows(k_refs, kvh)) * scale
        bias = lax.cond(near,
                        lambda: _bias_chain(dist, functools.partial(head_val, kvh)),
                        lambda: jnp.broadcast_to(head_val(kvh, N_BUCKETS - 1), (R, GW)))
        update(kvh, s + bias + mask, head_rows(v_refs, kvh))

    @pl.when(near)
    def _():
        kn = kn_ref[0].astype(CDT)
        vn = vn_ref[0].astype(CDT)
        mnew = mnew_ref[0, 0][:, :LANE]
        t_r = lax.broadcasted_iota(jnp.int32, (R, LANE), 0) % t_new
        c = lax.broadcasted_iota(jnp.int32, (R, LANE), 1)
        d_new = jnp.maximum(t_r - c, 0)
        for kvh in range(n_kv):
            sl = slice(kvh * hd, (kvh + 1) * hd)
            s = _dot_nt(q_ref[0, kvh], kn[:, sl]) * scale
            bias = _bias_chain(d_new, functools.partial(head_val, kvh))
            update(kvh, s + bias + mnew, vn[:, sl])
            o_ref[0, kvh] = (acc_ref[kvh] / l_ref[kvh]).astype(o_ref.dtype)


def _dsa_decode(q4, kn, vn, mask, rel_bias, cache_k, cache_v, layer, page_table, B, n_kv, group, hd, t_new):
    n_pages = page_table.shape[1]
    page = cache_k.shape[2] // n_kv
    GWs = mask.shape[3]
    G = math.gcd(math.gcd(n_pages, PAGES_PER_STEP_DECODE), GWs // page)
    NG = n_pages // G
    GW = G * page
    ratio = GWs // GW
    R = group * t_new
    SB = mask.shape[2] // R
    kern = functools.partial(_dsa_decode_kernel, n_pages_step=G, n_kv=n_kv, group=group, hd=hd, t_new=t_new,
                             past_len=n_pages * page)

    def page_map(i):
        return lambda b, g, pt: (layer, pt[b, g * G + i], 0, 0)

    in_specs = [pl.BlockSpec(memory_space=pltpu.SMEM),
                pl.BlockSpec((1, n_kv, R, hd), lambda b, g, pt: (b, 0, 0, 0)),
                pl.BlockSpec((1, LANE, n_kv * hd), lambda b, g, pt: (b, 0, 0)),
                pl.BlockSpec((1, LANE, n_kv * hd), lambda b, g, pt: (b, 0, 0)),
                pl.BlockSpec((1, 1, R, GW), lambda b, g, pt: (b // SB, g // ratio, b % SB, g % ratio)),
                pl.BlockSpec((1, 1, R, GW), lambda b, g, pt: (b // SB, mask.shape[1] - 1, b % SB, 0))]
    in_specs += [pl.BlockSpec((1, 1, page * n_kv, hd), page_map(i)) for i in range(G)]
    in_specs += [pl.BlockSpec((1, 1, page * n_kv, hd), page_map(i)) for i in range(G)]
    return pl.pallas_call(
        kern,
        grid_spec=pltpu.PrefetchScalarGridSpec(
            num_scalar_prefetch=1,
            grid=(B, NG),
            in_specs=in_specs,
            out_specs=pl.BlockSpec((1, n_kv, R, hd), lambda b, g, pt: (b, 0, 0, 0)),
            scratch_shapes=[pltpu.VMEM((n_kv, R, 1), F32), pltpu.VMEM((n_kv, R, 1), F32),
                            pltpu.VMEM((n_kv, R, hd), F32)]),
        out_shape=jax.ShapeDtypeStruct((B, n_kv, R, hd), CDT),
        compiler_params=_cparams(("parallel", "arbitrary")),
        name="dsa_decode",
    )(page_table, rel_bias, q4, kn, vn, mask, mask, *([cache_k] * G), *([cache_v] * G))


def _ffn_kernel(*refs, seq_tiles, t_seq, conv_w):
    if seq_tiles:
        (x_ref, g_ref, sc_ref, sh_ref, gt_ref, wg_ref, wv_ref, wd_ref, cw_ref, cb_ref,
         y_ref, tail_ref, h_ref, acc_ref, stash_ref) = refs
    else:
        (x_ref, g_ref, sc_ref, sh_ref, gt_ref, wg_ref, wv_ref, wd_ref, cw_ref, cb_ref, p1_ref, p2_ref,
         y_ref, tail_ref, h_ref, acc_ref) = refs
    i = pl.program_id(0)
    j = pl.program_id(1)

    @pl.when(j == 0)
    def _():
        h = _rms(x_ref[...], g_ref[...]) * (1.0 + sc_ref[0]) + sh_ref[0]
        h_ref[...] = h.astype(CDT)
        acc_ref[...] = jnp.zeros(acc_ref.shape, F32)

    if seq_tiles:
        @pl.when((i == 0) & (j == 0))
        def _():
            stash_ref[...] = jnp.zeros(stash_ref.shape, F32)

    gate = jnp.dot(h_ref[...], wg_ref[...], preferred_element_type=F32)
    val = jnp.dot(h_ref[...], wv_ref[...], preferred_element_type=F32)
    tm = gate.shape[0]
    row = lax.broadcasted_iota(jnp.int32, gate.shape, 0)
    g1 = pltpu.roll(gate, 1, 0)
    g2 = pltpu.roll(gate, 2, 0)
    if seq_tiles:
        prev = jnp.where(i % seq_tiles == 0, 0.0, stash_ref[j])
        g1 = jnp.where(row == 0, prev[7:8], g1)
        g2 = jnp.where(row == 0, prev[6:7], jnp.where(row == 1, prev[7:8], g2))
        stash_ref[j] = gate[tm - 8:]
        tail_ref[0] = gate[tm - 8:]
    else:
        t = row % t_seq
        g1 = jnp.where(t == 0, p1_ref[...], g1)
        g2 = jnp.where(t < 2, p2_ref[...], g2)
        tail_ref[...] = gate
    cw = cw_ref[...]
    conv = cw[0:1] * g2 + cw[1:2] * g1 + cw[2:3] * gate + cb_ref[...]
    act = (_silu(conv) * val).astype(CDT)
    acc_ref[...] += jnp.dot(act, wd_ref[...], preferred_element_type=F32)

    @pl.when(j == pl.num_programs(1) - 1)
    def _():
        y_ref[...] = x_ref[...] + gt_ref[0] * acc_ref[...]


def _ffn(x, g, sc, sh, gt, w_up, w_down, conv_w, conv_b, rows_per_group, t_seq, prev=None):
    M, D = x.shape
    Fd = w_down.shape[0]
    assert conv_w.shape[0] == 3
    tf = _tile(Fd, FFN_TILE_F)
    nf = Fd // tf
    cw = jnp.pad(conv_w, ((0, 8 - conv_w.shape[0]), (0, 0)))
    cb = conv_b.reshape(1, Fd)
    seq_mode = prev is None
    if seq_mode:
        tm = _tile(t_seq, 512, 8)
        assert t_seq % tm == 0 and tm >= 8
        seq_tiles = t_seq // tm
    else:
        tm = M
        seq_tiles = 0
    nt = M // tm
    tpg = max(rows_per_group // tm, 1)
    mod = lambda m: pl.BlockSpec((1, m.shape[1], D), lambda i, j: (i // tpg, 0, 0))
    in_specs = [pl.BlockSpec((tm, D), lambda i, j: (i, 0)),
                pl.BlockSpec((1, D), lambda i, j: (0, 0)),
                mod(sc), mod(sh), mod(gt),
                pl.BlockSpec((D, tf), lambda i, j: (0, j)),
                pl.BlockSpec((D, tf), lambda i, j: (0, nf + j)),
                pl.BlockSpec((tf, D), lambda i, j: (j, 0)),
                pl.BlockSpec((8, tf), lambda i, j: (0, j)),
                pl.BlockSpec((1, tf), lambda i, j: (0, j))]
    args = [x, g.reshape(1, D), sc, sh, gt, w_up, w_up, w_down, cw, cb]
    scratch = [pltpu.VMEM((tm, D), CDT), pltpu.VMEM((tm, D), F32)]
    if seq_mode:
        tail_shape = jax.ShapeDtypeStruct((nt, 8, Fd), F32)
        tail_spec = pl.BlockSpec((1, 8, tf), lambda i, j: (i, 0, j))
        scratch.append(pltpu.VMEM((nf, 8, tf), F32))
    else:
        p1 = jnp.concatenate([prev[:, 1:2], jnp.zeros_like(prev[:, :1]).repeat(t_seq - 1, axis=1)], axis=1)
        p2 = jnp.concatenate([prev[:, 0:2], jnp.zeros_like(prev[:, :1]).repeat(t_seq - 2, axis=1)], axis=1)
        args += [p1.reshape(M, Fd), p2.reshape(M, Fd)]
        in_specs += [pl.BlockSpec((tm, tf), lambda i, j: (i, j))] * 2
        tail_shape = jax.ShapeDtypeStruct((M, Fd), F32)
        tail_spec = pl.BlockSpec((tm, tf), lambda i, j: (i, j))
    kern = functools.partial(_ffn_kernel, seq_tiles=seq_tiles, t_seq=t_seq, conv_w=conv_w.shape[0])
    return pl.pallas_call(
        kern,
        grid=(nt, nf),
        in_specs=in_specs,
        out_specs=[pl.BlockSpec((tm, D), lambda i, j: (i, 0)), tail_spec],
        out_shape=[jax.ShapeDtypeStruct((M, D), F32), tail_shape],
        scratch_shapes=scratch,
        compiler_params=_cparams(("arbitrary", "arbitrary")),
        name="ffn",
    )(*args)


def _final_norm_kernel(x_ref, g_ref, o_ref):
    o_ref[...] = _rms(x_ref[...], g_ref[...])


def _final_norm(x, g):
    M, D = x.shape
    tm = _tile(M, 1024, 8)
    return pl.pallas_call(
        _final_norm_kernel,
        grid=(M // tm,),
        in_specs=[pl.BlockSpec((tm, D), lambda i: (i, 0)), pl.BlockSpec((1, D), lambda i: (0, 0))],
        out_specs=pl.BlockSpec((tm, D), lambda i: (i, 0)),
        out_shape=jax.ShapeDtypeStruct((M, D), F32),
        compiler_params=_cparams(("parallel",)),
        name="final_norm",
    )(x, g.reshape(1, D))


def _rope_tables(pos, rope, n_heads, reps):
    half = rope // 2
    inv = ROPE_BASE ** (-jnp.arange(half, dtype=F32) / half)
    ang = pos.astype(F32)[:, None] * inv[None, :]
    cos = jnp.cos(ang)
    sin = jnp.sin(ang)
    cosk = jnp.concatenate([cos, cos], axis=1)
    sink = jnp.concatenate([-sin, sin], axis=1)
    tabs = (jnp.tile(cosk, (1, n_heads)), jnp.tile(sink, (1, n_heads)), cosk, sink)
    return tuple(jnp.tile(t, (reps, 1)) for t in tabs)


def _swap_halves(w, rope):
    half = rope // 2
    return jnp.concatenate([w[..., half:], w[..., :half]], axis=-1)


def _pad_cols(w, n):
    return jnp.pad(w, ((0, 0), (0, n - w.shape[1])))


def kernel(x_prompt, x_sample, cache_mla_ckv, cache_mla_kpe, state_gla, cache_dsa_k, cache_dsa_v, cache_dsa_kidx,
           state_ffn_conv, page_table, c_prompt, c_sample, ada_w, ada_b, norm1_g, norm2_g, final_g, mla_w_in,
           mla_g_q, mla_g_kv, mla_w_uq, mla_w_uk, mla_w_uv, mla_w_o, gla_w_in, gla_w_a2, gla_b_a2, gla_g_o,
           gla_w_o, dsa_w_in, dsa_w_o, rel_bias, ffn_w_up, ffn_conv_w, ffn_conv_b, ffn_w_down):
    Bp, Tp, D = x_prompt.shape
    Bs, Ts, _ = x_sample.shape
    depth = ada_w.shape[0]
    n_mod = ada_w.shape[2] // D
    n_pages, page = page_table.shape[1], cache_mla_ckv.shape[2]
    past_len = n_pages * page
    cache_kpe_t = jnp.swapaxes(cache_mla_kpe, 2, 3)
    cache_ki_t = jnp.swapaxes(cache_dsa_kidx, 2, 3)

    q_lora, mla_h, qk_dim = mla_w_uq.shape[1:]
    kv_lora, _, nope = mla_w_uk.shape[1:]
    mla_v = mla_w_uv.shape[3]
    rope = qk_dim - nope
    gla_h = state_gla.shape[2]
    gla_dk, gla_dv = state_gla.shape[3:]
    gla_rank = gla_w_a2.shape[1]
    dsa_kv, dsa_hd = cache_dsa_k.shape[3:]
    dsa_di = cache_dsa_kidx.shape[3]
    dsa_h = dsa_w_o.shape[1] // dsa_hd
    dsa_hi = (dsa_w_in.shape[2] - (dsa_h + 2 * dsa_kv) * dsa_hd - dsa_di) // (dsa_di + 1)
    assert kv_lora % LANE == 0 and q_lora % LANE == 0 and rope <= LANE

    mod_all = _ada_mod(jnp.concatenate([c_prompt, c_sample], axis=0), ada_w, ada_b)

    def mods(l, sample):
        m = mod_all[l, Bp:] if sample else mod_all[l, :Bp]
        parts = [m[:, i * D:(i + 1) * D] for i in range(n_mod)]
        if sample:
            return [jnp.repeat(p, Ts, axis=0).reshape(1, Bs * Ts, D) for p in parts]
        return [p.reshape(Bp, 1, D) for p in parts]

    mla_w = []
    for j in range(mla_w_in.shape[0]):
        w_in = mla_w_in[j]
        kcol = q_lora + kv_lora
        w_in_ext = jnp.concatenate([w_in, _swap_halves(w_in[:, kcol:kcol + rope], rope)], axis=1)
        w_in_ext = _pad_cols(w_in_ext, -(-w_in_ext.shape[1] // LANE) * LANE).astype(CDT)
        uq = mla_w_uq[j]
        wqn = uq[:, :, :nope].reshape(q_lora, mla_h * nope).astype(CDT)
        wqp = uq[:, :, nope:].reshape(q_lora, mla_h * rope).astype(CDT)
        wqs = _swap_halves(uq[:, :, nope:], rope).reshape(q_lora, mla_h * rope).astype(CDT)
        wukT = jnp.transpose(mla_w_uk[j], (1, 2, 0)).astype(CDT)
        wuv = jnp.transpose(mla_w_uv[j], (1, 0, 2)).astype(CDT)
        mla_w.append((w_in_ext, wqn, wqp, wqs, wukT, wuv, mla_w_o[j].astype(CDT)))
    gla_np = -(-gla_w_in.shape[2] // LANE) * LANE
    gla_w = []
    for j in range(gla_w_in.shape[0]):
        w_a2p = jnp.pad(gla_w_a2[j], ((0, LANE - gla_rank), (0, 0))).astype(CDT)
        gla_w.append((_pad_cols(gla_w_in[j], gla_np).astype(CDT), w_a2p, gla_w_o[j].astype(CDT)))
    dsa_np = -(-dsa_w_in.shape[2] // LANE) * LANE
    dsa_w = [(_pad_cols(dsa_w_in[j], dsa_np).astype(CDT), dsa_w_o[j].astype(CDT)) for j in range(dsa_w_in.shape[0])]
    ffn_up = ffn_w_up.astype(CDT)
    ffn_down = ffn_w_down.astype(CDT)

    mla_dims = (mla_h, q_lora, kv_lora, rope, nope)
    mla_scale = qk_dim ** -0.5

    def trunk(x3, sample):
        B, T, _ = x3.shape
        M = B * T
        x = x3.reshape(M, D)
        rpg = M if sample else T
        pos = (past_len if sample else 0) + jnp.arange(T, dtype=jnp.int32)
        tabs = _rope_tables(pos, rope, mla_h, B if sample else 1)
        outs = dict(mla_ckv=[], mla_kpe=[], gla=[], dsa_k=[], dsa_v=[], dsa_ki=[], conv=[])
        for l in range(depth):
            sh1, sc1, gt1, sh2, sc2, gt2 = mods(l, sample)
            j = l // 3
            if l % 3 == 0:
                w_in_ext, wqn, wqp, wqs, wukT, wuv, wo = mla_w[j]
                p = _nm_linear(x, norm1_g[l], sc1, sh1, w_in_ext, rpg)
                ckv, kpe, kcat, qcat = _mla_prep(p, mla_g_q[j], mla_g_kv[j], tabs, wqn, wqp, wqs, wukT, mla_dims)
                if sample:
                    o_lat = _mla_decode(qcat, kcat, cache_mla_ckv, cache_kpe_t, j, page_table, B, T, mla_h,
                                        kv_lora, rope, mla_scale)
                else:
                    o_lat = _mla_flash(qcat, kcat, B, T, mla_h, kv_lora, mla_scale)
                x = _mla_out(o_lat, wuv, wo, x, gt1, rpg, mla_h, kv_lora)
                outs["mla_ckv"].append(ckv.reshape(B, T, kv_lora))
                outs["mla_kpe"].append(kpe.reshape(B, T, rope))
            elif l % 3 == 1:
                w_in_p, w_a2p, wo = gla_w[j]
                hk, hv = gla_h * gla_dk, gla_h * gla_dv
                p = _nm_linear(x, norm1_g[l], sc1, sh1, w_in_p, rpg, tn_target=NM_LINEAR_TN)
                log_a = _gla_gate(p, (2 * hk + 2 * hv) // LANE, w_a2p, gla_b_a2[j])
                s0 = state_gla[j] if sample else jnp.zeros((B, gla_h, gla_dk, gla_dv), F32)
                o, s_fin = _gla_recurrence(p, log_a, s0, B, T, gla_h, gla_dk, gla_dv)
                x = _gla_out(o, p, (2 * hk + hv) // hv, gla_g_o[j], wo, x, gt1, rpg, gla_h, gla_dv)
                outs["gla"].append(s_fin)
            else:
                w_in_p, wo = dsa_w[j]
                hq, hkv, hi = dsa_h * dsa_hd, dsa_kv * dsa_hd, dsa_hi * dsa_di
                p = _nm_linear(x, norm1_g[l], sc1, sh1, w_in_p, rpg, tn_target=NM_LINEAR_TN, with_cdt_copy=not sample)
                if not sample:
                    p, p_c = p
                k_new = p[:, hq:hq + hkv]
                v_new = p[:, hq + hkv:hq + 2 * hkv]
                kk = p[:, hq + 2 * hkv + hi:]
                L_keys = (past_len if sample else 0) + T
                topk = min(DSA_TOPK, L_keys // 4)
                if sample:
                    group = dsa_h // dsa_kv
                    dup = lambda a: jnp.concatenate([a.reshape(B, T, dsa_hi, -1)] * group, axis=1)
                    qi_hm = dup(p[:, hq + 2 * hkv:hq + 2 * hkv + hi]).transpose(0, 2, 1, 3)
                    qi_hm = qi_hm.reshape(B, dsa_hi * group * T, dsa_di)
                    wi_hm = dup(kk[:, dsa_di:dsa_di + dsa_hi]).transpose(0, 2, 1, 3).reshape(B, dsa_hi * group * T, 1)
                    padn = lambda a: jnp.pad(a.reshape(B, T, -1), ((0, 0), (0, LANE - T), (0, 0)))
                    mask = _dsa_select(qi_hm, wi_hm, padn(kk), cache_ki_t, j, page_table, B, dsa_hi, dsa_di, topk, T)
                    q4 = p[:, :hq].reshape(B, T, dsa_kv, group, dsa_hd).transpose(0, 2, 3, 1, 4)
                    q4 = q4.reshape(B, dsa_kv, group * T, dsa_hd)
                    ck = cache_dsa_k.reshape(*cache_dsa_k.shape[:2], page * dsa_kv, dsa_hd)
                    cv = cache_dsa_v.reshape(*cache_dsa_v.shape[:2], page * dsa_kv, dsa_hd)
                    o4 = _dsa_decode(q4, padn(k_new), padn(v_new), mask, rel_bias, ck, cv, j, page_table, B,
                                     dsa_kv, group, dsa_hd, T)
                    o = o4.reshape(B, dsa_kv, group, T, dsa_hd).transpose(0, 3, 1, 2, 4).reshape(M, hq)
                else:
                    o = _dsa_prompt(p, p_c, rel_bias, B, T, (dsa_h, dsa_kv, dsa_hd, dsa_hi, dsa_di), topk)
                x = _proj_res(o, wo, x, gt1, rpg)
                outs["dsa_k"].append(k_new.reshape(B, T, dsa_kv, dsa_hd))
                outs["dsa_v"].append(v_new.reshape(B, T, dsa_kv, dsa_hd))
                outs["dsa_ki"].append(kk[:, :dsa_di].reshape(B, T, dsa_di))
            if sample:
                x, tail = _ffn(x, norm2_g[l], sc2, sh2, gt2, ffn_up[l], ffn_down[l], ffn_conv_w[l], ffn_conv_b[l],
                               rpg, T, prev=state_ffn_conv[l])
                outs["conv"].append(tail.reshape(B, T, -1)[:, T - 2:])
            else:
                x, tail = _ffn(x, norm2_g[l], sc2, sh2, gt2, ffn_up[l], ffn_down[l], ffn_conv_w[l], ffn_conv_b[l],
                               rpg, T)
                nt = tail.shape[0] // B
                outs["conv"].append(tail.reshape(B, nt, 8, -1)[:, nt - 1, 6:8])
        y = _final_norm(x, final_g).reshape(B, T, D)
        return (y, jnp.stack(outs["mla_ckv"]), jnp.stack(outs["mla_kpe"]), jnp.stack(outs["gla"]),
                jnp.stack(outs["dsa_k"]), jnp.stack(outs["dsa_v"]), jnp.stack(outs["dsa_ki"]),
                jnp.stack(outs["conv"]))

    rp = trunk(x_prompt, False)
    rs = trunk(x_sample, True)
    return (rp[0], rs[0]) + tuple(rp[1:]) + tuple(rs[1:])
```
